```python
import math
import jax
import jax.numpy as jnp
from jax import lax
import numpy as np

D_MODEL = 1024
BATCH = 8
SEQ = 2048
DEPTH = 4
DEC_BATCH = 32
DEC_SEQ = 4
PAST_LEN = 8192
PAGE_SIZE = 128

N_MIXERS = 2
N_NSA_LAYERS = (DEPTH + N_MIXERS - 1) // N_MIXERS
N_SSD_LAYERS = DEPTH // N_MIXERS

NSA_HEADS = 16
NSA_HEAD_DIM = D_MODEL // NSA_HEADS
NSA_KV_GROUPS = 4
NSA_GROUP_SIZE = NSA_HEADS // NSA_KV_GROUPS
CMP_BLOCK = 32
CMP_STRIDE = 16
CMP_OVERLAP = CMP_BLOCK // CMP_STRIDE
CMP_HIDDEN = 4 * NSA_HEAD_DIM
SEL_BLOCK = 64
SEL_CHUNKS = SEL_BLOCK // CMP_STRIDE
SEL_TOPN = 16
WINDOW = 512
NSA_QUERY_BLOCK = 32
NSA_Q_DIM = NSA_HEADS * NSA_HEAD_DIM
NSA_KV_DIM = 2 * NSA_KV_GROUPS * NSA_HEAD_DIM
NSA_IN = NSA_Q_DIM + 3 * NSA_KV_DIM + 3 * NSA_HEADS

SSM_EXPAND = 2
SSM_D_INNER = SSM_EXPAND * D_MODEL
SSM_HEAD_DIM = 64
SSM_HEADS = SSM_D_INNER // SSM_HEAD_DIM
SSM_GROUPS = 4
SSM_HEADS_PER_GROUP = SSM_HEADS // SSM_GROUPS
SSM_STATE = 128
SSM_CONV = 4
SSM_CONV_DIM = SSM_D_INNER + 2 * SSM_GROUPS * SSM_STATE
SSM_IN = SSM_D_INNER + SSM_CONV_DIM + SSM_HEADS
SSM_CHUNK = 128

FFN_HIDDEN = -(-(8 * D_MODEL) // (3 * 256)) * 256

NORM_EPS = 1e-6
NEG_INF = -1e30
FORCE_SCORE = 1e4

kernel_name = "nsa_mamba2_hybrid_decode_step"


def rms_norm(x, g):
    xf = x.astype(jnp.float32)
    y = xf * lax.rsqrt(jnp.mean(xf * xf, axis=-1, keepdims=True) + NORM_EPS)
    return (y * g.astype(jnp.float32)).astype(x.dtype)


def masked_softmax(logits, mask):
    logits = jnp.where(mask, logits, NEG_INF)
    return jnp.where(mask, jax.nn.softmax(logits, axis=-1), 0.0)


def alibi_slopes(n):
    start = 2.0 ** (-8.0 / n)
    return jnp.asarray(start ** np.arange(1, n + 1), dtype=jnp.float32)


def swiglu(h, w_gate, w_up, w_down):
    return (jax.nn.silu(h @ w_gate) * (h @ w_up)) @ w_down


def gather_pages(pool, page_table):
    pages = pool[page_table]
    b, n_pages = page_table.shape
    return pages.reshape((b, n_pages * PAGE_SIZE) + pages.shape[3:])


def compress_blocks(rows, pe, w1, w2):
    b, t = rows.shape[0], rows.shape[1]
    n_chunk = t // CMP_STRIDE
    n_blk = n_chunk - CMP_OVERLAP + 1
    chunks = rows[:, : n_chunk * CMP_STRIDE].reshape(b, n_chunk, CMP_STRIDE, NSA_KV_GROUPS, NSA_HEAD_DIM)
    pe = pe.reshape(CMP_OVERLAP, CMP_STRIDE, NSA_HEAD_DIM).astype(rows.dtype)
    w1 = w1.reshape(CMP_OVERLAP, CMP_STRIDE, NSA_HEAD_DIM, CMP_HIDDEN)
    pre = 0.0
    for m in range(CMP_OVERLAP):
        part = jnp.einsum("bcjgd,jdh->bcgh", chunks + pe[m][None, None, :, None, :], w1[m])
        pre = pre + part[:, m:m + n_blk]
    return jnp.einsum("bngh,hd->bngd", jax.nn.silu(pre), w2)


def nsa_attend(q, q_pos, kc, vc, ks, vs, kw, vw, kw_pos):
    b, t = q.shape[0], q.shape[1]
    f32 = jnp.float32
    qb = NSA_QUERY_BLOCK if t % NSA_QUERY_BLOCK == 0 else t
    n_qblk = t // qb
    span = kw.shape[1] - (n_qblk - 1) * qb
    n_cmp = kc.shape[1]
    n_chunk = n_cmp + CMP_OVERLAP - 1
    kc_pos = jnp.arange(n_cmp, dtype=jnp.int32) * CMP_STRIDE + (CMP_BLOCK - 1)
    t_k = ks.shape[1]
    n_sel = -(-t_k // SEL_BLOCK)
    top_n = min(SEL_TOPN, n_sel)
    pad = n_sel * SEL_BLOCK - t_k

    def to_blocks(r):
        r = jnp.pad(r, ((0, 0), (0, pad), (0, 0), (0, 0)))
        r = r.reshape(b, n_sel, SEL_BLOCK, NSA_KV_GROUPS, NSA_HEAD_DIM)
        return r.transpose(0, 3, 1, 2, 4)

    ks_blk = to_blocks(ks)
    vs_blk = to_blocks(vs)
    gather = jax.vmap(jax.vmap(lambda blocks, idx: blocks[idx]))
    slopes = alibi_slopes(NSA_HEADS).reshape(NSA_KV_GROUPS, NSA_GROUP_SIZE)
    scale = NSA_HEAD_DIM ** -0.5
    blk_ids = jnp.arange(n_sel, dtype=jnp.int32)
    in_blk = jnp.arange(SEL_BLOCK, dtype=jnp.int32)

    def block_fn(i):
        q0 = i * qb
        qg = lax.dynamic_slice_in_dim(q, q0, qb, axis=1).reshape(b, qb, NSA_KV_GROUPS, NSA_GROUP_SIZE, NSA_HEAD_DIM)
        tq = lax.dynamic_slice_in_dim(q_pos, q0, qb)
        dist_c = tq[:, None] - kc_pos[None, :]
        lc = (jnp.einsum("bqgrd,bngd->bqgrn", qg, kc).astype(f32) * scale
              - slopes[None, None, :, :, None] * dist_c[None, :, None, None, :].astype(f32))
        pc = masked_softmax(lc, (dist_c >= 0)[None, :, None, None, :])
        o_c = jnp.einsum("bqgrn,bngd->bqgrd", pc.astype(vc.dtype), vc)
        imp = jnp.pad(pc.sum(axis=3), ((0, 0), (0, 0), (0, 0), (CMP_OVERLAP - 1, CMP_OVERLAP - 1)))
        p_chunk = 0.0
        for n in range(CMP_OVERLAP):
            p_chunk = p_chunk + imp[..., CMP_OVERLAP - 1 - n: CMP_OVERLAP - 1 - n + n_chunk]
        p_chunk = jnp.pad(p_chunk, ((0, 0), (0, 0), (0, 0), (0, n_sel * SEL_CHUNKS - n_chunk)))
        p_slc = p_chunk.reshape(b, qb, NSA_KV_GROUPS, n_sel, SEL_CHUNKS).sum(axis=-1)
        cur = tq // SEL_BLOCK
        forced = (blk_ids[None] == 0) | (blk_ids[None] == cur[:, None]) | (blk_ids[None] == cur[:, None] - 1)
        visible = blk_ids[None] * SEL_BLOCK <= tq[:, None]
        score = jnp.where(forced[None, :, None], FORCE_SCORE, jnp.where(visible[None, :, None], p_slc, -1.0))
        _, sel = lax.top_k(score, top_n)
        sel_t = sel.transpose(0, 2, 1, 3).reshape(b, NSA_KV_GROUPS, qb * top_n)
        k_sel = gather(ks_blk, sel_t).reshape(b, NSA_KV_GROUPS, qb, top_n, SEL_BLOCK, NSA_HEAD_DIM)
        v_sel = gather(vs_blk, sel_t).reshape(b, NSA_KV_GROUPS, qb, top_n, SEL_BLOCK, NSA_HEAD_DIM)
        s_pos = sel[..., None] * SEL_BLOCK + in_blk
        dist_s = tq[None, :, None, None, None] - s_pos
        ls = (jnp.einsum("bqgrd,bgqksd->bqgrks", qg, k_sel).astype(f32) * scale
              - slopes[None, None, :, :, None, None] * dist_s[:, :, :, None].astype(f32))
        ps = masked_softmax(ls.reshape(b, qb, NSA_KV_GROUPS, NSA_GROUP_SIZE, top_n * SEL_BLOCK),
                            (dist_s >= 0).reshape(b, qb, NSA_KV_GROUPS, 1, top_n * SEL_BLOCK))
        ps = ps.reshape(b, qb, NSA_KV_GROUPS, NSA_GROUP_SIZE, top_n, SEL_BLOCK)
        o_s = jnp.einsum("bqgrks,bgqksd->bqgrd", ps.astype(vs.dtype), v_sel)
        kwb = lax.dynamic_slice_in_dim(kw, q0, span, axis=1)
        vwb = lax.dynamic_slice_in_dim(vw, q0, span, axis=1)
        wpos = lax.dynamic_slice_in_dim(kw_pos, q0, span)
        dist_w = tq[:, None] - wpos[None, :]
        mask_w = (dist_w >= 0) & (dist_w < WINDOW) & (wpos[None, :] >= 0)
        lw = (jnp.einsum("bqgrd,bsgd->bqgrs", qg, kwb).astype(f32) * scale
              - slopes[None, None, :, :, None] * dist_w[None, :, None, None, :].astype(f32))
        pw = masked_softmax(lw, mask_w[None, :, None, None, :])
        o_w = jnp.einsum("bqgrs,bsgd->bqgrd", pw.astype(vw.dtype), vwb)
        shp = (b, qb, NSA_HEADS, NSA_HEAD_DIM)
        return (o_c.reshape(shp), o_s.reshape(shp), o_w.reshape(shp))

    o_c, o_s, o_w = lax.map(block_fn, jnp.arange(n_qblk))
    def unblock(o):
        return o.transpose(1, 0, 2, 3, 4).reshape(b, t, NSA_HEADS, NSA_HEAD_DIM)
    return unblock(o_c), unblock(o_s), unblock(o_w)


def nsa_layer(h, past_len, kvc_past, kvs_past, kvw_buf, keep, w_in, qk_gain, cmp_pe, cmp_w1, cmp_w2, w_out):
    b, t = h.shape[0], h.shape[1]
    proj = h @ w_in
    q, kvc, kvs, kvw, gate = jnp.split(
        proj, [NSA_Q_DIM, NSA_Q_DIM + NSA_KV_DIM, NSA_Q_DIM + 2 * NSA_KV_DIM, NSA_Q_DIM + 3 * NSA_KV_DIM], axis=-1)
    kv_shape = (b, t, 2, NSA_KV_GROUPS, NSA_HEAD_DIM)
    q = rms_norm(q.reshape(b, t, NSA_HEADS, NSA_HEAD_DIM), qk_gain[0])
    kvc = kvc.reshape(kv_shape)
    kvs = kvs.reshape(kv_shape)
    kvs = jnp.stack([rms_norm(kvs[:, :, 0], qk_gain[2]), kvs[:, :, 1]], axis=2)
    kvw = kvw.reshape(kv_shape)
    kvw = jnp.stack([rms_norm(kvw[:, :, 0], qk_gain[3]), kvw[:, :, 1]], axis=2)
    gate = jax.nn.sigmoid(gate.astype(jnp.float32)).reshape(b, t, 3, NSA_HEADS)
    full_c = jnp.concatenate([kvc_past.astype(h.dtype), kvc], axis=1)
    full_s = jnp.concatenate([kvs_past.astype(h.dtype), kvs], axis=1)
    win = jnp.concatenate([kvw_buf.astype(h.dtype), kvw], axis=1)
    kc = rms_norm(compress_blocks(full_c[:, :, 0], cmp_pe[0], cmp_w1[0], cmp_w2[0]), qk_gain[1])
    vc = compress_blocks(full_c[:, :, 1], cmp_pe[1], cmp_w1[1], cmp_w2[1])
    q_pos = past_len + jnp.arange(t, dtype=jnp.int32)
    buf = kvw_buf.shape[1]
    w_pos = (past_len - buf) + jnp.arange(buf + t, dtype=jnp.int32)
    o_c, o_s, o_w = nsa_attend(q, q_pos, kc, vc, full_s[:, :, 0], full_s[:, :, 1], win[:, :, 0], win[:, :, 1], w_pos)
    o = gate[:, :, 0, :, None] * o_c + gate[:, :, 1, :, None] * o_s + gate[:, :, 2, :, None] * o_w
    y = o.astype(h.dtype).reshape(b, t, NSA_Q_DIM) @ w_out
    return y, kvc, kvs, win[:, win.shape[1] - keep:]


def ssd_scan(x, dt, a, bm, cm, h0):
    b, t = x.shape[0], x.shape[1]
    f32 = jnp.float32
    lc = SSM_CHUNK if t % SSM_CHUNK == 0 else t
    nc = t // lc

    def chunked(v, tail):
        return jnp.moveaxis(v.astype(f32).reshape((b, nc, lc) + tail), 1, 0)

    xs = chunked(x, (SSM_GROUPS, SSM_HEADS_PER_GROUP, SSM_HEAD_DIM))
    dts = chunked(dt, (SSM_GROUPS, SSM_HEADS_PER_GROUP))
    bs = chunked(bm, (SSM_GROUPS, SSM_STATE))
    cs = chunked(cm, (SSM_GROUPS, SSM_STATE))
    a_g = a.astype(f32).reshape(SSM_GROUPS, SSM_HEADS_PER_GROUP)
    causal = jnp.tril(jnp.ones((lc, lc), dtype=bool))[None, :, :, None, None]

    def step(h_prev, inp):
        xc, dtc, bc, cc = inp
        acum = jnp.cumsum(dtc * a_g, axis=1)
        seg = acum[:, :, None] - acum[:, None, :]
        decay = jnp.exp(jnp.where(causal, seg, -jnp.inf))
        cb = jnp.einsum("blgn,bsgn->blsg", cc, bc)
        xdt = xc * dtc[..., None]
        y = (jnp.einsum("blsgr,bsgrp->blgrp", cb[..., None] * decay, xdt)
             + jnp.einsum("blgn,bgrpn->blgrp", cc, h_prev) * jnp.exp(acum)[..., None])
        to_end = jnp.exp(acum[:, -1:] - acum)
        h_new = (h_prev * jnp.exp(acum[:, -1])[..., None, None]
                 + jnp.einsum("bsgn,bsgrp->bgrpn", bc, xdt * to_end[..., None]))
        return h_new, y

    h0g = h0.astype(f32).reshape(b, SSM_GROUPS, SSM_HEADS_PER_GROUP, SSM_HEAD_DIM, SSM_STATE)
    h_last, ys = lax.scan(step, h0g, (xs, dts, bs, cs))
    y = jnp.moveaxis(ys, 0, 1).reshape(b, t, SSM_HEADS, SSM_HEAD_DIM)
    return y, h_last.reshape(b, SSM_HEADS, SSM_HEAD_DIM, SSM_STATE)


def ssd_layer(h, conv_state, ssm_state, w_in, conv_w, conv_b, dt_bias, a_log, d_skip, norm_w, w_out):
    b, t = h.shape[0], h.shape[1]
    proj = h @ w_in
    z, xbc, dt_raw = jnp.split(proj, [SSM_D_INNER, SSM_D_INNER + SSM_CONV_DIM], axis=-1)
    xbc_full = jnp.concatenate([conv_state.astype(h.dtype), xbc], axis=1)
    conv = conv_b
    for k in range(SSM_CONV):
        conv = conv + xbc_full[:, k:k + t] * conv_w[k]
    conv = jax.nn.silu(conv)
    xh, bm, cm = jnp.split(conv, [SSM_D_INNER, SSM_D_INNER + SSM_GROUPS * SSM_STATE], axis=-1)
    xh = xh.reshape(b, t, SSM_HEADS, SSM_HEAD_DIM)
    bm = bm.reshape(b, t, SSM_GROUPS, SSM_STATE)
    cm = cm.reshape(b, t, SSM_GROUPS, SSM_STATE)
    dt = jax.nn.softplus(dt_raw.astype(jnp.float32) + dt_bias.astype(jnp.float32))
    a = -jnp.exp(a_log.astype(jnp.float32))
    y, h_last = ssd_scan(xh, dt, a, bm, cm, ssm_state)
    y = y + xh.astype(jnp.float32) * d_skip.astype(jnp.float32)[:, None]
    gated = y.reshape(b, t, SSM_D_INNER) * jax.nn.silu(z.astype(jnp.float32))
    gated = rms_norm(gated.reshape(b, t, SSM_GROUPS, SSM_D_INNER // SSM_GROUPS),
                     norm_w.reshape(SSM_GROUPS, SSM_D_INNER // SSM_GROUPS)).reshape(b, t, SSM_D_INNER)
    out = gated.astype(h.dtype) @ w_out
    return out, xbc_full[:, xbc_full.shape[1] - (SSM_CONV - 1):], h_last.astype(ssm_state.dtype)


def setup_inputs(seed: int = 0) -> dict:
    key = jax.random.key(seed)
    ks = jax.random.split(key, 32)
    f32 = jnp.float32
    nrm = jax.random.normal
    n_pages = PAST_LEN // PAGE_SIZE
    n_used = DEC_BATCH * n_pages
    n_phys = n_used + -(-n_used // 4)
    win_rows = min(WINDOW, PAST_LEN)
    kvp = (2, NSA_KV_GROUPS, NSA_HEAD_DIM)

    def gain(k, shape):
        return 1.0 + 0.02 * nrm(k, shape, f32)

    x_prompt = nrm(ks[0], (BATCH, SEQ, D_MODEL), f32)
    x_sample = nrm(ks[1], (DEC_BATCH, DEC_SEQ, D_MODEL), f32)
    cache_kv_cmp = nrm(ks[2], (N_NSA_LAYERS, n_phys, PAGE_SIZE) + kvp, f32)
    cache_kv_sel = nrm(ks[3], (N_NSA_LAYERS, n_phys, PAGE_SIZE) + kvp, f32)
    cache_kv_win = nrm(ks[4], (N_NSA_LAYERS, DEC_BATCH, win_rows) + kvp, f32)
    state_conv = nrm(ks[5], (N_SSD_LAYERS, DEC_BATCH, SSM_CONV - 1, SSM_CONV_DIM), f32)
    state_ssm = 0.1 * nrm(ks[6], (N_SSD_LAYERS, DEC_BATCH, SSM_HEADS, SSM_HEAD_DIM, SSM_STATE), f32)
    page_table = jax.random.permutation(ks[7], n_phys)[:n_used].reshape(DEC_BATCH, n_pages).astype(jnp.int32)

    norm_mix = gain(ks[8], (DEPTH, D_MODEL))
    norm_ffn = gain(ks[9], (DEPTH, D_MODEL))
    nsa_w_in = nrm(ks[10], (N_NSA_LAYERS, D_MODEL, NSA_IN), f32) * D_MODEL ** -0.5
    nsa_qk_gain = gain(ks[11], (N_NSA_LAYERS, 4, NSA_HEAD_DIM))
    nsa_cmp_pe = 0.1 * nrm(ks[12], (N_NSA_LAYERS, 2, CMP_BLOCK, NSA_HEAD_DIM), f32)
    nsa_cmp_w1 = nrm(ks[13], (N_NSA_LAYERS, 2, CMP_BLOCK, NSA_HEAD_DIM, CMP_HIDDEN), f32) * (CMP_BLOCK * NSA_HEAD_DIM) ** -0.5
    nsa_cmp_w2 = nrm(ks[14], (N_NSA_LAYERS, 2, CMP_HIDDEN, NSA_HEAD_DIM), f32) * CMP_HIDDEN ** -0.5
    nsa_w_out = nrm(ks[15], (N_NSA_LAYERS, NSA_Q_DIM, D_MODEL), f32) * NSA_Q_DIM ** -0.5

    ssd_w_in = nrm(ks[16], (N_SSD_LAYERS, D_MODEL, SSM_IN), f32) * D_MODEL ** -0.5
    ssd_conv_w = nrm(ks[17], (N_SSD_LAYERS, SSM_CONV, SSM_CONV_DIM), f32) * SSM_CONV ** -0.5
    ssd_conv_b = 0.01 * nrm(ks[18], (N_SSD_LAYERS, SSM_CONV_DIM), f32)
    dt0 = jnp.exp(jax.random.uniform(ks[19], (N_SSD_LAYERS, SSM_HEADS), f32, math.log(1e-3), math.log(1e-1)))
    ssd_dt_bias = dt0 + jnp.log(-jnp.expm1(-dt0))
    ssd_a_log = jnp.log(jax.random.uniform(ks[20], (N_SSD_LAYERS, SSM_HEADS), f32, 1.0, 16.0))
    ssd_d = gain(ks[21], (N_SSD_LAYERS, SSM_HEADS))
    ssd_norm = gain(ks[22], (N_SSD_LAYERS, SSM_D_INNER))
    ssd_w_out = nrm(ks[23], (N_SSD_LAYERS, SSM_D_INNER, D_MODEL), f32) * SSM_D_INNER ** -0.5

    ffn_w_gate = nrm(ks[24], (DEPTH, D_MODEL, FFN_HIDDEN), f32) * D_MODEL ** -0.5
    ffn_w_up = nrm(ks[25], (DEPTH, D_MODEL, FFN_HIDDEN), f32) * D_MODEL ** -0.5
    ffn_w_down = nrm(ks[26], (DEPTH, FFN_HIDDEN, D_MODEL), f32) * FFN_HIDDEN ** -0.5

    return {
        "x_prompt": x_prompt, "x_sample": x_sample,
        "cache_kv_cmp": cache_kv_cmp, "cache_kv_sel": cache_kv_sel, "cache_kv_win": cache_kv_win,
        "state_conv": state_conv, "state_ssm": state_ssm, "page_table": page_table,
        "norm_mix": norm_mix, "norm_ffn": norm_ffn,
        "nsa_w_in": nsa_w_in, "nsa_qk_gain": nsa_qk_gain, "nsa_cmp_pe": nsa_cmp_pe,
        "nsa_cmp_w1": nsa_cmp_w1, "nsa_cmp_w2": nsa_cmp_w2, "nsa_w_out": nsa_w_out,
        "ssd_w_in": ssd_w_in, "ssd_conv_w": ssd_conv_w, "ssd_conv_b": ssd_conv_b,
        "ssd_dt_bias": ssd_dt_bias, "ssd_a_log": ssd_a_log, "ssd_d": ssd_d,
        "ssd_norm": ssd_norm, "ssd_w_out": ssd_w_out,
        "ffn_w_gate": ffn_w_gate, "ffn_w_up": ffn_w_up, "ffn_w_down": ffn_w_down,
    }


def reference(x_prompt, x_sample, cache_kv_cmp, cache_kv_sel, cache_kv_win, state_conv, state_ssm, page_table,
              norm_mix, norm_ffn, nsa_w_in, nsa_qk_gain, nsa_cmp_pe, nsa_cmp_w1, nsa_cmp_w2, nsa_w_out,
              ssd_w_in, ssd_conv_w, ssd_conv_b, ssd_dt_bias, ssd_a_log, ssd_d, ssd_norm, ssd_w_out,
              ffn_w_gate, ffn_w_up, ffn_w_down):
    xp, xs = x_prompt, x_sample
    bp, tp = xp.shape[0], xp.shape[1]
    past_len = page_table.shape[1] * PAGE_SIZE
    kvp = (2, NSA_KV_GROUPS, NSA_HEAD_DIM)
    cmp_p, sel_p, win_p, conv_p, ssm_p = [], [], [], [], []
    cmp_s, sel_s, win_s, conv_s, ssm_s = [], [], [], [], []
    for i in range(DEPTH):
        l = i // N_MIXERS
        hp = rms_norm(xp, norm_mix[i])
        hs = rms_norm(xs, norm_mix[i])
        if i % N_MIXERS == 0:
            w = (nsa_w_in[l], nsa_qk_gain[l], nsa_cmp_pe[l], nsa_cmp_w1[l], nsa_cmp_w2[l], nsa_w_out[l])
            empty = jnp.zeros((bp, 0) + kvp, xp.dtype)
            yp, c_new, s_new, w_new = nsa_layer(hp, 0, empty, empty, jnp.zeros((bp, WINDOW) + kvp, xp.dtype),
                                                min(WINDOW, tp), *w)
            cmp_p.append(c_new)
            sel_p.append(s_new)
            win_p.append(w_new)
            past_c = gather_pages(cache_kv_cmp[l], page_table)
            past_s = gather_pages(cache_kv_sel[l], page_table)
            y_s, c_new, s_new, w_new = nsa_layer(hs, past_len, past_c, past_s, cache_kv_win[l],
                                                 cache_kv_win.shape[2], *w)
            cmp_s.append(c_new)
            sel_s.append(s_new)
            win_s.append(w_new)
        else:
            w = (ssd_w_in[l], ssd_conv_w[l], ssd_conv_b[l], ssd_dt_bias[l], ssd_a_log[l], ssd_d[l],
                 ssd_norm[l], ssd_w_out[l])
            yp, cv_new, st_new = ssd_layer(hp, jnp.zeros((bp, SSM_CONV - 1, SSM_CONV_DIM), xp.dtype),
                                           jnp.zeros((bp, SSM_HEADS, SSM_HEAD_DIM, SSM_STATE), xp.dtype), *w)
            conv_p.append(cv_new)
            ssm_p.append(st_new)
            y_s, cv_new, st_new = ssd_layer(hs, state_conv[l], state_ssm[l], *w)
            conv_s.append(cv_new)
            ssm_s.append(st_new)
        xp = xp + yp
        xs = xs + y_s
        xp = xp + swiglu(rms_norm(xp, norm_ffn[i]), ffn_w_gate[i], ffn_w_up[i], ffn_w_down[i])
        xs = xs + swiglu(rms_norm(xs, norm_ffn[i]), ffn_w_gate[i], ffn_w_up[i], ffn_w_down[i])
    return (xp, xs,
            jnp.stack(cmp_p), jnp.stack(sel_p), jnp.stack(win_p), jnp.stack(conv_p), jnp.stack(ssm_p),
            jnp.stack(cmp_s), jnp.stack(sel_s), jnp.stack(win_s), jnp.stack(conv_s), jnp.stack(ssm_s))
```

```python
import functools
import math

import jax
import jax.numpy as jnp
import numpy as np
from jax import lax
from jax.experimental import pallas as pl
from jax.experimental.pallas import tpu as pltpu

F32 = jnp.float32
BF16 = jnp.bfloat16
HIGHEST = lax.Precision.HIGHEST

D_MODEL = 1024
PAGE_SIZE = 128
NSA_HEADS = 16
HEAD_DIM = 64
KV_GROUPS = 4
GROUP_SIZE = NSA_HEADS // KV_GROUPS
CMP_BLOCK = 32
CMP_STRIDE = 16
CMP_HIDDEN = 256
SEL_BLOCK = 64
SEL_CHUNKS = SEL_BLOCK // CMP_STRIDE
SEL_TOPN = 16
WINDOW = 512
NSA_Q_DIM = NSA_HEADS * HEAD_DIM
NSA_KV_DIM = 2 * KV_GROUPS * HEAD_DIM
KV_HALF = KV_GROUPS * HEAD_DIM
SSM_D_INNER = 2048
SSM_HEAD_DIM = 64
SSM_HEADS = 32
SSM_GROUPS = 4
SSM_HPG = SSM_HEADS // SSM_GROUPS
SSM_STATE = 128
SSM_CONV = 4
SSM_BC = 2 * SSM_GROUPS * SSM_STATE
SSM_GW = SSM_D_INNER // SSM_GROUPS
SSM_CHUNK = 128
NORM_EPS = 1e-6
NEG_INF = -1e30
FORCE_SCORE = 1e4
LANES = 128
VMEM_LIMIT = 56 * 1024 * 1024

_SLOPES = [float(np.float32((2.0 ** (-8.0 / NSA_HEADS)) ** (i + 1))) for i in range(NSA_HEADS)]


def _cparams(sem):
    return pltpu.CompilerParams(dimension_semantics=sem, vmem_limit_bytes=VMEM_LIMIT)


def _nt(a, b, precision=None):
    return lax.dot_general(a, b, (((1,), (1,)), ((), ())), preferred_element_type=F32, precision=precision)


def _dot(a, b, precision=None):
    return jnp.dot(a, b, preferred_element_type=F32, precision=precision)


def _rms(x, gain):
    return x * lax.rsqrt(jnp.mean(x * x, axis=-1, keepdims=True) + NORM_EPS) * gain


def _norm_matmul_kernel(x_ref, g_ref, w_ref, hg_ref, hm_ref, e_ref, et_ref, o_ref, xn_ref, *, head_norm):
    @pl.when(pl.program_id(1) == 0)
    def _():
        xn_ref[...] = _rms(x_ref[...], g_ref[...]).astype(BF16)

    y = _dot(xn_ref[...], w_ref[...])
    if head_norm:
        ss = _dot(y * y, e_ref[...], HIGHEST)
        inv = lax.rsqrt(ss * (1.0 / HEAD_DIM) + NORM_EPS)
        yn = y * _dot(inv, et_ref[...], HIGHEST) * hg_ref[...]
        y = jnp.where(hm_ref[...] > 0.5, yn, y)
    o_ref[...] = y


def norm_matmul(x, gain, w, head_gain=None, head_mask=None, *, tm, tn):
    m, k = x.shape
    n = w.shape[1]
    head_norm = head_gain is not None
    if not head_norm:
        head_gain = jnp.zeros((1, n), F32)
        head_mask = jnp.zeros((1, n), F32)
    lane = np.arange(tn)
    e = np.zeros((tn, LANES), np.float32)
    e[lane, lane // HEAD_DIM] = 1.0
    return pl.pallas_call(
        functools.partial(_norm_matmul_kernel, head_norm=head_norm),
        grid=(m // tm, n // tn),
        in_specs=[
            pl.BlockSpec((tm, k), lambda i, j: (i, 0)),
            pl.BlockSpec((1, k), lambda i, j: (0, 0)),
            pl.BlockSpec((k, tn), lambda i, j: (0, j)),
            pl.BlockSpec((1, tn), lambda i, j: (0, j)),
            pl.BlockSpec((1, tn), lambda i, j: (0, j)),
            pl.BlockSpec((tn, LANES), lambda i, j: (0, 0)),
            pl.BlockSpec((LANES, tn), lambda i, j: (0, 0)),
        ],
        out_specs=pl.BlockSpec((tm, tn), lambda i, j: (i, j)),
        out_shape=jax.ShapeDtypeStruct((m, n), F32),
        scratch_shapes=[pltpu.VMEM((tm, k), BF16)],
        compiler_params=_cparams(("parallel", "arbitrary")),
        name="norm_matmul",
    )(x, gain.reshape(1, k), w, head_gain, head_mask, jnp.asarray(e), jnp.asarray(e.T))


def _ffn_kernel(x_ref, g_ref, wg_ref, wu_ref, wd_ref, o_ref, xn_ref):
    @pl.when(pl.program_id(1) == 0)
    def _():
        x = x_ref[...]
        xn_ref[...] = _rms(x, g_ref[...]).astype(BF16)
        o_ref[...] = x

    h = xn_ref[...]
    gate = _dot(h, wg_ref[...])
    up = _dot(h, wu_ref[...])
    act = (gate * jax.nn.sigmoid(gate) * up).astype(BF16)
    o_ref[...] += _dot(act, wd_ref[...])


def ffn(x, gain, wg, wu, wd, *, tm, th):
    m, k = x.shape
    hid = wg.shape[1]
    return pl.pallas_call(
        _ffn_kernel,
        grid=(m // tm, hid // th),
        in_specs=[
            pl.BlockSpec((tm, k), lambda i, j: (i, 0)),
            pl.BlockSpec((1, k), lambda i, j: (0, 0)),
            pl.BlockSpec((k, th), lambda i, j: (0, j)),
            pl.BlockSpec((k, th), lambda i, j: (0, j)),
            pl.BlockSpec((th, k), lambda i, j: (j, 0)),
        ],
        out_specs=pl.BlockSpec((tm, k), lambda i, j: (i, 0)),
        out_shape=jax.ShapeDtypeStruct((m, k), F32),
        scratch_shapes=[pltpu.VMEM((tm, k), BF16)],
        compiler_params=_cparams(("parallel", "arbitrary")),
        name="ffn",
    )(x, gain.reshape(1, k), wg, wu, wd)


def _matmul_res_kernel(a_ref, w_ref, r_ref, o_ref):
    o_ref[...] = r_ref[...] + _dot(a_ref[...], w_ref[...])


def matmul_residual(a, w, res, *, tm):
    m, k = a.shape
    n = w.shape[1]
    return pl.pallas_call(
        _matmul_res_kernel,
        grid=(m // tm,),
        in_specs=[
            pl.BlockSpec((tm, k), lambda i: (i, 0)),
            pl.BlockSpec((k, n), lambda i: (0, 0)),
            pl.BlockSpec((tm, n), lambda i: (i, 0)),
        ],
        out_specs=pl.BlockSpec((tm, n), lambda i: (i, 0)),
        out_shape=jax.ShapeDtypeStruct((m, n), F32),
        compiler_params=_cparams(("parallel",)),
        name="matmul_residual",
    )(a, w, res)


def _compress_accumulate(read_rows, pe_ref, w1_ref, kv):
    acc = [None, None]
    for j in range(CMP_STRIDE):
        xs = read_rows(j)
        for m in range(2):
            lhs = (xs + pe_ref[kv, m, j:j + 1, :]).astype(BF16)
            part = _dot(lhs, w1_ref[kv, m, j])
            acc[m] = part if acc[m] is None else acc[m] + part
    return acc


def _compress_finish(acc, w2_ref, kg_ref, kv):
    rows = acc[0].shape[0]
    pre = acc[0] + pltpu.roll(acc[1], rows - 1, axis=0)
    hid = (pre * jax.nn.sigmoid(pre)).astype(BF16)
    out = _dot(hid, w2_ref[kv])
    if kv == 0:
        out = _rms(out, kg_ref[...])
    return out


def _compress_rows_kernel(x_ref, pe_ref, w1_ref, w2_ref, kg_ref, o_ref, xs_ref, *, n_chunk):
    n_pair = NSA_KV_DIM // LANES
    for c in range(n_pair):
        xs_ref[c] = x_ref[:, c * LANES:(c + 1) * LANES]
    for kv in range(2):
        def read_rows(j, kv=kv):
            parts = []
            for c in range(kv * n_pair // 2, (kv + 1) * n_pair // 2):
                pair = xs_ref.at[c][pl.ds(j, n_chunk, stride=CMP_STRIDE), :]
                parts += [pair[:, :HEAD_DIM], pair[:, HEAD_DIM:]]
            return jnp.concatenate(parts, axis=0)
        acc = _compress_accumulate(read_rows, pe_ref, w1_ref, kv)
        o_ref[kv] = _compress_finish(acc, w2_ref, kg_ref, kv)


def compress_rows(proj, col_block, pe, w1, w2, kgain, *, batch, t):
    n_chunk = t // CMP_STRIDE
    return pl.pallas_call(
        functools.partial(_compress_rows_kernel, n_chunk=n_chunk),
        grid=(batch,),
        in_specs=[
            pl.BlockSpec((t, NSA_KV_DIM), lambda b: (b, col_block)),
            pl.BlockSpec(pe.shape, lambda b: (0, 0, 0, 0)),
            pl.BlockSpec(w1.shape, lambda b: (0, 0, 0, 0, 0)),
            pl.BlockSpec(w2.shape, lambda b: (0, 0, 0)),
            pl.BlockSpec((1, HEAD_DIM), lambda b: (0, 0)),
        ],
        out_specs=pl.BlockSpec((None, 2, KV_GROUPS * n_chunk, HEAD_DIM), lambda b: (b, 0, 0, 0)),
        out_shape=jax.ShapeDtypeStruct((batch, 2, KV_GROUPS * n_chunk, HEAD_DIM), F32),
        scratch_shapes=[pltpu.VMEM((NSA_KV_DIM // LANES, t, LANES), F32)],
        compiler_params=_cparams(("parallel",)),
        name="compress_rows",
    )(proj, pe, w1, w2, kgain.reshape(1, HEAD_DIM))


def _slope_col(g, rows, tq):
    rblk = lax.broadcasted_iota(jnp.int32, (rows, 1), 0) // tq
    col = jnp.full((rows, 1), _SLOPES[GROUP_SIZE * g + GROUP_SIZE - 1], F32)
    for r in range(GROUP_SIZE - 1):
        col = jnp.where(rblk == r, _SLOPES[GROUP_SIZE * g + r], col)
    return col


def _softmax_tile(s, mask, m, l, acc, v):
    s = jnp.where(mask, s, NEG_INF)
    m_new = jnp.maximum(m, jnp.max(s, axis=1, keepdims=True))
    alpha = jnp.exp(m - m_new)
    p = jnp.where(mask, jnp.exp(s - m_new), 0.0)
    l_new = alpha * l + jnp.sum(p, axis=1, keepdims=True)
    acc_new = alpha * acc + _dot(p.astype(BF16), v)
    return m_new, l_new, acc_new


def _normalize(acc, l):
    return acc * jnp.where(l > 0.0, 1.0 / jnp.where(l > 0.0, l, 1.0), 0.0)


def _nsa_prompt_kernel(q_ref, ks_ref, kw_ref, cmp_ref, gate_ref, mmap_ref, esel_ref, eye_ref, o_ref,
                       ksb_ref, vsb_ref, kwb_ref, vwb_ref, selx_ref, *, t, tq, tk):
    qi = pl.program_id(1)
    n_sel = t // SEL_BLOCK
    n_cmp = t // CMP_STRIDE
    top_n = min(SEL_TOPN, n_sel)
    rows = GROUP_SIZE * tq

    @pl.when(qi == 0)
    def _():
        for g in range(KV_GROUPS):
            lo, hi = g * HEAD_DIM, (g + 1) * HEAD_DIM
            ksb_ref[g] = ks_ref[:, lo:hi].astype(BF16)
            vsb_ref[g] = ks_ref[:, KV_HALF + lo:KV_HALF + hi].astype(BF16)
            kwb_ref[g] = kw_ref[:, lo:hi].astype(BF16)
            vwb_ref[g] = kw_ref[:, KV_HALF + lo:KV_HALF + hi].astype(BF16)

    q0 = qi * tq
    t_max = q0 + (tq - 1)
    q_all = (q_ref[...] * (HEAD_DIM ** -0.5)).astype(BF16)
    gates = jax.nn.sigmoid(gate_ref[...])
    row = lax.broadcasted_iota(jnp.int32, (rows, 1), 0)
    t_col = q0 + row % tq
    jblk = lax.broadcasted_iota(jnp.int32, (n_sel, tq), 0)
    t_lane = q0 + lax.broadcasted_iota(jnp.int32, (n_sel, tq), 1)
    cur = t_lane // SEL_BLOCK
    forced = (jblk == 0) | (jblk == cur) | (jblk == cur - 1)
    visible = jblk * SEL_BLOCK <= t_lane

    for g in range(KV_GROUPS):
        qg = jnp.concatenate(
            [q_all[:, (GROUP_SIZE * g + r) * HEAD_DIM:(GROUP_SIZE * g + r + 1) * HEAD_DIM] for r in range(GROUP_SIZE)],
            axis=0)
        slope = _slope_col(g, rows, tq)

        kc = cmp_ref[0, g * n_cmp:(g + 1) * n_cmp, :].astype(BF16)
        vc = cmp_ref[1, g * n_cmp:(g + 1) * n_cmp, :].astype(BF16)
        kpos = lax.broadcasted_iota(jnp.int32, (1, n_cmp), 1) * CMP_STRIDE + (CMP_BLOCK - 1)
        s = _nt(qg, kc) + slope * (kpos - t_max).astype(F32)
        mask = kpos <= t_col
        s = jnp.where(mask, s, NEG_INF)
        p = jnp.where(mask, jnp.exp(s - jnp.max(s, axis=1, keepdims=True)), 0.0)
        l = jnp.sum(p, axis=1, keepdims=True)
        pc = _normalize(p, l)
        o_c = _dot(pc.astype(BF16), vc)
        imp = pc[0:tq]
        for r in range(1, GROUP_SIZE):
            imp = imp + pc[r * tq:(r + 1) * tq]

        p_slc = _nt(mmap_ref[...], imp, HIGHEST)
        score = jnp.where(forced, FORCE_SCORE, jnp.where(visible, p_slc, -1.0))
        cnt = jnp.zeros((n_sel, tq), jnp.int32)
        for k in range(n_sel):
            sk = score[k:k + 1, :]
            beats = (sk > score) | ((sk == score) & (jblk > k))
            cnt = cnt + beats.astype(jnp.int32)
        sel_t = (cnt < top_n).astype(BF16)
        sel = _nt(eye_ref[...], sel_t).astype(BF16)
        for kt in range(t // tk):
            selx_ref[kt] = _dot(sel, esel_ref[:, kt * tk:(kt + 1) * tk])

        def sel_body(kt, carry, g=g, qg=qg, slope=slope):
            k0 = pl.multiple_of(kt * tk, tk)
            kpos = k0 + lax.broadcasted_iota(jnp.int32, (1, tk), 1)
            s = _nt(qg, ksb_ref[g, pl.ds(k0, tk), :]) + slope * (kpos - t_max).astype(F32)
            chosen = jnp.concatenate([selx_ref[kt]] * GROUP_SIZE, axis=0) > 0.5
            return _softmax_tile(s, chosen & (kpos <= t_col), *carry, vsb_ref[g, pl.ds(k0, tk), :])

        init = (jnp.full((rows, 1), NEG_INF, F32), jnp.zeros((rows, 1), F32), jnp.zeros((rows, HEAD_DIM), F32))
        _, l, acc = lax.fori_loop(0, (q0 + tq + tk - 1) // tk, sel_body, init)
        o_s = _normalize(acc, l)

        def win_body(i, carry, g=g, qg=qg, slope=slope):
            k0 = pl.multiple_of(q0 - WINDOW + i * tq, tq)
            kpos = k0 + lax.broadcasted_iota(jnp.int32, (1, tq), 1)
            s = _nt(qg, kwb_ref[g, pl.ds(k0, tq), :]) + slope * (kpos - t_max).astype(F32)
            dist = t_col - kpos
            return _softmax_tile(s, (dist >= 0) & (dist < WINDOW), *carry, vwb_ref[g, pl.ds(k0, tq), :])

        _, l, acc = lax.fori_loop(jnp.maximum(0, WINDOW // tq - qi), WINDOW // tq + 1, win_body, init)
        o_w = _normalize(acc, l)

        for r in range(GROUP_SIZE):
            h = GROUP_SIZE * g + r
            sl = slice(r * tq, (r + 1) * tq)
            out = (gates[:, h:h + 1] * o_c[sl] + gates[:, NSA_HEADS + h:NSA_HEADS + h + 1] * o_s[sl]
                   + gates[:, 2 * NSA_HEADS + h:2 * NSA_HEADS + h + 1] * o_w[sl])
            o_ref[:, h * HEAD_DIM:(h + 1) * HEAD_DIM] = out.astype(o_ref.dtype)


def nsa_prompt_attention(proj, gates, cmp, *, batch, t, tq, tk):
    n_sel = t // SEL_BLOCK
    n_cmp = t // CMP_STRIDE
    nq = t // tq
    n = np.arange(n_cmp)
    mmap = np.zeros((n_sel, n_cmp), np.float32)
    real = n < n_cmp - 1
    np.add.at(mmap, (n[real] // SEL_CHUNKS, n[real]), 1.0)
    nxt = real & ((n + 1) // SEL_CHUNKS < n_sel)
    np.add.at(mmap, ((n[nxt] + 1) // SEL_CHUNKS, n[nxt]), 1.0)
    esel = (np.arange(t)[None, :] // SEL_BLOCK == np.arange(n_sel)[:, None]).astype(np.float32)
    kv_scratch = pltpu.VMEM((KV_GROUPS, t, HEAD_DIM), BF16)
    return pl.pallas_call(
        functools.partial(_nsa_prompt_kernel, t=t, tq=tq, tk=tk),
        grid=(batch, nq),
        in_specs=[
            pl.BlockSpec((tq, NSA_Q_DIM), lambda b, i: (b * nq + i, 0)),
            pl.BlockSpec((t, NSA_KV_DIM), lambda b, i: (b, 3)),
            pl.BlockSpec((t, NSA_KV_DIM), lambda b, i: (b, 4)),
            pl.BlockSpec((None, 2, KV_GROUPS * n_cmp, HEAD_DIM), lambda b, i: (b, 0, 0, 0)),
            pl.BlockSpec((tq, LANES), lambda b, i: (b * nq + i, 0)),
            pl.BlockSpec((n_sel, n_cmp), lambda b, i: (0, 0)),
            pl.BlockSpec((n_sel, t), lambda b, i: (0, 0)),
            pl.BlockSpec((tq, tq), lambda b, i: (0, 0)),
        ],
        out_specs=pl.BlockSpec((tq, NSA_Q_DIM), lambda b, i: (b * nq + i, 0)),
        out_shape=jax.ShapeDtypeStruct((batch * t, NSA_Q_DIM), BF16),
        scratch_shapes=[kv_scratch, kv_scratch, kv_scratch, kv_scratch, pltpu.VMEM((t // tk, tq, tk), F32)],
        compiler_params=_cparams(("parallel", "arbitrary")),
        name="nsa_prompt_attention",
    )(proj, proj, proj, cmp, gates, jnp.asarray(mmap), jnp.asarray(esel, BF16), jnp.eye(tq, dtype=BF16))


def _nsa_weights(w_in, qk_gain, cmp_pe, cmp_w1, cmp_w2, w_out):
    n_main = NSA_Q_DIM + 3 * NSA_KV_DIM
    zeros_kv = jnp.zeros((KV_HALF,), F32)
    ones_kv = jnp.ones((KV_HALF,), F32)
    head_gain = jnp.concatenate([
        jnp.tile(qk_gain[0], NSA_HEADS), zeros_kv, zeros_kv,
        jnp.tile(qk_gain[2], KV_GROUPS), zeros_kv, jnp.tile(qk_gain[3], KV_GROUPS), zeros_kv]).reshape(1, n_main)
    head_mask = jnp.concatenate([
        jnp.ones((NSA_Q_DIM,), F32), zeros_kv, zeros_kv, ones_kv, zeros_kv, ones_kv, zeros_kv]).reshape(1, n_main)
    w_gate = jnp.pad(w_in[:, n_main:], ((0, 0), (0, LANES - 3 * NSA_HEADS)))
    return dict(
        w_main=w_in[:, :n_main].astype(BF16), w_gate=w_gate.astype(BF16), head_gain=head_gain, head_mask=head_mask,
        pe=cmp_pe.reshape(2, 2, CMP_STRIDE, HEAD_DIM),
        w1=cmp_w1.reshape(2, 2, CMP_STRIDE, HEAD_DIM, CMP_HIDDEN).astype(BF16),
        w2=cmp_w2.astype(BF16), kgain=qk_gain[1], w_out=w_out.astype(BF16))


def _nsa_project(x, gain, w, *, tm):
    proj = norm_matmul(x, gain, w["w_main"], w["head_gain"], w["head_mask"], tm=tm, tn=512)
    gates = norm_matmul(x, gain, w["w_gate"], tm=tm, tn=LANES)
    return proj, gates


def nsa_prompt_layer(x, gain, w, *, batch, t):
    proj, gates = _nsa_project(x, gain, w, tm=min(512, batch * t))
    cmp = compress_rows(proj, 2, w["pe"], w["w1"], w["w2"], w["kgain"], batch=batch, t=t)
    o = nsa_prompt_attention(proj, gates, cmp, batch=batch, t=t, tq=128, tk=min(256, t))
    return matmul_residual(o, w["w_out"], x, tm=min(512, batch * t)), proj


PAGES_PER_STEP = 8
CHUNKS_PER_PAGE = PAGE_SIZE // CMP_STRIDE


def _compress_pages_kernel(pt_ref, *refs, n_chunk):
    del pt_ref
    pages = refs[:PAGES_PER_STEP]
    pe_ref, w1_ref, w2_ref, kg_ref, o_ref, xs_ref, pre_ref = refs[PAGES_PER_STEP:]
    s = pl.program_id(1)
    n_pair = NSA_KV_DIM // LANES
    step_chunks = PAGES_PER_STEP * CHUNKS_PER_PAGE
    for i, page in enumerate(pages):
        for c in range(n_pair):
            xs_ref[c, i * PAGE_SIZE:(i + 1) * PAGE_SIZE, :] = jnp.transpose(page[c * LANES:(c + 1) * LANES, :])
    for kv in range(2):
        def read_rows(j, kv=kv):
            parts = []
            for c in range(kv * n_pair // 2, (kv + 1) * n_pair // 2):
                pair = xs_ref.at[c][pl.ds(j, step_chunks, stride=CMP_STRIDE), :]
                parts += [pair[:, :HEAD_DIM], pair[:, HEAD_DIM:]]
            return jnp.concatenate(parts, axis=0)
        acc = _compress_accumulate(read_rows, pe_ref, w1_ref, kv)
        for m in range(2):
            for g in range(KV_GROUPS):
                pre_ref[kv, m, pl.ds(pl.multiple_of(g * n_chunk + s * step_chunks, step_chunks), step_chunks), :] = (
                    acc[m][g * step_chunks:(g + 1) * step_chunks])

    @pl.when(s == pl.num_programs(1) - 1)
    def _():
        for kv in range(2):
            o_ref[kv] = _compress_finish([pre_ref[kv, 0], pre_ref[kv, 1]], w2_ref, kg_ref, kv)


def compress_pages(cache_t, layer, page_table_flat, pe, w1, w2, kgain, *, batch, n_pages):
    n_chunk = n_pages * CHUNKS_PER_PAGE
    n_steps = n_pages // PAGES_PER_STEP

    def page_spec(i):
        return pl.BlockSpec((None, None, NSA_KV_DIM, PAGE_SIZE),
                            lambda b, s, pt: (layer, pt[b * n_pages + s * PAGES_PER_STEP + i], 0, 0))

    grid_spec = pltpu.PrefetchScalarGridSpec(
        num_scalar_prefetch=1,
        grid=(batch, n_steps),
        in_specs=[page_spec(i) for i in range(PAGES_PER_STEP)] + [
            pl.BlockSpec(pe.shape, lambda b, s, pt: (0, 0, 0, 0)),
            pl.BlockSpec(w1.shape, lambda b, s, pt: (0, 0, 0, 0, 0)),
            pl.BlockSpec(w2.shape, lambda b, s, pt: (0, 0, 0)),
            pl.BlockSpec((1, HEAD_DIM), lambda b, s, pt: (0, 0)),
        ],
        out_specs=pl.BlockSpec((None, 2, KV_GROUPS * n_chunk, HEAD_DIM), lambda b, s, pt: (b, 0, 0, 0)),
        scratch_shapes=[pltpu.VMEM((NSA_KV_DIM // LANES, PAGES_PER_STEP * PAGE_SIZE, LANES), F32),
                        pltpu.VMEM((2, 2, KV_GROUPS * n_chunk, CMP_HIDDEN), F32)],
    )
    return pl.pallas_call(
        functools.partial(_compress_pages_kernel, n_chunk=n_chunk),
        grid_spec=grid_spec,
        out_shape=jax.ShapeDtypeStruct((batch, 2, KV_GROUPS * n_chunk, HEAD_DIM), F32),
        compiler_params=_cparams(("parallel", "arbitrary")),
        name="compress_pages",
    )(page_table_flat, *([cache_t] * PAGES_PER_STEP), pe, w1, w2, kgain.reshape(1, HEAD_DIM))


Q_PAD = 8
HEAD_PAD = 16


def _nsa_sample_select_kernel(q_ref, cmp_ref, mmap_ref, oc_ref, idx_ref, score_ref, *, past_len, n_cmp, n_sel):
    rows = GROUP_SIZE * Q_PAD
    n_selp = score_ref.shape[0]
    t_max = past_len + Q_PAD - 1
    row = lax.broadcasted_iota(jnp.int32, (rows, 1), 0)
    t_col = past_len + row % Q_PAD
    jblk = lax.broadcasted_iota(jnp.int32, (n_selp, Q_PAD), 0)
    t_lane = past_len + lax.broadcasted_iota(jnp.int32, (n_selp, Q_PAD), 1)
    cur = t_lane // SEL_BLOCK
    forced = (jblk == 0) | (jblk == cur) | (jblk == cur - 1)
    visible = jblk * SEL_BLOCK <= t_lane
    for g in range(KV_GROUPS):
        qg = (q_ref[g] * (HEAD_DIM ** -0.5)).astype(BF16)
        slope = _slope_col(g, rows, Q_PAD)
        kc = cmp_ref[0, g * n_cmp:(g + 1) * n_cmp, :].astype(BF16)
        vc = cmp_ref[1, g * n_cmp:(g + 1) * n_cmp, :].astype(BF16)
        kpos = lax.broadcasted_iota(jnp.int32, (1, n_cmp), 1) * CMP_STRIDE + (CMP_BLOCK - 1)
        s = _nt(qg, kc) + slope * (kpos - t_max).astype(F32)
        mask = kpos <= t_col
        s = jnp.where(mask, s, NEG_INF)
        p = jnp.where(mask, jnp.exp(s - jnp.max(s, axis=1, keepdims=True)), 0.0)
        pc = _normalize(p, jnp.sum(p, axis=1, keepdims=True))
        oc_ref[g] = _dot(pc.astype(BF16), vc)
        imp = pc[0:Q_PAD]
        for r in range(1, GROUP_SIZE):
            imp = imp + pc[r * Q_PAD:(r + 1) * Q_PAD]
        p_slc = _nt(mmap_ref[...], imp, HIGHEST)
        score = jnp.where(forced, FORCE_SCORE, jnp.where(visible, p_slc, -1.0))
        score = jnp.where(jblk < n_sel, score, -2.0)
        score_ref[...] = score

        def rank_body(k, cnt, score=score):
            sk = score_ref[pl.ds(k, 1), :]
            beats = (sk > score) | ((sk == score) & (jblk > k))
            return cnt + beats.astype(jnp.int32)

        cnt = lax.fori_loop(0, n_sel, rank_body, jnp.zeros((n_selp, Q_PAD), jnp.int32))
        slots = [jnp.sum(jnp.where(cnt == sl, jblk, 0), axis=0, keepdims=True) for sl in range(SEL_TOPN)]
        idx_ref[g] = jnp.concatenate(slots, axis=0)


def nsa_sample_select(q_stack, cmp, *, batch, past_len, t_new):
    n_cmp = past_len // CMP_STRIDE
    n_sel = -(-(past_len + t_new) // SEL_BLOCK)
    n_selp = -(-n_sel // 8) * 8
    n = np.arange(n_cmp)
    mmap = np.zeros((n_selp, n_cmp), np.float32)
    real = n < n_cmp - 1
    np.add.at(mmap, (n[real] // SEL_CHUNKS, n[real]), 1.0)
    np.add.at(mmap, ((n[real] + 1) // SEL_CHUNKS, n[real]), 1.0)
    rows = GROUP_SIZE * Q_PAD
    return pl.pallas_call(
        functools.partial(_nsa_sample_select_kernel, past_len=past_len, n_cmp=n_cmp, n_sel=n_sel),
        grid=(batch,),
        in_specs=[
            pl.BlockSpec((None, KV_GROUPS, rows, HEAD_DIM), lambda b: (b, 0, 0, 0)),
            pl.BlockSpec((None, 2, KV_GROUPS * n_cmp, HEAD_DIM), lambda b: (b, 0, 0, 0)),
            pl.BlockSpec((n_selp, n_cmp), lambda b: (0, 0)),
        ],
        out_specs=[
            pl.BlockSpec((None, KV_GROUPS, rows, HEAD_DIM), lambda b: (b, 0, 0, 0)),
            pl.BlockSpec((None, KV_GROUPS, SEL_TOPN, Q_PAD), lambda b: (b, 0, 0, 0)),
        ],
        out_shape=[jax.ShapeDtypeStruct((batch, KV_GROUPS, rows, HEAD_DIM), F32),
                   jax.ShapeDtypeStruct((batch, KV_GROUPS, SEL_TOPN, Q_PAD), jnp.int32)],
        scratch_shapes=[pltpu.VMEM((n_selp, Q_PAD), F32)],
        compiler_params=_cparams(("parallel",)),
        name="nsa_sample_select",
    )(q_stack, cmp, jnp.asarray(mmap))


def _nsa_sample_attend_kernel(pt_ref, idx_ref, *refs, past_len, t_new, n_past_blocks):
    del pt_ref
    blocks = refs[:SEL_TOPN]
    (q_ref, slope_ref, snew_ref, win_ref, wnew_ref, wnew_t_ref, oc_ref, gate_ref,
     o_ref, wout_ref) = refs[SEL_TOPN:]
    b, g, qi = pl.program_id(0), pl.program_id(1), pl.program_id(2)
    t = past_len + qi
    q = (q_ref[...] * (HEAD_DIM ** -0.5)).astype(BF16)
    slope = slope_ref[:, 0:1]
    lane = lax.broadcasted_iota(jnp.int32, (1, LANES), 1)
    new_pos = past_len + lane
    new_ok = (lane < t_new) & (new_pos <= t)

    def attend(tiles, v_new):
        m = jnp.full((HEAD_PAD, 1), NEG_INF, F32)
        for s, mask, _ in tiles:
            m = jnp.maximum(m, jnp.max(jnp.where(mask, s, NEG_INF), axis=1, keepdims=True))
        l = jnp.zeros((HEAD_PAD, 1), F32)
        acc = jnp.zeros((HEAD_PAD, HEAD_DIM), F32)
        for i, (s, mask, v) in enumerate(tiles):
            p = jnp.where(mask, jnp.exp(jnp.where(mask, s, NEG_INF) - m), 0.0)
            l = l + jnp.sum(p, axis=1, keepdims=True)
            if i == len(tiles) - 1:
                acc = acc + _dot(p.astype(BF16), v_new)
            else:
                acc = acc + _nt(p.astype(BF16), v)
        return _normalize(acc, l)

    base = ((b * KV_GROUPS + g) * t_new + qi) * SEL_TOPN
    tiles = []
    for sl, blk_ref in enumerate(blocks):
        blk = idx_ref[base + sl]
        kpos = (blk // 2) * PAGE_SIZE + lane
        mask = (lane // SEL_BLOCK == blk % 2) & (blk < n_past_blocks) & (kpos <= t)
        s = _dot(q, blk_ref[0].astype(BF16)) + slope * (kpos - t).astype(F32)
        tiles.append((s, mask, blk_ref[1].astype(BF16)))
    s_new = _nt(q, snew_ref[0].astype(BF16)) + slope * (new_pos - t).astype(F32)
    tiles.append((s_new, new_ok, None))
    o_s = attend(tiles, snew_ref[1].astype(BF16))

    buf = win_ref.shape[2]
    wlane = lax.broadcasted_iota(jnp.int32, (1, buf), 1)
    wpos = past_len - buf + wlane
    dist = t - wpos
    s_w = _dot(q, win_ref[0].astype(BF16)) + slope * (wpos - t).astype(F32)
    s_wn = _nt(q, wnew_ref[0].astype(BF16)) + slope * (new_pos - t).astype(F32)
    o_w = attend([(s_w, (dist >= 0) & (dist < WINDOW), win_ref[1].astype(BF16)),
                  (s_wn, new_ok & (t - new_pos < WINDOW), None)], wnew_ref[1].astype(BF16))

    gates = jax.nn.sigmoid(gate_ref[...])
    o_ref[...] = gates[:, 0:1] * oc_ref[...] + gates[:, 1:2] * o_s + gates[:, 2:3] * o_w

    @pl.when(qi == 0)
    def _():
        for kv in range(2):
            rolled = pltpu.roll(win_ref[kv], buf - t_new, axis=1)
            wout_ref[kv, :, 0:buf - LANES] = rolled[:, 0:buf - LANES]
            wout_ref[kv, :, buf - LANES:] = jnp.where(lane >= LANES - t_new, wnew_t_ref[kv], rolled[:, buf - LANES:])


def nsa_sample_attend(cache_t, win_t, layer, page_table_flat, idx_flat, q_heads, slopes, sel_new, win_new,
                      win_new_t, o_c, gates, *, batch, past_len, t_new):
    n_pages = past_len // PAGE_SIZE
    n_past_blocks = past_len // SEL_BLOCK
    buf = win_t.shape[-1]

    def block_spec(sl):
        def index(b, g, q, pt, idx):
            blk = idx[((b * KV_GROUPS + g) * t_new + q) * SEL_TOPN + sl]
            page = pt[b * n_pages + jnp.minimum(blk, n_past_blocks - 1) // 2]
            return (layer, page, 0, g, 0, 0)
        return pl.BlockSpec((None, None, 2, None, HEAD_DIM, PAGE_SIZE), index)

    per_bg = lambda shape: pl.BlockSpec((None, None) + shape, lambda b, g, q, pt, idx: (b, g) + (0,) * len(shape))
    per_bgq = lambda shape: pl.BlockSpec((None, None, None) + shape,
                                         lambda b, g, q, pt, idx: (b, g, q) + (0,) * len(shape))
    grid_spec = pltpu.PrefetchScalarGridSpec(
        num_scalar_prefetch=2,
        grid=(batch, KV_GROUPS, t_new),
        in_specs=[block_spec(sl) for sl in range(SEL_TOPN)] + [
            per_bgq((HEAD_PAD, HEAD_DIM)),
            pl.BlockSpec((None, HEAD_PAD, LANES), lambda b, g, q, pt, idx: (g, 0, 0)),
            per_bg((2, LANES, HEAD_DIM)),
            pl.BlockSpec((None, None, 2, None, HEAD_DIM, buf), lambda b, g, q, pt, idx: (layer, b, 0, g, 0, 0)),
            per_bg((2, LANES, HEAD_DIM)),
            per_bg((2, HEAD_DIM, LANES)),
            per_bgq((HEAD_PAD, HEAD_DIM)),
            per_bgq((HEAD_PAD, LANES)),
        ],
        out_specs=[
            per_bgq((HEAD_PAD, HEAD_DIM)),
            pl.BlockSpec((None, 2, None, HEAD_DIM, buf), lambda b, g, q, pt, idx: (b, 0, g, 0, 0)),
        ],
    )
    return pl.pallas_call(
        functools.partial(_nsa_sample_attend_kernel, past_len=past_len, t_new=t_new, n_past_blocks=n_past_blocks),
        grid_spec=grid_spec,
        out_shape=[jax.ShapeDtypeStruct((batch, KV_GROUPS, t_new, HEAD_PAD, HEAD_DIM), F32),
                   jax.ShapeDtypeStruct((batch, 2, KV_GROUPS, HEAD_DIM, buf), F32)],
        compiler_params=_cparams(("parallel", "arbitrary", "arbitrary")),
        name="nsa_sample_attend",
    )(page_table_flat, idx_flat, *([cache_t] * SEL_TOPN), q_heads, slopes, sel_new, win_t, win_new,
      win_new_t, o_c, gates)


def nsa_sample_layer(x, gain, w, cmp_t, sel_t, win_t, layer, page_table_flat, *, batch, t_new, past_len):
    m = batch * t_new
    proj, gates = _nsa_project(x, gain, w, tm=m)
    cmp = compress_pages(cmp_t.reshape(cmp_t.shape[0], cmp_t.shape[1], NSA_KV_DIM, PAGE_SIZE), layer,
                         page_table_flat, w["pe"], w["w1"], w["w2"], w["kgain"], batch=batch,
                         n_pages=past_len // PAGE_SIZE)
    q = proj[:, :NSA_Q_DIM].reshape(batch, t_new, KV_GROUPS, GROUP_SIZE, HEAD_DIM)
    q_stack = jnp.pad(q.transpose(0, 2, 3, 1, 4), ((0, 0), (0, 0), (0, 0), (0, Q_PAD - t_new), (0, 0)))
    o_c, idx = nsa_sample_select(q_stack.reshape(batch, KV_GROUPS, GROUP_SIZE * Q_PAD, HEAD_DIM), cmp,
                                 batch=batch, past_len=past_len, t_new=t_new)
    pad_heads = ((0, 0), (0, 0), (0, 0), (0, HEAD_PAD - GROUP_SIZE), (0, 0))
    o_c = o_c.reshape(batch, KV_GROUPS, GROUP_SIZE, Q_PAD, HEAD_DIM)[:, :, :, :t_new].transpose(0, 1, 3, 2, 4)
    idx_flat = idx[..., :t_new].transpose(0, 1, 3, 2).reshape(-1)
    q_heads = jnp.pad(q.transpose(0, 2, 1, 3, 4), pad_heads)

    def new_rows(cols):
        a = cols.reshape(batch, t_new, 2, KV_GROUPS, HEAD_DIM).transpose(0, 3, 2, 1, 4)
        return jnp.pad(a, ((0, 0), (0, 0), (0, 0), (0, LANES - t_new), (0, 0)))

    kvs = proj[:, NSA_Q_DIM + NSA_KV_DIM:NSA_Q_DIM + 2 * NSA_KV_DIM]
    kvw = proj[:, NSA_Q_DIM + 2 * NSA_KV_DIM:]
    win_new_t = kvw.reshape(batch, t_new, 2, KV_GROUPS, HEAD_DIM).transpose(0, 3, 2, 4, 1)
    win_new_t = jnp.pad(win_new_t, ((0, 0), (0, 0), (0, 0), (0, 0), (LANES - t_new, 0)))
    g_arr = gates[:, :3 * NSA_HEADS].reshape(batch, t_new, 3, KV_GROUPS, GROUP_SIZE).transpose(0, 3, 1, 4, 2)
    g_arr = jnp.pad(g_arr, ((0, 0), (0, 0), (0, 0), (0, HEAD_PAD - GROUP_SIZE), (0, LANES - 3)))
    slopes = np.zeros((KV_GROUPS, HEAD_PAD, LANES), np.float32)
    slopes[:, :GROUP_SIZE, :] = np.asarray(_SLOPES, np.float32).reshape(KV_GROUPS, GROUP_SIZE, 1)
    o, win_out = nsa_sample_attend(sel_t, win_t, layer, page_table_flat, idx_flat, q_heads, jnp.asarray(slopes),
                                   new_rows(kvs), new_rows(kvw), win_new_t, jnp.pad(o_c, pad_heads), g_arr,
                                   batch=batch, past_len=past_len, t_new=t_new)
    o = o[:, :, :, :GROUP_SIZE].transpose(0, 2, 1, 3, 4).reshape(m, NSA_Q_DIM)
    return matmul_residual(o.astype(BF16), w["w_out"], x, tm=m), proj, win_out


def _ssd_kernel(z_ref, x_ref, bc_ref, dt_ref, cw_ref, cb_ref, dtb_ref, alog_ref, drow_ref, nw_ref, cinit_ref,
                hinit_ref, tril_ref, e_ref, et_ref, y_ref, hout_ref, stage_ref, h_ref, yacc_ref, *, t_valid):
    L = SSM_CHUNK
    c = pl.program_id(1)

    @pl.when(c == 0)
    def _():
        stage_ref[0:8, :] = cinit_ref[...]
        h_ref[...] = hinit_ref[...]

    stage_ref[8:8 + L, 0:SSM_D_INNER] = x_ref[...]
    stage_ref[8:8 + L, SSM_D_INNER:] = bc_ref[...]
    conv = cb_ref[...]
    for k in range(SSM_CONV):
        conv = conv + stage_ref[5 + k:5 + k + L, :] * cw_ref[k:k + 1, :]
    stage_ref[0:8, :] = stage_ref[L:L + 8, :]
    conv = conv * jax.nn.sigmoid(conv)
    xh = conv[:, :SSM_D_INNER]

    row = lax.broadcasted_iota(jnp.int32, (L, LANES), 0)
    dt_in = dt_ref[...] + dtb_ref[...]
    dt = jnp.maximum(dt_in, 0.0) + jnp.log1p(jnp.exp(-jnp.abs(dt_in)))
    dt = jnp.where(row < t_valid, dt, 0.0)
    da = dt * -jnp.exp(alog_ref[...])
    acum = _dot(tril_ref[...], da, HIGHEST)
    acum_t = jnp.transpose(acum)
    acum_last = acum[L - 1:L, :]
    dt_full = _dot(dt, e_ref[...], HIGHEST)
    grow_full = _dot(jnp.exp(acum), e_ref[...], HIGHEST)
    toend_full = _dot(jnp.exp(acum_last - acum), e_ref[...], HIGHEST)
    last_rows = jnp.broadcast_to(jnp.exp(acum_last), (LANES, LANES))
    xdt = xh * dt_full
    causal = lax.broadcasted_iota(jnp.int32, (L, L), 0) >= lax.broadcasted_iota(jnp.int32, (L, L), 1)

    for g in range(SSM_GROUPS):
        gl, gh = g * SSM_GW, (g + 1) * SSM_GW
        bm = conv[:, SSM_D_INNER + g * SSM_STATE:SSM_D_INNER + (g + 1) * SSM_STATE].astype(BF16)
        cm = conv[:, SSM_D_INNER + (SSM_GROUPS + g) * SSM_STATE:
                  SSM_D_INNER + (SSM_GROUPS + g + 1) * SSM_STATE].astype(BF16)
        cb = _nt(cm, bm)
        xdt_g = xdt[:, gl:gh]
        for r in range(SSM_HPG):
            h = g * SSM_HPG + r
            seg = acum[:, h:h + 1] - acum_t[h:h + 1, :]
            decay = jnp.exp(jnp.where(causal, seg, NEG_INF))
            yacc_ref[:, gl + r * SSM_HEAD_DIM:gl + (r + 1) * SSM_HEAD_DIM] = _dot(
                (cb * decay).astype(BF16), xdt_g[:, r * SSM_HEAD_DIM:(r + 1) * SSM_HEAD_DIM].astype(BF16))
        h_g = h_ref[gl:gh, :]
        y_state = _nt(cm, h_g.astype(BF16)) * grow_full[:, gl:gh]
        yacc_ref[:, gl:gh] = yacc_ref[:, gl:gh] + y_state
        xw = (xdt_g * toend_full[:, gl:gh]).astype(BF16)
        h_mul = _nt(et_ref[gl:gh, :], last_rows, HIGHEST)
        h_ref[gl:gh, :] = h_g * h_mul + lax.dot_general(xw, bm, (((0,), (0,)), ((), ())),
                                                        preferred_element_type=F32)

    zf = z_ref[...]
    gated = (yacc_ref[...] + xh * drow_ref[...]) * (zf * jax.nn.sigmoid(zf))
    for g in range(SSM_GROUPS):
        gl, gh = g * SSM_GW, (g + 1) * SSM_GW
        y_ref[:, gl:gh] = _rms(gated[:, gl:gh], nw_ref[:, gl:gh]).astype(y_ref.dtype)

    @pl.when(c == pl.num_programs(1) - 1)
    def _():
        hout_ref[...] = h_ref[...]


def ssd_mixer(proj, dt_raw, conv_init, h_init, w, *, batch, n_chunks, t_valid):
    L = SSM_CHUNK
    rows = batch * n_chunks * L
    conv_dim = SSM_D_INNER + SSM_BC
    tril = np.tril(np.ones((L, L), np.float32))
    e = np.zeros((LANES, SSM_D_INNER), np.float32)
    e[np.arange(SSM_D_INNER) // SSM_HEAD_DIM, np.arange(SSM_D_INNER)] = 1.0
    const = lambda shape: pl.BlockSpec(shape, lambda b, c: (0,) * len(shape))
    return pl.pallas_call(
        functools.partial(_ssd_kernel, t_valid=t_valid),
        grid=(batch, n_chunks),
        in_specs=[
            pl.BlockSpec((L, SSM_D_INNER), lambda b, c: (b * n_chunks + c, 0)),
            pl.BlockSpec((L, SSM_D_INNER), lambda b, c: (b * n_chunks + c, 1)),
            pl.BlockSpec((L, SSM_BC), lambda b, c: (b * n_chunks + c, 2 * SSM_D_INNER // SSM_BC)),
            pl.BlockSpec((L, LANES), lambda b, c: (b * n_chunks + c, 0)),
            const((SSM_CONV, conv_dim)), const((1, conv_dim)), const((1, LANES)), const((1, LANES)),
            const((1, SSM_D_INNER)), const((1, SSM_D_INNER)),
            pl.BlockSpec((None, 8, conv_dim), lambda b, c: (b, 0, 0)),
            pl.BlockSpec((None, SSM_D_INNER, SSM_STATE), lambda b, c: (b, 0, 0)),
            const((L, L)), const((LANES, SSM_D_INNER)), const((SSM_D_INNER, LANES)),
        ],
        out_specs=[
            pl.BlockSpec((L, SSM_D_INNER), lambda b, c: (b * n_chunks + c, 0)),
            pl.BlockSpec((None, SSM_D_INNER, SSM_STATE), lambda b, c: (b, 0, 0)),
        ],
        out_shape=[jax.ShapeDtypeStruct((rows, SSM_D_INNER), BF16),
                   jax.ShapeDtypeStruct((batch, SSM_D_INNER, SSM_STATE), F32)],
        scratch_shapes=[pltpu.VMEM((L + 8, conv_dim), F32), pltpu.VMEM((SSM_D_INNER, SSM_STATE), F32),
                        pltpu.VMEM((L, SSM_D_INNER), F32)],
        compiler_params=_cparams(("parallel", "arbitrary")),
        name="ssd_mixer",
    )(proj, proj, proj, dt_raw, w["conv_w"], w["conv_b"], w["dt_bias"], w["a_log"], w["d_row"], w["norm_w"],
      conv_init, h_init, jnp.asarray(tril), jnp.asarray(e), jnp.asarray(e.T))


def _ssd_weights(w_in, conv_w, conv_b, dt_bias, a_log, d_skip, norm_w, w_out):
    n_main = 2 * SSM_D_INNER + SSM_BC
    pad = lambda v: jnp.pad(v, (0, LANES - SSM_HEADS)).reshape(1, LANES)
    return dict(
        w_main=w_in[:, :n_main].astype(BF16),
        w_dt=jnp.pad(w_in[:, n_main:], ((0, 0), (0, LANES - SSM_HEADS))).astype(BF16),
        conv_w=conv_w, conv_b=conv_b.reshape(1, -1), dt_bias=pad(dt_bias), a_log=pad(a_log),
        d_row=jnp.repeat(d_skip, SSM_HEAD_DIM).reshape(1, SSM_D_INNER), norm_w=norm_w.reshape(1, SSM_D_INNER),
        w_out=w_out.astype(BF16))


def ssd_layer(x, gain, w, conv_state, h_init, *, batch, t):
    tm = min(512, x.shape[0])
    proj = norm_matmul(x, gain, w["w_main"], tm=tm, tn=512)
    dt_raw = norm_matmul(x, gain, w["w_dt"], tm=tm, tn=LANES)
    conv_init = jnp.pad(conv_state, ((0, 0), (8 - (SSM_CONV - 1), 0), (0, 0)))
    if t % SSM_CHUNK == 0:
        y, h_last = ssd_mixer(proj, dt_raw, conv_init, h_init, w, batch=batch, n_chunks=t // SSM_CHUNK,
                              t_valid=SSM_CHUNK)
    else:
        pad_rows = lambda a: jnp.pad(a.reshape(batch, t, -1), ((0, 0), (0, SSM_CHUNK - t), (0, 0))).reshape(
            batch * SSM_CHUNK, -1)
        y, h_last = ssd_mixer(pad_rows(proj), pad_rows(dt_raw), conv_init, h_init, w, batch=batch, n_chunks=1,
                              t_valid=t)
        y = y.reshape(batch, SSM_CHUNK, -1)[:, :t].reshape(batch * t, -1)
    return matmul_residual(y, w["w_out"], x, tm=tm), proj, h_last


def kernel(x_prompt, x_sample, cache_kv_cmp, cache_kv_sel, cache_kv_win, state_conv, state_ssm, page_table,
           norm_mix, norm_ffn, nsa_w_in, nsa_qk_gain, nsa_cmp_pe, nsa_cmp_w1, nsa_cmp_w2, nsa_w_out,
           ssd_w_in, ssd_conv_w, ssd_conv_b, ssd_dt_bias, ssd_a_log, ssd_d, ssd_norm, ssd_w_out,
           ffn_w_gate, ffn_w_up, ffn_w_down):
    bp, tp = x_prompt.shape[0], x_prompt.shape[1]
    bs, ts = x_sample.shape[0], x_sample.shape[1]
    depth = norm_mix.shape[0]
    past_len = page_table.shape[1] * PAGE_SIZE
    conv_dim = SSM_D_INNER + SSM_BC
    assert tp % SSM_CHUNK == 0 and ts < SSM_CHUNK and ts <= Q_PAD and ts >= SSM_CONV - 1
    xp = x_prompt.reshape(bp * tp, D_MODEL)
    xs = x_sample.reshape(bs * ts, D_MODEL)
    pt_flat = page_table.reshape(-1)
    feature_major = lambda a: a.transpose(0, 1, 3, 4, 5, 2)
    cmp_t, sel_t, win_t = feature_major(cache_kv_cmp), feature_major(cache_kv_sel), feature_major(cache_kv_win)
    kv_shape = (2, KV_GROUPS, HEAD_DIM)
    outs = {k: [] for k in ("cmp_p", "sel_p", "win_p", "conv_p", "ssm_p", "cmp_s", "sel_s", "win_s", "conv_s",
                            "ssm_s")}
    for i in range(depth):
        l = i // 2
        if i % 2 == 0:
            w = _nsa_weights(nsa_w_in[l], nsa_qk_gain[l], nsa_cmp_pe[l], nsa_cmp_w1[l], nsa_cmp_w2[l], nsa_w_out[l])
            xp, proj = nsa_prompt_layer(xp, norm_mix[i], w, batch=bp, t=tp)
            kv = lambda a, b, t, j: a[:, NSA_Q_DIM + j * NSA_KV_DIM:NSA_Q_DIM + (j + 1) * NSA_KV_DIM].reshape(
                (b, t) + kv_shape)
            outs["cmp_p"].append(kv(proj, bp, tp, 0))
            outs["sel_p"].append(kv(proj, bp, tp, 1))
            outs["win_p"].append(kv(proj, bp, tp, 2)[:, tp - min(WINDOW, tp):])
            xs, proj_s, win_out = nsa_sample_layer(xs, norm_mix[i], w, cmp_t, sel_t, win_t, l, pt_flat, batch=bs,
                                                   t_new=ts, past_len=past_len)
            outs["cmp_s"].append(kv(proj_s, bs, ts, 0))
            outs["sel_s"].append(kv(proj_s, bs, ts, 1))
            outs["win_s"].append(win_out.transpose(0, 4, 1, 2, 3))
        else:
            w = _ssd_weights(ssd_w_in[l], ssd_conv_w[l], ssd_conv_b[l], ssd_dt_bias[l], ssd_a_log[l], ssd_d[l],
                             ssd_norm[l], ssd_w_out[l])
            xp, proj, h_last = ssd_layer(xp, norm_mix[i], w, jnp.zeros((bp, SSM_CONV - 1, conv_dim), F32),
                                         jnp.zeros((bp, SSM_D_INNER, SSM_STATE), F32), batch=bp, t=tp)
            xbc = proj[:, SSM_D_INNER:].reshape(bp, tp, conv_dim)
            outs["conv_p"].append(xbc[:, tp - (SSM_CONV - 1):])
            outs["ssm_p"].append(h_last.reshape(bp, SSM_HEADS, SSM_HEAD_DIM, SSM_STATE))
            xs, proj_s, h_last = ssd_layer(xs, norm_mix[i], w, state_conv[l],
                                           state_ssm[l].reshape(bs, SSM_D_INNER, SSM_STATE), batch=bs, t=ts)
            xbc = proj_s[:, SSM_D_INNER:].reshape(bs, ts, conv_dim)
            outs["conv_s"].append(xbc[:, ts - (SSM_CONV - 1):])
            outs["ssm_s"].append(h_last.reshape(bs, SSM_HEADS, SSM_HEAD_DIM, SSM_STATE))
        wg, wu, wd = ffn_w_gate[i].astype(BF16), ffn_w_up[i].astype(BF16), ffn_w_down[i].astype(BF16)
        xp = ffn(xp, norm_ffn[i], wg, wu, wd, tm=1024, th=256)
        xs = ffn(xs, norm_ffn[i], wg, wu, wd, tm=bs * ts, th=256)
    order = ("cmp_p", "sel_p", "win_p", "conv_p", "ssm_p", "cmp_s", "sel_s", "win_s", "conv_s", "ssm_s")
    return (xp.reshape(bp, tp, D_MODEL), xs.reshape(bs, ts, D_MODEL)) + tuple(jnp.stack(outs[k]) for k in order)
```

```python
import functools
import math

import jax
import jax.numpy as jnp
import ml_dtypes
import numpy as np
from jax import lax
from jax.experimental import pallas as pl
from jax.experimental.pallas import tpu as pltpu

F32 = jnp.float32
BF16 = jnp.bfloat16
HIGHEST = lax.Precision.HIGHEST

D_MODEL = 1024
PAGE_SIZE = 128
NSA_HEADS = 16
HEAD_DIM = 64
KV_GROUPS = 4
GROUP_SIZE = NSA_HEADS // KV_GROUPS
CMP_BLOCK = 32
CMP_STRIDE = 16
CMP_HIDDEN = 256
SEL_BLOCK = 64
SEL_CHUNKS = SEL_BLOCK // CMP_STRIDE
SEL_TOPN = 16
WINDOW = 512
NSA_Q_DIM = NSA_HEADS * HEAD_DIM
NSA_KV_DIM = 2 * KV_GROUPS * HEAD_DIM
KV_HALF = KV_GROUPS * HEAD_DIM
SSM_D_INNER = 2048
SSM_HEAD_DIM = 64
SSM_HEADS = 32
SSM_GROUPS = 4
SSM_HPG = SSM_HEADS // SSM_GROUPS
SSM_STATE = 128
SSM_CONV = 4
SSM_BC = 2 * SSM_GROUPS * SSM_STATE
SSM_GW = SSM_D_INNER // SSM_GROUPS
SSM_CHUNK = 128
NORM_EPS = 1e-6
NEG_INF = -1e30
FORCE_SCORE = 1e4
LANES = 128
VMEM_LIMIT = 56 * 1024 * 1024

_SLOPES = [float(np.float32((2.0 ** (-8.0 / NSA_HEADS)) ** (i + 1))) for i in range(NSA_HEADS)]


def _cparams(sem):
    return pltpu.CompilerParams(dimension_semantics=sem, vmem_limit_bytes=VMEM_LIMIT)


def _nt(a, b, precision=None):
    return lax.dot_general(a, b, (((1,), (1,)), ((), ())), preferred_element_type=F32, precision=precision)


def _dot(a, b, precision=None):
    return jnp.dot(a, b, preferred_element_type=F32, precision=precision)


def _rms(x, gain):
    return x * lax.rsqrt(jnp.mean(x * x, axis=-1, keepdims=True) + NORM_EPS) * gain


def _norm_matmul_kernel(x_ref, g_ref, w_ref, hg_ref, hm_ref, e_ref, et_ref, o_ref, xn_ref, *, head_norm):
    @pl.when(pl.program_id(1) == 0)
    def _():
        xn_ref[...] = _rms(x_ref[...], g_ref[...]).astype(BF16)

    y = _dot(xn_ref[...], w_ref[...])
    if head_norm:
        ss = _dot(y * y, e_ref[...], HIGHEST)
        inv = lax.rsqrt(ss * (1.0 / HEAD_DIM) + NORM_EPS)
        yn = y * _dot(inv, et_ref[...], HIGHEST) * hg_ref[...]
        y = jnp.where(hm_ref[...] > 0.5, yn, y)
    o_ref[...] = y


def norm_matmul(x, gain, w, head_gain=None, head_mask=None, *, tm, tn):
    m, k = x.shape
    n = w.shape[1]
    head_norm = head_gain is not None
    if not head_norm:
        head_gain = jnp.zeros((1, n), F32)
        head_mask = jnp.zeros((1, n), F32)
    lane = np.arange(tn)
    e = np.zeros((tn, LANES), np.float32)
    e[lane, lane // HEAD_DIM] = 1.0
    return pl.pallas_call(
        functools.partial(_norm_matmul_kernel, head_norm=head_norm),
        grid=(m // tm, n // tn),
        in_specs=[
            pl.BlockSpec((tm, k), lambda i, j: (i, 0)),
            pl.BlockSpec((1, k), lambda i, j: (0, 0)),
            pl.BlockSpec((k, tn), lambda i, j: (0, j)),
            pl.BlockSpec((1, tn), lambda i, j: (0, j)),
            pl.BlockSpec((1, tn), lambda i, j: (0, j)),
            pl.BlockSpec((tn, LANES), lambda i, j: (0, 0)),
            pl.BlockSpec((LANES, tn), lambda i, j: (0, 0)),
        ],
        out_specs=pl.BlockSpec((tm, tn), lambda i, j: (i, j)),
        out_shape=jax.ShapeDtypeStruct((m, n), F32),
        scratch_shapes=[pltpu.VMEM((tm, k), BF16)],
        compiler_params=_cparams(("parallel", "arbitrary")),
        name="norm_matmul",
    )(x, gain.reshape(1, k), w, head_gain, head_mask, jnp.asarray(e), jnp.asarray(e.T))


def _ffn_kernel(x_ref, g_ref, wg_ref, wu_ref, wd_ref, o_ref, xn_ref):
    @pl.when(pl.program_id(1) == 0)
    def _():
        x = x_ref[...]
        xn_ref[...] = _rms(x, g_ref[...]).astype(BF16)
        o_ref[...] = x

    h = xn_ref[...]
    gate = _dot(h, wg_ref[...])
    up = _dot(h, wu_ref[...])
    act = (gate * jax.nn.sigmoid(gate) * up).astype(BF16)
    o_ref[...] += _dot(act, wd_ref[...])


def ffn(x, gain, wg, wu, wd, *, tm, th):
    m, k = x.shape
    hid = wg.shape[1]
    return pl.pallas_call(
        _ffn_kernel,
        grid=(m // tm, hid // th),
        in_specs=[
            pl.BlockSpec((tm, k), lambda i, j: (i, 0)),
            pl.BlockSpec((1, k), lambda i, j: (0, 0)),
            pl.BlockSpec((k, th), lambda i, j: (0, j)),
            pl.BlockSpec((k, th), lambda i, j: (0, j)),
            pl.BlockSpec((th, k), lambda i, j: (j, 0)),
        ],
        out_specs=pl.BlockSpec((tm, k), lambda i, j: (i, 0)),
        out_shape=jax.ShapeDtypeStruct((m, k), F32),
        scratch_shapes=[pltpu.VMEM((tm, k), BF16)],
        compiler_params=_cparams(("parallel", "arbitrary")),
        name="ffn",
    )(x, gain.reshape(1, k), wg, wu, wd)


def _matmul_res_kernel(a_ref, w_ref, r_ref, o_ref):
    o_ref[...] = r_ref[...] + _dot(a_ref[...], w_ref[...])


def matmul_residual(a, w, res, *, tm):
    m, k = a.shape
    n = w.shape[1]
    return pl.pallas_call(
        _matmul_res_kernel,
        grid=(m // tm,),
        in_specs=[
            pl.BlockSpec((tm, k), lambda i: (i, 0)),
            pl.BlockSpec((k, n), lambda i: (0, 0)),
            pl.BlockSpec((tm, n), lambda i: (i, 0)),
        ],
        out_specs=pl.BlockSpec((tm, n), lambda i: (i, 0)),
        out_shape=jax.ShapeDtypeStruct((m, n), F32),
        compiler_params=_cparams(("parallel",)),
        name="matmul_residual",
    )(a, w, res)


def _compress_accumulate(read_rows, pe_ref, w1_ref, kv):
    acc = [None, None]
    for j in range(CMP_STRIDE):
        xs = read_rows(j)
        for m in range(2):
            lhs = (xs + pe_ref[kv, m, j:j + 1, :]).astype(BF16)
            part = _dot(lhs, w1_ref[kv, m, j])
            acc[m] = part if acc[m] is None else acc[m] + part
    return acc


def _compress_finish(acc, w2_ref, kg_ref, kv):
    rows = acc[0].shape[0]
    pre = acc[0] + pltpu.roll(acc[1], rows - 1, axis=0)
    hid = (pre * jax.nn.sigmoid(pre)).astype(BF16)
    out = _dot(hid, w2_ref[kv])
    if kv == 0:
        out = _rms(out, kg_ref[...])
    return out


def _compress_rows_kernel(x_ref, pe_ref, w1_ref, w2_ref, kg_ref, o_ref, xt_ref, xs_ref, *, n_chunk):
    xt_ref[...] = jnp.transpose(x_ref[...])
    n_pair = NSA_KV_DIM // LANES
    for c in range(n_pair):
        xs_ref[c] = x_ref[:, c * LANES:(c + 1) * LANES]
    for kv in range(2):
        def read_rows(j, kv=kv):
            parts = []
            for c in range(kv * n_pair // 2, (kv + 1) * n_pair // 2):
                pair = xs_ref.at[c][pl.ds(j, n_chunk, stride=CMP_STRIDE), :]
                parts += [pair[:, :HEAD_DIM], pair[:, HEAD_DIM:]]
            return jnp.concatenate(parts, axis=0)
        acc = _compress_accumulate(read_rows, pe_ref, w1_ref, kv)
        o_ref[kv] = _compress_finish(acc, w2_ref, kg_ref, kv)


def compress_rows(proj, col_block, pe, w1, w2, kgain, *, batch, t):
    n_chunk = t // CMP_STRIDE
    return pl.pallas_call(
        functools.partial(_compress_rows_kernel, n_chunk=n_chunk),
        grid=(batch,),
        in_specs=[
            pl.BlockSpec((t, NSA_KV_DIM), lambda b: (b, col_block)),
            pl.BlockSpec(pe.shape, lambda b: (0, 0, 0, 0)),
            pl.BlockSpec(w1.shape, lambda b: (0, 0, 0, 0, 0)),
            pl.BlockSpec(w2.shape, lambda b: (0, 0, 0)),
            pl.BlockSpec((1, HEAD_DIM), lambda b: (0, 0)),
        ],
        out_specs=[pl.BlockSpec((None, 2, KV_GROUPS * n_chunk, HEAD_DIM), lambda b: (b, 0, 0, 0)),
                   pl.BlockSpec((None, NSA_KV_DIM, t), lambda b: (b, 0, 0))],
        out_shape=[jax.ShapeDtypeStruct((batch, 2, KV_GROUPS * n_chunk, HEAD_DIM), F32),
                   jax.ShapeDtypeStruct((batch, NSA_KV_DIM, t), F32)],
        scratch_shapes=[pltpu.VMEM((NSA_KV_DIM // LANES, t, LANES), F32)],
        compiler_params=_cparams(("parallel",)),
        name="compress_rows",
    )(proj, pe, w1, w2, kgain.reshape(1, HEAD_DIM))


def _slope_col(g, rows, tq):
    rblk = lax.broadcasted_iota(jnp.int32, (rows, 1), 0) // tq
    col = jnp.full((rows, 1), _SLOPES[GROUP_SIZE * g + GROUP_SIZE - 1], F32)
    for r in range(GROUP_SIZE - 1):
        col = jnp.where(rblk == r, _SLOPES[GROUP_SIZE * g + r], col)
    return col


def _normalize(acc, l):
    return acc * jnp.where(l > 0.0, 1.0 / jnp.where(l > 0.0, l, 1.0), 0.0)


MASK_BIG = 2.0 ** 100
LOG2E = 1.4426950408889634
AUG = 64
AUG_POS = 32


def _bf16_terms(x, n=3):
    x = np.asarray(x, np.float64)
    terms = []
    for _ in range(n):
        term = x.astype(ml_dtypes.bfloat16).astype(np.float64)
        terms.append(term.astype(np.float32))
        x = x - term
    return terms


def _key_aug(pos, flags=None):
    pos = np.asarray(pos, np.int64)
    aug = np.zeros((pos.shape[0], AUG), np.float32)
    if flags is not None:
        aug[:, :flags.shape[1]] = flags
    coarse = (pos // SEL_BLOCK) * SEL_BLOCK
    aug[:, AUG_POS:AUG_POS + 3] = coarse[:, None]
    aug[:, AUG_POS + 3:AUG_POS + 6] = (pos - coarse)[:, None]
    return aug


def _query_aug_rows(tq):
    rows = np.zeros((KV_GROUPS, AUG - AUG_POS, GROUP_SIZE * tq), np.float32)
    for h in range(NSA_HEADS):
        terms = _bf16_terms(np.float64(_SLOPES[h]) * LOG2E)
        g, r = divmod(h, GROUP_SIZE)
        for i, term in enumerate(terms):
            rows[g, i, r * tq:(r + 1) * tq] = term
            rows[g, 3 + i, r * tq:(r + 1) * tq] = term
    return rows


def _online_update(s, v_t, m, l, acc):
    m_new = jnp.maximum(m, jnp.max(s, axis=0, keepdims=True))
    alpha = jnp.exp2(m - m_new)
    p = jnp.exp2(s - m_new)
    return m_new, alpha * l + jnp.sum(p, axis=0, keepdims=True), alpha * acc + _dot(v_t, p.astype(BF16))


def _nsa_prompt_kernel(q_ref, ks_ref, kw_ref, cmp_ref, gate_ref, gq_ref, gks_ref, gkw_ref, gksc_ref, gkwc_ref,
                       csel_ref, cwin_ref, ccmp_ref, qrows_ref, mmap_ref, eye_ref,
                       o_ref, kst_out_ref, kwt_out_ref,
                       ksa_ref, kwa_ref, vst_ref, vwt_ref, kca_ref, vct_ref, ot_ref, *, t, tq):
    qi = pl.program_id(1)
    n_sel = t // SEL_BLOCK
    n_cmp = t // CMP_STRIDE
    top_n = min(SEL_TOPN, n_sel)
    cols = GROUP_SIZE * tq
    nt = t // tq
    n_win = WINDOW // tq + 1
    n_pad = WINDOW // tq

    @pl.when(qi == 0)
    def _():
        def prepare(src_ref, grow_ref, gcol_ref, const_ref, ka_ref, vt_ref, out_ref, pad_tiles):
            pad = pad_tiles * tq
            x = src_ref[...]
            xt = jnp.transpose(x)
            width = out_ref.shape[1]
            out_ref[KV_HALF:, :] = xt[KV_HALF:, t - width:]
            for g in range(KV_GROUPS):
                lo, hi = g * HEAD_DIM, (g + 1) * HEAD_DIM
                kn = _rms(x[:, lo:hi], grow_ref[...])
                ka_ref[g, pad:pad + t, :] = jnp.concatenate([kn, const_ref[pad:pad + t, :]], axis=1).astype(BF16)
                if pad:
                    ka_ref[g, 0:pad, :] = jnp.concatenate(
                        [jnp.zeros((pad, HEAD_DIM), F32), const_ref[0:pad, :]], axis=1).astype(BF16)
                kt = xt[lo:hi, :]
                ktn = kt * lax.rsqrt(jnp.mean(kt * kt, axis=0, keepdims=True) + NORM_EPS) * gcol_ref[...]
                out_ref[lo:hi, :] = ktn[:, t - width:]
                vt = xt[KV_HALF + lo:KV_HALF + hi, :].astype(BF16)
                base = g * (nt + pad_tiles)
                for i in range(pad_tiles):
                    vt_ref[base + i] = jnp.zeros((HEAD_DIM, tq), BF16)
                for i in range(nt):
                    vt_ref[base + pad_tiles + i] = vt[:, i * tq:(i + 1) * tq]

        prepare(ks_ref, gks_ref, gksc_ref, csel_ref, ksa_ref, vst_ref, kst_out_ref, 0)
        prepare(kw_ref, gkw_ref, gkwc_ref, cwin_ref, kwa_ref, vwt_ref, kwt_out_ref, n_pad)
        for g in range(KV_GROUPS):
            kc = cmp_ref[0, g * n_cmp:(g + 1) * n_cmp, :]
            kca_ref[g] = jnp.concatenate([kc, ccmp_ref[...]], axis=1).astype(BF16)
            vct_ref[g] = _nt(eye_ref[...], cmp_ref[1, g * n_cmp:(g + 1) * n_cmp, :].astype(BF16)).astype(BF16)

    q0 = pl.multiple_of(qi * tq, tq)
    lane4 = lax.broadcasted_iota(jnp.int32, (1, cols), 1) % tq
    key_off = lax.broadcasted_iota(jnp.int32, (tq, cols), 0)
    causal = key_off <= lane4
    win_old = key_off > lane4
    cmp_ok = (lax.broadcasted_iota(jnp.int32, (n_cmp, cols), 0) * CMP_STRIDE + (CMP_BLOCK - 1)) <= q0 + lane4
    qt = jnp.transpose(q_ref[...])
    gt = jax.nn.sigmoid(jnp.transpose(gate_ref[...]))
    jblk = lax.broadcasted_iota(jnp.int32, (n_sel, tq), 0)
    t_lane = q0 + lax.broadcasted_iota(jnp.int32, (n_sel, tq), 1)
    cur = t_lane // SEL_BLOCK
    forced = (jblk == 0) | (jblk == cur) | (jblk == cur - 1)
    visible = jblk * SEL_BLOCK <= t_lane
    init = (jnp.full((1, cols), -MASK_BIG, F32), jnp.zeros((1, cols), F32), jnp.zeros((HEAD_DIM, cols), F32))

    o_c, q_sel, q_win = [], [], []
    for g in range(KV_GROUPS):
        heads = []
        for r in range(GROUP_SIZE):
            h = GROUP_SIZE * g + r
            qh = qt[h * HEAD_DIM:(h + 1) * HEAD_DIM, :]
            qh = qh * lax.rsqrt(jnp.mean(qh * qh, axis=0, keepdims=True) + NORM_EPS)
            heads.append(qh * (gq_ref[...] * (HEAD_DIM ** -0.5 * LOG2E)))
        q_top = jnp.concatenate(heads, axis=1)
        q_low = qrows_ref[g]

        def with_mask_rows(rows_, q_top=q_top, q_low=q_low):
            return jnp.concatenate([q_top, rows_, q_low], axis=0).astype(BF16)

        s = _dot(kca_ref[g], with_mask_rows(jnp.zeros((AUG_POS, cols), F32)))
        s = jnp.where(cmp_ok, s, -MASK_BIG)
        p = jnp.where(cmp_ok, jnp.exp2(s - jnp.max(s, axis=0, keepdims=True)), 0.0)
        pc = _normalize(p, jnp.sum(p, axis=0, keepdims=True))
        o_c.append(_dot(vct_ref[g], pc.astype(BF16)))
        imp = pc[:, 0:tq]
        for r in range(1, GROUP_SIZE):
            imp = imp + pc[:, r * tq:(r + 1) * tq]

        p_slc = _dot(mmap_ref[...], imp, HIGHEST)
        score = jnp.where(forced, FORCE_SCORE, jnp.where(visible, p_slc, -1.0))
        cnt = jnp.zeros((n_sel, tq), jnp.int32)
        for k in range(n_sel):
            sk = score[k:k + 1, :]
            beats = (sk > score) | ((sk == score) & (jblk > k))
            cnt = cnt + beats.astype(jnp.int32)
        drop = jnp.where(cnt < top_n, 0.0, -MASK_BIG)
        if n_sel < AUG_POS:
            drop = jnp.concatenate([drop, jnp.zeros((AUG_POS - n_sel, tq), F32)], axis=0)
        q_sel.append(with_mask_rows(jnp.concatenate([drop] * GROUP_SIZE, axis=1)))
        q_win.append(with_mask_rows(jnp.full((AUG_POS, cols), -MASK_BIG, F32)))

    def sel_scores(k0):
        return [_dot(ksa_ref[g, pl.ds(k0, tq), :], q_sel[g]) for g in range(KV_GROUPS)]

    def sel_body(i, carry):
        scores = sel_scores(pl.multiple_of(i * tq, tq))
        return tuple(_online_update(scores[g], vst_ref[g * nt + i], *carry[g]) for g in range(KV_GROUPS))

    carry = lax.fori_loop(0, qi, sel_body, (init,) * KV_GROUPS)
    o_s = []
    for g, s in enumerate(sel_scores(q0)):
        _, l, acc = _online_update(jnp.where(causal, s, -MASK_BIG), vst_ref[g * nt + qi], *carry[g])
        o_s.append(_normalize(acc, l))

    win_scores = [[_dot(kwa_ref[g, pl.ds(q0 + i * tq, tq), :], q_win[g]) for i in range(n_win)]
                  for g in range(KV_GROUPS)]
    o_w = []
    for g in range(KV_GROUPS):
        tiles = win_scores[g]
        tiles[0] = jnp.where(win_old, tiles[0], -MASK_BIG)
        tiles[-1] = jnp.where(causal, tiles[-1], -MASK_BIG)
        m = functools.reduce(jnp.maximum, [jnp.max(s, axis=0, keepdims=True) for s in tiles])
        probs = [jnp.exp2(s - m) for s in tiles]
        l = functools.reduce(jnp.add, [jnp.sum(p, axis=0, keepdims=True) for p in probs])
        acc = functools.reduce(jnp.add, [_dot(vwt_ref[g * (nt + n_pad) + qi + i], p.astype(BF16))
                                         for i, p in enumerate(probs)])
        o_w.append(_normalize(acc, l))

    for h in range(NSA_HEADS):
        g, r = divmod(h, GROUP_SIZE)
        sl = slice(r * tq, (r + 1) * tq)
        ot_ref[h * HEAD_DIM:(h + 1) * HEAD_DIM, :] = (
            gt[h:h + 1, :] * o_c[g][:, sl] + gt[NSA_HEADS + h:NSA_HEADS + h + 1, :] * o_s[g][:, sl]
            + gt[2 * NSA_HEADS + h:2 * NSA_HEADS + h + 1, :] * o_w[g][:, sl])
    o_ref[...] = jnp.transpose(ot_ref[...]).astype(o_ref.dtype)


def nsa_prompt_attention(proj, gates, cmp, qk_gain, *, batch, t, tq):
    n_sel = t // SEL_BLOCK
    n_cmp = t // CMP_STRIDE
    nq = t // tq
    nt = t // tq
    n_pad = WINDOW // tq
    keep = min(WINDOW, t)
    assert n_sel <= AUG_POS and t % tq == 0 and WINDOW % tq == 0 and tq % SEL_BLOCK == 0
    n = np.arange(n_cmp)
    mmap = np.zeros((n_sel, n_cmp), np.float32)
    real = n < n_cmp - 1
    np.add.at(mmap, (n[real] // SEL_CHUNKS, n[real]), 1.0)
    nxt = real & ((n + 1) // SEL_CHUNKS < n_sel)
    np.add.at(mmap, ((n[nxt] + 1) // SEL_CHUNKS, n[nxt]), 1.0)
    pos = np.arange(t)
    csel = _key_aug(pos, (pos[:, None] // SEL_BLOCK == np.arange(n_sel)[None, :]).astype(np.float32))
    wpos = np.arange(-WINDOW, t)
    cwin = _key_aug(wpos, (wpos < 0).astype(np.float32)[:, None])
    ccmp = _key_aug(np.arange(n_cmp) * CMP_STRIDE + (CMP_BLOCK - 1))
    const = lambda shape: pl.BlockSpec(shape, lambda b, i: (0,) * len(shape))
    col = lambda v: v.reshape(HEAD_DIM, 1)
    row = lambda v: v.reshape(1, HEAD_DIM)
    return pl.pallas_call(
        functools.partial(_nsa_prompt_kernel, t=t, tq=tq),
        grid=(batch, nq),
        in_specs=[
            pl.BlockSpec((tq, NSA_Q_DIM), lambda b, i: (b * nq + i, 0)),
            pl.BlockSpec((t, NSA_KV_DIM), lambda b, i: (b, 3)),
            pl.BlockSpec((t, NSA_KV_DIM), lambda b, i: (b, 4)),
            pl.BlockSpec((None, 2, KV_GROUPS * n_cmp, HEAD_DIM), lambda b, i: (b, 0, 0, 0)),
            pl.BlockSpec((tq, LANES), lambda b, i: (b * nq + i, 0)),
            const((HEAD_DIM, 1)), const((1, HEAD_DIM)), const((1, HEAD_DIM)), const((HEAD_DIM, 1)),
            const((HEAD_DIM, 1)),
            const((t, AUG)), const((t + WINDOW, AUG)), const((n_cmp, AUG)),
            const((KV_GROUPS, AUG - AUG_POS, GROUP_SIZE * tq)), const((n_sel, n_cmp)), const((HEAD_DIM, HEAD_DIM)),
        ],
        out_specs=[
            pl.BlockSpec((tq, NSA_Q_DIM), lambda b, i: (b * nq + i, 0)),
            pl.BlockSpec((None, NSA_KV_DIM, t), lambda b, i: (b, 0, 0)),
            pl.BlockSpec((None, NSA_KV_DIM, keep), lambda b, i: (b, 0, 0)),
        ],
        out_shape=[jax.ShapeDtypeStruct((batch * t, NSA_Q_DIM), BF16),
                   jax.ShapeDtypeStruct((batch, NSA_KV_DIM, t), F32),
                   jax.ShapeDtypeStruct((batch, NSA_KV_DIM, keep), F32)],
        scratch_shapes=[
            pltpu.VMEM((KV_GROUPS, t, HEAD_DIM + AUG), BF16),
            pltpu.VMEM((KV_GROUPS, t + WINDOW, HEAD_DIM + AUG), BF16),
            pltpu.VMEM((KV_GROUPS * nt, HEAD_DIM, tq), BF16),
            pltpu.VMEM((KV_GROUPS * (nt + n_pad), HEAD_DIM, tq), BF16),
            pltpu.VMEM((KV_GROUPS, n_cmp, HEAD_DIM + AUG), BF16),
            pltpu.VMEM((KV_GROUPS, HEAD_DIM, n_cmp), BF16),
            pltpu.VMEM((NSA_Q_DIM, tq), F32),
        ],
        compiler_params=_cparams(("parallel", "arbitrary")),
        name="nsa_prompt_attention",
    )(proj, proj, proj, cmp, gates, col(qk_gain[0]), row(qk_gain[2]), row(qk_gain[3]), col(qk_gain[2]),
      col(qk_gain[3]), jnp.asarray(csel), jnp.asarray(cwin), jnp.asarray(ccmp), jnp.asarray(_query_aug_rows(tq)),
      jnp.asarray(mmap), jnp.eye(HEAD_DIM, dtype=BF16))


def _nsa_weights(w_in, qk_gain, cmp_pe, cmp_w1, cmp_w2, w_out):
    n_main = NSA_Q_DIM + 3 * NSA_KV_DIM
    zeros_kv = jnp.zeros((KV_HALF,), F32)
    ones_kv = jnp.ones((KV_HALF,), F32)
    head_gain = jnp.concatenate([
        jnp.tile(qk_gain[0], NSA_HEADS), zeros_kv, zeros_kv,
        jnp.tile(qk_gain[2], KV_GROUPS), zeros_kv, jnp.tile(qk_gain[3], KV_GROUPS), zeros_kv]).reshape(1, n_main)
    head_mask = jnp.concatenate([
        jnp.ones((NSA_Q_DIM,), F32), zeros_kv, zeros_kv, ones_kv, zeros_kv, ones_kv, zeros_kv]).reshape(1, n_main)
    w_gate = jnp.pad(w_in[:, n_main:], ((0, 0), (0, LANES - 3 * NSA_HEADS)))
    return dict(
        w_main=w_in[:, :n_main].astype(BF16), w_gate=w_gate.astype(BF16), head_gain=head_gain, head_mask=head_mask,
        pe=cmp_pe.reshape(2, 2, CMP_STRIDE, HEAD_DIM),
        w1=cmp_w1.reshape(2, 2, CMP_STRIDE, HEAD_DIM, CMP_HIDDEN).astype(BF16),
        w2=cmp_w2.astype(BF16), kgain=qk_gain[1], qk_gain=qk_gain, w_out=w_out.astype(BF16))


def _nsa_project(x, gain, w, *, tm, head_norm):
    if head_norm:
        proj = norm_matmul(x, gain, w["w_main"], w["head_gain"], w["head_mask"], tm=tm, tn=512)
    else:
        proj = norm_matmul(x, gain, w["w_main"], tm=tm, tn=512)
    return proj, norm_matmul(x, gain, w["w_gate"], tm=tm, tn=LANES)


def nsa_prompt_layer(x, gain, w, *, batch, t):
    tm = min(1024, batch * t)
    proj, gates = _nsa_project(x, gain, w, tm=tm, head_norm=False)
    cmp, cmp_t = compress_rows(proj, 2, w["pe"], w["w1"], w["w2"], w["kgain"], batch=batch, t=t)
    o, sel_t, win_t = nsa_prompt_attention(proj, gates, cmp, w["qk_gain"], batch=batch, t=t, tq=128)
    return matmul_residual(o, w["w_out"], x, tm=tm), cmp_t, sel_t, win_t


PAGES_PER_STEP = 8
CHUNKS_PER_PAGE = PAGE_SIZE // CMP_STRIDE


def _compress_pages_kernel(pt_ref, *refs, n_chunk):
    del pt_ref
    pages = refs[:PAGES_PER_STEP]
    pe_ref, w1_ref, w2_ref, kg_ref, o_ref, xs_ref, pre_ref = refs[PAGES_PER_STEP:]
    s = pl.program_id(1)
    n_pair = NSA_KV_DIM // LANES
    step_chunks = PAGES_PER_STEP * CHUNKS_PER_PAGE
    for i, page in enumerate(pages):
        for c in range(n_pair):
            xs_ref[c, i * PAGE_SIZE:(i + 1) * PAGE_SIZE, :] = jnp.transpose(page[c * LANES:(c + 1) * LANES, :])
    for kv in range(2):
        def read_rows(j, kv=kv):
            parts = []
            for c in range(kv * n_pair // 2, (kv + 1) * n_pair // 2):
                pair = xs_ref.at[c][pl.ds(j, step_chunks, stride=CMP_STRIDE), :]
                parts += [pair[:, :HEAD_DIM], pair[:, HEAD_DIM:]]
            return jnp.concatenate(parts, axis=0)
        acc = _compress_accumulate(read_rows, pe_ref, w1_ref, kv)
        for m in range(2):
            for g in range(KV_GROUPS):
                pre_ref[kv, m, pl.ds(pl.multiple_of(g * n_chunk + s * step_chunks, step_chunks), step_chunks), :] = (
                    acc[m][g * step_chunks:(g + 1) * step_chunks])

    @pl.when(s == pl.num_programs(1) - 1)
    def _():
        for kv in range(2):
            o_ref[kv] = _compress_finish([pre_ref[kv, 0], pre_ref[kv, 1]], w2_ref, kg_ref, kv)


def compress_pages(cache_t, layer, page_table_flat, pe, w1, w2, kgain, *, batch, n_pages):
    n_chunk = n_pages * CHUNKS_PER_PAGE
    n_steps = n_pages // PAGES_PER_STEP

    def page_spec(i):
        return pl.BlockSpec((None, None, NSA_KV_DIM, PAGE_SIZE),
                            lambda b, s, pt: (layer, pt[b * n_pages + s * PAGES_PER_STEP + i], 0, 0))

    grid_spec = pltpu.PrefetchScalarGridSpec(
        num_scalar_prefetch=1,
        grid=(batch, n_steps),
        in_specs=[page_spec(i) for i in range(PAGES_PER_STEP)] + [
            pl.BlockSpec(pe.shape, lambda b, s, pt: (0, 0, 0, 0)),
            pl.BlockSpec(w1.shape, lambda b, s, pt: (0, 0, 0, 0, 0)),
            pl.BlockSpec(w2.shape, lambda b, s, pt: (0, 0, 0)),
            pl.BlockSpec((1, HEAD_DIM), lambda b, s, pt: (0, 0)),
        ],
        out_specs=pl.BlockSpec((None, 2, KV_GROUPS * n_chunk, HEAD_DIM), lambda b, s, pt: (b, 0, 0, 0)),
        scratch_shapes=[pltpu.VMEM((NSA_KV_DIM // LANES, PAGES_PER_STEP * PAGE_SIZE, LANES), F32),
                        pltpu.VMEM((2, 2, KV_GROUPS * n_chunk, CMP_HIDDEN), F32)],
    )
    return pl.pallas_call(
        functools.partial(_compress_pages_kernel, n_chunk=n_chunk),
        grid_spec=grid_spec,
        out_shape=jax.ShapeDtypeStruct((batch, 2, KV_GROUPS * n_chunk, HEAD_DIM), F32),
        compiler_params=_cparams(("parallel", "arbitrary")),
        name="compress_pages",
    )(page_table_flat, *([cache_t] * PAGES_PER_STEP), pe, w1, w2, kgain.reshape(1, HEAD_DIM))


Q_PAD = 8
HEAD_PAD = 16


def _nsa_sample_select_kernel(q_ref, cmp_ref, mmap_ref, oc_ref, idx_ref, score_ref, *, past_len, n_cmp, n_sel):
    rows = GROUP_SIZE * Q_PAD
    n_selp = score_ref.shape[0]
    t_max = past_len + Q_PAD - 1
    row = lax.broadcasted_iota(jnp.int32, (rows, 1), 0)
    t_col = past_len + row % Q_PAD
    jblk = lax.broadcasted_iota(jnp.int32, (n_selp, Q_PAD), 0)
    t_lane = past_len + lax.broadcasted_iota(jnp.int32, (n_selp, Q_PAD), 1)
    cur = t_lane // SEL_BLOCK
    forced = (jblk == 0) | (jblk == cur) | (jblk == cur - 1)
    visible = jblk * SEL_BLOCK <= t_lane
    for g in range(KV_GROUPS):
        qg = (q_ref[g] * (HEAD_DIM ** -0.5)).astype(BF16)
        slope = _slope_col(g, rows, Q_PAD)
        kc = cmp_ref[0, g * n_cmp:(g + 1) * n_cmp, :].astype(BF16)
        vc = cmp_ref[1, g * n_cmp:(g + 1) * n_cmp, :].astype(BF16)
        kpos = lax.broadcasted_iota(jnp.int32, (1, n_cmp), 1) * CMP_STRIDE + (CMP_BLOCK - 1)
        s = _nt(qg, kc) + slope * (kpos - t_max).astype(F32)
        mask = kpos <= t_col
        s = jnp.where(mask, s, NEG_INF)
        p = jnp.where(mask, jnp.exp(s - jnp.max(s, axis=1, keepdims=True)), 0.0)
        pc = _normalize(p, jnp.sum(p, axis=1, keepdims=True))
        oc_ref[g] = _dot(pc.astype(BF16), vc)
        imp = pc[0:Q_PAD]
        for r in range(1, GROUP_SIZE):
            imp = imp + pc[r * Q_PAD:(r + 1) * Q_PAD]
        p_slc = _nt(mmap_ref[...], imp, HIGHEST)
        score = jnp.where(forced, FORCE_SCORE, jnp.where(visible, p_slc, -1.0))
        score_ref[:, g * Q_PAD:(g + 1) * Q_PAD] = jnp.where(jblk < n_sel, score, -2.0)

    score = score_ref[...]
    jall = lax.broadcasted_iota(jnp.int32, score.shape, 0)

    def rank_body(k, cnt):
        sk = score_ref[pl.ds(k, 1), :]
        beats = (sk > score) | ((sk == score) & (jall > k))
        return cnt + beats.astype(jnp.int32)

    cnt = lax.fori_loop(0, n_sel, rank_body, jnp.zeros(score.shape, jnp.int32), unroll=4)
    slots = [jnp.sum(jnp.where(cnt == sl, jall, 0), axis=0, keepdims=True) for sl in range(SEL_TOPN)]
    idx_ref[...] = jnp.concatenate(slots, axis=0)


def nsa_sample_select(q_stack, cmp, *, batch, past_len, t_new):
    n_cmp = past_len // CMP_STRIDE
    n_sel = -(-(past_len + t_new) // SEL_BLOCK)
    n_selp = -(-n_sel // 8) * 8
    n = np.arange(n_cmp)
    mmap = np.zeros((n_selp, n_cmp), np.float32)
    real = n < n_cmp - 1
    np.add.at(mmap, (n[real] // SEL_CHUNKS, n[real]), 1.0)
    np.add.at(mmap, ((n[real] + 1) // SEL_CHUNKS, n[real]), 1.0)
    rows = GROUP_SIZE * Q_PAD
    return pl.pallas_call(
        functools.partial(_nsa_sample_select_kernel, past_len=past_len, n_cmp=n_cmp, n_sel=n_sel),
        grid=(batch,),
        in_specs=[
            pl.BlockSpec((None, KV_GROUPS, rows, HEAD_DIM), lambda b: (b, 0, 0, 0)),
            pl.BlockSpec((None, 2, KV_GROUPS * n_cmp, HEAD_DIM), lambda b: (b, 0, 0, 0)),
            pl.BlockSpec((n_selp, n_cmp), lambda b: (0, 0)),
        ],
        out_specs=[
            pl.BlockSpec((None, KV_GROUPS, rows, HEAD_DIM), lambda b: (b, 0, 0, 0)),
            pl.BlockSpec((None, SEL_TOPN, KV_GROUPS * Q_PAD), lambda b: (b, 0, 0)),
        ],
        out_shape=[jax.ShapeDtypeStruct((batch, KV_GROUPS, rows, HEAD_DIM), F32),
                   jax.ShapeDtypeStruct((batch, SEL_TOPN, KV_GROUPS * Q_PAD), jnp.int32)],
        scratch_shapes=[pltpu.VMEM((n_selp, KV_GROUPS * Q_PAD), F32)],
        compiler_params=_cparams(("parallel",)),
        name="nsa_sample_select",
    )(q_stack, cmp, jnp.asarray(mmap))


def _nsa_sample_attend_kernel(pt_ref, idx_ref, *refs, past_len, t_new, n_past_blocks):
    del pt_ref
    blocks = refs[:SEL_TOPN]
    (q_ref, slope_ref, snew_ref, win_ref, wnew_ref, wnew_t_ref, oc_ref, gate_ref,
     o_ref, wout_ref) = refs[SEL_TOPN:]
    b, g, qi = pl.program_id(0), pl.program_id(1), pl.program_id(2)
    t = past_len + qi
    q = (q_ref[...] * (HEAD_DIM ** -0.5)).astype(BF16)
    slope = slope_ref[:, 0:1]
    lane = lax.broadcasted_iota(jnp.int32, (1, LANES), 1)
    new_pos = past_len + lane
    new_ok = (lane < t_new) & (new_pos <= t)

    def attend(tiles, v_new):
        m = jnp.full((HEAD_PAD, 1), NEG_INF, F32)
        for s, mask, _ in tiles:
            m = jnp.maximum(m, jnp.max(jnp.where(mask, s, NEG_INF), axis=1, keepdims=True))
        l = jnp.zeros((HEAD_PAD, 1), F32)
        acc = jnp.zeros((HEAD_PAD, HEAD_DIM), F32)
        for i, (s, mask, v) in enumerate(tiles):
            p = jnp.where(mask, jnp.exp(jnp.where(mask, s, NEG_INF) - m), 0.0)
            l = l + jnp.sum(p, axis=1, keepdims=True)
            if i == len(tiles) - 1:
                acc = acc + _dot(p.astype(BF16), v_new)
            else:
                acc = acc + _nt(p.astype(BF16), v)
        return _normalize(acc, l)

    base = ((b * KV_GROUPS + g) * t_new + qi) * SEL_TOPN
    tiles = []
    for sl, blk_ref in enumerate(blocks):
        blk = idx_ref[base + sl]
        kpos = (blk // 2) * PAGE_SIZE + lane
        mask = (lane // SEL_BLOCK == blk % 2) & (blk < n_past_blocks) & (kpos <= t)
        s = _dot(q, blk_ref[0].astype(BF16)) + slope * (kpos - t).astype(F32)
        tiles.append((s, mask, blk_ref[1].astype(BF16)))
    s_new = _nt(q, snew_ref[0].astype(BF16)) + slope * (new_pos - t).astype(F32)
    tiles.append((s_new, new_ok, None))
    o_s = attend(tiles, snew_ref[1].astype(BF16))

    buf = win_ref.shape[2]
    wlane = lax.broadcasted_iota(jnp.int32, (1, buf), 1)
    wpos = past_len - buf + wlane
    dist = t - wpos
    s_w = _dot(q, win_ref[0].astype(BF16)) + slope * (wpos - t).astype(F32)
    s_wn = _nt(q, wnew_ref[0].astype(BF16)) + slope * (new_pos - t).astype(F32)
    o_w = attend([(s_w, (dist >= 0) & (dist < WINDOW), win_ref[1].astype(BF16)),
                  (s_wn, new_ok & (t - new_pos < WINDOW), None)], wnew_ref[1].astype(BF16))

    gates = jax.nn.sigmoid(gate_ref[...])
    o_ref[...] = gates[:, 0:1] * oc_ref[...] + gates[:, 1:2] * o_s + gates[:, 2:3] * o_w

    @pl.when(qi == 0)
    def _():
        for kv in range(2):
            rolled = pltpu.roll(win_ref[kv], buf - t_new, axis=1)
            wout_ref[kv, :, 0:buf - LANES] = rolled[:, 0:buf - LANES]
            wout_ref[kv, :, buf - LANES:] = jnp.where(lane >= LANES - t_new, wnew_t_ref[kv], rolled[:, buf - LANES:])


def nsa_sample_attend(cache_t, win_t, layer, page_table_flat, idx_flat, q_heads, slopes, sel_new, win_new,
                      win_new_t, o_c, gates, *, batch, past_len, t_new):
    n_pages = past_len // PAGE_SIZE
    n_past_blocks = past_len // SEL_BLOCK
    buf = win_t.shape[-1]

    def block_spec(sl):
        def index(b, g, q, pt, idx):
            blk = idx[((b * KV_GROUPS + g) * t_new + q) * SEL_TOPN + sl]
            page = pt[b * n_pages + jnp.minimum(blk, n_past_blocks - 1) // 2]
            return (layer, page, 0, g, 0, 0)
        return pl.BlockSpec((None, None, 2, None, HEAD_DIM, PAGE_SIZE), index)

    per_bg = lambda shape: pl.BlockSpec((None, None) + shape, lambda b, g, q, pt, idx: (b, g) + (0,) * len(shape))
    per_bgq = lambda shape: pl.BlockSpec((None, None, None) + shape,
                                         lambda b, g, q, pt, idx: (b, g, q) + (0,) * len(shape))
    grid_spec = pltpu.PrefetchScalarGridSpec(
        num_scalar_prefetch=2,
        grid=(batch, KV_GROUPS, t_new),
        in_specs=[block_spec(sl) for sl in range(SEL_TOPN)] + [
            per_bgq((HEAD_PAD, HEAD_DIM)),
            pl.BlockSpec((None, HEAD_PAD, LANES), lambda b, g, q, pt, idx: (g, 0, 0)),
            per_bg((2, LANES, HEAD_DIM)),
            pl.BlockSpec((None, None, 2, None, HEAD_DIM, buf), lambda b, g, q, pt, idx: (layer, b, 0, g, 0, 0)),
            per_bg((2, LANES, HEAD_DIM)),
            per_bg((2, HEAD_DIM, LANES)),
            per_bgq((HEAD_PAD, HEAD_DIM)),
            per_bgq((HEAD_PAD, LANES)),
        ],
        out_specs=[
            per_bgq((HEAD_PAD, HEAD_DIM)),
            pl.BlockSpec((None, 2, None, HEAD_DIM, buf), lambda b, g, q, pt, idx: (b, 0, g, 0, 0)),
        ],
    )
    return pl.pallas_call(
        functools.partial(_nsa_sample_attend_kernel, past_len=past_len, t_new=t_new, n_past_blocks=n_past_blocks),
        grid_spec=grid_spec,
        out_shape=[jax.ShapeDtypeStruct((batch, KV_GROUPS, t_new, HEAD_PAD, HEAD_DIM), F32),
                   jax.ShapeDtypeStruct((batch, 2, KV_GROUPS, HEAD_DIM, buf), F32)],
        compiler_params=_cparams(("parallel", "arbitrary", "arbitrary")),
        name="nsa_sample_attend",
    )(page_table_flat, idx_flat, *([cache_t] * SEL_TOPN), q_heads, slopes, sel_new, win_t, win_new,
      win_new_t, o_c, gates)


def nsa_sample_layer(x, gain, w, cmp_t, sel_t, win_t, layer, page_table_flat, *, batch, t_new, past_len):
    m = batch * t_new
    proj, gates = _nsa_project(x, gain, w, tm=m, head_norm=True)
    cmp = compress_pages(cmp_t.reshape(cmp_t.shape[0], cmp_t.shape[1], NSA_KV_DIM, PAGE_SIZE), layer,
                         page_table_flat, w["pe"], w["w1"], w["w2"], w["kgain"], batch=batch,
                         n_pages=past_len // PAGE_SIZE)
    q = proj[:, :NSA_Q_DIM].reshape(batch, t_new, KV_GROUPS, GROUP_SIZE, HEAD_DIM)
    q_stack = jnp.pad(q.transpose(0, 2, 3, 1, 4), ((0, 0), (0, 0), (0, 0), (0, Q_PAD - t_new), (0, 0)))
    o_c, idx = nsa_sample_select(q_stack.reshape(batch, KV_GROUPS, GROUP_SIZE * Q_PAD, HEAD_DIM), cmp,
                                 batch=batch, past_len=past_len, t_new=t_new)
    pad_heads = ((0, 0), (0, 0), (0, 0), (0, HEAD_PAD - GROUP_SIZE), (0, 0))
    o_c = o_c.reshape(batch, KV_GROUPS, GROUP_SIZE, Q_PAD, HEAD_DIM)[:, :, :, :t_new].transpose(0, 1, 3, 2, 4)
    idx_flat = idx.reshape(batch, SEL_TOPN, KV_GROUPS, Q_PAD)[..., :t_new].transpose(0, 2, 3, 1).reshape(-1)
    q_heads = jnp.pad(q.transpose(0, 2, 1, 3, 4), pad_heads)

    def new_rows(cols):
        a = cols.reshape(batch, t_new, 2, KV_GROUPS, HEAD_DIM).transpose(0, 3, 2, 1, 4)
        return jnp.pad(a, ((0, 0), (0, 0), (0, 0), (0, LANES - t_new), (0, 0)))

    kvs = proj[:, NSA_Q_DIM + NSA_KV_DIM:NSA_Q_DIM + 2 * NSA_KV_DIM]
    kvw = proj[:, NSA_Q_DIM + 2 * NSA_KV_DIM:]
    win_new_t = kvw.reshape(batch, t_new, 2, KV_GROUPS, HEAD_DIM).transpose(0, 3, 2, 4, 1)
    win_new_t = jnp.pad(win_new_t, ((0, 0), (0, 0), (0, 0), (0, 0), (LANES - t_new, 0)))
    g_arr = gates[:, :3 * NSA_HEADS].reshape(batch, t_new, 3, KV_GROUPS, GROUP_SIZE).transpose(0, 3, 1, 4, 2)
    g_arr = jnp.pad(g_arr, ((0, 0), (0, 0), (0, 0), (0, HEAD_PAD - GROUP_SIZE), (0, LANES - 3)))
    slopes = np.zeros((KV_GROUPS, HEAD_PAD, LANES), np.float32)
    slopes[:, :GROUP_SIZE, :] = np.asarray(_SLOPES, np.float32).reshape(KV_GROUPS, GROUP_SIZE, 1)
    o, win_out = nsa_sample_attend(sel_t, win_t, layer, page_table_flat, idx_flat, q_heads, jnp.asarray(slopes),
                                   new_rows(kvs), new_rows(kvw), win_new_t, jnp.pad(o_c, pad_heads), g_arr,
                                   batch=batch, past_len=past_len, t_new=t_new)
    o = o[:, :, :, :GROUP_SIZE].transpose(0, 2, 1, 3, 4).reshape(m, NSA_Q_DIM)
    return matmul_residual(o.astype(BF16), w["w_out"], x, tm=m), proj, win_out


def _ssd_kernel(z_ref, x_ref, bc_ref, dt_ref, cw_ref, cb_ref, dtb_ref, alog_ref, drow_ref, nw_ref, cinit_ref,
                hinit_ref, tril_ref, e_ref, et_ref, y_ref, hout_ref, stage_ref, h_ref, yacc_ref, *, t_valid):
    L = SSM_CHUNK
    c = pl.program_id(1)

    @pl.when(c == 0)
    def _():
        stage_ref[0:8, :] = cinit_ref[...]
        h_ref[...] = hinit_ref[...]

    stage_ref[8:8 + L, 0:SSM_D_INNER] = x_ref[...]
    stage_ref[8:8 + L, SSM_D_INNER:] = bc_ref[...]
    conv = cb_ref[...]
    for k in range(SSM_CONV):
        conv = conv + stage_ref[5 + k:5 + k + L, :] * cw_ref[k:k + 1, :]
    stage_ref[0:8, :] = stage_ref[L:L + 8, :]
    conv = conv * jax.nn.sigmoid(conv)
    xh = conv[:, :SSM_D_INNER]

    row = lax.broadcasted_iota(jnp.int32, (L, LANES), 0)
    dt_in = dt_ref[...] + dtb_ref[...]
    dt = jnp.maximum(dt_in, 0.0) + jnp.log1p(jnp.exp(-jnp.abs(dt_in)))
    dt = jnp.where(row < t_valid, dt, 0.0)
    da = dt * -jnp.exp(alog_ref[...])
    acum = _dot(tril_ref[...], da, HIGHEST)
    acum_t = jnp.transpose(acum)
    acum_last = acum[L - 1:L, :]
    dt_full = _dot(dt, e_ref[...], HIGHEST)
    grow_full = _dot(jnp.exp(acum), e_ref[...], HIGHEST)
    toend_full = _dot(jnp.exp(acum_last - acum), e_ref[...], HIGHEST)
    last_rows = jnp.broadcast_to(jnp.exp(acum_last), (LANES, LANES))
    xdt = xh * dt_full
    causal = lax.broadcasted_iota(jnp.int32, (L, L), 0) >= lax.broadcasted_iota(jnp.int32, (L, L), 1)

    for g in range(SSM_GROUPS):
        gl, gh = g * SSM_GW, (g + 1) * SSM_GW
        bm = conv[:, SSM_D_INNER + g * SSM_STATE:SSM_D_INNER + (g + 1) * SSM_STATE].astype(BF16)
        cm = conv[:, SSM_D_INNER + (SSM_GROUPS + g) * SSM_STATE:
                  SSM_D_INNER + (SSM_GROUPS + g + 1) * SSM_STATE].astype(BF16)
        cb = _nt(cm, bm)
        xdt_g = xdt[:, gl:gh]
        for r in range(SSM_HPG):
            h = g * SSM_HPG + r
            seg = acum[:, h:h + 1] - acum_t[h:h + 1, :]
            decay = jnp.exp(jnp.where(causal, seg, NEG_INF))
            yacc_ref[:, gl + r * SSM_HEAD_DIM:gl + (r + 1) * SSM_HEAD_DIM] = _dot(
                (cb * decay).astype(BF16), xdt_g[:, r * SSM_HEAD_DIM:(r + 1) * SSM_HEAD_DIM].astype(BF16))
        h_g = h_ref[gl:gh, :]
        y_state = _nt(cm, h_g.astype(BF16)) * grow_full[:, gl:gh]
        yacc_ref[:, gl:gh] = yacc_ref[:, gl:gh] + y_state
        xw = (xdt_g * toend_full[:, gl:gh]).astype(BF16)
        h_mul = _nt(et_ref[gl:gh, :], last_rows, HIGHEST)
        h_ref[gl:gh, :] = h_g * h_mul + lax.dot_general(xw, bm, (((0,), (0,)), ((), ())),
                                                        preferred_element_type=F32)

    zf = z_ref[...]
    gated = (yacc_ref[...] + xh * drow_ref[...]) * (zf * jax.nn.sigmoid(zf))
    for g in range(SSM_GROUPS):
        gl, gh = g * SSM_GW, (g + 1) * SSM_GW
        y_ref[:, gl:gh] = _rms(gated[:, gl:gh], nw_ref[:, gl:gh]).astype(y_ref.dtype)

    @pl.when(c == pl.num_programs(1) - 1)
    def _():
        hout_ref[...] = h_ref[...]


def ssd_mixer(proj, dt_raw, conv_init, h_init, w, *, batch, n_chunks, t_valid):
    L = SSM_CHUNK
    rows = batch * n_chunks * L
    conv_dim = SSM_D_INNER + SSM_BC
    tril = np.tril(np.ones((L, L), np.float32))
    e = np.zeros((LANES, SSM_D_INNER), np.float32)
    e[np.arange(SSM_D_INNER) // SSM_HEAD_DIM, np.arange(SSM_D_INNER)] = 1.0
    const = lambda shape: pl.BlockSpec(shape, lambda b, c: (0,) * len(shape))
    return pl.pallas_call(
        functools.partial(_ssd_kernel, t_valid=t_valid),
        grid=(batch, n_chunks),
        in_specs=[
            pl.BlockSpec((L, SSM_D_INNER), lambda b, c: (b * n_chunks + c, 0)),
            pl.BlockSpec((L, SSM_D_INNER), lambda b, c: (b * n_chunks + c, 1)),
            pl.BlockSpec((L, SSM_BC), lambda b, c: (b * n_chunks + c, 2 * SSM_D_INNER // SSM_BC)),
            pl.BlockSpec((L, LANES), lambda b, c: (b * n_chunks + c, 0)),
            const((SSM_CONV, conv_dim)), const((1, conv_dim)), const((1, LANES)), const((1, LANES)),
            const((1, SSM_D_INNER)), const((1, SSM_D_INNER)),
            pl.BlockSpec((None, 8, conv_dim), lambda b, c: (b, 0, 0)),
            pl.BlockSpec((None, SSM_D_INNER, SSM_STATE), lambda b, c: (b, 0, 0)),
            const((L, L)), const((LANES, SSM_D_INNER)), const((SSM_D_INNER, LANES)),
        ],
        out_specs=[
            pl.BlockSpec((L, SSM_D_INNER), lambda b, c: (b * n_chunks + c, 0)),
            pl.BlockSpec((None, SSM_D_INNER, SSM_STATE), lambda b, c: (b, 0, 0)),
        ],
        out_shape=[jax.ShapeDtypeStruct((rows, SSM_D_INNER), BF16),
                   jax.ShapeDtypeStruct((batch, SSM_D_INNER, SSM_STATE), F32)],
        scratch_shapes=[pltpu.VMEM((L + 8, conv_dim), F32), pltpu.VMEM((SSM_D_INNER, SSM_STATE), F32),
                        pltpu.VMEM((L, SSM_D_INNER), F32)],
        compiler_params=_cparams(("parallel", "arbitrary")),
        name="ssd_mixer",
    )(proj, proj, proj, dt_raw, w["conv_w"], w["conv_b"], w["dt_bias"], w["a_log"], w["d_row"], w["norm_w"],
      conv_init, h_init, jnp.asarray(tril), jnp.asarray(e), jnp.asarray(e.T))


def _ssd_weights(w_in, conv_w, conv_b, dt_bias, a_log, d_skip, norm_w, w_out):
    n_main = 2 * SSM_D_INNER + SSM_BC
    pad = lambda v: jnp.pad(v, (0, LANES - SSM_HEADS)).reshape(1, LANES)
    return dict(
        w_main=w_in[:, :n_main].astype(BF16),
        w_dt=jnp.pad(w_in[:, n_main:], ((0, 0), (0, LANES - SSM_HEADS))).astype(BF16),
        conv_w=conv_w, conv_b=conv_b.reshape(1, -1), dt_bias=pad(dt_bias), a_log=pad(a_log),
        d_row=jnp.repeat(d_skip, SSM_HEAD_DIM).reshape(1, SSM_D_INNER), norm_w=norm_w.reshape(1, SSM_D_INNER),
        w_out=w_out.astype(BF16))


def ssd_layer(x, gain, w, conv_state, h_init, *, batch, t):
    tm = min(1024, x.shape[0])
    proj = norm_matmul(x, gain, w["w_main"], tm=tm, tn=512)
    dt_raw = norm_matmul(x, gain, w["w_dt"], tm=tm, tn=LANES)
    conv_init = jnp.pad(conv_state, ((0, 0), (8 - (SSM_CONV - 1), 0), (0, 0)))
    if t % SSM_CHUNK == 0:
        y, h_last = ssd_mixer(proj, dt_raw, conv_init, h_init, w, batch=batch, n_chunks=t // SSM_CHUNK,
                              t_valid=SSM_CHUNK)
    else:
        pad_rows = lambda a: jnp.pad(a.reshape(batch, t, -1), ((0, 0), (0, SSM_CHUNK - t), (0, 0))).reshape(
            batch * SSM_CHUNK, -1)
        y, h_last = ssd_mixer(pad_rows(proj), pad_rows(dt_raw), conv_init, h_init, w, batch=batch, n_chunks=1,
                              t_valid=t)
        y = y.reshape(batch, SSM_CHUNK, -1)[:, :t].reshape(batch * t, -1)
    return matmul_residual(y, w["w_out"], x, tm=tm), proj, h_last


def kernel(x_prompt, x_sample, cache_kv_cmp, cache_kv_sel, cache_kv_win, state_conv, state_ssm, page_table,
           norm_mix, norm_ffn, nsa_w_in, nsa_qk_gain, nsa_cmp_pe, nsa_cmp_w1, nsa_cmp_w2, nsa_w_out,
           ssd_w_in, ssd_conv_w, ssd_conv_b, ssd_dt_bias, ssd_a_log, ssd_d, ssd_norm, ssd_w_out,
           ffn_w_gate, ffn_w_up, ffn_w_down):
    bp, tp = x_prompt.shape[0], x_prompt.shape[1]
    bs, ts = x_sample.shape[0], x_sample.shape[1]
    depth = norm_mix.shape[0]
    past_len = page_table.shape[1] * PAGE_SIZE
    conv_dim = SSM_D_INNER + SSM_BC
    assert tp % SSM_CHUNK == 0 and ts < SSM_CHUNK and ts <= Q_PAD and ts >= SSM_CONV - 1
    xp = x_prompt.reshape(bp * tp, D_MODEL)
    xs = x_sample.reshape(bs * ts, D_MODEL)
    pt_flat = page_table.reshape(-1)
    feature_major = lambda a: a.transpose(0, 1, 3, 4, 5, 2)
    cmp_t, sel_t, win_t = feature_major(cache_kv_cmp), feature_major(cache_kv_sel), feature_major(cache_kv_win)
    kv_shape = (2, KV_GROUPS, HEAD_DIM)
    outs = {k: [] for k in ("cmp_p", "sel_p", "win_p", "conv_p", "ssm_p", "cmp_s", "sel_s", "win_s", "conv_s",
                            "ssm_s")}
    for i in range(depth):
        l = i // 2
        if i % 2 == 0:
            w = _nsa_weights(nsa_w_in[l], nsa_qk_gain[l], nsa_cmp_pe[l], nsa_cmp_w1[l], nsa_cmp_w2[l], nsa_w_out[l])
            xp, cmp_rows, sel_rows, win_rows = nsa_prompt_layer(xp, norm_mix[i], w, batch=bp, t=tp)
            row_major = lambda a: a.reshape((bp,) + kv_shape + (a.shape[-1],)).transpose(0, 4, 1, 2, 3)
            outs["cmp_p"].append(row_major(cmp_rows))
            outs["sel_p"].append(row_major(sel_rows))
            outs["win_p"].append(row_major(win_rows))
            kv = lambda a, b, t, j: a[:, NSA_Q_DIM + j * NSA_KV_DIM:NSA_Q_DIM + (j + 1) * NSA_KV_DIM].reshape(
                (b, t) + kv_shape)
            xs, proj_s, win_out = nsa_sample_layer(xs, norm_mix[i], w, cmp_t, sel_t, win_t, l, pt_flat, batch=bs,
                                                   t_new=ts, past_len=past_len)
            outs["cmp_s"].append(kv(proj_s, bs, ts, 0))
            outs["sel_s"].append(kv(proj_s, bs, ts, 1))
            outs["win_s"].append(win_out.transpose(0, 4, 1, 2, 3))
        else:
            w = _ssd_weights(ssd_w_in[l], ssd_conv_w[l], ssd_conv_b[l], ssd_dt_bias[l], ssd_a_log[l], ssd_d[l],
                             ssd_norm[l], ssd_w_out[l])
            xp, proj, h_last = ssd_layer(xp, norm_mix[i], w, jnp.zeros((bp, SSM_CONV - 1, conv_dim), F32),
                                         jnp.zeros((bp, SSM_D_INNER, SSM_STATE), F32), batch=bp, t=tp)
            xbc = proj[:, SSM_D_INNER:].reshape(bp, tp, conv_dim)
            outs["conv_p"].append(xbc[:, tp - (SSM_CONV - 1):])
            outs["ssm_p"].append(h_last.reshape(bp, SSM_HEADS, SSM_HEAD_DIM, SSM_STATE))
            xs, proj_s, h_last = ssd_layer(xs, norm_mix[i], w, state_conv[l],
                                           state_ssm[l].reshape(bs, SSM_D_INNER, SSM_STATE), batch=bs, t=ts)
            xbc = proj_s[:, SSM_D_INNER:].reshape(bs, ts, conv_dim)
            outs["conv_s"].append(xbc[:, ts - (SSM_CONV - 1):])
            outs["ssm_s"].append(h_last.reshape(bs, SSM_HEADS, SSM_HEAD_DIM, SSM_STATE))
        wg, wu, wd = ffn_w_gate[i].astype(BF16), ffn_w_up[i].astype(BF16), ffn_w_down[i].astype(BF16)
        xp = ffn(xp, norm_ffn[i], wg, wu, wd, tm=1024, th=256)
        xs = ffn(xs, norm_ffn[i], wg, wu, wd, tm=bs * ts, th=256)
    order = ("cmp_p", "sel_p", "win_p", "conv_p", "ssm_p", "cmp_s", "sel_s", "win_s", "conv_s", "ssm_s")
    return (xp.reshape(bp, tp, D_MODEL), xs.reshape(bs, ts, D_MODEL)) + tuple(jnp.stack(outs[k]) for k in order)
```

```python
import functools
import math

import jax
import jax.numpy as jnp
import ml_dtypes
import numpy as np
from jax import lax
from jax.experimental import pallas as pl
from jax.experimental.pallas import tpu as pltpu

F32 = jnp.float32
BF16 = jnp.bfloat16
HIGHEST = lax.Precision.HIGHEST

D_MODEL = 1024
PAGE_SIZE = 128
NSA_HEADS = 16
HEAD_DIM = 64
KV_GROUPS = 4
GROUP_SIZE = NSA_HEADS // KV_GROUPS
CMP_BLOCK = 32
CMP_STRIDE = 16
CMP_HIDDEN = 256
SEL_BLOCK = 64
SEL_CHUNKS = SEL_BLOCK // CMP_STRIDE
SEL_TOPN = 16
WINDOW = 512
NSA_Q_DIM = NSA_HEADS * HEAD_DIM
NSA_KV_DIM = 2 * KV_GROUPS * HEAD_DIM
KV_HALF = KV_GROUPS * HEAD_DIM
SSM_D_INNER = 2048
SSM_HEAD_DIM = 64
SSM_HEADS = 32
SSM_GROUPS = 4
SSM_HPG = SSM_HEADS // SSM_GROUPS
SSM_STATE = 128
SSM_CONV = 4
SSM_BC = 2 * SSM_GROUPS * SSM_STATE
SSM_GW = SSM_D_INNER // SSM_GROUPS
SSM_CHUNK = 128
NORM_EPS = 1e-6
NEG_INF = -1e30
FORCE_SCORE = 1e4
LANES = 128
VMEM_LIMIT = 56 * 1024 * 1024

_SLOPES = [float(np.float32((2.0 ** (-8.0 / NSA_HEADS)) ** (i + 1))) for i in range(NSA_HEADS)]


def _cparams(sem):
    return pltpu.CompilerParams(dimension_semantics=sem, vmem_limit_bytes=VMEM_LIMIT)


def _nt(a, b, precision=None):
    return lax.dot_general(a, b, (((1,), (1,)), ((), ())), preferred_element_type=F32, precision=precision)


def _dot(a, b, precision=None):
    return jnp.dot(a, b, preferred_element_type=F32, precision=precision)


def _rms(x, gain):
    return x * lax.rsqrt(jnp.mean(x * x, axis=-1, keepdims=True) + NORM_EPS) * gain


def _norm_matmul_kernel(x_ref, g_ref, w_ref, hg_ref, hm_ref, e_ref, et_ref, o_ref, xn_ref, *, head_norm):
    @pl.when(pl.program_id(1) == 0)
    def _():
        xn_ref[...] = _rms(x_ref[...], g_ref[...]).astype(BF16)

    y = _dot(xn_ref[...], w_ref[...])
    if head_norm:
        ss = _dot(y * y, e_ref[...], HIGHEST)
        inv = lax.rsqrt(ss * (1.0 / HEAD_DIM) + NORM_EPS)
        yn = y * _dot(inv, et_ref[...], HIGHEST) * hg_ref[...]
        y = jnp.where(hm_ref[...] > 0.5, yn, y)
    o_ref[...] = y


def norm_matmul(x, gain, w, head_gain=None, head_mask=None, *, tm, tn):
    m, k = x.shape
    n = w.shape[1]
    head_norm = head_gain is not None
    if not head_norm:
        head_gain = jnp.zeros((1, n), F32)
        head_mask = jnp.zeros((1, n), F32)
    lane = np.arange(tn)
    e = np.zeros((tn, LANES), np.float32)
    e[lane, lane // HEAD_DIM] = 1.0
    return pl.pallas_call(
        functools.partial(_norm_matmul_kernel, head_norm=head_norm),
        grid=(m // tm, n // tn),
        in_specs=[
            pl.BlockSpec((tm, k), lambda i, j: (i, 0)),
            pl.BlockSpec((1, k), lambda i, j: (0, 0)),
            pl.BlockSpec((k, tn), lambda i, j: (0, j)),
            pl.BlockSpec((1, tn), lambda i, j: (0, j)),
            pl.BlockSpec((1, tn), lambda i, j: (0, j)),
            pl.BlockSpec((tn, LANES), lambda i, j: (0, 0)),
            pl.BlockSpec((LANES, tn), lambda i, j: (0, 0)),
        ],
        out_specs=pl.BlockSpec((tm, tn), lambda i, j: (i, j)),
        out_shape=jax.ShapeDtypeStruct((m, n), F32),
        scratch_shapes=[pltpu.VMEM((tm, k), BF16)],
        compiler_params=_cparams(("parallel", "arbitrary")),
        name="norm_matmul",
    )(x, gain.reshape(1, k), w, head_gain, head_mask, jnp.asarray(e), jnp.asarray(e.T))


def _ffn_kernel(x_ref, g_ref, wg_ref, wu_ref, wd_ref, o_ref, xn_ref):
    @pl.when(pl.program_id(1) == 0)
    def _():
        x = x_ref[...]
        xn_ref[...] = _rms(x, g_ref[...]).astype(BF16)
        o_ref[...] = x

    h = xn_ref[...]
    gate = _dot(h, wg_ref[...])
    up = _dot(h, wu_ref[...])
    act = (gate * jax.nn.sigmoid(gate) * up).astype(BF16)
    o_ref[...] += _dot(act, wd_ref[...])


def ffn(x, gain, wg, wu, wd, *, tm, th):
    m, k = x.shape
    hid = wg.shape[1]
    return pl.pallas_call(
        _ffn_kernel,
        grid=(m // tm, hid // th),
        in_specs=[
            pl.BlockSpec((tm, k), lambda i, j: (i, 0)),
            pl.BlockSpec((1, k), lambda i, j: (0, 0)),
            pl.BlockSpec((k, th), lambda i, j: (0, j)),
            pl.BlockSpec((k, th), lambda i, j: (0, j)),
            pl.BlockSpec((th, k), lambda i, j: (j, 0)),
        ],
        out_specs=pl.BlockSpec((tm, k), lambda i, j: (i, 0)),
        out_shape=jax.ShapeDtypeStruct((m, k), F32),
        scratch_shapes=[pltpu.VMEM((tm, k), BF16)],
        compiler_params=_cparams(("parallel", "arbitrary")),
        name="ffn",
    )(x, gain.reshape(1, k), wg, wu, wd)


def _matmul_res_kernel(a_ref, w_ref, r_ref, o_ref):
    o_ref[...] = r_ref[...] + _dot(a_ref[...], w_ref[...])


def matmul_residual(a, w, res, *, tm):
    m, k = a.shape
    n = w.shape[1]
    return pl.pallas_call(
        _matmul_res_kernel,
        grid=(m // tm,),
        in_specs=[
            pl.BlockSpec((tm, k), lambda i: (i, 0)),
            pl.BlockSpec((k, n), lambda i: (0, 0)),
            pl.BlockSpec((tm, n), lambda i: (i, 0)),
        ],
        out_specs=pl.BlockSpec((tm, n), lambda i: (i, 0)),
        out_shape=jax.ShapeDtypeStruct((m, n), F32),
        compiler_params=_cparams(("parallel",)),
        name="matmul_residual",
    )(a, w, res)


CMP_PACK = 4


def _compress_accumulate(xs_ref, n_rows, pe_ref, w1_ref, kv):
    n_pair = NSA_KV_DIM // LANES
    low = lax.broadcasted_iota(jnp.int32, (n_rows, LANES), 1) < HEAD_DIM
    acc = [None, None]
    for jp in range(CMP_STRIDE // CMP_PACK):
        groups = []
        for c in range(kv * n_pair // 2, (kv + 1) * n_pair // 2):
            rows = [xs_ref.at[c][pl.ds(jp * CMP_PACK + i, n_rows, stride=CMP_STRIDE), :] for i in range(CMP_PACK)]
            swapped = [pltpu.roll(r, HEAD_DIM, axis=1) for r in rows]
            groups.append(jnp.concatenate([jnp.where(low, rows[i], swapped[i + 1])
                                           for i in range(0, CMP_PACK, 2)], axis=1))
            groups.append(jnp.concatenate([jnp.where(low, swapped[i], rows[i + 1])
                                           for i in range(0, CMP_PACK, 2)], axis=1))
        xs = jnp.concatenate(groups, axis=0)
        for m in range(2):
            lhs = (xs + pe_ref[kv, m, jp:jp + 1, :]).astype(BF16)
            part = _dot(lhs, w1_ref[kv, m, jp])
            acc[m] = part if acc[m] is None else acc[m] + part
    return acc


def _compress_finish(acc, w2_ref, kg_ref, kv):
    rows = acc[0].shape[0]
    pre = acc[0] + pltpu.roll(acc[1], rows - 1, axis=0)
    hid = (pre * jax.nn.sigmoid(pre)).astype(BF16)
    out = _dot(hid, w2_ref[kv])
    if kv == 0:
        out = _rms(out, kg_ref[...])
    return out


def _compress_rows_kernel(x_ref, pe_ref, w1_ref, w2_ref, kg_ref, o_ref, xt_ref, xs_ref, *, n_chunk):
    xt_ref[...] = jnp.transpose(x_ref[...])
    n_pair = NSA_KV_DIM // LANES
    for c in range(n_pair):
        xs_ref[c] = x_ref[:, c * LANES:(c + 1) * LANES]
    for kv in range(2):
        o_ref[kv] = _compress_finish(_compress_accumulate(xs_ref, n_chunk, pe_ref, w1_ref, kv), w2_ref, kg_ref, kv)


def compress_rows(proj, col_block, pe, w1, w2, kgain, *, batch, t):
    n_chunk = t // CMP_STRIDE
    return pl.pallas_call(
        functools.partial(_compress_rows_kernel, n_chunk=n_chunk),
        grid=(batch,),
        in_specs=[
            pl.BlockSpec((t, NSA_KV_DIM), lambda b: (b, col_block)),
            pl.BlockSpec(pe.shape, lambda b: (0, 0, 0, 0)),
            pl.BlockSpec(w1.shape, lambda b: (0, 0, 0, 0, 0)),
            pl.BlockSpec(w2.shape, lambda b: (0, 0, 0)),
            pl.BlockSpec((1, HEAD_DIM), lambda b: (0, 0)),
        ],
        out_specs=[pl.BlockSpec((None, 2, KV_GROUPS * n_chunk, HEAD_DIM), lambda b: (b, 0, 0, 0)),
                   pl.BlockSpec((None, NSA_KV_DIM, t), lambda b: (b, 0, 0))],
        out_shape=[jax.ShapeDtypeStruct((batch, 2, KV_GROUPS * n_chunk, HEAD_DIM), F32),
                   jax.ShapeDtypeStruct((batch, NSA_KV_DIM, t), F32)],
        scratch_shapes=[pltpu.VMEM((NSA_KV_DIM // LANES, t, LANES), F32)],
        compiler_params=_cparams(("parallel",)),
        name="compress_rows",
    )(proj, pe, w1, w2, kgain.reshape(1, HEAD_DIM))


def _slope_col(g, rows, tq):
    rblk = lax.broadcasted_iota(jnp.int32, (rows, 1), 0) // tq
    col = jnp.full((rows, 1), _SLOPES[GROUP_SIZE * g + GROUP_SIZE - 1], F32)
    for r in range(GROUP_SIZE - 1):
        col = jnp.where(rblk == r, _SLOPES[GROUP_SIZE * g + r], col)
    return col


def _normalize(acc, l):
    return acc * jnp.where(l > 0.0, 1.0 / jnp.where(l > 0.0, l, 1.0), 0.0)


MASK_BIG = 2.0 ** 100
LOG2E = 1.4426950408889634
AUG = 64
AUG_POS = 32


def _bf16_terms(x, n=3):
    x = np.asarray(x, np.float64)
    terms = []
    for _ in range(n):
        term = x.astype(ml_dtypes.bfloat16).astype(np.float64)
        terms.append(term.astype(np.float32))
        x = x - term
    return terms


def _key_aug(pos, flags=None):
    pos = np.asarray(pos, np.int64)
    aug = np.zeros((pos.shape[0], AUG), np.float32)
    if flags is not None:
        aug[:, :flags.shape[1]] = flags
    coarse = (pos // SEL_BLOCK) * SEL_BLOCK
    aug[:, AUG_POS:AUG_POS + 3] = coarse[:, None]
    aug[:, AUG_POS + 3:AUG_POS + 6] = (pos - coarse)[:, None]
    return aug


def _query_aug_rows(tq):
    rows = np.zeros((KV_GROUPS, AUG - AUG_POS, GROUP_SIZE * tq), np.float32)
    for h in range(NSA_HEADS):
        terms = _bf16_terms(np.float64(_SLOPES[h]) * LOG2E)
        g, r = divmod(h, GROUP_SIZE)
        for i, term in enumerate(terms):
            rows[g, i, r * tq:(r + 1) * tq] = term
            rows[g, 3 + i, r * tq:(r + 1) * tq] = term
    return rows


def _online_update(s, v_t, m, l, acc):
    m_new = jnp.maximum(m, jnp.max(s, axis=0, keepdims=True))
    alpha = jnp.exp2(m - m_new)
    p = jnp.exp2(s - m_new)
    return m_new, alpha * l + jnp.sum(p, axis=0, keepdims=True), alpha * acc + _dot(v_t, p.astype(BF16))


def _nsa_prompt_kernel(q_ref, ks_ref, kw_ref, cmp_ref, gate_ref, gq_ref, gks_ref, gkw_ref, gksc_ref, gkwc_ref,
                       csel_ref, cwin_ref, ccmp_ref, qrows_ref, mmap_ref, eye_ref,
                       o_ref, kst_out_ref, kwt_out_ref,
                       ksa_ref, kwa_ref, vst_ref, vwt_ref, kca_ref, vct_ref, ot_ref, *, t, tq):
    qi = pl.program_id(1)
    n_sel = t // SEL_BLOCK
    n_cmp = t // CMP_STRIDE
    top_n = min(SEL_TOPN, n_sel)
    cols = GROUP_SIZE * tq
    nt = t // tq
    n_win = WINDOW // tq + 1
    n_pad = WINDOW // tq

    @pl.when(qi == 0)
    def _():
        def prepare(src_ref, grow_ref, gcol_ref, const_ref, ka_ref, vt_ref, out_ref, pad_tiles):
            pad = pad_tiles * tq
            x = src_ref[...]
            xt = jnp.transpose(x)
            width = out_ref.shape[1]
            out_ref[KV_HALF:, :] = xt[KV_HALF:, t - width:]
            for g in range(KV_GROUPS):
                lo, hi = g * HEAD_DIM, (g + 1) * HEAD_DIM
                kn = _rms(x[:, lo:hi], grow_ref[...])
                ka_ref[g, pad:pad + t, :] = jnp.concatenate([kn, const_ref[pad:pad + t, :]], axis=1).astype(BF16)
                if pad:
                    ka_ref[g, 0:pad, :] = jnp.concatenate(
                        [jnp.zeros((pad, HEAD_DIM), F32), const_ref[0:pad, :]], axis=1).astype(BF16)
                kt = xt[lo:hi, :]
                ktn = kt * lax.rsqrt(jnp.mean(kt * kt, axis=0, keepdims=True) + NORM_EPS) * gcol_ref[...]
                out_ref[lo:hi, :] = ktn[:, t - width:]
                vt = xt[KV_HALF + lo:KV_HALF + hi, :].astype(BF16)
                base = g * (nt + pad_tiles)
                for i in range(pad_tiles):
                    vt_ref[base + i] = jnp.zeros((HEAD_DIM, tq), BF16)
                for i in range(nt):
                    vt_ref[base + pad_tiles + i] = vt[:, i * tq:(i + 1) * tq]

        prepare(ks_ref, gks_ref, gksc_ref, csel_ref, ksa_ref, vst_ref, kst_out_ref, 0)
        prepare(kw_ref, gkw_ref, gkwc_ref, cwin_ref, kwa_ref, vwt_ref, kwt_out_ref, n_pad)
        for g in range(KV_GROUPS):
            kc = cmp_ref[0, g * n_cmp:(g + 1) * n_cmp, :]
            kca_ref[g] = jnp.concatenate([kc, ccmp_ref[...]], axis=1).astype(BF16)
            vct_ref[g] = _nt(eye_ref[...], cmp_ref[1, g * n_cmp:(g + 1) * n_cmp, :].astype(BF16)).astype(BF16)

    q0 = pl.multiple_of(qi * tq, tq)
    lane4 = lax.broadcasted_iota(jnp.int32, (1, cols), 1) % tq
    key_off = lax.broadcasted_iota(jnp.int32, (tq, cols), 0)
    causal = key_off <= lane4
    win_old = key_off > lane4
    cmp_ok = (lax.broadcasted_iota(jnp.int32, (n_cmp, cols), 0) * CMP_STRIDE + (CMP_BLOCK - 1)) <= q0 + lane4
    qt = jnp.transpose(q_ref[...])
    gt = jax.nn.sigmoid(jnp.transpose(gate_ref[...]))
    jblk = lax.broadcasted_iota(jnp.int32, (n_sel, tq), 0)
    t_lane = q0 + lax.broadcasted_iota(jnp.int32, (n_sel, tq), 1)
    cur = t_lane // SEL_BLOCK
    forced = (jblk == 0) | (jblk == cur) | (jblk == cur - 1)
    visible = jblk * SEL_BLOCK <= t_lane
    init = (jnp.full((1, cols), -MASK_BIG, F32), jnp.zeros((1, cols), F32), jnp.zeros((HEAD_DIM, cols), F32))

    o_c, q_sel, q_win = [], [], []
    for g in range(KV_GROUPS):
        heads = []
        for r in range(GROUP_SIZE):
            h = GROUP_SIZE * g + r
            qh = qt[h * HEAD_DIM:(h + 1) * HEAD_DIM, :]
            qh = qh * lax.rsqrt(jnp.mean(qh * qh, axis=0, keepdims=True) + NORM_EPS)
            heads.append(qh * (gq_ref[...] * (HEAD_DIM ** -0.5 * LOG2E)))
        q_top = jnp.concatenate(heads, axis=1)
        q_low = qrows_ref[g]

        def with_mask_rows(rows_, q_top=q_top, q_low=q_low):
            return jnp.concatenate([q_top, rows_, q_low], axis=0).astype(BF16)

        s = _dot(kca_ref[g], with_mask_rows(jnp.zeros((AUG_POS, cols), F32)))
        s = jnp.where(cmp_ok, s, -MASK_BIG)
        p = jnp.where(cmp_ok, jnp.exp2(s - jnp.max(s, axis=0, keepdims=True)), 0.0)
        pc = _normalize(p, jnp.sum(p, axis=0, keepdims=True))
        o_c.append(_dot(vct_ref[g], pc.astype(BF16)))
        imp = pc[:, 0:tq]
        for r in range(1, GROUP_SIZE):
            imp = imp + pc[:, r * tq:(r + 1) * tq]

        p_slc = _dot(mmap_ref[...], imp, HIGHEST)
        score = jnp.where(forced, FORCE_SCORE, jnp.where(visible, p_slc, -1.0))
        cnt = jnp.zeros((n_sel, tq), jnp.int32)
        for k in range(n_sel):
            sk = score[k:k + 1, :]
            beats = (sk > score) | ((sk == score) & (jblk > k))
            cnt = cnt + beats.astype(jnp.int32)
        drop = jnp.where(cnt < top_n, 0.0, -MASK_BIG)
        if n_sel < AUG_POS:
            drop = jnp.concatenate([drop, jnp.zeros((AUG_POS - n_sel, tq), F32)], axis=0)
        q_sel.append(with_mask_rows(jnp.concatenate([drop] * GROUP_SIZE, axis=1)))
        q_win.append(with_mask_rows(jnp.full((AUG_POS, cols), -MASK_BIG, F32)))

    def sel_scores(k0):
        return [_dot(ksa_ref[g, pl.ds(k0, tq), :], q_sel[g]) for g in range(KV_GROUPS)]

    def sel_body(i, carry):
        scores = sel_scores(pl.multiple_of(i * tq, tq))
        return tuple(_online_update(scores[g], vst_ref[g * nt + i], *carry[g]) for g in range(KV_GROUPS))

    carry = lax.fori_loop(0, qi, sel_body, (init,) * KV_GROUPS)
    o_s = []
    for g, s in enumerate(sel_scores(q0)):
        _, l, acc = _online_update(jnp.where(causal, s, -MASK_BIG), vst_ref[g * nt + qi], *carry[g])
        o_s.append(_normalize(acc, l))

    win_scores = [[_dot(kwa_ref[g, pl.ds(q0 + i * tq, tq), :], q_win[g]) for i in range(n_win)]
                  for g in range(KV_GROUPS)]
    o_w = []
    for g in range(KV_GROUPS):
        tiles = win_scores[g]
        tiles[0] = jnp.where(win_old, tiles[0], -MASK_BIG)
        tiles[-1] = jnp.where(causal, tiles[-1], -MASK_BIG)
        m = functools.reduce(jnp.maximum, [jnp.max(s, axis=0, keepdims=True) for s in tiles])
        probs = [jnp.exp2(s - m) for s in tiles]
        l = functools.reduce(jnp.add, [jnp.sum(p, axis=0, keepdims=True) for p in probs])
        acc = functools.reduce(jnp.add, [_dot(vwt_ref[g * (nt + n_pad) + qi + i], p.astype(BF16))
                                         for i, p in enumerate(probs)])
        o_w.append(_normalize(acc, l))

    for h in range(NSA_HEADS):
        g, r = divmod(h, GROUP_SIZE)
        sl = slice(r * tq, (r + 1) * tq)
        ot_ref[h * HEAD_DIM:(h + 1) * HEAD_DIM, :] = (
            gt[h:h + 1, :] * o_c[g][:, sl] + gt[NSA_HEADS + h:NSA_HEADS + h + 1, :] * o_s[g][:, sl]
            + gt[2 * NSA_HEADS + h:2 * NSA_HEADS + h + 1, :] * o_w[g][:, sl])
    o_ref[...] = jnp.transpose(ot_ref[...]).astype(o_ref.dtype)


def nsa_prompt_attention(proj, gates, cmp, qk_gain, *, batch, t, tq):
    n_sel = t // SEL_BLOCK
    n_cmp = t // CMP_STRIDE
    nq = t // tq
    nt = t // tq
    n_pad = WINDOW // tq
    keep = min(WINDOW, t)
    assert n_sel <= AUG_POS and t % tq == 0 and WINDOW % tq == 0 and tq % SEL_BLOCK == 0
    n = np.arange(n_cmp)
    mmap = np.zeros((n_sel, n_cmp), np.float32)
    real = n < n_cmp - 1
    np.add.at(mmap, (n[real] // SEL_CHUNKS, n[real]), 1.0)
    nxt = real & ((n + 1) // SEL_CHUNKS < n_sel)
    np.add.at(mmap, ((n[nxt] + 1) // SEL_CHUNKS, n[nxt]), 1.0)
    pos = np.arange(t)
    csel = _key_aug(pos, (pos[:, None] // SEL_BLOCK == np.arange(n_sel)[None, :]).astype(np.float32))
    wpos = np.arange(-WINDOW, t)
    cwin = _key_aug(wpos, (wpos < 0).astype(np.float32)[:, None])
    ccmp = _key_aug(np.arange(n_cmp) * CMP_STRIDE + (CMP_BLOCK - 1))
    const = lambda shape: pl.BlockSpec(shape, lambda b, i: (0,) * len(shape))
    col = lambda v: v.reshape(HEAD_DIM, 1)
    row = lambda v: v.reshape(1, HEAD_DIM)
    return pl.pallas_call(
        functools.partial(_nsa_prompt_kernel, t=t, tq=tq),
        grid=(batch, nq),
        in_specs=[
            pl.BlockSpec((tq, NSA_Q_DIM), lambda b, i: (b * nq + i, 0)),
            pl.BlockSpec((t, NSA_KV_DIM), lambda b, i: (b, 3)),
            pl.BlockSpec((t, NSA_KV_DIM), lambda b, i: (b, 4)),
            pl.BlockSpec((None, 2, KV_GROUPS * n_cmp, HEAD_DIM), lambda b, i: (b, 0, 0, 0)),
            pl.BlockSpec((tq, LANES), lambda b, i: (b * nq + i, 0)),
            const((HEAD_DIM, 1)), const((1, HEAD_DIM)), const((1, HEAD_DIM)), const((HEAD_DIM, 1)),
            const((HEAD_DIM, 1)),
            const((t, AUG)), const((t + WINDOW, AUG)), const((n_cmp, AUG)),
            const((KV_GROUPS, AUG - AUG_POS, GROUP_SIZE * tq)), const((n_sel, n_cmp)), const((HEAD_DIM, HEAD_DIM)),
        ],
        out_specs=[
            pl.BlockSpec((tq, NSA_Q_DIM), lambda b, i: (b * nq + i, 0)),
            pl.BlockSpec((None, NSA_KV_DIM, t), lambda b, i: (b, 0, 0)),
            pl.BlockSpec((None, NSA_KV_DIM, keep), lambda b, i: (b, 0, 0)),
        ],
        out_shape=[jax.ShapeDtypeStruct((batch * t, NSA_Q_DIM), BF16),
                   jax.ShapeDtypeStruct((batch, NSA_KV_DIM, t), F32),
                   jax.ShapeDtypeStruct((batch, NSA_KV_DIM, keep), F32)],
        scratch_shapes=[
            pltpu.VMEM((KV_GROUPS, t, HEAD_DIM + AUG), BF16),
            pltpu.VMEM((KV_GROUPS, t + WINDOW, HEAD_DIM + AUG), BF16),
            pltpu.VMEM((KV_GROUPS * nt, HEAD_DIM, tq), BF16),
            pltpu.VMEM((KV_GROUPS * (nt + n_pad), HEAD_DIM, tq), BF16),
            pltpu.VMEM((KV_GROUPS, n_cmp, HEAD_DIM + AUG), BF16),
            pltpu.VMEM((KV_GROUPS, HEAD_DIM, n_cmp), BF16),
            pltpu.VMEM((NSA_Q_DIM, tq), F32),
        ],
        compiler_params=_cparams(("parallel", "arbitrary")),
        name="nsa_prompt_attention",
    )(proj, proj, proj, cmp, gates, col(qk_gain[0]), row(qk_gain[2]), row(qk_gain[3]), col(qk_gain[2]),
      col(qk_gain[3]), jnp.asarray(csel), jnp.asarray(cwin), jnp.asarray(ccmp), jnp.asarray(_query_aug_rows(tq)),
      jnp.asarray(mmap), jnp.eye(HEAD_DIM, dtype=BF16))


def _nsa_weights(w_in, qk_gain, cmp_pe, cmp_w1, cmp_w2, w_out):
    n_main = NSA_Q_DIM + 3 * NSA_KV_DIM
    zeros_kv = jnp.zeros((KV_HALF,), F32)
    ones_kv = jnp.ones((KV_HALF,), F32)
    head_gain = jnp.concatenate([
        jnp.tile(qk_gain[0], NSA_HEADS), zeros_kv, zeros_kv,
        jnp.tile(qk_gain[2], KV_GROUPS), zeros_kv, jnp.tile(qk_gain[3], KV_GROUPS), zeros_kv]).reshape(1, n_main)
    head_mask = jnp.concatenate([
        jnp.ones((NSA_Q_DIM,), F32), zeros_kv, zeros_kv, ones_kv, zeros_kv, ones_kv, zeros_kv]).reshape(1, n_main)
    w_gate = jnp.pad(w_in[:, n_main:], ((0, 0), (0, LANES - 3 * NSA_HEADS)))
    return dict(
        w_main=w_in[:, :n_main].astype(BF16), w_gate=w_gate.astype(BF16), head_gain=head_gain, head_mask=head_mask,
        pe=cmp_pe.reshape(2, 2, CMP_STRIDE // CMP_PACK, CMP_PACK * HEAD_DIM),
        w1=cmp_w1.reshape(2, 2, CMP_STRIDE // CMP_PACK, CMP_PACK * HEAD_DIM, CMP_HIDDEN).astype(BF16),
        w2=cmp_w2.astype(BF16), kgain=qk_gain[1], qk_gain=qk_gain, w_out=w_out.astype(BF16))


def _nsa_project(x, gain, w, *, tm, head_norm):
    if head_norm:
        proj = norm_matmul(x, gain, w["w_main"], w["head_gain"], w["head_mask"], tm=tm, tn=512)
    else:
        proj = norm_matmul(x, gain, w["w_main"], tm=tm, tn=512)
    return proj, norm_matmul(x, gain, w["w_gate"], tm=tm, tn=LANES)


def nsa_prompt_layer(x, gain, w, *, batch, t):
    tm = min(1024, batch * t)
    proj, gates = _nsa_project(x, gain, w, tm=tm, head_norm=False)
    cmp, cmp_t = compress_rows(proj, 2, w["pe"], w["w1"], w["w2"], w["kgain"], batch=batch, t=t)
    o, sel_t, win_t = nsa_prompt_attention(proj, gates, cmp, w["qk_gain"], batch=batch, t=t, tq=128)
    return matmul_residual(o, w["w_out"], x, tm=tm), cmp_t, sel_t, win_t


PAGES_PER_STEP = 8
CHUNKS_PER_PAGE = PAGE_SIZE // CMP_STRIDE


def _compress_pages_kernel(pt_ref, *refs, n_chunk):
    del pt_ref
    pages = refs[:PAGES_PER_STEP]
    pe_ref, w1_ref, w2_ref, kg_ref, o_ref, xs_ref, pre_ref = refs[PAGES_PER_STEP:]
    s = pl.program_id(1)
    n_pair = NSA_KV_DIM // LANES
    step_chunks = PAGES_PER_STEP * CHUNKS_PER_PAGE
    for i, page in enumerate(pages):
        for c in range(n_pair):
            xs_ref[c, i * PAGE_SIZE:(i + 1) * PAGE_SIZE, :] = jnp.transpose(page[c * LANES:(c + 1) * LANES, :])
    for kv in range(2):
        acc = _compress_accumulate(xs_ref, step_chunks, pe_ref, w1_ref, kv)
        for m in range(2):
            for g in range(KV_GROUPS):
                pre_ref[kv, m, pl.ds(pl.multiple_of(g * n_chunk + s * step_chunks, step_chunks), step_chunks), :] = (
                    acc[m][g * step_chunks:(g + 1) * step_chunks])

    @pl.when(s == pl.num_programs(1) - 1)
    def _():
        for kv in range(2):
            o_ref[kv] = _compress_finish([pre_ref[kv, 0], pre_ref[kv, 1]], w2_ref, kg_ref, kv)


def compress_pages(cache_t, layer, page_table_flat, pe, w1, w2, kgain, *, batch, n_pages):
    n_chunk = n_pages * CHUNKS_PER_PAGE
    n_steps = n_pages // PAGES_PER_STEP

    def page_spec(i):
        return pl.BlockSpec((None, None, NSA_KV_DIM, PAGE_SIZE),
                            lambda b, s, pt: (layer, pt[b * n_pages + s * PAGES_PER_STEP + i], 0, 0))

    grid_spec = pltpu.PrefetchScalarGridSpec(
        num_scalar_prefetch=1,
        grid=(batch, n_steps),
        in_specs=[page_spec(i) for i in range(PAGES_PER_STEP)] + [
            pl.BlockSpec(pe.shape, lambda b, s, pt: (0, 0, 0, 0)),
            pl.BlockSpec(w1.shape, lambda b, s, pt: (0, 0, 0, 0, 0)),
            pl.BlockSpec(w2.shape, lambda b, s, pt: (0, 0, 0)),
            pl.BlockSpec((1, HEAD_DIM), lambda b, s, pt: (0, 0)),
        ],
        out_specs=pl.BlockSpec((None, 2, KV_GROUPS * n_chunk, HEAD_DIM), lambda b, s, pt: (b, 0, 0, 0)),
        scratch_shapes=[pltpu.VMEM((NSA_KV_DIM // LANES, PAGES_PER_STEP * PAGE_SIZE, LANES), F32),
                        pltpu.VMEM((2, 2, KV_GROUPS * n_chunk, CMP_HIDDEN), F32)],
    )
    return pl.pallas_call(
        functools.partial(_compress_pages_kernel, n_chunk=n_chunk),
        grid_spec=grid_spec,
        out_shape=jax.ShapeDtypeStruct((batch, 2, KV_GROUPS * n_chunk, HEAD_DIM), F32),
        compiler_params=_cparams(("parallel", "arbitrary")),
        name="compress_pages",
    )(page_table_flat, *([cache_t] * PAGES_PER_STEP), pe, w1, w2, kgain.reshape(1, HEAD_DIM))


Q_PAD = 8
HEAD_PAD = 16


def _nsa_sample_select_kernel(q_ref, cmp_ref, mmap_ref, oc_ref, idx_ref, score_ref, *, past_len, n_cmp, n_sel):
    rows = GROUP_SIZE * Q_PAD
    n_selp = score_ref.shape[0]
    t_max = past_len + Q_PAD - 1
    row = lax.broadcasted_iota(jnp.int32, (rows, 1), 0)
    t_col = past_len + row % Q_PAD
    jblk = lax.broadcasted_iota(jnp.int32, (n_selp, Q_PAD), 0)
    t_lane = past_len + lax.broadcasted_iota(jnp.int32, (n_selp, Q_PAD), 1)
    cur = t_lane // SEL_BLOCK
    forced = (jblk == 0) | (jblk == cur) | (jblk == cur - 1)
    visible = jblk * SEL_BLOCK <= t_lane
    for g in range(KV_GROUPS):
        qg = (q_ref[g] * (HEAD_DIM ** -0.5)).astype(BF16)
        slope = _slope_col(g, rows, Q_PAD)
        kc = cmp_ref[0, g * n_cmp:(g + 1) * n_cmp, :].astype(BF16)
        vc = cmp_ref[1, g * n_cmp:(g + 1) * n_cmp, :].astype(BF16)
        kpos = lax.broadcasted_iota(jnp.int32, (1, n_cmp), 1) * CMP_STRIDE + (CMP_BLOCK - 1)
        s = _nt(qg, kc) + slope * (kpos - t_max).astype(F32)
        mask = kpos <= t_col
        s = jnp.where(mask, s, NEG_INF)
        p = jnp.where(mask, jnp.exp(s - jnp.max(s, axis=1, keepdims=True)), 0.0)
        pc = _normalize(p, jnp.sum(p, axis=1, keepdims=True))
        oc_ref[g] = _dot(pc.astype(BF16), vc)
        imp = pc[0:Q_PAD]
        for r in range(1, GROUP_SIZE):
            imp = imp + pc[r * Q_PAD:(r + 1) * Q_PAD]
        p_slc = _nt(mmap_ref[...], imp, HIGHEST)
        score = jnp.where(forced, FORCE_SCORE, jnp.where(visible, p_slc, -1.0))
        score_ref[:, g * Q_PAD:(g + 1) * Q_PAD] = jnp.where(jblk < n_sel, score, -2.0)

    score = score_ref[...]
    jall = lax.broadcasted_iota(jnp.int32, score.shape, 0)

    def rank_body(k, cnt):
        sk = score_ref[pl.ds(k, 1), :]
        beats = (sk > score) | ((sk == score) & (jall > k))
        return cnt + beats.astype(jnp.int32)

    cnt = lax.fori_loop(0, n_sel, rank_body, jnp.zeros(score.shape, jnp.int32), unroll=4)
    slots = [jnp.sum(jnp.where(cnt == sl, jall, 0), axis=0, keepdims=True) for sl in range(SEL_TOPN)]
    idx_ref[...] = jnp.concatenate(slots, axis=0)


def nsa_sample_select(q_stack, cmp, *, batch, past_len, t_new):
    n_cmp = past_len // CMP_STRIDE
    n_sel = -(-(past_len + t_new) // SEL_BLOCK)
    n_selp = -(-n_sel // 8) * 8
    n = np.arange(n_cmp)
    mmap = np.zeros((n_selp, n_cmp), np.float32)
    real = n < n_cmp - 1
    np.add.at(mmap, (n[real] // SEL_CHUNKS, n[real]), 1.0)
    np.add.at(mmap, ((n[real] + 1) // SEL_CHUNKS, n[real]), 1.0)
    rows = GROUP_SIZE * Q_PAD
    return pl.pallas_call(
        functools.partial(_nsa_sample_select_kernel, past_len=past_len, n_cmp=n_cmp, n_sel=n_sel),
        grid=(batch,),
        in_specs=[
            pl.BlockSpec((None, KV_GROUPS, rows, HEAD_DIM), lambda b: (b, 0, 0, 0)),
            pl.BlockSpec((None, 2, KV_GROUPS * n_cmp, HEAD_DIM), lambda b: (b, 0, 0, 0)),
            pl.BlockSpec((n_selp, n_cmp), lambda b: (0, 0)),
        ],
        out_specs=[
            pl.BlockSpec((None, KV_GROUPS, rows, HEAD_DIM), lambda b: (b, 0, 0, 0)),
            pl.BlockSpec((None, SEL_TOPN, KV_GROUPS * Q_PAD), lambda b: (b, 0, 0)),
        ],
        out_shape=[jax.ShapeDtypeStruct((batch, KV_GROUPS, rows, HEAD_DIM), F32),
                   jax.ShapeDtypeStruct((batch, SEL_TOPN, KV_GROUPS * Q_PAD), jnp.int32)],
        scratch_shapes=[pltpu.VMEM((n_selp, KV_GROUPS * Q_PAD), F32)],
        compiler_params=_cparams(("parallel",)),
        name="nsa_sample_select",
    )(q_stack, cmp, jnp.asarray(mmap))


def _nsa_sample_attend_kernel(pt_ref, idx_ref, *refs, past_len, t_new, n_past_blocks):
    del pt_ref
    blocks = refs[:SEL_TOPN]
    (q_ref, slope_ref, snew_ref, win_ref, wnew_ref, wnew_t_ref, oc_ref, gate_ref,
     o_ref, wout_ref, kcat_ref, vcat_ref) = refs[SEL_TOPN:]
    b, g, qi = pl.program_id(0), pl.program_id(1), pl.program_id(2)
    t = past_len + qi
    q = (q_ref[...] * (HEAD_DIM ** -0.5)).astype(BF16)
    slope = slope_ref[:, 0:1]
    lane = lax.broadcasted_iota(jnp.int32, (1, LANES), 1)
    new_pos = past_len + lane
    new_ok = (lane < t_new) & (new_pos <= t)

    def attend(tiles, v_new):
        m = jnp.full((HEAD_PAD, 1), NEG_INF, F32)
        for s, mask, _ in tiles:
            m = jnp.maximum(m, jnp.max(jnp.where(mask, s, NEG_INF), axis=1, keepdims=True))
        l = jnp.zeros((HEAD_PAD, 1), F32)
        acc = jnp.zeros((HEAD_PAD, HEAD_DIM), F32)
        for i, (s, mask, v) in enumerate(tiles):
            p = jnp.where(mask, jnp.exp(jnp.where(mask, s, NEG_INF) - m), 0.0)
            l = l + jnp.sum(p, axis=1, keepdims=True)
            if i == len(tiles) - 1:
                acc = acc + _dot(p.astype(BF16), v_new)
            else:
                acc = acc + _nt(p.astype(BF16), v)
        return _normalize(acc, l)

    base = ((b * KV_GROUPS + g) * t_new + qi) * SEL_TOPN
    kpos, mask = [], []
    for sl, blk_ref in enumerate(blocks):
        blk = idx_ref[base + sl]
        kcat_ref[:, sl * PAGE_SIZE:(sl + 1) * PAGE_SIZE] = blk_ref[0].astype(BF16)
        vcat_ref[:, sl * PAGE_SIZE:(sl + 1) * PAGE_SIZE] = blk_ref[1].astype(BF16)
        kpos.append((blk // 2) * PAGE_SIZE + lane)
        mask.append((lane // SEL_BLOCK == blk % 2) & (blk < n_past_blocks) & (kpos[-1] <= t))
    kpos = jnp.concatenate(kpos, axis=1)
    mask = jnp.concatenate([m_.astype(jnp.int32) for m_ in mask], axis=1) > 0
    s_sel = _dot(q, kcat_ref[...]) + slope * (kpos - t).astype(F32)
    s_new = _nt(q, snew_ref[0].astype(BF16)) + slope * (new_pos - t).astype(F32)
    o_s = attend([(s_sel, mask, vcat_ref[...]), (s_new, new_ok, None)], snew_ref[1].astype(BF16))

    buf = win_ref.shape[2]
    wlane = lax.broadcasted_iota(jnp.int32, (1, buf), 1)
    wpos = past_len - buf + wlane
    dist = t - wpos
    s_w = _dot(q, win_ref[0].astype(BF16)) + slope * (wpos - t).astype(F32)
    s_wn = _nt(q, wnew_ref[0].astype(BF16)) + slope * (new_pos - t).astype(F32)
    o_w = attend([(s_w, (dist >= 0) & (dist < WINDOW), win_ref[1].astype(BF16)),
                  (s_wn, new_ok & (t - new_pos < WINDOW), None)], wnew_ref[1].astype(BF16))

    gates = jax.nn.sigmoid(gate_ref[...])
    o_ref[...] = gates[:, 0:1] * oc_ref[...] + gates[:, 1:2] * o_s + gates[:, 2:3] * o_w

    @pl.when(qi == 0)
    def _():
        for kv in range(2):
            rolled = pltpu.roll(win_ref[kv], buf - t_new, axis=1)
            wout_ref[kv, :, 0:buf - LANES] = rolled[:, 0:buf - LANES]
            wout_ref[kv, :, buf - LANES:] = jnp.where(lane >= LANES - t_new, wnew_t_ref[kv], rolled[:, buf - LANES:])


def nsa_sample_attend(cache_t, win_t, layer, page_table_flat, idx_flat, q_heads, slopes, sel_new, win_new,
                      win_new_t, o_c, gates, *, batch, past_len, t_new):
    n_pages = past_len // PAGE_SIZE
    n_past_blocks = past_len // SEL_BLOCK
    buf = win_t.shape[-1]

    def block_spec(sl):
        def index(b, g, q, pt, idx):
            blk = idx[((b * KV_GROUPS + g) * t_new + q) * SEL_TOPN + sl]
            page = pt[b * n_pages + jnp.minimum(blk, n_past_blocks - 1) // 2]
            return (layer, page, 0, g, 0, 0)
        return pl.BlockSpec((None, None, 2, None, HEAD_DIM, PAGE_SIZE), index)

    per_bg = lambda shape: pl.BlockSpec((None, None) + shape, lambda b, g, q, pt, idx: (b, g) + (0,) * len(shape))
    per_bgq = lambda shape: pl.BlockSpec((None, None, None) + shape,
                                         lambda b, g, q, pt, idx: (b, g, q) + (0,) * len(shape))
    grid_spec = pltpu.PrefetchScalarGridSpec(
        num_scalar_prefetch=2,
        grid=(batch, KV_GROUPS, t_new),
        in_specs=[block_spec(sl) for sl in range(SEL_TOPN)] + [
            per_bgq((HEAD_PAD, HEAD_DIM)),
            pl.BlockSpec((None, HEAD_PAD, LANES), lambda b, g, q, pt, idx: (g, 0, 0)),
            per_bg((2, LANES, HEAD_DIM)),
            pl.BlockSpec((None, None, 2, None, HEAD_DIM, buf), lambda b, g, q, pt, idx: (layer, b, 0, g, 0, 0)),
            per_bg((2, LANES, HEAD_DIM)),
            per_bg((2, HEAD_DIM, LANES)),
            per_bgq((HEAD_PAD, HEAD_DIM)),
            per_bgq((HEAD_PAD, LANES)),
        ],
        out_specs=[
            per_bgq((HEAD_PAD, HEAD_DIM)),
            pl.BlockSpec((None, 2, None, HEAD_DIM, buf), lambda b, g, q, pt, idx: (b, 0, g, 0, 0)),
        ],
        scratch_shapes=[pltpu.VMEM((HEAD_DIM, SEL_TOPN * PAGE_SIZE), BF16)] * 2,
    )
    return pl.pallas_call(
        functools.partial(_nsa_sample_attend_kernel, past_len=past_len, t_new=t_new, n_past_blocks=n_past_blocks),
        grid_spec=grid_spec,
        out_shape=[jax.ShapeDtypeStruct((batch, KV_GROUPS, t_new, HEAD_PAD, HEAD_DIM), F32),
                   jax.ShapeDtypeStruct((batch, 2, KV_GROUPS, HEAD_DIM, buf), F32)],
        compiler_params=_cparams(("parallel", "arbitrary", "arbitrary")),
        name="nsa_sample_attend",
    )(page_table_flat, idx_flat, *([cache_t] * SEL_TOPN), q_heads, slopes, sel_new, win_t, win_new,
      win_new_t, o_c, gates)


def nsa_sample_layer(x, gain, w, cmp_t, sel_t, win_t, layer, page_table_flat, *, batch, t_new, past_len):
    m = batch * t_new
    proj, gates = _nsa_project(x, gain, w, tm=m, head_norm=True)
    cmp = compress_pages(cmp_t.reshape(cmp_t.shape[0], cmp_t.shape[1], NSA_KV_DIM, PAGE_SIZE), layer,
                         page_table_flat, w["pe"], w["w1"], w["w2"], w["kgain"], batch=batch,
                         n_pages=past_len // PAGE_SIZE)
    q = proj[:, :NSA_Q_DIM].reshape(batch, t_new, KV_GROUPS, GROUP_SIZE, HEAD_DIM)
    q_stack = jnp.pad(q.transpose(0, 2, 3, 1, 4), ((0, 0), (0, 0), (0, 0), (0, Q_PAD - t_new), (0, 0)))
    o_c, idx = nsa_sample_select(q_stack.reshape(batch, KV_GROUPS, GROUP_SIZE * Q_PAD, HEAD_DIM), cmp,
                                 batch=batch, past_len=past_len, t_new=t_new)
    pad_heads = ((0, 0), (0, 0), (0, 0), (0, HEAD_PAD - GROUP_SIZE), (0, 0))
    o_c = o_c.reshape(batch, KV_GROUPS, GROUP_SIZE, Q_PAD, HEAD_DIM)[:, :, :, :t_new].transpose(0, 1, 3, 2, 4)
    idx_flat = idx.reshape(batch, SEL_TOPN, KV_GROUPS, Q_PAD)[..., :t_new].transpose(0, 2, 3, 1).reshape(-1)
    q_heads = jnp.pad(q.transpose(0, 2, 1, 3, 4), pad_heads)

    def new_rows(cols):
        a = cols.reshape(batch, t_new, 2, KV_GROUPS, HEAD_DIM).transpose(0, 3, 2, 1, 4)
        return jnp.pad(a, ((0, 0), (0, 0), (0, 0), (0, LANES - t_new), (0, 0)))

    kvs = proj[:, NSA_Q_DIM + NSA_KV_DIM:NSA_Q_DIM + 2 * NSA_KV_DIM]
    kvw = proj[:, NSA_Q_DIM + 2 * NSA_KV_DIM:]
    win_new_t = kvw.reshape(batch, t_new, 2, KV_GROUPS, HEAD_DIM).transpose(0, 3, 2, 4, 1)
    win_new_t = jnp.pad(win_new_t, ((0, 0), (0, 0), (0, 0), (0, 0), (LANES - t_new, 0)))
    g_arr = gates[:, :3 * NSA_HEADS].reshape(batch, t_new, 3, KV_GROUPS, GROUP_SIZE).transpose(0, 3, 1, 4, 2)
    g_arr = jnp.pad(g_arr, ((0, 0), (0, 0), (0, 0), (0, HEAD_PAD - GROUP_SIZE), (0, LANES - 3)))
    slopes = np.zeros((KV_GROUPS, HEAD_PAD, LANES), np.float32)
    slopes[:, :GROUP_SIZE, :] = np.asarray(_SLOPES, np.float32).reshape(KV_GROUPS, GROUP_SIZE, 1)
    o, win_out = nsa_sample_attend(sel_t, win_t, layer, page_table_flat, idx_flat, q_heads, jnp.asarray(slopes),
                                   new_rows(kvs), new_rows(kvw), win_new_t, jnp.pad(o_c, pad_heads), g_arr,
                                   batch=batch, past_len=past_len, t_new=t_new)
    o = o[:, :, :, :GROUP_SIZE].transpose(0, 2, 1, 3, 4).reshape(m, NSA_Q_DIM)
    return matmul_residual(o.astype(BF16), w["w_out"], x, tm=m), proj, win_out


def _expand_heads(x, e):
    x1 = x.astype(BF16)
    rest = x - x1.astype(F32)
    x2 = rest.astype(BF16)
    x3 = (rest - x2.astype(F32)).astype(BF16)
    return (_dot(x1, e) + _dot(x2, e)) + _dot(x3, e)


def _ssd_kernel(z_ref, x_ref, bc_ref, dt_ref, cw_ref, cb_ref, dtb_ref, alog_ref, drow_ref, nw_ref, cinit_ref,
                hinit_ref, tril_ref, e_ref, y_ref, hout_ref, stage_ref, h_ref, yacc_ref, *, t_valid):
    L = SSM_CHUNK
    c = pl.program_id(1)

    @pl.when(c == 0)
    def _():
        stage_ref[0:8, :] = cinit_ref[...]
        h_ref[...] = hinit_ref[...]

    stage_ref[8:8 + L, 0:SSM_D_INNER] = x_ref[...]
    stage_ref[8:8 + L, SSM_D_INNER:] = bc_ref[...]
    conv = cb_ref[...]
    for k in range(SSM_CONV):
        conv = conv + stage_ref[5 + k:5 + k + L, :] * cw_ref[k:k + 1, :]
    stage_ref[0:8, :] = stage_ref[L:L + 8, :]
    conv = conv * jax.nn.sigmoid(conv)
    xh = conv[:, :SSM_D_INNER]

    row = lax.broadcasted_iota(jnp.int32, (L, LANES), 0)
    dt_in = dt_ref[...] + dtb_ref[...]
    dt = jnp.maximum(dt_in, 0.0) + jnp.log1p(jnp.exp(-jnp.abs(dt_in)))
    dt = jnp.where(row < t_valid, dt, 0.0)
    da = dt * -jnp.exp(alog_ref[...])
    acum = _dot(tril_ref[...], da, HIGHEST)
    acum_t = jnp.transpose(acum)
    acum_last = acum[L - 1:L, :]
    dt_full = _expand_heads(dt, e_ref[...])
    grow_full = _expand_heads(jnp.exp(acum), e_ref[...])
    toend_full = _expand_heads(jnp.exp(acum_last - acum), e_ref[...])
    last_t = jnp.exp(acum_t[:, L - 1:L])
    xdt = xh * dt_full
    causal = lax.broadcasted_iota(jnp.int32, (L, L), 0) >= lax.broadcasted_iota(jnp.int32, (L, L), 1)

    for g in range(SSM_GROUPS):
        gl, gh = g * SSM_GW, (g + 1) * SSM_GW
        bm = conv[:, SSM_D_INNER + g * SSM_STATE:SSM_D_INNER + (g + 1) * SSM_STATE].astype(BF16)
        cm = conv[:, SSM_D_INNER + (SSM_GROUPS + g) * SSM_STATE:
                  SSM_D_INNER + (SSM_GROUPS + g + 1) * SSM_STATE].astype(BF16)
        cb = _nt(cm, bm)
        xdt_g = xdt[:, gl:gh]
        for r in range(SSM_HPG):
            h = g * SSM_HPG + r
            seg = acum[:, h:h + 1] - acum_t[h:h + 1, :]
            decay = jnp.exp(jnp.where(causal, seg, NEG_INF))
            yacc_ref[:, gl + r * SSM_HEAD_DIM:gl + (r + 1) * SSM_HEAD_DIM] = _dot(
                (cb * decay).astype(BF16), xdt_g[:, r * SSM_HEAD_DIM:(r + 1) * SSM_HEAD_DIM].astype(BF16))
        h_g = h_ref[gl:gh, :]
        y_state = _nt(cm, h_g.astype(BF16)) * grow_full[:, gl:gh]
        yacc_ref[:, gl:gh] = yacc_ref[:, gl:gh] + y_state
        xw = (xdt_g * toend_full[:, gl:gh]).astype(BF16)
        h_add = lax.dot_general(xw, bm, (((0,), (0,)), ((), ())), preferred_element_type=F32)
        for r in range(SSM_HPG):
            h = g * SSM_HPG + r
            rl, rh = r * SSM_HEAD_DIM, (r + 1) * SSM_HEAD_DIM
            keep = jnp.broadcast_to(last_t[h:h + 1, :], (SSM_HEAD_DIM, SSM_STATE))
            h_ref[gl + rl:gl + rh, :] = h_g[rl:rh] * keep + h_add[rl:rh]

    zf = z_ref[...]
    gated = (yacc_ref[...] + xh * drow_ref[...]) * (zf * jax.nn.sigmoid(zf))
    for g in range(SSM_GROUPS):
        gl, gh = g * SSM_GW, (g + 1) * SSM_GW
        y_ref[:, gl:gh] = _rms(gated[:, gl:gh], nw_ref[:, gl:gh]).astype(y_ref.dtype)

    @pl.when(c == pl.num_programs(1) - 1)
    def _():
        hout_ref[...] = h_ref[...]


def ssd_mixer(proj, dt_raw, conv_init, h_init, w, *, batch, n_chunks, t_valid):
    L = SSM_CHUNK
    rows = batch * n_chunks * L
    conv_dim = SSM_D_INNER + SSM_BC
    tril = np.tril(np.ones((L, L), np.float32))
    e = np.zeros((LANES, SSM_D_INNER), np.float32)
    e[np.arange(SSM_D_INNER) // SSM_HEAD_DIM, np.arange(SSM_D_INNER)] = 1.0
    const = lambda shape: pl.BlockSpec(shape, lambda b, c: (0,) * len(shape))
    return pl.pallas_call(
        functools.partial(_ssd_kernel, t_valid=t_valid),
        grid=(batch, n_chunks),
        in_specs=[
            pl.BlockSpec((L, SSM_D_INNER), lambda b, c: (b * n_chunks + c, 0)),
            pl.BlockSpec((L, SSM_D_INNER), lambda b, c: (b * n_chunks + c, 1)),
            pl.BlockSpec((L, SSM_BC), lambda b, c: (b * n_chunks + c, 2 * SSM_D_INNER // SSM_BC)),
            pl.BlockSpec((L, LANES), lambda b, c: (b * n_chunks + c, 0)),
            const((SSM_CONV, conv_dim)), const((1, conv_dim)), const((1, LANES)), const((1, LANES)),
            const((1, SSM_D_INNER)), const((1, SSM_D_INNER)),
            pl.BlockSpec((None, 8, conv_dim), lambda b, c: (b, 0, 0)),
            pl.BlockSpec((None, SSM_D_INNER, SSM_STATE), lambda b, c: (b, 0, 0)),
            const((L, L)), const((LANES, SSM_D_INNER)),
        ],
        out_specs=[
            pl.BlockSpec((L, SSM_D_INNER), lambda b, c: (b * n_chunks + c, 0)),
            pl.BlockSpec((None, SSM_D_INNER, SSM_STATE), lambda b, c: (b, 0, 0)),
        ],
        out_shape=[jax.ShapeDtypeStruct((rows, SSM_D_INNER), BF16),
                   jax.ShapeDtypeStruct((batch, SSM_D_INNER, SSM_STATE), F32)],
        scratch_shapes=[pltpu.VMEM((L + 8, conv_dim), F32), pltpu.VMEM((SSM_D_INNER, SSM_STATE), F32),
                        pltpu.VMEM((L, SSM_D_INNER), F32)],
        compiler_params=_cparams(("parallel", "arbitrary")),
        name="ssd_mixer",
    )(proj, proj, proj, dt_raw, w["conv_w"], w["conv_b"], w["dt_bias"], w["a_log"], w["d_row"], w["norm_w"],
      conv_init, h_init, jnp.asarray(tril), jnp.asarray(e, BF16))


def _ssd_weights(w_in, conv_w, conv_b, dt_bias, a_log, d_skip, norm_w, w_out):
    n_main = 2 * SSM_D_INNER + SSM_BC
    pad = lambda v: jnp.pad(v, (0, LANES - SSM_HEADS)).reshape(1, LANES)
    return dict(
        w_main=w_in[:, :n_main].astype(BF16),
        w_dt=jnp.pad(w_in[:, n_main:], ((0, 0), (0, LANES - SSM_HEADS))).astype(BF16),
        conv_w=conv_w, conv_b=conv_b.reshape(1, -1), dt_bias=pad(dt_bias), a_log=pad(a_log),
        d_row=jnp.repeat(d_skip, SSM_HEAD_DIM).reshape(1, SSM_D_INNER), norm_w=norm_w.reshape(1, SSM_D_INNER),
        w_out=w_out.astype(BF16))


def ssd_layer(x, gain, w, conv_state, h_init, *, batch, t):
    tm = min(1024, x.shape[0])
    proj = norm_matmul(x, gain, w["w_main"], tm=tm, tn=512)
    dt_raw = norm_matmul(x, gain, w["w_dt"], tm=tm, tn=LANES)
    conv_init = jnp.pad(conv_state, ((0, 0), (8 - (SSM_CONV - 1), 0), (0, 0)))
    if t % SSM_CHUNK == 0:
        y, h_last = ssd_mixer(proj, dt_raw, conv_init, h_init, w, batch=batch, n_chunks=t // SSM_CHUNK,
                              t_valid=SSM_CHUNK)
    else:
        pad_rows = lambda a: jnp.pad(a.reshape(batch, t, -1), ((0, 0), (0, SSM_CHUNK - t), (0, 0))).reshape(
            batch * SSM_CHUNK, -1)
        y, h_last = ssd_mixer(pad_rows(proj), pad_rows(dt_raw), conv_init, h_init, w, batch=batch, n_chunks=1,
                              t_valid=t)
        y = y.reshape(batch, SSM_CHUNK, -1)[:, :t].reshape(batch * t, -1)
    return matmul_residual(y, w["w_out"], x, tm=tm), proj, h_last


def kernel(x_prompt, x_sample, cache_kv_cmp, cache_kv_sel, cache_kv_win, state_conv, state_ssm, page_table,
           norm_mix, norm_ffn, nsa_w_in, nsa_qk_gain, nsa_cmp_pe, nsa_cmp_w1, nsa_cmp_w2, nsa_w_out,
           ssd_w_in, ssd_conv_w, ssd_conv_b, ssd_dt_bias, ssd_a_log, ssd_d, ssd_norm, ssd_w_out,
           ffn_w_gate, ffn_w_up, ffn_w_down):
    bp, tp = x_prompt.shape[0], x_prompt.shape[1]
    bs, ts = x_sample.shape[0], x_sample.shape[1]
    depth = norm_mix.shape[0]
    past_len = page_table.shape[1] * PAGE_SIZE
    conv_dim = SSM_D_INNER + SSM_BC
    assert tp % SSM_CHUNK == 0 and ts < SSM_CHUNK and ts <= Q_PAD and ts >= SSM_CONV - 1
    xp = x_prompt.reshape(bp * tp, D_MODEL)
    xs = x_sample.reshape(bs * ts, D_MODEL)
    pt_flat = page_table.reshape(-1)
    feature_major = lambda a: a.transpose(0, 1, 3, 4, 5, 2)
    cmp_t, sel_t, win_t = feature_major(cache_kv_cmp), feature_major(cache_kv_sel), feature_major(cache_kv_win)
    kv_shape = (2, KV_GROUPS, HEAD_DIM)
    outs = {k: [] for k in ("cmp_p", "sel_p", "win_p", "conv_p", "ssm_p", "cmp_s", "sel_s", "win_s", "conv_s",
                            "ssm_s")}
    for i in range(depth):
        l = i // 2
        if i % 2 == 0:
            w = _nsa_weights(nsa_w_in[l], nsa_qk_gain[l], nsa_cmp_pe[l], nsa_cmp_w1[l], nsa_cmp_w2[l], nsa_w_out[l])
            xp, cmp_rows, sel_rows, win_rows = nsa_prompt_layer(xp, norm_mix[i], w, batch=bp, t=tp)
            row_major = lambda a: a.reshape((bp,) + kv_shape + (a.shape[-1],)).transpose(0, 4, 1, 2, 3)
            outs["cmp_p"].append(row_major(cmp_rows))
            outs["sel_p"].append(row_major(sel_rows))
            outs["win_p"].append(row_major(win_rows))
            kv = lambda a, b, t, j: a[:, NSA_Q_DIM + j * NSA_KV_DIM:NSA_Q_DIM + (j + 1) * NSA_KV_DIM].reshape(
                (b, t) + kv_shape)
            xs, proj_s, win_out = nsa_sample_layer(xs, norm_mix[i], w, cmp_t, sel_t, win_t, l, pt_flat, batch=bs,
                                                   t_new=ts, past_len=past_len)
            outs["cmp_s"].append(kv(proj_s, bs, ts, 0))
            outs["sel_s"].append(kv(proj_s, bs, ts, 1))
            outs["win_s"].append(win_out.transpose(0, 4, 1, 2, 3))
        else:
            w = _ssd_weights(ssd_w_in[l], ssd_conv_w[l], ssd_conv_b[l], ssd_dt_bias[l], ssd_a_log[l], ssd_d[l],
                             ssd_norm[l], ssd_w_out[l])
            xp, proj, h_last = ssd_layer(xp, norm_mix[i], w, jnp.zeros((bp, SSM_CONV - 1, conv_dim), F32),
                                         jnp.zeros((bp, SSM_D_INNER, SSM_STATE), F32), batch=bp, t=tp)
            outs["conv_p"].append(proj.reshape(bp, tp, -1)[:, tp - (SSM_CONV - 1):, SSM_D_INNER:])
            outs["ssm_p"].append(h_last.reshape(bp, SSM_HEADS, SSM_HEAD_DIM, SSM_STATE))
            xs, proj_s, h_last = ssd_layer(xs, norm_mix[i], w, state_conv[l],
                                           state_ssm[l].reshape(bs, SSM_D_INNER, SSM_STATE), batch=bs, t=ts)
            outs["conv_s"].append(proj_s.reshape(bs, ts, -1)[:, ts - (SSM_CONV - 1):, SSM_D_INNER:])
            outs["ssm_s"].append(h_last.reshape(bs, SSM_HEADS, SSM_HEAD_DIM, SSM_STATE))
        wg, wu, wd = ffn_w_gate[i].astype(BF16), ffn_w_up[i].astype(BF16), ffn_w_down[i].astype(BF16)
        xp = ffn(xp, norm_ffn[i], wg, wu, wd, tm=1024, th=256)
        xs = ffn(xs, norm_ffn[i], wg, wu, wd, tm=bs * ts, th=256)
    order = ("cmp_p", "sel_p", "win_p", "conv_p", "ssm_p", "cmp_s", "sel_s", "win_s", "conv_s", "ssm_s")
    return (xp.reshape(bp, tp, D_MODEL), xs.reshape(bs, ts, D_MODEL)) + tuple(jnp.stack(outs[k]) for k in order)
```

```python
import functools
import math

import jax
import jax.numpy as jnp
import ml_dtypes
import numpy as np
from jax import lax
from jax.experimental import pallas as pl
from jax.experimental.pallas import tpu as pltpu

F32 = jnp.float32
BF16 = jnp.bfloat16
HIGHEST = lax.Precision.HIGHEST

D_MODEL = 1024
PAGE_SIZE = 128
NSA_HEADS = 16
HEAD_DIM = 64
KV_GROUPS = 4
GROUP_SIZE = NSA_HEADS // KV_GROUPS
CMP_BLOCK = 32
CMP_STRIDE = 16
CMP_HIDDEN = 256
SEL_BLOCK = 64
SEL_CHUNKS = SEL_BLOCK // CMP_STRIDE
SEL_TOPN = 16
WINDOW = 512
NSA_Q_DIM = NSA_HEADS * HEAD_DIM
NSA_KV_DIM = 2 * KV_GROUPS * HEAD_DIM
KV_HALF = KV_GROUPS * HEAD_DIM
SSM_D_INNER = 2048
SSM_HEAD_DIM = 64
SSM_HEADS = 32
SSM_GROUPS = 4
SSM_HPG = SSM_HEADS // SSM_GROUPS
SSM_STATE = 128
SSM_CONV = 4
SSM_BC = 2 * SSM_GROUPS * SSM_STATE
SSM_GW = SSM_D_INNER // SSM_GROUPS
SSM_CHUNK = 128
NORM_EPS = 1e-6
NEG_INF = -1e30
FORCE_SCORE = 1e4
LANES = 128
VMEM_LIMIT = 56 * 1024 * 1024

_SLOPES = [float(np.float32((2.0 ** (-8.0 / NSA_HEADS)) ** (i + 1))) for i in range(NSA_HEADS)]


def _cparams(sem):
    return pltpu.CompilerParams(dimension_semantics=sem, vmem_limit_bytes=VMEM_LIMIT)


def _nt(a, b, precision=None):
    return lax.dot_general(a, b, (((1,), (1,)), ((), ())), preferred_element_type=F32, precision=precision)


def _dot(a, b, precision=None):
    return jnp.dot(a, b, preferred_element_type=F32, precision=precision)


def _rms(x, gain):
    return x * lax.rsqrt(jnp.mean(x * x, axis=-1, keepdims=True) + NORM_EPS) * gain


def _norm_matmul_kernel(x_ref, g_ref, w_ref, hg_ref, hm_ref, e_ref, et_ref, o_ref, xn_ref, *, head_norm):
    @pl.when(pl.program_id(1) == 0)
    def _():
        xn_ref[...] = _rms(x_ref[...], g_ref[...]).astype(BF16)

    y = _dot(xn_ref[...], w_ref[...])
    if head_norm:
        ss = _dot(y * y, e_ref[...], HIGHEST)
        inv = lax.rsqrt(ss * (1.0 / HEAD_DIM) + NORM_EPS)
        yn = y * _dot(inv, et_ref[...], HIGHEST) * hg_ref[...]
        y = jnp.where(hm_ref[...] > 0.5, yn, y)
    o_ref[...] = y


def norm_matmul(x, gain, w, head_gain=None, head_mask=None, *, tm, tn):
    m, k = x.shape
    n = w.shape[1]
    head_norm = head_gain is not None
    if not head_norm:
        head_gain = jnp.zeros((1, n), F32)
        head_mask = jnp.zeros((1, n), F32)
    lane = np.arange(tn)
    e = np.zeros((tn, LANES), np.float32)
    e[lane, lane // HEAD_DIM] = 1.0
    return pl.pallas_call(
        functools.partial(_norm_matmul_kernel, head_norm=head_norm),
        grid=(m // tm, n // tn),
        in_specs=[
            pl.BlockSpec((tm, k), lambda i, j: (i, 0)),
            pl.BlockSpec((1, k), lambda i, j: (0, 0)),
            pl.BlockSpec((k, tn), lambda i, j: (0, j)),
            pl.BlockSpec((1, tn), lambda i, j: (0, j)),
            pl.BlockSpec((1, tn), lambda i, j: (0, j)),
            pl.BlockSpec((tn, LANES), lambda i, j: (0, 0)),
            pl.BlockSpec((LANES, tn), lambda i, j: (0, 0)),
        ],
        out_specs=pl.BlockSpec((tm, tn), lambda i, j: (i, j)),
        out_shape=jax.ShapeDtypeStruct((m, n), F32),
        scratch_shapes=[pltpu.VMEM((tm, k), BF16)],
        compiler_params=_cparams(("parallel", "arbitrary")),
        name="norm_matmul",
    )(x, gain.reshape(1, k), w, head_gain, head_mask, jnp.asarray(e), jnp.asarray(e.T))


def _ffn_kernel(x_ref, g_ref, wg_ref, wu_ref, wd_ref, o_ref, xn_ref):
    @pl.when(pl.program_id(1) == 0)
    def _():
        x = x_ref[...]
        xn_ref[...] = _rms(x, g_ref[...]).astype(BF16)
        o_ref[...] = x

    h = xn_ref[...]
    gate = _dot(h, wg_ref[...])
    up = _dot(h, wu_ref[...])
    act = (gate * jax.nn.sigmoid(gate) * up).astype(BF16)
    o_ref[...] += _dot(act, wd_ref[...])


def ffn(x, gain, wg, wu, wd, *, tm, th):
    m, k = x.shape
    hid = wg.shape[1]
    return pl.pallas_call(
        _ffn_kernel,
        grid=(m // tm, hid // th),
        in_specs=[
            pl.BlockSpec((tm, k), lambda i, j: (i, 0)),
            pl.BlockSpec((1, k), lambda i, j: (0, 0)),
            pl.BlockSpec((k, th), lambda i, j: (0, j)),
            pl.BlockSpec((k, th), lambda i, j: (0, j)),
            pl.BlockSpec((th, k), lambda i, j: (j, 0)),
        ],
        out_specs=pl.BlockSpec((tm, k), lambda i, j: (i, 0)),
        out_shape=jax.ShapeDtypeStruct((m, k), F32),
        scratch_shapes=[pltpu.VMEM((tm, k), BF16)],
        compiler_params=_cparams(("parallel", "arbitrary")),
        name="ffn",
    )(x, gain.reshape(1, k), wg, wu, wd)


def _matmul_res_kernel(a_ref, w_ref, r_ref, o_ref):
    o_ref[...] = r_ref[...] + _dot(a_ref[...], w_ref[...])


def matmul_residual(a, w, res, *, tm):
    m, k = a.shape
    n = w.shape[1]
    return pl.pallas_call(
        _matmul_res_kernel,
        grid=(m // tm,),
        in_specs=[
            pl.BlockSpec((tm, k), lambda i: (i, 0)),
            pl.BlockSpec((k, n), lambda i: (0, 0)),
            pl.BlockSpec((tm, n), lambda i: (i, 0)),
        ],
        out_specs=pl.BlockSpec((tm, n), lambda i: (i, 0)),
        out_shape=jax.ShapeDtypeStruct((m, n), F32),
        compiler_params=_cparams(("parallel",)),
        name="matmul_residual",
    )(a, w, res)


CMP_PACK = 4


def _compress_accumulate(xs_ref, n_rows, pe_ref, w1_ref, kv):
    n_pair = NSA_KV_DIM // LANES
    low = lax.broadcasted_iota(jnp.int32, (n_rows, LANES), 1) < HEAD_DIM
    acc = [None, None]
    for jp in range(CMP_STRIDE // CMP_PACK):
        groups = []
        for c in range(kv * n_pair // 2, (kv + 1) * n_pair // 2):
            rows = [xs_ref.at[c][pl.ds(jp * CMP_PACK + i, n_rows, stride=CMP_STRIDE), :] for i in range(CMP_PACK)]
            swapped = [pltpu.roll(r, HEAD_DIM, axis=1) for r in rows]
            groups.append(jnp.concatenate([jnp.where(low, rows[i], swapped[i + 1])
                                           for i in range(0, CMP_PACK, 2)], axis=1))
            groups.append(jnp.concatenate([jnp.where(low, swapped[i], rows[i + 1])
                                           for i in range(0, CMP_PACK, 2)], axis=1))
        xs = jnp.concatenate(groups, axis=0)
        for m in range(2):
            lhs = (xs + pe_ref[kv, m, jp:jp + 1, :]).astype(BF16)
            part = _dot(lhs, w1_ref[kv, m, jp])
            acc[m] = part if acc[m] is None else acc[m] + part
    return acc


def _compress_finish(acc, w2_ref, kg_ref, kv):
    rows = acc[0].shape[0]
    pre = acc[0] + pltpu.roll(acc[1], rows - 1, axis=0)
    hid = (pre * jax.nn.sigmoid(pre)).astype(BF16)
    out = _dot(hid, w2_ref[kv])
    if kv == 0:
        out = _rms(out, kg_ref[...])
    return out


def _compress_rows_kernel(x_ref, pe_ref, w1_ref, w2_ref, kg_ref, o_ref, xt_ref, xs_ref, *, n_chunk):
    xt_ref[...] = jnp.transpose(x_ref[...])
    n_pair = NSA_KV_DIM // LANES
    for c in range(n_pair):
        xs_ref[c] = x_ref[:, c * LANES:(c + 1) * LANES]
    for kv in range(2):
        o_ref[kv] = _compress_finish(_compress_accumulate(xs_ref, n_chunk, pe_ref, w1_ref, kv), w2_ref, kg_ref, kv)


def compress_rows(proj, col_block, pe, w1, w2, kgain, *, batch, t):
    n_chunk = t // CMP_STRIDE
    return pl.pallas_call(
        functools.partial(_compress_rows_kernel, n_chunk=n_chunk),
        grid=(batch,),
        in_specs=[
            pl.BlockSpec((t, NSA_KV_DIM), lambda b: (b, col_block)),
            pl.BlockSpec(pe.shape, lambda b: (0, 0, 0, 0)),
            pl.BlockSpec(w1.shape, lambda b: (0, 0, 0, 0, 0)),
            pl.BlockSpec(w2.shape, lambda b: (0, 0, 0)),
            pl.BlockSpec((1, HEAD_DIM), lambda b: (0, 0)),
        ],
        out_specs=[pl.BlockSpec((None, 2, KV_GROUPS * n_chunk, HEAD_DIM), lambda b: (b, 0, 0, 0)),
                   pl.BlockSpec((None, NSA_KV_DIM, t), lambda b: (b, 0, 0))],
        out_shape=[jax.ShapeDtypeStruct((batch, 2, KV_GROUPS * n_chunk, HEAD_DIM), F32),
                   jax.ShapeDtypeStruct((batch, NSA_KV_DIM, t), F32)],
        scratch_shapes=[pltpu.VMEM((NSA_KV_DIM // LANES, t, LANES), F32)],
        compiler_params=_cparams(("parallel",)),
        name="compress_rows",
    )(proj, pe, w1, w2, kgain.reshape(1, HEAD_DIM))


def _slope_col(g, rows, tq):
    rblk = lax.broadcasted_iota(jnp.int32, (rows, 1), 0) // tq
    col = jnp.full((rows, 1), _SLOPES[GROUP_SIZE * g + GROUP_SIZE - 1], F32)
    for r in range(GROUP_SIZE - 1):
        col = jnp.where(rblk == r, _SLOPES[GROUP_SIZE * g + r], col)
    return col


def _normalize(acc, l):
    return acc * jnp.where(l > 0.0, 1.0 / jnp.where(l > 0.0, l, 1.0), 0.0)


MASK_BIG = 2.0 ** 100
LOG2E = 1.4426950408889634
AUG = 64
AUG_POS = 32


def _bf16_terms(x, n=3):
    x = np.asarray(x, np.float64)
    terms = []
    for _ in range(n):
        term = x.astype(ml_dtypes.bfloat16).astype(np.float64)
        terms.append(term.astype(np.float32))
        x = x - term
    return terms


def _key_aug(pos, flags=None):
    pos = np.asarray(pos, np.int64)
    aug = np.zeros((pos.shape[0], AUG), np.float32)
    if flags is not None:
        aug[:, :flags.shape[1]] = flags
    coarse = (pos // SEL_BLOCK) * SEL_BLOCK
    aug[:, AUG_POS:AUG_POS + 3] = coarse[:, None]
    aug[:, AUG_POS + 3:AUG_POS + 6] = (pos - coarse)[:, None]
    return aug


def _query_aug_rows(tq):
    rows = np.zeros((KV_GROUPS, AUG - AUG_POS, GROUP_SIZE * tq), np.float32)
    for h in range(NSA_HEADS):
        terms = _bf16_terms(np.float64(_SLOPES[h]) * LOG2E)
        g, r = divmod(h, GROUP_SIZE)
        for i, term in enumerate(terms):
            rows[g, i, r * tq:(r + 1) * tq] = term
            rows[g, 3 + i, r * tq:(r + 1) * tq] = term
    return rows


def _online_update(s, v_t, m, l, acc):
    m_new = jnp.maximum(m, jnp.max(s, axis=0, keepdims=True))
    alpha = jnp.exp2(m - m_new)
    p = jnp.exp2(s - m_new)
    return m_new, alpha * l + jnp.sum(p, axis=0, keepdims=True), alpha * acc + _dot(v_t, p.astype(BF16))


def _nsa_prompt_kernel(q_ref, ks_ref, kw_ref, cmp_ref, gate_ref, gq_ref, gks_ref, gkw_ref, gksc_ref, gkwc_ref,
                       csel_ref, cwin_ref, ccmp_ref, qrows_ref, mmap_ref, eye_ref,
                       o_ref, kst_out_ref, kwt_out_ref,
                       ksa_ref, kwa_ref, vst_ref, vwt_ref, kca_ref, vct_ref, ot_ref, ss_ref, ps_ref, sacc_ref,
                       *, t, tq):
    qi = pl.program_id(1)
    n_sel = t // SEL_BLOCK
    n_cmp = t // CMP_STRIDE
    top_n = min(SEL_TOPN, n_sel)
    cols = GROUP_SIZE * tq
    nt = t // tq
    n_win = WINDOW // tq + 1
    n_pad = WINDOW // tq

    @pl.when(qi == 0)
    def _():
        def prepare(src_ref, grow_ref, gcol_ref, const_ref, ka_ref, vt_ref, out_ref, pad_tiles):
            pad = pad_tiles * tq
            x = src_ref[...]
            xt = jnp.transpose(x)
            width = out_ref.shape[1]
            out_ref[KV_HALF:, :] = xt[KV_HALF:, t - width:]
            for g in range(KV_GROUPS):
                lo, hi = g * HEAD_DIM, (g + 1) * HEAD_DIM
                kn = _rms(x[:, lo:hi], grow_ref[...])
                ka_ref[g, pad:pad + t, :] = jnp.concatenate([kn, const_ref[pad:pad + t, :]], axis=1).astype(BF16)
                if pad:
                    ka_ref[g, 0:pad, :] = jnp.concatenate(
                        [jnp.zeros((pad, HEAD_DIM), F32), const_ref[0:pad, :]], axis=1).astype(BF16)
                kt = xt[lo:hi, :]
                ktn = kt * lax.rsqrt(jnp.mean(kt * kt, axis=0, keepdims=True) + NORM_EPS) * gcol_ref[...]
                out_ref[lo:hi, :] = ktn[:, t - width:]
                vt = xt[KV_HALF + lo:KV_HALF + hi, :].astype(BF16)
                base = g * (nt + pad_tiles)
                for i in range(pad_tiles):
                    vt_ref[base + i] = jnp.zeros((HEAD_DIM, tq), BF16)
                for i in range(nt):
                    vt_ref[base + pad_tiles + i] = vt[:, i * tq:(i + 1) * tq]

        prepare(ks_ref, gks_ref, gksc_ref, csel_ref, ksa_ref, vst_ref, kst_out_ref, 0)
        prepare(kw_ref, gkw_ref, gkwc_ref, cwin_ref, kwa_ref, vwt_ref, kwt_out_ref, n_pad)
        for g in range(KV_GROUPS):
            kc = cmp_ref[0, g * n_cmp:(g + 1) * n_cmp, :]
            kca_ref[g] = jnp.concatenate([kc, ccmp_ref[...]], axis=1).astype(BF16)
            vct_ref[g] = _nt(eye_ref[...], cmp_ref[1, g * n_cmp:(g + 1) * n_cmp, :].astype(BF16)).astype(BF16)

    q0 = pl.multiple_of(qi * tq, tq)
    lane4 = lax.broadcasted_iota(jnp.int32, (1, cols), 1) % tq
    key_off = lax.broadcasted_iota(jnp.int32, (tq, cols), 0)
    causal = key_off <= lane4
    win_old = key_off > lane4
    cmp_ok = (lax.broadcasted_iota(jnp.int32, (n_cmp, cols), 0) * CMP_STRIDE + (CMP_BLOCK - 1)) <= q0 + lane4
    qt = jnp.transpose(q_ref[...])
    gt = jax.nn.sigmoid(jnp.transpose(gate_ref[...]))
    jblk = lax.broadcasted_iota(jnp.int32, (n_sel, tq), 0)
    t_lane = q0 + lax.broadcasted_iota(jnp.int32, (n_sel, tq), 1)
    cur = t_lane // SEL_BLOCK
    forced = (jblk == 0) | (jblk == cur) | (jblk == cur - 1)
    visible = jblk * SEL_BLOCK <= t_lane

    o_c, q_sel, q_win = [], [], []
    for g in range(KV_GROUPS):
        heads = []
        for r in range(GROUP_SIZE):
            h = GROUP_SIZE * g + r
            qh = qt[h * HEAD_DIM:(h + 1) * HEAD_DIM, :]
            qh = qh * lax.rsqrt(jnp.mean(qh * qh, axis=0, keepdims=True) + NORM_EPS)
            heads.append(qh * (gq_ref[...] * (HEAD_DIM ** -0.5 * LOG2E)))
        q_top = jnp.concatenate(heads, axis=1)
        q_low = qrows_ref[g]

        def with_mask_rows(rows_, q_top=q_top, q_low=q_low):
            return jnp.concatenate([q_top, rows_, q_low], axis=0).astype(BF16)

        s = _dot(kca_ref[g], with_mask_rows(jnp.zeros((AUG_POS, cols), F32)))
        s = jnp.where(cmp_ok, s, -MASK_BIG)
        p = jnp.where(cmp_ok, jnp.exp2(s - jnp.max(s, axis=0, keepdims=True)), 0.0)
        pc = _normalize(p, jnp.sum(p, axis=0, keepdims=True))
        o_c.append(_dot(vct_ref[g], pc.astype(BF16)))
        imp = pc[:, 0:tq]
        for r in range(1, GROUP_SIZE):
            imp = imp + pc[:, r * tq:(r + 1) * tq]

        p_slc = _dot(mmap_ref[...], imp, HIGHEST)
        score = jnp.where(forced, FORCE_SCORE, jnp.where(visible, p_slc, -1.0))
        cnt = jnp.zeros((n_sel, tq), jnp.int32)
        for k in range(n_sel):
            sk = score[k:k + 1, :]
            beats = (sk > score) | ((sk == score) & (jblk > k))
            cnt = cnt + beats.astype(jnp.int32)
        drop = jnp.where(cnt < top_n, 0.0, -MASK_BIG)
        if n_sel < AUG_POS:
            drop = jnp.concatenate([drop, jnp.zeros((AUG_POS - n_sel, tq), F32)], axis=0)
        q_sel.append(with_mask_rows(jnp.concatenate([drop] * GROUP_SIZE, axis=1)))
        q_win.append(with_mask_rows(jnp.full((AUG_POS, cols), -MASK_BIG, F32)))

    def sel_scores(tile, slot):
        k0 = pl.multiple_of(tile * tq, tq)
        for g in range(KV_GROUPS):
            ss_ref[slot, g] = _dot(ksa_ref[g, pl.ds(k0, tq), :], q_sel[g])

    def sel_values(tile):
        return [_dot(vst_ref[g * nt + tile], ps_ref[g]) for g in range(KV_GROUPS)]

    def sel_softmax(slot, stats, pv, mask=None):
        new_stats = []
        for g in range(KV_GROUPS):
            m, l = stats[g]
            s = ss_ref[slot, g]
            if mask is not None:
                s = jnp.where(mask, s, -MASK_BIG)
            m_new = jnp.maximum(m, jnp.max(s, axis=0, keepdims=True))
            alpha = jnp.exp2(m - m_new)
            p = jnp.exp2(s - m_new)
            sacc_ref[g] = alpha * (sacc_ref[g] + pv[g])
            ps_ref[g] = p.astype(BF16)
            new_stats.append((m_new, alpha * l + jnp.sum(p, axis=0, keepdims=True)))
        return tuple(new_stats)

    def sel_step(slot, next_tile, prev_tile, stats, mask=None):
        pv = sel_values(jnp.maximum(prev_tile, 0))
        if next_tile is not None:
            sel_scores(next_tile, 1 - slot)
        return sel_softmax(slot, stats, pv, mask)

    def sel_finish(stats):
        pv = sel_values(qi)
        for g in range(KV_GROUPS):
            sacc_ref[g] = _normalize(sacc_ref[g] + pv[g], stats[g][1])

    for g in range(KV_GROUPS):
        ps_ref[g] = jnp.zeros((tq, cols), BF16)
        sacc_ref[g] = jnp.zeros((HEAD_DIM, cols), F32)
    sel_scores(0, 0)

    def sel_pair(j, stats):
        stats = sel_step(0, 2 * j + 1, 2 * j - 1, stats)
        return sel_step(1, 2 * j + 2, 2 * j, stats)

    stats0 = ((jnp.full((1, cols), -MASK_BIG, F32), jnp.zeros((1, cols), F32)),) * KV_GROUPS
    stats = lax.fori_loop(0, qi // 2, sel_pair, stats0)

    @pl.when(qi % 2 == 1)
    def _():
        sel_finish(sel_step(1, None, qi - 1, sel_step(0, qi, qi - 2, stats), mask=causal))

    @pl.when(qi % 2 == 0)
    def _():
        sel_finish(sel_step(0, None, qi - 1, stats, mask=causal))

    o_s = [sacc_ref[g] for g in range(KV_GROUPS)]

    win_scores = [[_dot(kwa_ref[g, pl.ds(q0 + i * tq, tq), :], q_win[g]) for i in range(n_win)]
                  for g in range(KV_GROUPS)]
    o_w = []
    for g in range(KV_GROUPS):
        tiles = win_scores[g]
        tiles[0] = jnp.where(win_old, tiles[0], -MASK_BIG)
        tiles[-1] = jnp.where(causal, tiles[-1], -MASK_BIG)
        m = functools.reduce(jnp.maximum, [jnp.max(s, axis=0, keepdims=True) for s in tiles])
        probs = [jnp.exp2(s - m) for s in tiles]
        l = functools.reduce(jnp.add, [jnp.sum(p, axis=0, keepdims=True) for p in probs])
        acc = functools.reduce(jnp.add, [_dot(vwt_ref[g * (nt + n_pad) + qi + i], p.astype(BF16))
                                         for i, p in enumerate(probs)])
        o_w.append(_normalize(acc, l))

    for h in range(NSA_HEADS):
        g, r = divmod(h, GROUP_SIZE)
        sl = slice(r * tq, (r + 1) * tq)
        ot_ref[h * HEAD_DIM:(h + 1) * HEAD_DIM, :] = (
            gt[h:h + 1, :] * o_c[g][:, sl] + gt[NSA_HEADS + h:NSA_HEADS + h + 1, :] * o_s[g][:, sl]
            + gt[2 * NSA_HEADS + h:2 * NSA_HEADS + h + 1, :] * o_w[g][:, sl])
    o_ref[...] = jnp.transpose(ot_ref[...]).astype(o_ref.dtype)


def nsa_prompt_attention(proj, gates, cmp, qk_gain, *, batch, t, tq):
    n_sel = t // SEL_BLOCK
    n_cmp = t // CMP_STRIDE
    nq = t // tq
    nt = t // tq
    n_pad = WINDOW // tq
    keep = min(WINDOW, t)
    assert n_sel <= AUG_POS and t % tq == 0 and WINDOW % tq == 0 and tq % SEL_BLOCK == 0
    n = np.arange(n_cmp)
    mmap = np.zeros((n_sel, n_cmp), np.float32)
    real = n < n_cmp - 1
    np.add.at(mmap, (n[real] // SEL_CHUNKS, n[real]), 1.0)
    nxt = real & ((n + 1) // SEL_CHUNKS < n_sel)
    np.add.at(mmap, ((n[nxt] + 1) // SEL_CHUNKS, n[nxt]), 1.0)
    pos = np.arange(t)
    csel = _key_aug(pos, (pos[:, None] // SEL_BLOCK == np.arange(n_sel)[None, :]).astype(np.float32))
    wpos = np.arange(-WINDOW, t)
    cwin = _key_aug(wpos, (wpos < 0).astype(np.float32)[:, None])
    ccmp = _key_aug(np.arange(n_cmp) * CMP_STRIDE + (CMP_BLOCK - 1))
    const = lambda shape: pl.BlockSpec(shape, lambda b, i: (0,) * len(shape))
    col = lambda v: v.reshape(HEAD_DIM, 1)
    row = lambda v: v.reshape(1, HEAD_DIM)
    return pl.pallas_call(
        functools.partial(_nsa_prompt_kernel, t=t, tq=tq),
        grid=(batch, nq),
        in_specs=[
            pl.BlockSpec((tq, NSA_Q_DIM), lambda b, i: (b * nq + i, 0)),
            pl.BlockSpec((t, NSA_KV_DIM), lambda b, i: (b, 3)),
            pl.BlockSpec((t, NSA_KV_DIM), lambda b, i: (b, 4)),
            pl.BlockSpec((None, 2, KV_GROUPS * n_cmp, HEAD_DIM), lambda b, i: (b, 0, 0, 0)),
            pl.BlockSpec((tq, LANES), lambda b, i: (b * nq + i, 0)),
            const((HEAD_DIM, 1)), const((1, HEAD_DIM)), const((1, HEAD_DIM)), const((HEAD_DIM, 1)),
            const((HEAD_DIM, 1)),
            const((t, AUG)), const((t + WINDOW, AUG)), const((n_cmp, AUG)),
            const((KV_GROUPS, AUG - AUG_POS, GROUP_SIZE * tq)), const((n_sel, n_cmp)), const((HEAD_DIM, HEAD_DIM)),
        ],
        out_specs=[
            pl.BlockSpec((tq, NSA_Q_DIM), lambda b, i: (b * nq + i, 0)),
            pl.BlockSpec((None, NSA_KV_DIM, t), lambda b, i: (b, 0, 0)),
            pl.BlockSpec((None, NSA_KV_DIM, keep), lambda b, i: (b, 0, 0)),
        ],
        out_shape=[jax.ShapeDtypeStruct((batch * t, NSA_Q_DIM), BF16),
                   jax.ShapeDtypeStruct((batch, NSA_KV_DIM, t), F32),
                   jax.ShapeDtypeStruct((batch, NSA_KV_DIM, keep), F32)],
        scratch_shapes=[
            pltpu.VMEM((KV_GROUPS, t, HEAD_DIM + AUG), BF16),
            pltpu.VMEM((KV_GROUPS, t + WINDOW, HEAD_DIM + AUG), BF16),
            pltpu.VMEM((KV_GROUPS * nt, HEAD_DIM, tq), BF16),
            pltpu.VMEM((KV_GROUPS * (nt + n_pad), HEAD_DIM, tq), BF16),
            pltpu.VMEM((KV_GROUPS, n_cmp, HEAD_DIM + AUG), BF16),
            pltpu.VMEM((KV_GROUPS, HEAD_DIM, n_cmp), BF16),
            pltpu.VMEM((NSA_Q_DIM, tq), F32),
            pltpu.VMEM((2, KV_GROUPS, tq, GROUP_SIZE * tq), F32),
            pltpu.VMEM((KV_GROUPS, tq, GROUP_SIZE * tq), BF16),
            pltpu.VMEM((KV_GROUPS, HEAD_DIM, GROUP_SIZE * tq), F32),
        ],
        compiler_params=_cparams(("parallel", "arbitrary")),
        name="nsa_prompt_attention",
    )(proj, proj, proj, cmp, gates, col(qk_gain[0]), row(qk_gain[2]), row(qk_gain[3]), col(qk_gain[2]),
      col(qk_gain[3]), jnp.asarray(csel), jnp.asarray(cwin), jnp.asarray(ccmp), jnp.asarray(_query_aug_rows(tq)),
      jnp.asarray(mmap), jnp.eye(HEAD_DIM, dtype=BF16))


def _nsa_weights(w_in, qk_gain, cmp_pe, cmp_w1, cmp_w2, w_out):
    n_main = NSA_Q_DIM + 3 * NSA_KV_DIM
    zeros_kv = jnp.zeros((KV_HALF,), F32)
    ones_kv = jnp.ones((KV_HALF,), F32)
    head_gain = jnp.concatenate([
        jnp.tile(qk_gain[0], NSA_HEADS), zeros_kv, zeros_kv,
        jnp.tile(qk_gain[2], KV_GROUPS), zeros_kv, jnp.tile(qk_gain[3], KV_GROUPS), zeros_kv]).reshape(1, n_main)
    head_mask = jnp.concatenate([
        jnp.ones((NSA_Q_DIM,), F32), zeros_kv, zeros_kv, ones_kv, zeros_kv, ones_kv, zeros_kv]).reshape(1, n_main)
    w_gate = jnp.pad(w_in[:, n_main:], ((0, 0), (0, LANES - 3 * NSA_HEADS)))
    return dict(
        w_main=w_in[:, :n_main].astype(BF16), w_gate=w_gate.astype(BF16), head_gain=head_gain, head_mask=head_mask,
        pe=cmp_pe.reshape(2, 2, CMP_STRIDE // CMP_PACK, CMP_PACK * HEAD_DIM),
        w1=cmp_w1.reshape(2, 2, CMP_STRIDE // CMP_PACK, CMP_PACK * HEAD_DIM, CMP_HIDDEN).astype(BF16),
        w2=cmp_w2.astype(BF16), kgain=qk_gain[1], qk_gain=qk_gain, w_out=w_out.astype(BF16))


def _nsa_project(x, gain, w, *, tm, head_norm):
    if head_norm:
        proj = norm_matmul(x, gain, w["w_main"], w["head_gain"], w["head_mask"], tm=tm, tn=512)
    else:
        proj = norm_matmul(x, gain, w["w_main"], tm=tm, tn=1280)
    return proj, norm_matmul(x, gain, w["w_gate"], tm=tm, tn=LANES)


def nsa_prompt_layer(x, gain, w, *, batch, t):
    tm = min(1024, batch * t)
    proj, gates = _nsa_project(x, gain, w, tm=tm, head_norm=False)
    cmp, cmp_t = compress_rows(proj, 2, w["pe"], w["w1"], w["w2"], w["kgain"], batch=batch, t=t)
    o, sel_t, win_t = nsa_prompt_attention(proj, gates, cmp, w["qk_gain"], batch=batch, t=t, tq=128)
    return matmul_residual(o, w["w_out"], x, tm=tm), cmp_t, sel_t, win_t


PAGES_PER_STEP = 8
CHUNKS_PER_PAGE = PAGE_SIZE // CMP_STRIDE


def _compress_pages_kernel(pt_ref, *refs, n_chunk):
    del pt_ref
    pages = refs[:PAGES_PER_STEP]
    pe_ref, w1_ref, w2_ref, kg_ref, o_ref, xs_ref, pre_ref = refs[PAGES_PER_STEP:]
    s = pl.program_id(1)
    n_pair = NSA_KV_DIM // LANES
    step_chunks = PAGES_PER_STEP * CHUNKS_PER_PAGE
    for i, page in enumerate(pages):
        for c in range(n_pair):
            xs_ref[c, i * PAGE_SIZE:(i + 1) * PAGE_SIZE, :] = jnp.transpose(page[c * LANES:(c + 1) * LANES, :])
    for kv in range(2):
        acc = _compress_accumulate(xs_ref, step_chunks, pe_ref, w1_ref, kv)
        for m in range(2):
            for g in range(KV_GROUPS):
                pre_ref[kv, m, pl.ds(pl.multiple_of(g * n_chunk + s * step_chunks, step_chunks), step_chunks), :] = (
                    acc[m][g * step_chunks:(g + 1) * step_chunks])

    @pl.when(s == pl.num_programs(1) - 1)
    def _():
        for kv in range(2):
            o_ref[kv] = _compress_finish([pre_ref[kv, 0], pre_ref[kv, 1]], w2_ref, kg_ref, kv)


def compress_pages(cache_t, layer, page_table_flat, pe, w1, w2, kgain, *, batch, n_pages):
    n_chunk = n_pages * CHUNKS_PER_PAGE
    n_steps = n_pages // PAGES_PER_STEP

    def page_spec(i):
        return pl.BlockSpec((None, None, NSA_KV_DIM, PAGE_SIZE),
                            lambda b, s, pt: (layer, pt[b * n_pages + s * PAGES_PER_STEP + i], 0, 0))

    grid_spec = pltpu.PrefetchScalarGridSpec(
        num_scalar_prefetch=1,
        grid=(batch, n_steps),
        in_specs=[page_spec(i) for i in range(PAGES_PER_STEP)] + [
            pl.BlockSpec(pe.shape, lambda b, s, pt: (0, 0, 0, 0)),
            pl.BlockSpec(w1.shape, lambda b, s, pt: (0, 0, 0, 0, 0)),
            pl.BlockSpec(w2.shape, lambda b, s, pt: (0, 0, 0)),
            pl.BlockSpec((1, HEAD_DIM), lambda b, s, pt: (0, 0)),
        ],
        out_specs=pl.BlockSpec((None, 2, KV_GROUPS * n_chunk, HEAD_DIM), lambda b, s, pt: (b, 0, 0, 0)),
        scratch_shapes=[pltpu.VMEM((NSA_KV_DIM // LANES, PAGES_PER_STEP * PAGE_SIZE, LANES), F32),
                        pltpu.VMEM((2, 2, KV_GROUPS * n_chunk, CMP_HIDDEN), F32)],
    )
    return pl.pallas_call(
        functools.partial(_compress_pages_kernel, n_chunk=n_chunk),
        grid_spec=grid_spec,
        out_shape=jax.ShapeDtypeStruct((batch, 2, KV_GROUPS * n_chunk, HEAD_DIM), F32),
        compiler_params=_cparams(("parallel", "arbitrary")),
        name="compress_pages",
    )(page_table_flat, *([cache_t] * PAGES_PER_STEP), pe, w1, w2, kgain.reshape(1, HEAD_DIM))


Q_PAD = 8
HEAD_PAD = 16


def _nsa_sample_select_kernel(q_ref, cmp_ref, mmap_ref, oc_ref, idx_ref, score_ref, *, past_len, n_cmp, n_sel):
    rows = GROUP_SIZE * Q_PAD
    n_selp = score_ref.shape[0]
    t_max = past_len + Q_PAD - 1
    row = lax.broadcasted_iota(jnp.int32, (rows, 1), 0)
    t_col = past_len + row % Q_PAD
    jblk = lax.broadcasted_iota(jnp.int32, (n_selp, Q_PAD), 0)
    t_lane = past_len + lax.broadcasted_iota(jnp.int32, (n_selp, Q_PAD), 1)
    cur = t_lane // SEL_BLOCK
    forced = (jblk == 0) | (jblk == cur) | (jblk == cur - 1)
    visible = jblk * SEL_BLOCK <= t_lane
    for g in range(KV_GROUPS):
        qg = (q_ref[g] * (HEAD_DIM ** -0.5)).astype(BF16)
        slope = _slope_col(g, rows, Q_PAD)
        kc = cmp_ref[0, g * n_cmp:(g + 1) * n_cmp, :].astype(BF16)
        vc = cmp_ref[1, g * n_cmp:(g + 1) * n_cmp, :].astype(BF16)
        kpos = lax.broadcasted_iota(jnp.int32, (1, n_cmp), 1) * CMP_STRIDE + (CMP_BLOCK - 1)
        s = _nt(qg, kc) + slope * (kpos - t_max).astype(F32)
        mask = kpos <= t_col
        s = jnp.where(mask, s, NEG_INF)
        p = jnp.where(mask, jnp.exp(s - jnp.max(s, axis=1, keepdims=True)), 0.0)
        pc = _normalize(p, jnp.sum(p, axis=1, keepdims=True))
        oc_ref[g] = _dot(pc.astype(BF16), vc)
        imp = pc[0:Q_PAD]
        for r in range(1, GROUP_SIZE):
            imp = imp + pc[r * Q_PAD:(r + 1) * Q_PAD]
        p_slc = _nt(mmap_ref[...], imp, HIGHEST)
        score = jnp.where(forced, FORCE_SCORE, jnp.where(visible, p_slc, -1.0))
        score_ref[:, g * Q_PAD:(g + 1) * Q_PAD] = jnp.where(jblk < n_sel, score, -2.0)

    score = score_ref[...]
    jall = lax.broadcasted_iota(jnp.int32, score.shape, 0)

    def rank_body(k, cnt):
        sk = score_ref[pl.ds(k, 1), :]
        beats = (sk > score) | ((sk == score) & (jall > k))
        return cnt + beats.astype(jnp.int32)

    cnt = lax.fori_loop(0, n_sel, rank_body, jnp.zeros(score.shape, jnp.int32), unroll=4)
    slots = [jnp.sum(jnp.where(cnt == sl, jall, 0), axis=0, keepdims=True) for sl in range(SEL_TOPN)]
    idx_ref[...] = jnp.concatenate(slots, axis=0)


def nsa_sample_select(q_stack, cmp, *, batch, past_len, t_new):
    n_cmp = past_len // CMP_STRIDE
    n_sel = -(-(past_len + t_new) // SEL_BLOCK)
    n_selp = -(-n_sel // 8) * 8
    n = np.arange(n_cmp)
    mmap = np.zeros((n_selp, n_cmp), np.float32)
    real = n < n_cmp - 1
    np.add.at(mmap, (n[real] // SEL_CHUNKS, n[real]), 1.0)
    np.add.at(mmap, ((n[real] + 1) // SEL_CHUNKS, n[real]), 1.0)
    rows = GROUP_SIZE * Q_PAD
    return pl.pallas_call(
        functools.partial(_nsa_sample_select_kernel, past_len=past_len, n_cmp=n_cmp, n_sel=n_sel),
        grid=(batch,),
        in_specs=[
            pl.BlockSpec((None, KV_GROUPS, rows, HEAD_DIM), lambda b: (b, 0, 0, 0)),
            pl.BlockSpec((None, 2, KV_GROUPS * n_cmp, HEAD_DIM), lambda b: (b, 0, 0, 0)),
            pl.BlockSpec((n_selp, n_cmp), lambda b: (0, 0)),
        ],
        out_specs=[
            pl.BlockSpec((None, KV_GROUPS, rows, HEAD_DIM), lambda b: (b, 0, 0, 0)),
            pl.BlockSpec((None, SEL_TOPN, KV_GROUPS * Q_PAD), lambda b: (b, 0, 0)),
        ],
        out_shape=[jax.ShapeDtypeStruct((batch, KV_GROUPS, rows, HEAD_DIM), F32),
                   jax.ShapeDtypeStruct((batch, SEL_TOPN, KV_GROUPS * Q_PAD), jnp.int32)],
        scratch_shapes=[pltpu.VMEM((n_selp, KV_GROUPS * Q_PAD), F32)],
        compiler_params=_cparams(("parallel",)),
        name="nsa_sample_select",
    )(q_stack, cmp, jnp.asarray(mmap))


def _nsa_sample_attend_kernel(pt_ref, idx_ref, *refs, past_len, t_new, n_past_blocks):
    del pt_ref
    blocks = refs[:SEL_TOPN]
    (q_ref, slope_ref, snew_ref, win_ref, wnew_ref, wnew_t_ref, oc_ref, gate_ref,
     o_ref, wout_ref, kcat_ref, vcat_ref) = refs[SEL_TOPN:]
    b, g, qi = pl.program_id(0), pl.program_id(1), pl.program_id(2)
    t = past_len + qi
    q = (q_ref[...] * (HEAD_DIM ** -0.5)).astype(BF16)
    slope = slope_ref[:, 0:1]
    lane = lax.broadcasted_iota(jnp.int32, (1, LANES), 1)
    new_pos = past_len + lane
    new_ok = (lane < t_new) & (new_pos <= t)

    def attend(tiles, v_new):
        m = jnp.full((HEAD_PAD, 1), NEG_INF, F32)
        for s, mask, _ in tiles:
            m = jnp.maximum(m, jnp.max(jnp.where(mask, s, NEG_INF), axis=1, keepdims=True))
        l = jnp.zeros((HEAD_PAD, 1), F32)
        acc = jnp.zeros((HEAD_PAD, HEAD_DIM), F32)
        for i, (s, mask, v) in enumerate(tiles):
            p = jnp.where(mask, jnp.exp(jnp.where(mask, s, NEG_INF) - m), 0.0)
            l = l + jnp.sum(p, axis=1, keepdims=True)
            if i == len(tiles) - 1:
                acc = acc + _dot(p.astype(BF16), v_new)
            else:
                acc = acc + _nt(p.astype(BF16), v)
        return _normalize(acc, l)

    base = ((b * KV_GROUPS + g) * t_new + qi) * SEL_TOPN
    kpos, mask = [], []
    for sl, blk_ref in enumerate(blocks):
        blk = idx_ref[base + sl]
        kcat_ref[:, sl * PAGE_SIZE:(sl + 1) * PAGE_SIZE] = blk_ref[0].astype(BF16)
        vcat_ref[:, sl * PAGE_SIZE:(sl + 1) * PAGE_SIZE] = blk_ref[1].astype(BF16)
        kpos.append((blk // 2) * PAGE_SIZE + lane)
        mask.append((lane // SEL_BLOCK == blk % 2) & (blk < n_past_blocks) & (kpos[-1] <= t))
    kpos = jnp.concatenate(kpos, axis=1)
    mask = jnp.concatenate([m_.astype(jnp.int32) for m_ in mask], axis=1) > 0
    s_sel = _dot(q, kcat_ref[...]) + slope * (kpos - t).astype(F32)
    s_new = _nt(q, snew_ref[0].astype(BF16)) + slope * (new_pos - t).astype(F32)
    o_s = attend([(s_sel, mask, vcat_ref[...]), (s_new, new_ok, None)], snew_ref[1].astype(BF16))

    buf = win_ref.shape[2]
    wlane = lax.broadcasted_iota(jnp.int32, (1, buf), 1)
    wpos = past_len - buf + wlane
    dist = t - wpos
    s_w = _dot(q, win_ref[0].astype(BF16)) + slope * (wpos - t).astype(F32)
    s_wn = _nt(q, wnew_ref[0].astype(BF16)) + slope * (new_pos - t).astype(F32)
    o_w = attend([(s_w, (dist >= 0) & (dist < WINDOW), win_ref[1].astype(BF16)),
                  (s_wn, new_ok & (t - new_pos < WINDOW), None)], wnew_ref[1].astype(BF16))

    gates = jax.nn.sigmoid(gate_ref[...])
    o_ref[...] = gates[:, 0:1] * oc_ref[...] + gates[:, 1:2] * o_s + gates[:, 2:3] * o_w

    @pl.when(qi == 0)
    def _():
        for kv in range(2):
            rolled = pltpu.roll(win_ref[kv], buf - t_new, axis=1)
            wout_ref[kv, :, 0:buf - LANES] = rolled[:, 0:buf - LANES]
            wout_ref[kv, :, buf - LANES:] = jnp.where(lane >= LANES - t_new, wnew_t_ref[kv], rolled[:, buf - LANES:])


def nsa_sample_attend(cache_t, win_t, layer, page_table_flat, idx_flat, q_heads, slopes, sel_new, win_new,
                      win_new_t, o_c, gates, *, batch, past_len, t_new):
    n_pages = past_len // PAGE_SIZE
    n_past_blocks = past_len // SEL_BLOCK
    buf = win_t.shape[-1]

    def block_spec(sl):
        def index(b, g, q, pt, idx):
            blk = idx[((b * KV_GROUPS + g) * t_new + q) * SEL_TOPN + sl]
            page = pt[b * n_pages + jnp.minimum(blk, n_past_blocks - 1) // 2]
            return (layer, page, 0, g, 0, 0)
        return pl.BlockSpec((None, None, 2, None, HEAD_DIM, PAGE_SIZE), index)

    per_bg = lambda shape: pl.BlockSpec((None, None) + shape, lambda b, g, q, pt, idx: (b, g) + (0,) * len(shape))
    per_bgq = lambda shape: pl.BlockSpec((None, None, None) + shape,
                                         lambda b, g, q, pt, idx: (b, g, q) + (0,) * len(shape))
    grid_spec = pltpu.PrefetchScalarGridSpec(
        num_scalar_prefetch=2,
        grid=(batch, KV_GROUPS, t_new),
        in_specs=[block_spec(sl) for sl in range(SEL_TOPN)] + [
            per_bgq((HEAD_PAD, HEAD_DIM)),
            pl.BlockSpec((None, HEAD_PAD, LANES), lambda b, g, q, pt, idx: (g, 0, 0)),
            per_bg((2, LANES, HEAD_DIM)),
            pl.BlockSpec((None, None, 2, None, HEAD_DIM, buf), lambda b, g, q, pt, idx: (layer, b, 0, g, 0, 0)),
            per_bg((2, LANES, HEAD_DIM)),
            per_bg((2, HEAD_DIM, LANES)),
            per_bgq((HEAD_PAD, HEAD_DIM)),
            per_bgq((HEAD_PAD, LANES)),
        ],
        out_specs=[
            per_bgq((HEAD_PAD, HEAD_DIM)),
            pl.BlockSpec((None, 2, None, HEAD_DIM, buf), lambda b, g, q, pt, idx: (b, 0, g, 0, 0)),
        ],
        scratch_shapes=[pltpu.VMEM((HEAD_DIM, SEL_TOPN * PAGE_SIZE), BF16)] * 2,
    )
    return pl.pallas_call(
        functools.partial(_nsa_sample_attend_kernel, past_len=past_len, t_new=t_new, n_past_blocks=n_past_blocks),
        grid_spec=grid_spec,
        out_shape=[jax.ShapeDtypeStruct((batch, KV_GROUPS, t_new, HEAD_PAD, HEAD_DIM), F32),
                   jax.ShapeDtypeStruct((batch, 2, KV_GROUPS, HEAD_DIM, buf), F32)],
        compiler_params=_cparams(("parallel", "arbitrary", "arbitrary")),
        name="nsa_sample_attend",
    )(page_table_flat, idx_flat, *([cache_t] * SEL_TOPN), q_heads, slopes, sel_new, win_t, win_new,
      win_new_t, o_c, gates)


def nsa_sample_layer(x, gain, w, cmp_t, sel_t, win_t, layer, page_table_flat, *, batch, t_new, past_len):
    m = batch * t_new
    proj, gates = _nsa_project(x, gain, w, tm=m, head_norm=True)
    cmp = compress_pages(cmp_t.reshape(cmp_t.shape[0], cmp_t.shape[1], NSA_KV_DIM, PAGE_SIZE), layer,
                         page_table_flat, w["pe"], w["w1"], w["w2"], w["kgain"], batch=batch,
                         n_pages=past_len // PAGE_SIZE)
    q = proj[:, :NSA_Q_DIM].reshape(batch, t_new, KV_GROUPS, GROUP_SIZE, HEAD_DIM)
    q_stack = jnp.pad(q.transpose(0, 2, 3, 1, 4), ((0, 0), (0, 0), (0, 0), (0, Q_PAD - t_new), (0, 0)))
    o_c, idx = nsa_sample_select(q_stack.reshape(batch, KV_GROUPS, GROUP_SIZE * Q_PAD, HEAD_DIM), cmp,
                                 batch=batch, past_len=past_len, t_new=t_new)
    pad_heads = ((0, 0), (0, 0), (0, 0), (0, HEAD_PAD - GROUP_SIZE), (0, 0))
    o_c = o_c.reshape(batch, KV_GROUPS, GROUP_SIZE, Q_PAD, HEAD_DIM)[:, :, :, :t_new].transpose(0, 1, 3, 2, 4)
    idx_flat = idx.reshape(batch, SEL_TOPN, KV_GROUPS, Q_PAD)[..., :t_new].transpose(0, 2, 3, 1).reshape(-1)
    q_heads = jnp.pad(q.transpose(0, 2, 1, 3, 4), pad_heads)

    def new_rows(cols):
        a = cols.reshape(batch, t_new, 2, KV_GROUPS, HEAD_DIM).transpose(0, 3, 2, 1, 4)
        return jnp.pad(a, ((0, 0), (0, 0), (0, 0), (0, LANES - t_new), (0, 0)))

    kvs = proj[:, NSA_Q_DIM + NSA_KV_DIM:NSA_Q_DIM + 2 * NSA_KV_DIM]
    kvw = proj[:, NSA_Q_DIM + 2 * NSA_KV_DIM:]
    win_new_t = kvw.reshape(batch, t_new, 2, KV_GROUPS, HEAD_DIM).transpose(0, 3, 2, 4, 1)
    win_new_t = jnp.pad(win_new_t, ((0, 0), (0, 0), (0, 0), (0, 0), (LANES - t_new, 0)))
    g_arr = gates[:, :3 * NSA_HEADS].reshape(batch, t_new, 3, KV_GROUPS, GROUP_SIZE).transpose(0, 3, 1, 4, 2)
    g_arr = jnp.pad(g_arr, ((0, 0), (0, 0), (0, 0), (0, HEAD_PAD - GROUP_SIZE), (0, LANES - 3)))
    slopes = np.zeros((KV_GROUPS, HEAD_PAD, LANES), np.float32)
    slopes[:, :GROUP_SIZE, :] = np.asarray(_SLOPES, np.float32).reshape(KV_GROUPS, GROUP_SIZE, 1)
    o, win_out = nsa_sample_attend(sel_t, win_t, layer, page_table_flat, idx_flat, q_heads, jnp.asarray(slopes),
                                   new_rows(kvs), new_rows(kvw), win_new_t, jnp.pad(o_c, pad_heads), g_arr,
                                   batch=batch, past_len=past_len, t_new=t_new)
    o = o[:, :, :, :GROUP_SIZE].transpose(0, 2, 1, 3, 4).reshape(m, NSA_Q_DIM)
    return matmul_residual(o.astype(BF16), w["w_out"], x, tm=m), proj, win_out


def _expand_heads(x, e):
    x1 = x.astype(BF16)
    rest = x - x1.astype(F32)
    x2 = rest.astype(BF16)
    x3 = (rest - x2.astype(F32)).astype(BF16)
    return (_dot(x1, e) + _dot(x2, e)) + _dot(x3, e)


def _ssd_kernel(z_ref, x_ref, bc_ref, dt_ref, cw_ref, cb_ref, dtb_ref, alog_ref, drow_ref, nw_ref, cinit_ref,
                hinit_ref, tril_ref, e_ref, y_ref, hout_ref, stage_ref, h_ref, yacc_ref, *, t_valid):
    L = SSM_CHUNK
    c = pl.program_id(1)

    @pl.when(c == 0)
    def _():
        stage_ref[0:8, :] = cinit_ref[...]
        h_ref[...] = hinit_ref[...]

    stage_ref[8:8 + L, 0:SSM_D_INNER] = x_ref[...]
    stage_ref[8:8 + L, SSM_D_INNER:] = bc_ref[...]
    conv = cb_ref[...]
    for k in range(SSM_CONV):
        conv = conv + stage_ref[5 + k:5 + k + L, :] * cw_ref[k:k + 1, :]
    stage_ref[0:8, :] = stage_ref[L:L + 8, :]
    conv = conv * jax.nn.sigmoid(conv)
    xh = conv[:, :SSM_D_INNER]

    row = lax.broadcasted_iota(jnp.int32, (L, LANES), 0)
    dt_in = dt_ref[...] + dtb_ref[...]
    dt = jnp.maximum(dt_in, 0.0) + jnp.log1p(jnp.exp(-jnp.abs(dt_in)))
    dt = jnp.where(row < t_valid, dt, 0.0)
    da = dt * -jnp.exp(alog_ref[...])
    acum = _dot(tril_ref[...], da, HIGHEST)
    acum_t = jnp.transpose(acum)
    acum_last = acum[L - 1:L, :]
    dt_full = _expand_heads(dt, e_ref[...])
    grow_full = _expand_heads(jnp.exp(acum), e_ref[...])
    toend_full = _expand_heads(jnp.exp(acum_last - acum), e_ref[...])
    last_t = jnp.exp(acum_t[:, L - 1:L])
    xdt = xh * dt_full
    causal = lax.broadcasted_iota(jnp.int32, (L, L), 0) >= lax.broadcasted_iota(jnp.int32, (L, L), 1)

    for g in range(SSM_GROUPS):
        gl, gh = g * SSM_GW, (g + 1) * SSM_GW
        bm = conv[:, SSM_D_INNER + g * SSM_STATE:SSM_D_INNER + (g + 1) * SSM_STATE].astype(BF16)
        cm = conv[:, SSM_D_INNER + (SSM_GROUPS + g) * SSM_STATE:
                  SSM_D_INNER + (SSM_GROUPS + g + 1) * SSM_STATE].astype(BF16)
        cb = _nt(cm, bm)
        xdt_g = xdt[:, gl:gh]
        for r in range(SSM_HPG):
            h = g * SSM_HPG + r
            seg = acum[:, h:h + 1] - acum_t[h:h + 1, :]
            decay = jnp.exp(jnp.where(causal, seg, NEG_INF))
            yacc_ref[:, gl + r * SSM_HEAD_DIM:gl + (r + 1) * SSM_HEAD_DIM] = _dot(
                (cb * decay).astype(BF16), xdt_g[:, r * SSM_HEAD_DIM:(r + 1) * SSM_HEAD_DIM].astype(BF16))
        h_g = h_ref[gl:gh, :]
        y_state = _nt(cm, h_g.astype(BF16)) * grow_full[:, gl:gh]
        yacc_ref[:, gl:gh] = yacc_ref[:, gl:gh] + y_state
        xw = (xdt_g * toend_full[:, gl:gh]).astype(BF16)
        h_add = lax.dot_general(xw, bm, (((0,), (0,)), ((), ())), preferred_element_type=F32)
        for r in range(SSM_HPG):
            h = g * SSM_HPG + r
            rl, rh = r * SSM_HEAD_DIM, (r + 1) * SSM_HEAD_DIM
            keep = jnp.broadcast_to(last_t[h:h + 1, :], (SSM_HEAD_DIM, SSM_STATE))
            h_ref[gl + rl:gl + rh, :] = h_g[rl:rh] * keep + h_add[rl:rh]

    zf = z_ref[...]
    gated = (yacc_ref[...] + xh * drow_ref[...]) * (zf * jax.nn.sigmoid(zf))
    for g in range(SSM_GROUPS):
        gl, gh = g * SSM_GW, (g + 1) * SSM_GW
        y_ref[:, gl:gh] = _rms(gated[:, gl:gh], nw_ref[:, gl:gh]).astype(y_ref.dtype)

    @pl.when(c == pl.num_programs(1) - 1)
    def _():
        hout_ref[...] = h_ref[...]


def ssd_mixer(proj, dt_raw, conv_init, h_init, w, *, batch, n_chunks, t_valid):
    L = SSM_CHUNK
    rows = batch * n_chunks * L
    conv_dim = SSM_D_INNER + SSM_BC
    tril = np.tril(np.ones((L, L), np.float32))
    e = np.zeros((LANES, SSM_D_INNER), np.float32)
    e[np.arange(SSM_D_INNER) // SSM_HEAD_DIM, np.arange(SSM_D_INNER)] = 1.0
    const = lambda shape: pl.BlockSpec(shape, lambda b, c: (0,) * len(shape))
    return pl.pallas_call(
        functools.partial(_ssd_kernel, t_valid=t_valid),
        grid=(batch, n_chunks),
        in_specs=[
            pl.BlockSpec((L, SSM_D_INNER), lambda b, c: (b * n_chunks + c, 0)),
            pl.BlockSpec((L, SSM_D_INNER), lambda b, c: (b * n_chunks + c, 1)),
            pl.BlockSpec((L, SSM_BC), lambda b, c: (b * n_chunks + c, 2 * SSM_D_INNER // SSM_BC)),
            pl.BlockSpec((L, LANES), lambda b, c: (b * n_chunks + c, 0)),
            const((SSM_CONV, conv_dim)), const((1, conv_dim)), const((1, LANES)), const((1, LANES)),
            const((1, SSM_D_INNER)), const((1, SSM_D_INNER)),
            pl.BlockSpec((None, 8, conv_dim), lambda b, c: (b, 0, 0)),
            pl.BlockSpec((None, SSM_D_INNER, SSM_STATE), lambda b, c: (b, 0, 0)),
            const((L, L)), const((LANES, SSM_D_INNER)),
        ],
        out_specs=[
            pl.BlockSpec((L, SSM_D_INNER), lambda b, c: (b * n_chunks + c, 0)),
            pl.BlockSpec((None, SSM_D_INNER, SSM_STATE), lambda b, c: (b, 0, 0)),
        ],
        out_shape=[jax.ShapeDtypeStruct((rows, SSM_D_INNER), BF16),
                   jax.ShapeDtypeStruct((batch, SSM_D_INNER, SSM_STATE), F32)],
        scratch_shapes=[pltpu.VMEM((L + 8, conv_dim), F32), pltpu.VMEM((SSM_D_INNER, SSM_STATE), F32),
                        pltpu.VMEM((L, SSM_D_INNER), F32)],
        compiler_params=_cparams(("parallel", "arbitrary")),
        name="ssd_mixer",
    )(proj, proj, proj, dt_raw, w["conv_w"], w["conv_b"], w["dt_bias"], w["a_log"], w["d_row"], w["norm_w"],
      conv_init, h_init, jnp.asarray(tril), jnp.asarray(e, BF16))


def _ssd_weights(w_in, conv_w, conv_b, dt_bias, a_log, d_skip, norm_w, w_out):
    n_main = 2 * SSM_D_INNER + SSM_BC
    pad = lambda v: jnp.pad(v, (0, LANES - SSM_HEADS)).reshape(1, LANES)
    return dict(
        w_main=w_in[:, :n_main].astype(BF16),
        w_dt=jnp.pad(w_in[:, n_main:], ((0, 0), (0, LANES - SSM_HEADS))).astype(BF16),
        conv_w=conv_w, conv_b=conv_b.reshape(1, -1), dt_bias=pad(dt_bias), a_log=pad(a_log),
        d_row=jnp.repeat(d_skip, SSM_HEAD_DIM).reshape(1, SSM_D_INNER), norm_w=norm_w.reshape(1, SSM_D_INNER),
        w_out=w_out.astype(BF16))


def ssd_layer(x, gain, w, conv_state, h_init, *, batch, t):
    tm = min(1024, x.shape[0])
    proj = norm_matmul(x, gain, w["w_main"], tm=tm, tn=1024)
    dt_raw = norm_matmul(x, gain, w["w_dt"], tm=tm, tn=LANES)
    conv_init = jnp.pad(conv_state, ((0, 0), (8 - (SSM_CONV - 1), 0), (0, 0)))
    if t % SSM_CHUNK == 0:
        y, h_last = ssd_mixer(proj, dt_raw, conv_init, h_init, w, batch=batch, n_chunks=t // SSM_CHUNK,
                              t_valid=SSM_CHUNK)
    else:
        pad_rows = lambda a: jnp.pad(a.reshape(batch, t, -1), ((0, 0), (0, SSM_CHUNK - t), (0, 0))).reshape(
            batch * SSM_CHUNK, -1)
        y, h_last = ssd_mixer(pad_rows(proj), pad_rows(dt_raw), conv_init, h_init, w, batch=batch, n_chunks=1,
                              t_valid=t)
        y = y.reshape(batch, SSM_CHUNK, -1)[:, :t].reshape(batch * t, -1)
    return matmul_residual(y, w["w_out"], x, tm=tm), proj, h_last


def kernel(x_prompt, x_sample, cache_kv_cmp, cache_kv_sel, cache_kv_win, state_conv, state_ssm, page_table,
           norm_mix, norm_ffn, nsa_w_in, nsa_qk_gain, nsa_cmp_pe, nsa_cmp_w1, nsa_cmp_w2, nsa_w_out,
           ssd_w_in, ssd_conv_w, ssd_conv_b, ssd_dt_bias, ssd_a_log, ssd_d, ssd_norm, ssd_w_out,
           ffn_w_gate, ffn_w_up, ffn_w_down):
    bp, tp = x_prompt.shape[0], x_prompt.shape[1]
    bs, ts = x_sample.shape[0], x_sample.shape[1]
    depth = norm_mix.shape[0]
    past_len = page_table.shape[1] * PAGE_SIZE
    conv_dim = SSM_D_INNER + SSM_BC
    assert tp % SSM_CHUNK == 0 and ts < SSM_CHUNK and ts <= Q_PAD and ts >= SSM_CONV - 1
    xp = x_prompt.reshape(bp * tp, D_MODEL)
    xs = x_sample.reshape(bs * ts, D_MODEL)
    pt_flat = page_table.reshape(-1)
    feature_major = lambda a: a.transpose(0, 1, 3, 4, 5, 2)
    cmp_t, sel_t, win_t = feature_major(cache_kv_cmp), feature_major(cache_kv_sel), feature_major(cache_kv_win)
    kv_shape = (2, KV_GROUPS, HEAD_DIM)
    outs = {k: [] for k in ("cmp_p", "sel_p", "win_p", "conv_p", "ssm_p", "cmp_s", "sel_s", "win_s", "conv_s",
                            "ssm_s")}
    for i in range(depth):
        l = i // 2
        if i % 2 == 0:
            w = _nsa_weights(nsa_w_in[l], nsa_qk_gain[l], nsa_cmp_pe[l], nsa_cmp_w1[l], nsa_cmp_w2[l], nsa_w_out[l])
            xp, cmp_rows, sel_rows, win_rows = nsa_prompt_layer(xp, norm_mix[i], w, batch=bp, t=tp)
            row_major = lambda a: a.reshape((bp,) + kv_shape + (a.shape[-1],)).transpose(0, 4, 1, 2, 3)
            outs["cmp_p"].append(row_major(cmp_rows))
            outs["sel_p"].append(row_major(sel_rows))
            outs["win_p"].append(row_major(win_rows))
            kv = lambda a, b, t, j: a[:, NSA_Q_DIM + j * NSA_KV_DIM:NSA_Q_DIM + (j + 1) * NSA_KV_DIM].reshape(
                (b, t) + kv_shape)
            xs, proj_s, win_out = nsa_sample_layer(xs, norm_mix[i], w, cmp_t, sel_t, win_t, l, pt_flat, batch=bs,
                                                   t_new=ts, past_len=past_len)
            outs["cmp_s"].append(kv(proj_s, bs, ts, 0))
            outs["sel_s"].append(kv(proj_s, bs, ts, 1))
            outs["win_s"].append(win_out.transpose(0, 4, 1, 2, 3))
        else:
            w = _ssd_weights(ssd_w_in[l], ssd_conv_w[l], ssd_conv_b[l], ssd_dt_bias[l], ssd_a_log[l], ssd_d[l],
                             ssd_norm[l], ssd_w_out[l])
            xp, proj, h_last = ssd_layer(xp, norm_mix[i], w, jnp.zeros((bp, SSM_CONV - 1, conv_dim), F32),
                                         jnp.zeros((bp, SSM_D_INNER, SSM_STATE), F32), batch=bp, t=tp)
            outs["conv_p"].append(proj.reshape(bp, tp, -1)[:, tp - (SSM_CONV - 1):, SSM_D_INNER:])
            outs["ssm_p"].append(h_last.reshape(bp, SSM_HEADS, SSM_HEAD_DIM, SSM_STATE))
            xs, proj_s, h_last = ssd_layer(xs, norm_mix[i], w, state_conv[l],
                                           state_ssm[l].reshape(bs, SSM_D_INNER, SSM_STATE), batch=bs, t=ts)
            outs["conv_s"].append(proj_s.reshape(bs, ts, -1)[:, ts - (SSM_CONV - 1):, SSM_D_INNER:])
            outs["ssm_s"].append(h_last.reshape(bs, SSM_HEADS, SSM_HEAD_DIM, SSM_STATE))
        wg, wu, wd = ffn_w_gate[i].astype(BF16), ffn_w_up[i].astype(BF16), ffn_w_down[i].astype(BF16)
        xp = ffn(xp, norm_ffn[i], wg, wu, wd, tm=1024, th=1408)
        xs = ffn(xs, norm_ffn[i], wg, wu, wd, tm=bs * ts, th=256)
    order = ("cmp_p", "sel_p", "win_p", "conv_p", "ssm_p", "cmp_s", "sel_s", "win_s", "conv_s", "ssm_s")
    return (xp.reshape(bp, tp, D_MODEL), xs.reshape(bs, ts, D_MODEL)) + tuple(jnp.stack(outs[k]) for k in order)
```

```python
import functools
import math

import jax
import jax.numpy as jnp
import ml_dtypes
import numpy as np
from jax import lax
from jax.experimental import pallas as pl
from jax.experimental.pallas import tpu as pltpu

F32 = jnp.float32
BF16 = jnp.bfloat16
HIGHEST = lax.Precision.HIGHEST

D_MODEL = 1024
PAGE_SIZE = 128
NSA_HEADS = 16
HEAD_DIM = 64
KV_GROUPS = 4
GROUP_SIZE = NSA_HEADS // KV_GROUPS
CMP_BLOCK = 32
CMP_STRIDE = 16
CMP_HIDDEN = 256
SEL_BLOCK = 64
SEL_CHUNKS = SEL_BLOCK // CMP_STRIDE
SEL_TOPN = 16
WINDOW = 512
NSA_Q_DIM = NSA_HEADS * HEAD_DIM
NSA_KV_DIM = 2 * KV_GROUPS * HEAD_DIM
KV_HALF = KV_GROUPS * HEAD_DIM
SSM_D_INNER = 2048
SSM_HEAD_DIM = 64
SSM_HEADS = 32
SSM_GROUPS = 4
SSM_HPG = SSM_HEADS // SSM_GROUPS
SSM_STATE = 128
SSM_CONV = 4
SSM_BC = 2 * SSM_GROUPS * SSM_STATE
SSM_GW = SSM_D_INNER // SSM_GROUPS
SSM_CHUNK = 128
NORM_EPS = 1e-6
NEG_INF = -1e30
FORCE_SCORE = 1e4
LANES = 128
VMEM_LIMIT = 56 * 1024 * 1024

_SLOPES = [float(np.float32((2.0 ** (-8.0 / NSA_HEADS)) ** (i + 1))) for i in range(NSA_HEADS)]


def _cparams(sem):
    return pltpu.CompilerParams(dimension_semantics=sem, vmem_limit_bytes=VMEM_LIMIT)


def _nt(a, b, precision=None):
    return lax.dot_general(a, b, (((1,), (1,)), ((), ())), preferred_element_type=F32, precision=precision)


def _dot(a, b, precision=None):
    return jnp.dot(a, b, preferred_element_type=F32, precision=precision)


def _rms(x, gain):
    return x * lax.rsqrt(jnp.mean(x * x, axis=-1, keepdims=True) + NORM_EPS) * gain


def _norm_matmul_kernel(x_ref, g_ref, w_ref, hg_ref, hm_ref, e_ref, et_ref, o_ref, xn_ref, *, head_norm):
    @pl.when(pl.program_id(1) == 0)
    def _():
        xn_ref[...] = _rms(x_ref[...], g_ref[...]).astype(BF16)

    y = _dot(xn_ref[...], w_ref[...])
    if head_norm:
        ss = _dot(y * y, e_ref[...], HIGHEST)
        inv = lax.rsqrt(ss * (1.0 / HEAD_DIM) + NORM_EPS)
        yn = y * _dot(inv, et_ref[...], HIGHEST) * hg_ref[...]
        y = jnp.where(hm_ref[...] > 0.5, yn, y)
    o_ref[...] = y


def norm_matmul(x, gain, w, head_gain=None, head_mask=None, *, tm, tn):
    m, k = x.shape
    n = w.shape[1]
    head_norm = head_gain is not None
    if not head_norm:
        head_gain = jnp.zeros((1, n), F32)
        head_mask = jnp.zeros((1, n), F32)
    lane = np.arange(tn)
    e = np.zeros((tn, LANES), np.float32)
    e[lane, lane // HEAD_DIM] = 1.0
    return pl.pallas_call(
        functools.partial(_norm_matmul_kernel, head_norm=head_norm),
        grid=(m // tm, n // tn),
        in_specs=[
            pl.BlockSpec((tm, k), lambda i, j: (i, 0)),
            pl.BlockSpec((1, k), lambda i, j: (0, 0)),
            pl.BlockSpec((k, tn), lambda i, j: (0, j)),
            pl.BlockSpec((1, tn), lambda i, j: (0, j)),
            pl.BlockSpec((1, tn), lambda i, j: (0, j)),
            pl.BlockSpec((tn, LANES), lambda i, j: (0, 0)),
            pl.BlockSpec((LANES, tn), lambda i, j: (0, 0)),
        ],
        out_specs=pl.BlockSpec((tm, tn), lambda i, j: (i, j)),
        out_shape=jax.ShapeDtypeStruct((m, n), F32),
        scratch_shapes=[pltpu.VMEM((tm, k), BF16)],
        compiler_params=_cparams(("parallel", "arbitrary")),
        name="norm_matmul",
    )(x, gain.reshape(1, k), w, head_gain, head_mask, jnp.asarray(e), jnp.asarray(e.T))


def _ffn_kernel(x_ref, g_ref, wg_ref, wu_ref, wd_ref, o_ref, xn_ref):
    @pl.when(pl.program_id(1) == 0)
    def _():
        x = x_ref[...]
        xn_ref[...] = _rms(x, g_ref[...]).astype(BF16)
        o_ref[...] = x

    h = xn_ref[...]
    gate = _dot(h, wg_ref[...])
    up = _dot(h, wu_ref[...])
    act = (gate * jax.nn.sigmoid(gate) * up).astype(BF16)
    o_ref[...] += _dot(act, wd_ref[...])


def ffn(x, gain, wg, wu, wd, *, tm, th):
    m, k = x.shape
    hid = wg.shape[1]
    return pl.pallas_call(
        _ffn_kernel,
        grid=(m // tm, hid // th),
        in_specs=[
            pl.BlockSpec((tm, k), lambda i, j: (i, 0)),
            pl.BlockSpec((1, k), lambda i, j: (0, 0)),
            pl.BlockSpec((k, th), lambda i, j: (0, j)),
            pl.BlockSpec((k, th), lambda i, j: (0, j)),
            pl.BlockSpec((th, k), lambda i, j: (j, 0)),
        ],
        out_specs=pl.BlockSpec((tm, k), lambda i, j: (i, 0)),
        out_shape=jax.ShapeDtypeStruct((m, k), F32),
        scratch_shapes=[pltpu.VMEM((tm, k), BF16)],
        compiler_params=_cparams(("parallel", "arbitrary")),
        name="ffn",
    )(x, gain.reshape(1, k), wg, wu, wd)


def _matmul_res_kernel(a_ref, w_ref, r_ref, o_ref):
    o_ref[...] = r_ref[...] + _dot(a_ref[...], w_ref[...])


def matmul_residual(a, w, res, *, tm):
    m, k = a.shape
    n = w.shape[1]
    return pl.pallas_call(
        _matmul_res_kernel,
        grid=(m // tm,),
        in_specs=[
            pl.BlockSpec((tm, k), lambda i: (i, 0)),
            pl.BlockSpec((k, n), lambda i: (0, 0)),
            pl.BlockSpec((tm, n), lambda i: (i, 0)),
        ],
        out_specs=pl.BlockSpec((tm, n), lambda i: (i, 0)),
        out_shape=jax.ShapeDtypeStruct((m, n), F32),
        compiler_params=_cparams(("parallel",)),
        name="matmul_residual",
    )(a, w, res)


CMP_PACK = 4


def _compress_accumulate(xs_ref, n_rows, pe_ref, w1_ref, kv):
    n_pair = NSA_KV_DIM // LANES
    low = lax.broadcasted_iota(jnp.int32, (n_rows, LANES), 1) < HEAD_DIM
    acc = [None, None]
    for jp in range(CMP_STRIDE // CMP_PACK):
        groups = []
        for c in range(kv * n_pair // 2, (kv + 1) * n_pair // 2):
            rows = [xs_ref.at[c][pl.ds(jp * CMP_PACK + i, n_rows, stride=CMP_STRIDE), :] for i in range(CMP_PACK)]
            swapped = [pltpu.roll(r, HEAD_DIM, axis=1) for r in rows]
            groups.append(jnp.concatenate([jnp.where(low, rows[i], swapped[i + 1])
                                           for i in range(0, CMP_PACK, 2)], axis=1))
            groups.append(jnp.concatenate([jnp.where(low, swapped[i], rows[i + 1])
                                           for i in range(0, CMP_PACK, 2)], axis=1))
        xs = jnp.concatenate(groups, axis=0)
        for m in range(2):
            lhs = (xs + pe_ref[kv, m, jp:jp + 1, :]).astype(BF16)
            part = _dot(lhs, w1_ref[kv, m, jp])
            acc[m] = part if acc[m] is None else acc[m] + part
    return acc


def _compress_finish(acc, w2_ref, kg_ref, kv):
    rows = acc[0].shape[0]
    pre = acc[0] + pltpu.roll(acc[1], rows - 1, axis=0)
    hid = (pre * jax.nn.sigmoid(pre)).astype(BF16)
    out = _dot(hid, w2_ref[kv])
    if kv == 0:
        out = _rms(out, kg_ref[...])
    return out


def _compress_rows_kernel(x_ref, pe_ref, w1_ref, w2_ref, kg_ref, o_ref, xt_ref, xs_ref, *, n_chunk):
    xt_ref[...] = jnp.transpose(x_ref[...])
    n_pair = NSA_KV_DIM // LANES
    for c in range(n_pair):
        xs_ref[c] = x_ref[:, c * LANES:(c + 1) * LANES]
    for kv in range(2):
        o_ref[kv] = _compress_finish(_compress_accumulate(xs_ref, n_chunk, pe_ref, w1_ref, kv), w2_ref, kg_ref, kv)


def compress_rows(proj, col_block, pe, w1, w2, kgain, *, batch, t):
    n_chunk = t // CMP_STRIDE
    return pl.pallas_call(
        functools.partial(_compress_rows_kernel, n_chunk=n_chunk),
        grid=(batch,),
        in_specs=[
            pl.BlockSpec((t, NSA_KV_DIM), lambda b: (b, col_block)),
            pl.BlockSpec(pe.shape, lambda b: (0, 0, 0, 0)),
            pl.BlockSpec(w1.shape, lambda b: (0, 0, 0, 0, 0)),
            pl.BlockSpec(w2.shape, lambda b: (0, 0, 0)),
            pl.BlockSpec((1, HEAD_DIM), lambda b: (0, 0)),
        ],
        out_specs=[pl.BlockSpec((None, 2, KV_GROUPS * n_chunk, HEAD_DIM), lambda b: (b, 0, 0, 0)),
                   pl.BlockSpec((None, NSA_KV_DIM, t), lambda b: (b, 0, 0))],
        out_shape=[jax.ShapeDtypeStruct((batch, 2, KV_GROUPS * n_chunk, HEAD_DIM), F32),
                   jax.ShapeDtypeStruct((batch, NSA_KV_DIM, t), F32)],
        scratch_shapes=[pltpu.VMEM((NSA_KV_DIM // LANES, t, LANES), F32)],
        compiler_params=_cparams(("parallel",)),
        name="compress_rows",
    )(proj, pe, w1, w2, kgain.reshape(1, HEAD_DIM))


def _slope_col(g, rows, tq):
    rblk = lax.broadcasted_iota(jnp.int32, (rows, 1), 0) // tq
    col = jnp.full((rows, 1), _SLOPES[GROUP_SIZE * g + GROUP_SIZE - 1], F32)
    for r in range(GROUP_SIZE - 1):
        col = jnp.where(rblk == r, _SLOPES[GROUP_SIZE * g + r], col)
    return col


def _normalize(acc, l):
    return acc * jnp.where(l > 0.0, 1.0 / jnp.where(l > 0.0, l, 1.0), 0.0)


MASK_BIG = 2.0 ** 100
LOG2E = 1.4426950408889634
AUG = 64
AUG_POS = 32


def _bf16_terms(x, n=3):
    x = np.asarray(x, np.float64)
    terms = []
    for _ in range(n):
        term = x.astype(ml_dtypes.bfloat16).astype(np.float64)
        terms.append(term.astype(np.float32))
        x = x - term
    return terms


def _key_aug(pos, flags=None):
    pos = np.asarray(pos, np.int64)
    aug = np.zeros((pos.shape[0], AUG), np.float32)
    if flags is not None:
        aug[:, :flags.shape[1]] = flags
    coarse = (pos // SEL_BLOCK) * SEL_BLOCK
    aug[:, AUG_POS:AUG_POS + 3] = coarse[:, None]
    aug[:, AUG_POS + 3:AUG_POS + 6] = (pos - coarse)[:, None]
    return aug


def _query_aug_rows(tq):
    rows = np.zeros((KV_GROUPS, AUG - AUG_POS, GROUP_SIZE * tq), np.float32)
    for h in range(NSA_HEADS):
        terms = _bf16_terms(np.float64(_SLOPES[h]) * LOG2E)
        g, r = divmod(h, GROUP_SIZE)
        for i, term in enumerate(terms):
            rows[g, i, r * tq:(r + 1) * tq] = term
            rows[g, 3 + i, r * tq:(r + 1) * tq] = term
    return rows


def _online_update(s, v_t, m, l, acc):
    m_new = jnp.maximum(m, jnp.max(s, axis=0, keepdims=True))
    alpha = jnp.exp2(m - m_new)
    p = jnp.exp2(s - m_new)
    return m_new, alpha * l + jnp.sum(p, axis=0, keepdims=True), alpha * acc + _dot(v_t, p.astype(BF16))


def _nsa_prompt_kernel(q_ref, ks_ref, kw_ref, cmp_ref, gate_ref, gq_ref, gks_ref, gkw_ref, gksc_ref, gkwc_ref,
                       csel_ref, cwin_ref, ccmp_ref, qrows_ref, mmap_ref, eye_ref,
                       o_ref, kst_out_ref, kwt_out_ref,
                       ksa_ref, kwa_ref, vst_ref, vwt_ref, kca_ref, vct_ref, ot_ref, ss_ref, ps_ref, sacc_ref,
                       *, t, tq):
    qi = pl.program_id(1)
    n_sel = t // SEL_BLOCK
    n_cmp = t // CMP_STRIDE
    top_n = min(SEL_TOPN, n_sel)
    cols = GROUP_SIZE * tq
    nt = t // tq
    n_win = WINDOW // tq + 1
    n_pad = WINDOW // tq

    @pl.when(qi == 0)
    def _():
        def prepare(src_ref, grow_ref, gcol_ref, const_ref, ka_ref, vt_ref, out_ref, pad_tiles):
            pad = pad_tiles * tq
            x = src_ref[...]
            xt = jnp.transpose(x)
            width = out_ref.shape[1]
            out_ref[KV_HALF:, :] = xt[KV_HALF:, t - width:]
            for g in range(KV_GROUPS):
                lo, hi = g * HEAD_DIM, (g + 1) * HEAD_DIM
                kn = _rms(x[:, lo:hi], grow_ref[...])
                ka_ref[g, pad:pad + t, :] = jnp.concatenate([kn, const_ref[pad:pad + t, :]], axis=1).astype(BF16)
                if pad:
                    ka_ref[g, 0:pad, :] = jnp.concatenate(
                        [jnp.zeros((pad, HEAD_DIM), F32), const_ref[0:pad, :]], axis=1).astype(BF16)
                kt = xt[lo:hi, :]
                ktn = kt * lax.rsqrt(jnp.mean(kt * kt, axis=0, keepdims=True) + NORM_EPS) * gcol_ref[...]
                out_ref[lo:hi, :] = ktn[:, t - width:]
                vt = xt[KV_HALF + lo:KV_HALF + hi, :].astype(BF16)
                base = g * (nt + pad_tiles)
                for i in range(pad_tiles):
                    vt_ref[base + i] = jnp.zeros((HEAD_DIM, tq), BF16)
                for i in range(nt):
                    vt_ref[base + pad_tiles + i] = vt[:, i * tq:(i + 1) * tq]

        prepare(ks_ref, gks_ref, gksc_ref, csel_ref, ksa_ref, vst_ref, kst_out_ref, 0)
        prepare(kw_ref, gkw_ref, gkwc_ref, cwin_ref, kwa_ref, vwt_ref, kwt_out_ref, n_pad)
        for g in range(KV_GROUPS):
            kc = cmp_ref[0, g * n_cmp:(g + 1) * n_cmp, :]
            kca_ref[g] = jnp.concatenate([kc, ccmp_ref[...]], axis=1).astype(BF16)
            vct_ref[g] = _nt(eye_ref[...], cmp_ref[1, g * n_cmp:(g + 1) * n_cmp, :].astype(BF16)).astype(BF16)

    q0 = pl.multiple_of(qi * tq, tq)
    lane4 = lax.broadcasted_iota(jnp.int32, (1, cols), 1) % tq
    key_off = lax.broadcasted_iota(jnp.int32, (tq, cols), 0)
    causal = key_off <= lane4
    win_old = key_off > lane4
    cmp_ok = (lax.broadcasted_iota(jnp.int32, (n_cmp, cols), 0) * CMP_STRIDE + (CMP_BLOCK - 1)) <= q0 + lane4
    qt = jnp.transpose(q_ref[...])
    gt = jax.nn.sigmoid(jnp.transpose(gate_ref[...]))
    jblk = lax.broadcasted_iota(jnp.int32, (n_sel, tq), 0)
    t_lane = q0 + lax.broadcasted_iota(jnp.int32, (n_sel, tq), 1)
    cur = t_lane // SEL_BLOCK
    forced = (jblk == 0) | (jblk == cur) | (jblk == cur - 1)
    visible = jblk * SEL_BLOCK <= t_lane

    o_c, q_sel, q_win = [], [], []
    for g in range(KV_GROUPS):
        heads = []
        for r in range(GROUP_SIZE):
            h = GROUP_SIZE * g + r
            qh = qt[h * HEAD_DIM:(h + 1) * HEAD_DIM, :]
            qh = qh * lax.rsqrt(jnp.mean(qh * qh, axis=0, keepdims=True) + NORM_EPS)
            heads.append(qh * (gq_ref[...] * (HEAD_DIM ** -0.5 * LOG2E)))
        q_top = jnp.concatenate(heads, axis=1)
        q_low = qrows_ref[g]

        def with_mask_rows(rows_, q_top=q_top, q_low=q_low):
            return jnp.concatenate([q_top, rows_, q_low], axis=0).astype(BF16)

        s = _dot(kca_ref[g], with_mask_rows(jnp.zeros((AUG_POS, cols), F32)))
        s = jnp.where(cmp_ok, s, -MASK_BIG)
        p = jnp.where(cmp_ok, jnp.exp2(s - jnp.max(s, axis=0, keepdims=True)), 0.0)
        pc = _normalize(p, jnp.sum(p, axis=0, keepdims=True))
        o_c.append(_dot(vct_ref[g], pc.astype(BF16)))
        imp = pc[:, 0:tq]
        for r in range(1, GROUP_SIZE):
            imp = imp + pc[:, r * tq:(r + 1) * tq]

        p_slc = _dot(mmap_ref[...], imp, HIGHEST)
        score = jnp.where(forced, FORCE_SCORE, jnp.where(visible, p_slc, -1.0))
        cnt = jnp.zeros((n_sel, tq), jnp.int32)
        for k in range(n_sel):
            sk = score[k:k + 1, :]
            beats = (sk > score) | ((sk == score) & (jblk > k))
            cnt = cnt + beats.astype(jnp.int32)
        drop = jnp.where(cnt < top_n, 0.0, -MASK_BIG)
        if n_sel < AUG_POS:
            drop = jnp.concatenate([drop, jnp.zeros((AUG_POS - n_sel, tq), F32)], axis=0)
        q_sel.append(with_mask_rows(jnp.concatenate([drop] * GROUP_SIZE, axis=1)))
        q_win.append(with_mask_rows(jnp.full((AUG_POS, cols), -MASK_BIG, F32)))

    def sel_scores(tile, slot):
        k0 = pl.multiple_of(tile * tq, tq)
        for g in range(KV_GROUPS):
            ss_ref[slot, g] = _dot(ksa_ref[g, pl.ds(k0, tq), :], q_sel[g])

    def sel_values(tile):
        return [_dot(vst_ref[g * nt + tile], ps_ref[g]) for g in range(KV_GROUPS)]

    def sel_softmax(slot, stats, pv, mask=None):
        new_stats = []
        for g in range(KV_GROUPS):
            m, l = stats[g]
            s = ss_ref[slot, g]
            if mask is not None:
                s = jnp.where(mask, s, -MASK_BIG)
            m_new = jnp.maximum(m, jnp.max(s, axis=0, keepdims=True))
            alpha = jnp.exp2(m - m_new)
            p = jnp.exp2(s - m_new)
            sacc_ref[g] = alpha * (sacc_ref[g] + pv[g])
            ps_ref[g] = p.astype(BF16)
            new_stats.append((m_new, alpha * l + jnp.sum(p, axis=0, keepdims=True)))
        return tuple(new_stats)

    def sel_step(slot, next_tile, prev_tile, stats, mask=None):
        pv = sel_values(jnp.maximum(prev_tile, 0))
        if next_tile is not None:
            sel_scores(next_tile, 1 - slot)
        return sel_softmax(slot, stats, pv, mask)

    def sel_finish(stats):
        pv = sel_values(qi)
        for g in range(KV_GROUPS):
            sacc_ref[g] = _normalize(sacc_ref[g] + pv[g], stats[g][1])

    for g in range(KV_GROUPS):
        ps_ref[g] = jnp.zeros((tq, cols), BF16)
        sacc_ref[g] = jnp.zeros((HEAD_DIM, cols), F32)
    sel_scores(0, 0)

    def sel_pair(j, stats):
        stats = sel_step(0, 2 * j + 1, 2 * j - 1, stats)
        return sel_step(1, 2 * j + 2, 2 * j, stats)

    stats0 = ((jnp.full((1, cols), -MASK_BIG, F32), jnp.zeros((1, cols), F32)),) * KV_GROUPS
    stats = lax.fori_loop(0, qi // 2, sel_pair, stats0)

    @pl.when(qi % 2 == 1)
    def _():
        sel_finish(sel_step(1, None, qi - 1, sel_step(0, qi, qi - 2, stats), mask=causal))

    @pl.when(qi % 2 == 0)
    def _():
        sel_finish(sel_step(0, None, qi - 1, stats, mask=causal))

    o_s = [sacc_ref[g] for g in range(KV_GROUPS)]

    win_scores = [[_dot(kwa_ref[g, pl.ds(q0 + i * tq, tq), :], q_win[g]) for i in range(n_win)]
                  for g in range(KV_GROUPS)]
    o_w = []
    for g in range(KV_GROUPS):
        tiles = win_scores[g]
        tiles[0] = jnp.where(win_old, tiles[0], -MASK_BIG)
        tiles[-1] = jnp.where(causal, tiles[-1], -MASK_BIG)
        m = functools.reduce(jnp.maximum, [jnp.max(s, axis=0, keepdims=True) for s in tiles])
        probs = [jnp.exp2(s - m) for s in tiles]
        l = functools.reduce(jnp.add, [jnp.sum(p, axis=0, keepdims=True) for p in probs])
        acc = functools.reduce(jnp.add, [_dot(vwt_ref[g * (nt + n_pad) + qi + i], p.astype(BF16))
                                         for i, p in enumerate(probs)])
        o_w.append(_normalize(acc, l))

    for h in range(NSA_HEADS):
        g, r = divmod(h, GROUP_SIZE)
        sl = slice(r * tq, (r + 1) * tq)
        ot_ref[h * HEAD_DIM:(h + 1) * HEAD_DIM, :] = (
            gt[h:h + 1, :] * o_c[g][:, sl] + gt[NSA_HEADS + h:NSA_HEADS + h + 1, :] * o_s[g][:, sl]
            + gt[2 * NSA_HEADS + h:2 * NSA_HEADS + h + 1, :] * o_w[g][:, sl])
    o_ref[...] = jnp.transpose(ot_ref[...]).astype(o_ref.dtype)


def nsa_prompt_attention(proj, gates, cmp, qk_gain, *, batch, t, tq):
    n_sel = t // SEL_BLOCK
    n_cmp = t // CMP_STRIDE
    nq = t // tq
    nt = t // tq
    n_pad = WINDOW // tq
    keep = min(WINDOW, t)
    assert n_sel <= AUG_POS and t % tq == 0 and WINDOW % tq == 0 and tq % SEL_BLOCK == 0
    n = np.arange(n_cmp)
    mmap = np.zeros((n_sel, n_cmp), np.float32)
    real = n < n_cmp - 1
    np.add.at(mmap, (n[real] // SEL_CHUNKS, n[real]), 1.0)
    nxt = real & ((n + 1) // SEL_CHUNKS < n_sel)
    np.add.at(mmap, ((n[nxt] + 1) // SEL_CHUNKS, n[nxt]), 1.0)
    pos = np.arange(t)
    csel = _key_aug(pos, (pos[:, None] // SEL_BLOCK == np.arange(n_sel)[None, :]).astype(np.float32))
    wpos = np.arange(-WINDOW, t)
    cwin = _key_aug(wpos, (wpos < 0).astype(np.float32)[:, None])
    ccmp = _key_aug(np.arange(n_cmp) * CMP_STRIDE + (CMP_BLOCK - 1))
    const = lambda shape: pl.BlockSpec(shape, lambda b, i: (0,) * len(shape))
    col = lambda v: v.reshape(HEAD_DIM, 1)
    row = lambda v: v.reshape(1, HEAD_DIM)
    return pl.pallas_call(
        functools.partial(_nsa_prompt_kernel, t=t, tq=tq),
        grid=(batch, nq),
        in_specs=[
            pl.BlockSpec((tq, NSA_Q_DIM), lambda b, i: (b * nq + i, 0)),
            pl.BlockSpec((t, NSA_KV_DIM), lambda b, i: (b, 3)),
            pl.BlockSpec((t, NSA_KV_DIM), lambda b, i: (b, 4)),
            pl.BlockSpec((None, 2, KV_GROUPS * n_cmp, HEAD_DIM), lambda b, i: (b, 0, 0, 0)),
            pl.BlockSpec((tq, LANES), lambda b, i: (b * nq + i, 0)),
            const((HEAD_DIM, 1)), const((1, HEAD_DIM)), const((1, HEAD_DIM)), const((HEAD_DIM, 1)),
            const((HEAD_DIM, 1)),
            const((t, AUG)), const((t + WINDOW, AUG)), const((n_cmp, AUG)),
            const((KV_GROUPS, AUG - AUG_POS, GROUP_SIZE * tq)), const((n_sel, n_cmp)), const((HEAD_DIM, HEAD_DIM)),
        ],
        out_specs=[
            pl.BlockSpec((tq, NSA_Q_DIM), lambda b, i: (b * nq + i, 0)),
            pl.BlockSpec((None, NSA_KV_DIM, t), lambda b, i: (b, 0, 0)),
            pl.BlockSpec((None, NSA_KV_DIM, keep), lambda b, i: (b, 0, 0)),
        ],
        out_shape=[jax.ShapeDtypeStruct((batch * t, NSA_Q_DIM), BF16),
                   jax.ShapeDtypeStruct((batch, NSA_KV_DIM, t), F32),
                   jax.ShapeDtypeStruct((batch, NSA_KV_DIM, keep), F32)],
        scratch_shapes=[
            pltpu.VMEM((KV_GROUPS, t, HEAD_DIM + AUG), BF16),
            pltpu.VMEM((KV_GROUPS, t + WINDOW, HEAD_DIM + AUG), BF16),
            pltpu.VMEM((KV_GROUPS * nt, HEAD_DIM, tq), BF16),
            pltpu.VMEM((KV_GROUPS * (nt + n_pad), HEAD_DIM, tq), BF16),
            pltpu.VMEM((KV_GROUPS, n_cmp, HEAD_DIM + AUG), BF16),
            pltpu.VMEM((KV_GROUPS, HEAD_DIM, n_cmp), BF16),
            pltpu.VMEM((NSA_Q_DIM, tq), F32),
            pltpu.VMEM((2, KV_GROUPS, tq, GROUP_SIZE * tq), F32),
            pltpu.VMEM((KV_GROUPS, tq, GROUP_SIZE * tq), BF16),
            pltpu.VMEM((KV_GROUPS, HEAD_DIM, GROUP_SIZE * tq), F32),
        ],
        compiler_params=_cparams(("parallel", "arbitrary")),
        name="nsa_prompt_attention",
    )(proj, proj, proj, cmp, gates, col(qk_gain[0]), row(qk_gain[2]), row(qk_gain[3]), col(qk_gain[2]),
      col(qk_gain[3]), jnp.asarray(csel), jnp.asarray(cwin), jnp.asarray(ccmp), jnp.asarray(_query_aug_rows(tq)),
      jnp.asarray(mmap), jnp.eye(HEAD_DIM, dtype=BF16))


def _nsa_weights(w_in, qk_gain, cmp_pe, cmp_w1, cmp_w2, w_out):
    n_main = NSA_Q_DIM + 3 * NSA_KV_DIM
    zeros_kv = jnp.zeros((KV_HALF,), F32)
    ones_kv = jnp.ones((KV_HALF,), F32)
    head_gain = jnp.concatenate([
        jnp.tile(qk_gain[0], NSA_HEADS), zeros_kv, zeros_kv,
        jnp.tile(qk_gain[2], KV_GROUPS), zeros_kv, jnp.tile(qk_gain[3], KV_GROUPS), zeros_kv]).reshape(1, n_main)
    head_mask = jnp.concatenate([
        jnp.ones((NSA_Q_DIM,), F32), zeros_kv, zeros_kv, ones_kv, zeros_kv, ones_kv, zeros_kv]).reshape(1, n_main)
    w_gate = jnp.pad(w_in[:, n_main:], ((0, 0), (0, LANES - 3 * NSA_HEADS)))
    return dict(
        w_main=w_in[:, :n_main].astype(BF16), w_gate=w_gate.astype(BF16), head_gain=head_gain, head_mask=head_mask,
        pe=cmp_pe.reshape(2, 2, CMP_STRIDE // CMP_PACK, CMP_PACK * HEAD_DIM),
        w1=cmp_w1.reshape(2, 2, CMP_STRIDE // CMP_PACK, CMP_PACK * HEAD_DIM, CMP_HIDDEN).astype(BF16),
        w2=cmp_w2.astype(BF16), kgain=qk_gain[1], qk_gain=qk_gain, w_out=w_out.astype(BF16))


def _nsa_project(x, gain, w, *, tm, head_norm):
    if head_norm:
        proj = norm_matmul(x, gain, w["w_main"], w["head_gain"], w["head_mask"], tm=tm, tn=512)
    else:
        proj = norm_matmul(x, gain, w["w_main"], tm=tm, tn=1280)
    return proj, norm_matmul(x, gain, w["w_gate"], tm=tm, tn=LANES)


def nsa_prompt_layer(x, gain, w, *, batch, t):
    tm = min(1024, batch * t)
    proj, gates = _nsa_project(x, gain, w, tm=tm, head_norm=False)
    cmp, cmp_t = compress_rows(proj, 2, w["pe"], w["w1"], w["w2"], w["kgain"], batch=batch, t=t)
    o, sel_t, win_t = nsa_prompt_attention(proj, gates, cmp, w["qk_gain"], batch=batch, t=t, tq=128)
    return matmul_residual(o, w["w_out"], x, tm=tm), cmp_t, sel_t, win_t


PAGES_PER_STEP = 8
CHUNKS_PER_PAGE = PAGE_SIZE // CMP_STRIDE


def _compress_pages_kernel(pt_ref, *refs, n_chunk):
    del pt_ref
    pages = refs[:PAGES_PER_STEP]
    pe_ref, w1_ref, w2_ref, kg_ref, o_ref, xs_ref, pre_ref = refs[PAGES_PER_STEP:]
    s = pl.program_id(1)
    n_pair = NSA_KV_DIM // LANES
    step_chunks = PAGES_PER_STEP * CHUNKS_PER_PAGE
    for i, page in enumerate(pages):
        for c in range(n_pair):
            xs_ref[c, i * PAGE_SIZE:(i + 1) * PAGE_SIZE, :] = jnp.transpose(page[c * LANES:(c + 1) * LANES, :])
    for kv in range(2):
        acc = _compress_accumulate(xs_ref, step_chunks, pe_ref, w1_ref, kv)
        for m in range(2):
            for g in range(KV_GROUPS):
                pre_ref[kv, m, pl.ds(pl.multiple_of(g * n_chunk + s * step_chunks, step_chunks), step_chunks), :] = (
                    acc[m][g * step_chunks:(g + 1) * step_chunks])

    @pl.when(s == pl.num_programs(1) - 1)
    def _():
        for kv in range(2):
            o_ref[kv] = _compress_finish([pre_ref[kv, 0], pre_ref[kv, 1]], w2_ref, kg_ref, kv)


def compress_pages(cache_t, layer, page_table_flat, pe, w1, w2, kgain, *, batch, n_pages):
    n_chunk = n_pages * CHUNKS_PER_PAGE
    n_steps = n_pages // PAGES_PER_STEP

    def page_spec(i):
        return pl.BlockSpec((None, None, NSA_KV_DIM, PAGE_SIZE),
                            lambda b, s, pt: (layer, pt[b * n_pages + s * PAGES_PER_STEP + i], 0, 0))

    grid_spec = pltpu.PrefetchScalarGridSpec(
        num_scalar_prefetch=1,
        grid=(batch, n_steps),
        in_specs=[page_spec(i) for i in range(PAGES_PER_STEP)] + [
            pl.BlockSpec(pe.shape, lambda b, s, pt: (0, 0, 0, 0)),
            pl.BlockSpec(w1.shape, lambda b, s, pt: (0, 0, 0, 0, 0)),
            pl.BlockSpec(w2.shape, lambda b, s, pt: (0, 0, 0)),
            pl.BlockSpec((1, HEAD_DIM), lambda b, s, pt: (0, 0)),
        ],
        out_specs=pl.BlockSpec((None, 2, KV_GROUPS * n_chunk, HEAD_DIM), lambda b, s, pt: (b, 0, 0, 0)),
        scratch_shapes=[pltpu.VMEM((NSA_KV_DIM // LANES, PAGES_PER_STEP * PAGE_SIZE, LANES), F32),
                        pltpu.VMEM((2, 2, KV_GROUPS * n_chunk, CMP_HIDDEN), F32)],
    )
    return pl.pallas_call(
        functools.partial(_compress_pages_kernel, n_chunk=n_chunk),
        grid_spec=grid_spec,
        out_shape=jax.ShapeDtypeStruct((batch, 2, KV_GROUPS * n_chunk, HEAD_DIM), F32),
        compiler_params=_cparams(("parallel", "arbitrary")),
        name="compress_pages",
    )(page_table_flat, *([cache_t] * PAGES_PER_STEP), pe, w1, w2, kgain.reshape(1, HEAD_DIM))


Q_PAD = 8
HEAD_PAD = 16


def _nsa_sample_select_kernel(q_ref, cmp_ref, mmap_ref, oc_ref, idx_ref, score_ref, *, past_len, n_cmp, n_sel):
    rows = GROUP_SIZE * Q_PAD
    n_selp = score_ref.shape[0]
    t_max = past_len + Q_PAD - 1
    row = lax.broadcasted_iota(jnp.int32, (rows, 1), 0)
    t_col = past_len + row % Q_PAD
    jblk = lax.broadcasted_iota(jnp.int32, (n_selp, Q_PAD), 0)
    t_lane = past_len + lax.broadcasted_iota(jnp.int32, (n_selp, Q_PAD), 1)
    cur = t_lane // SEL_BLOCK
    forced = (jblk == 0) | (jblk == cur) | (jblk == cur - 1)
    visible = jblk * SEL_BLOCK <= t_lane
    for g in range(KV_GROUPS):
        qg = (q_ref[g] * (HEAD_DIM ** -0.5)).astype(BF16)
        slope = _slope_col(g, rows, Q_PAD)
        kc = cmp_ref[0, g * n_cmp:(g + 1) * n_cmp, :].astype(BF16)
        vc = cmp_ref[1, g * n_cmp:(g + 1) * n_cmp, :].astype(BF16)
        kpos = lax.broadcasted_iota(jnp.int32, (1, n_cmp), 1) * CMP_STRIDE + (CMP_BLOCK - 1)
        s = _nt(qg, kc) + slope * (kpos - t_max).astype(F32)
        mask = kpos <= t_col
        s = jnp.where(mask, s, NEG_INF)
        p = jnp.where(mask, jnp.exp(s - jnp.max(s, axis=1, keepdims=True)), 0.0)
        pc = _normalize(p, jnp.sum(p, axis=1, keepdims=True))
        oc_ref[g] = _dot(pc.astype(BF16), vc)
        imp = pc[0:Q_PAD]
        for r in range(1, GROUP_SIZE):
            imp = imp + pc[r * Q_PAD:(r + 1) * Q_PAD]
        p_slc = _nt(mmap_ref[...], imp, HIGHEST)
        score = jnp.where(forced, FORCE_SCORE, jnp.where(visible, p_slc, -1.0))
        score_ref[:, g * Q_PAD:(g + 1) * Q_PAD] = jnp.where(jblk < n_sel, score, -2.0)

    score = score_ref[...]
    jall = lax.broadcasted_iota(jnp.int32, score.shape, 0)

    def rank_body(k, cnt):
        sk = score_ref[pl.ds(k, 1), :]
        beats = (sk > score) | ((sk == score) & (jall > k))
        return cnt + beats.astype(jnp.int32)

    cnt = lax.fori_loop(0, n_sel, rank_body, jnp.zeros(score.shape, jnp.int32), unroll=4)
    slots = [jnp.sum(jnp.where(cnt == sl, jall, 0), axis=0, keepdims=True) for sl in range(SEL_TOPN)]
    idx_ref[...] = jnp.concatenate(slots, axis=0)


def nsa_sample_select(q_stack, cmp, *, batch, past_len, t_new):
    n_cmp = past_len // CMP_STRIDE
    n_sel = -(-(past_len + t_new) // SEL_BLOCK)
    n_selp = -(-n_sel // 8) * 8
    n = np.arange(n_cmp)
    mmap = np.zeros((n_selp, n_cmp), np.float32)
    real = n < n_cmp - 1
    np.add.at(mmap, (n[real] // SEL_CHUNKS, n[real]), 1.0)
    np.add.at(mmap, ((n[real] + 1) // SEL_CHUNKS, n[real]), 1.0)
    rows = GROUP_SIZE * Q_PAD
    return pl.pallas_call(
        functools.partial(_nsa_sample_select_kernel, past_len=past_len, n_cmp=n_cmp, n_sel=n_sel),
        grid=(batch,),
        in_specs=[
            pl.BlockSpec((None, KV_GROUPS, rows, HEAD_DIM), lambda b: (b, 0, 0, 0)),
            pl.BlockSpec((None, 2, KV_GROUPS * n_cmp, HEAD_DIM), lambda b: (b, 0, 0, 0)),
            pl.BlockSpec((n_selp, n_cmp), lambda b: (0, 0)),
        ],
        out_specs=[
            pl.BlockSpec((None, KV_GROUPS, rows, HEAD_DIM), lambda b: (b, 0, 0, 0)),
            pl.BlockSpec((None, SEL_TOPN, KV_GROUPS * Q_PAD), lambda b: (b, 0, 0)),
        ],
        out_shape=[jax.ShapeDtypeStruct((batch, KV_GROUPS, rows, HEAD_DIM), F32),
                   jax.ShapeDtypeStruct((batch, SEL_TOPN, KV_GROUPS * Q_PAD), jnp.int32)],
        scratch_shapes=[pltpu.VMEM((n_selp, KV_GROUPS * Q_PAD), F32)],
        compiler_params=_cparams(("parallel",)),
        name="nsa_sample_select",
    )(q_stack, cmp, jnp.asarray(mmap))


def _nsa_sample_attend_kernel(pt_ref, idx_ref, cache_ref, q_ref, slope_ref, snew_ref, win_ref, wnew_ref, wnew_t_ref,
                              oc_ref, gate_ref, o_ref, wout_ref, kcat_ref, vcat_ref, page_ref, sem_ref,
                              *, layer, past_len, t_new, n_past_blocks, n_steps):
    b, g, qi = pl.program_id(0), pl.program_id(1), pl.program_id(2)
    t = past_len + qi
    n_pages = past_len // PAGE_SIZE
    step = (b * KV_GROUPS + g) * t_new + qi

    def page_copy(src_page, src_group, slot, sl):
        return pltpu.make_async_copy(cache_ref.at[layer, src_page, :, src_group], page_ref.at[slot, sl],
                                     sem_ref.at[slot, sl])

    def start_gather(at_step, slot):
        at_b = at_step // (KV_GROUPS * t_new)
        at_g = (at_step // t_new) % KV_GROUPS
        for sl in range(SEL_TOPN):
            blk = jnp.minimum(idx_ref[at_step * SEL_TOPN + sl], n_past_blocks - 1)
            page_copy(pt_ref[at_b * n_pages + blk // 2], at_g, slot, sl).start()

    @pl.when(step == 0)
    def _():
        start_gather(step, 0)

    @pl.when(step + 1 < n_steps)
    def _():
        start_gather(step + 1, (step + 1) % 2)

    slot = step % 2
    for sl in range(SEL_TOPN):
        page_copy(0, 0, slot, sl).wait()
    q = (q_ref[...] * (HEAD_DIM ** -0.5)).astype(BF16)
    slope = slope_ref[:, 0:1]
    lane = lax.broadcasted_iota(jnp.int32, (1, LANES), 1)
    new_pos = past_len + lane
    new_ok = (lane < t_new) & (new_pos <= t)

    def attend(tiles, v_new):
        m = jnp.full((HEAD_PAD, 1), NEG_INF, F32)
        for s, mask, _ in tiles:
            m = jnp.maximum(m, jnp.max(jnp.where(mask, s, NEG_INF), axis=1, keepdims=True))
        l = jnp.zeros((HEAD_PAD, 1), F32)
        acc = jnp.zeros((HEAD_PAD, HEAD_DIM), F32)
        for i, (s, mask, v) in enumerate(tiles):
            p = jnp.where(mask, jnp.exp(jnp.where(mask, s, NEG_INF) - m), 0.0)
            l = l + jnp.sum(p, axis=1, keepdims=True)
            if i == len(tiles) - 1:
                acc = acc + _dot(p.astype(BF16), v_new)
            else:
                acc = acc + _nt(p.astype(BF16), v)
        return _normalize(acc, l)

    base = step * SEL_TOPN
    kpos, mask = [], []
    for sl in range(SEL_TOPN):
        blk = idx_ref[base + sl]
        kcat_ref[:, sl * PAGE_SIZE:(sl + 1) * PAGE_SIZE] = page_ref[slot, sl, 0].astype(BF16)
        vcat_ref[:, sl * PAGE_SIZE:(sl + 1) * PAGE_SIZE] = page_ref[slot, sl, 1].astype(BF16)
        kpos.append((blk // 2) * PAGE_SIZE + lane)
        mask.append((lane // SEL_BLOCK == blk % 2) & (blk < n_past_blocks) & (kpos[-1] <= t))
    kpos = jnp.concatenate(kpos, axis=1)
    mask = jnp.concatenate([m_.astype(jnp.int32) for m_ in mask], axis=1) > 0
    s_sel = _dot(q, kcat_ref[...]) + slope * (kpos - t).astype(F32)
    s_new = _nt(q, snew_ref[0].astype(BF16)) + slope * (new_pos - t).astype(F32)
    o_s = attend([(s_sel, mask, vcat_ref[...]), (s_new, new_ok, None)], snew_ref[1].astype(BF16))

    buf = win_ref.shape[2]
    wlane = lax.broadcasted_iota(jnp.int32, (1, buf), 1)
    wpos = past_len - buf + wlane
    dist = t - wpos
    s_w = _dot(q, win_ref[0].astype(BF16)) + slope * (wpos - t).astype(F32)
    s_wn = _nt(q, wnew_ref[0].astype(BF16)) + slope * (new_pos - t).astype(F32)
    o_w = attend([(s_w, (dist >= 0) & (dist < WINDOW), win_ref[1].astype(BF16)),
                  (s_wn, new_ok & (t - new_pos < WINDOW), None)], wnew_ref[1].astype(BF16))

    gates = jax.nn.sigmoid(gate_ref[...])
    o_ref[...] = gates[:, 0:1] * oc_ref[...] + gates[:, 1:2] * o_s + gates[:, 2:3] * o_w

    @pl.when(qi == 0)
    def _():
        for kv in range(2):
            rolled = pltpu.roll(win_ref[kv], buf - t_new, axis=1)
            wout_ref[kv, :, 0:buf - LANES] = rolled[:, 0:buf - LANES]
            wout_ref[kv, :, buf - LANES:] = jnp.where(lane >= LANES - t_new, wnew_t_ref[kv], rolled[:, buf - LANES:])


def nsa_sample_attend(cache_t, win_t, layer, page_table_flat, idx_flat, q_heads, slopes, sel_new, win_new,
                      win_new_t, o_c, gates, *, batch, past_len, t_new):
    n_past_blocks = past_len // SEL_BLOCK
    buf = win_t.shape[-1]
    per_bg = lambda shape: pl.BlockSpec((None, None) + shape, lambda b, g, q, pt, idx: (b, g) + (0,) * len(shape))
    per_bgq = lambda shape: pl.BlockSpec((None, None, None) + shape,
                                         lambda b, g, q, pt, idx: (b, g, q) + (0,) * len(shape))
    grid_spec = pltpu.PrefetchScalarGridSpec(
        num_scalar_prefetch=2,
        grid=(batch, KV_GROUPS, t_new),
        in_specs=[
            pl.BlockSpec(memory_space=pl.ANY),
            per_bgq((HEAD_PAD, HEAD_DIM)),
            pl.BlockSpec((None, HEAD_PAD, LANES), lambda b, g, q, pt, idx: (g, 0, 0)),
            per_bg((2, LANES, HEAD_DIM)),
            pl.BlockSpec((None, None, 2, None, HEAD_DIM, buf), lambda b, g, q, pt, idx: (layer, b, 0, g, 0, 0)),
            per_bg((2, LANES, HEAD_DIM)),
            per_bg((2, HEAD_DIM, LANES)),
            per_bgq((HEAD_PAD, HEAD_DIM)),
            per_bgq((HEAD_PAD, LANES)),
        ],
        out_specs=[
            per_bgq((HEAD_PAD, HEAD_DIM)),
            pl.BlockSpec((None, 2, None, HEAD_DIM, buf), lambda b, g, q, pt, idx: (b, 0, g, 0, 0)),
        ],
        scratch_shapes=[pltpu.VMEM((HEAD_DIM, SEL_TOPN * PAGE_SIZE), BF16)] * 2 + [
            pltpu.VMEM((2, SEL_TOPN, 2, HEAD_DIM, PAGE_SIZE), F32),
            pltpu.SemaphoreType.DMA((2, SEL_TOPN)),
        ],
    )
    return pl.pallas_call(
        functools.partial(_nsa_sample_attend_kernel, layer=layer, past_len=past_len, t_new=t_new,
                          n_past_blocks=n_past_blocks, n_steps=batch * KV_GROUPS * t_new),
        grid_spec=grid_spec,
        out_shape=[jax.ShapeDtypeStruct((batch, KV_GROUPS, t_new, HEAD_PAD, HEAD_DIM), F32),
                   jax.ShapeDtypeStruct((batch, 2, KV_GROUPS, HEAD_DIM, buf), F32)],
        compiler_params=_cparams(("arbitrary", "arbitrary", "arbitrary")),
        name="nsa_sample_attend",
    )(page_table_flat, idx_flat, cache_t, q_heads, slopes, sel_new, win_t, win_new, win_new_t, o_c, gates)


def nsa_sample_layer(x, gain, w, cmp_t, sel_t, win_t, layer, page_table_flat, *, batch, t_new, past_len):
    m = batch * t_new
    proj, gates = _nsa_project(x, gain, w, tm=m, head_norm=True)
    cmp = compress_pages(cmp_t.reshape(cmp_t.shape[0], cmp_t.shape[1], NSA_KV_DIM, PAGE_SIZE), layer,
                         page_table_flat, w["pe"], w["w1"], w["w2"], w["kgain"], batch=batch,
                         n_pages=past_len // PAGE_SIZE)
    q = proj[:, :NSA_Q_DIM].reshape(batch, t_new, KV_GROUPS, GROUP_SIZE, HEAD_DIM)
    q_stack = jnp.pad(q.transpose(0, 2, 3, 1, 4), ((0, 0), (0, 0), (0, 0), (0, Q_PAD - t_new), (0, 0)))
    o_c, idx = nsa_sample_select(q_stack.reshape(batch, KV_GROUPS, GROUP_SIZE * Q_PAD, HEAD_DIM), cmp,
                                 batch=batch, past_len=past_len, t_new=t_new)
    pad_heads = ((0, 0), (0, 0), (0, 0), (0, HEAD_PAD - GROUP_SIZE), (0, 0))
    o_c = o_c.reshape(batch, KV_GROUPS, GROUP_SIZE, Q_PAD, HEAD_DIM)[:, :, :, :t_new].transpose(0, 1, 3, 2, 4)
    idx_flat = idx.reshape(batch, SEL_TOPN, KV_GROUPS, Q_PAD)[..., :t_new].transpose(0, 2, 3, 1).reshape(-1)
    q_heads = jnp.pad(q.transpose(0, 2, 1, 3, 4), pad_heads)

    def new_rows(cols):
        a = cols.reshape(batch, t_new, 2, KV_GROUPS, HEAD_DIM).transpose(0, 3, 2, 1, 4)
        return jnp.pad(a, ((0, 0), (0, 0), (0, 0), (0, LANES - t_new), (0, 0)))

    kvs = proj[:, NSA_Q_DIM + NSA_KV_DIM:NSA_Q_DIM + 2 * NSA_KV_DIM]
    kvw = proj[:, NSA_Q_DIM + 2 * NSA_KV_DIM:]
    win_new_t = kvw.reshape(batch, t_new, 2, KV_GROUPS, HEAD_DIM).transpose(0, 3, 2, 4, 1)
    win_new_t = jnp.pad(win_new_t, ((0, 0), (0, 0), (0, 0), (0, 0), (LANES - t_new, 0)))
    g_arr = gates[:, :3 * NSA_HEADS].reshape(batch, t_new, 3, KV_GROUPS, GROUP_SIZE).transpose(0, 3, 1, 4, 2)
    g_arr = jnp.pad(g_arr, ((0, 0), (0, 0), (0, 0), (0, HEAD_PAD - GROUP_SIZE), (0, LANES - 3)))
    slopes = np.zeros((KV_GROUPS, HEAD_PAD, LANES), np.float32)
    slopes[:, :GROUP_SIZE, :] = np.asarray(_SLOPES, np.float32).reshape(KV_GROUPS, GROUP_SIZE, 1)
    o, win_out = nsa_sample_attend(sel_t, win_t, layer, page_table_flat, idx_flat, q_heads, jnp.asarray(slopes),
                                   new_rows(kvs), new_rows(kvw), win_new_t, jnp.pad(o_c, pad_heads), g_arr,
                                   batch=batch, past_len=past_len, t_new=t_new)
    o = o[:, :, :, :GROUP_SIZE].transpose(0, 2, 1, 3, 4).reshape(m, NSA_Q_DIM)
    return matmul_residual(o.astype(BF16), w["w_out"], x, tm=m), proj, win_out


def _expand_heads(x, e):
    x1 = x.astype(BF16)
    rest = x - x1.astype(F32)
    x2 = rest.astype(BF16)
    x3 = (rest - x2.astype(F32)).astype(BF16)
    return (_dot(x1, e) + _dot(x2, e)) + _dot(x3, e)


def _ssd_kernel(z_ref, x_ref, bc_ref, dt_ref, cw_ref, cb_ref, dtb_ref, alog_ref, drow_ref, nw_ref, cinit_ref,
                hinit_ref, tril_ref, e_ref, y_ref, hout_ref, stage_ref, h_ref, yacc_ref, *, t_valid):
    L = SSM_CHUNK
    c = pl.program_id(1)

    @pl.when(c == 0)
    def _():
        stage_ref[...] = cinit_ref[...]
        h_ref[...] = hinit_ref[...]

    cur = jnp.concatenate([x_ref[...], bc_ref[...]], axis=1)
    prev = stage_ref[...]
    first = lax.broadcasted_iota(jnp.int32, (8, 1), 0)
    conv = cb_ref[...] + cur * cw_ref[SSM_CONV - 1:SSM_CONV, :]
    for s in range(1, SSM_CONV):
        rolled = pltpu.roll(cur, s, axis=0)
        head = jnp.where(first < s, pltpu.roll(prev, s, axis=0), rolled[0:8])
        conv = conv + jnp.concatenate([head, rolled[8:]], axis=0) * cw_ref[SSM_CONV - 1 - s:SSM_CONV - s, :]
    stage_ref[...] = cur[L - 8:L]
    conv = conv * jax.nn.sigmoid(conv)
    xh = conv[:, :SSM_D_INNER]

    row = lax.broadcasted_iota(jnp.int32, (L, LANES), 0)
    dt_in = dt_ref[...] + dtb_ref[...]
    dt = jnp.maximum(dt_in, 0.0) + jnp.log1p(jnp.exp(-jnp.abs(dt_in)))
    dt = jnp.where(row < t_valid, dt, 0.0)
    da = dt * -jnp.exp(alog_ref[...])
    acum = _dot(tril_ref[...], da, HIGHEST)
    acum_t = jnp.transpose(acum)
    acum_last = acum[L - 1:L, :]
    dt_full = _expand_heads(dt, e_ref[...])
    grow_full = _expand_heads(jnp.exp(acum), e_ref[...])
    toend_full = _expand_heads(jnp.exp(acum_last - acum), e_ref[...])
    last_t = jnp.exp(acum_t[:, L - 1:L])
    xdt = xh * dt_full
    causal = lax.broadcasted_iota(jnp.int32, (L, L), 0) >= lax.broadcasted_iota(jnp.int32, (L, L), 1)

    for g in range(SSM_GROUPS):
        gl, gh = g * SSM_GW, (g + 1) * SSM_GW
        bm = conv[:, SSM_D_INNER + g * SSM_STATE:SSM_D_INNER + (g + 1) * SSM_STATE].astype(BF16)
        cm = conv[:, SSM_D_INNER + (SSM_GROUPS + g) * SSM_STATE:
                  SSM_D_INNER + (SSM_GROUPS + g + 1) * SSM_STATE].astype(BF16)
        cb = _nt(cm, bm)
        xdt_g = xdt[:, gl:gh]
        for r in range(SSM_HPG):
            h = g * SSM_HPG + r
            seg = acum[:, h:h + 1] - acum_t[h:h + 1, :]
            decay = jnp.exp(jnp.where(causal, seg, NEG_INF))
            yacc_ref[:, gl + r * SSM_HEAD_DIM:gl + (r + 1) * SSM_HEAD_DIM] = _dot(
                (cb * decay).astype(BF16), xdt_g[:, r * SSM_HEAD_DIM:(r + 1) * SSM_HEAD_DIM].astype(BF16))
        h_g = h_ref[gl:gh, :]
        y_state = _nt(cm, h_g.astype(BF16)) * grow_full[:, gl:gh]
        yacc_ref[:, gl:gh] = yacc_ref[:, gl:gh] + y_state
        xw = (xdt_g * toend_full[:, gl:gh]).astype(BF16)
        h_add = lax.dot_general(xw, bm, (((0,), (0,)), ((), ())), preferred_element_type=F32)
        for r in range(SSM_HPG):
            h = g * SSM_HPG + r
            rl, rh = r * SSM_HEAD_DIM, (r + 1) * SSM_HEAD_DIM
            keep = jnp.broadcast_to(last_t[h:h + 1, :], (SSM_HEAD_DIM, SSM_STATE))
            h_ref[gl + rl:gl + rh, :] = h_g[rl:rh] * keep + h_add[rl:rh]

    zf = z_ref[...]
    gated = (yacc_ref[...] + xh * drow_ref[...]) * (zf * jax.nn.sigmoid(zf))
    for g in range(SSM_GROUPS):
        gl, gh = g * SSM_GW, (g + 1) * SSM_GW
        y_ref[:, gl:gh] = _rms(gated[:, gl:gh], nw_ref[:, gl:gh]).astype(y_ref.dtype)

    @pl.when(c == pl.num_programs(1) - 1)
    def _():
        hout_ref[...] = h_ref[...]


def ssd_mixer(proj, dt_raw, conv_init, h_init, w, *, batch, n_chunks, t_valid):
    L = SSM_CHUNK
    rows = batch * n_chunks * L
    conv_dim = SSM_D_INNER + SSM_BC
    tril = np.tril(np.ones((L, L), np.float32))
    e = np.zeros((LANES, SSM_D_INNER), np.float32)
    e[np.arange(SSM_D_INNER) // SSM_HEAD_DIM, np.arange(SSM_D_INNER)] = 1.0
    const = lambda shape: pl.BlockSpec(shape, lambda b, c: (0,) * len(shape))
    return pl.pallas_call(
        functools.partial(_ssd_kernel, t_valid=t_valid),
        grid=(batch, n_chunks),
        in_specs=[
            pl.BlockSpec((L, SSM_D_INNER), lambda b, c: (b * n_chunks + c, 0)),
            pl.BlockSpec((L, SSM_D_INNER), lambda b, c: (b * n_chunks + c, 1)),
            pl.BlockSpec((L, SSM_BC), lambda b, c: (b * n_chunks + c, 2 * SSM_D_INNER // SSM_BC)),
            pl.BlockSpec((L, LANES), lambda b, c: (b * n_chunks + c, 0)),
            const((SSM_CONV, conv_dim)), const((1, conv_dim)), const((1, LANES)), const((1, LANES)),
            const((1, SSM_D_INNER)), const((1, SSM_D_INNER)),
            pl.BlockSpec((None, 8, conv_dim), lambda b, c: (b, 0, 0)),
            pl.BlockSpec((None, SSM_D_INNER, SSM_STATE), lambda b, c: (b, 0, 0)),
            const((L, L)), const((LANES, SSM_D_INNER)),
        ],
        out_specs=[
            pl.BlockSpec((L, SSM_D_INNER), lambda b, c: (b * n_chunks + c, 0)),
            pl.BlockSpec((None, SSM_D_INNER, SSM_STATE), lambda b, c: (b, 0, 0)),
        ],
        out_shape=[jax.ShapeDtypeStruct((rows, SSM_D_INNER), BF16),
                   jax.ShapeDtypeStruct((batch, SSM_D_INNER, SSM_STATE), F32)],
        scratch_shapes=[pltpu.VMEM((8, conv_dim), F32), pltpu.VMEM((SSM_D_INNER, SSM_STATE), F32),
                        pltpu.VMEM((L, SSM_D_INNER), F32)],
        compiler_params=_cparams(("parallel", "arbitrary")),
        name="ssd_mixer",
    )(proj, proj, proj, dt_raw, w["conv_w"], w["conv_b"], w["dt_bias"], w["a_log"], w["d_row"], w["norm_w"],
      conv_init, h_init, jnp.asarray(tril), jnp.asarray(e, BF16))


def _ssd_weights(w_in, conv_w, conv_b, dt_bias, a_log, d_skip, norm_w, w_out):
    n_main = 2 * SSM_D_INNER + SSM_BC
    pad = lambda v: jnp.pad(v, (0, LANES - SSM_HEADS)).reshape(1, LANES)
    return dict(
        w_main=w_in[:, :n_main].astype(BF16),
        w_dt=jnp.pad(w_in[:, n_main:], ((0, 0), (0, LANES - SSM_HEADS))).astype(BF16),
        conv_w=conv_w, conv_b=conv_b.reshape(1, -1), dt_bias=pad(dt_bias), a_log=pad(a_log),
        d_row=jnp.repeat(d_skip, SSM_HEAD_DIM).reshape(1, SSM_D_INNER), norm_w=norm_w.reshape(1, SSM_D_INNER),
        w_out=w_out.astype(BF16))


def ssd_layer(x, gain, w, conv_state, h_init, *, batch, t):
    tm = min(1024, x.shape[0])
    proj = norm_matmul(x, gain, w["w_main"], tm=tm, tn=1024)
    dt_raw = norm_matmul(x, gain, w["w_dt"], tm=tm, tn=LANES)
    conv_init = jnp.pad(conv_state, ((0, 0), (8 - (SSM_CONV - 1), 0), (0, 0)))
    if t % SSM_CHUNK == 0:
        y, h_last = ssd_mixer(proj, dt_raw, conv_init, h_init, w, batch=batch, n_chunks=t // SSM_CHUNK,
                              t_valid=SSM_CHUNK)
    else:
        pad_rows = lambda a: jnp.pad(a.reshape(batch, t, -1), ((0, 0), (0, SSM_CHUNK - t), (0, 0))).reshape(
            batch * SSM_CHUNK, -1)
        y, h_last = ssd_mixer(pad_rows(proj), pad_rows(dt_raw), conv_init, h_init, w, batch=batch, n_chunks=1,
                              t_valid=t)
        y = y.reshape(batch, SSM_CHUNK, -1)[:, :t].reshape(batch * t, -1)
    return matmul_residual(y, w["w_out"], x, tm=tm), proj, h_last


def kernel(x_prompt, x_sample, cache_kv_cmp, cache_kv_sel, cache_kv_win, state_conv, state_ssm, page_table,
           norm_mix, norm_ffn, nsa_w_in, nsa_qk_gain, nsa_cmp_pe, nsa_cmp_w1, nsa_cmp_w2, nsa_w_out,
           ssd_w_in, ssd_conv_w, ssd_conv_b, ssd_dt_bias, ssd_a_log, ssd_d, ssd_norm, ssd_w_out,
           ffn_w_gate, ffn_w_up, ffn_w_down):
    bp, tp = x_prompt.shape[0], x_prompt.shape[1]
    bs, ts = x_sample.shape[0], x_sample.shape[1]
    depth = norm_mix.shape[0]
    past_len = page_table.shape[1] * PAGE_SIZE
    conv_dim = SSM_D_INNER + SSM_BC
    assert tp % SSM_CHUNK == 0 and ts < SSM_CHUNK and ts <= Q_PAD and ts >= SSM_CONV - 1
    xp = x_prompt.reshape(bp * tp, D_MODEL)
    xs = x_sample.reshape(bs * ts, D_MODEL)
    pt_flat = page_table.reshape(-1)
    feature_major = lambda a: a.transpose(0, 1, 3, 4, 5, 2)
    cmp_t, sel_t, win_t = feature_major(cache_kv_cmp), feature_major(cache_kv_sel), feature_major(cache_kv_win)
    kv_shape = (2, KV_GROUPS, HEAD_DIM)
    outs = {k: [] for k in ("cmp_p", "sel_p", "win_p", "conv_p", "ssm_p", "cmp_s", "sel_s", "win_s", "conv_s",
                            "ssm_s")}
    for i in range(depth):
        l = i // 2
        if i % 2 == 0:
            w = _nsa_weights(nsa_w_in[l], nsa_qk_gain[l], nsa_cmp_pe[l], nsa_cmp_w1[l], nsa_cmp_w2[l], nsa_w_out[l])
            xp, cmp_rows, sel_rows, win_rows = nsa_prompt_layer(xp, norm_mix[i], w, batch=bp, t=tp)
            row_major = lambda a: a.reshape((bp,) + kv_shape + (a.shape[-1],)).transpose(0, 4, 1, 2, 3)
            outs["cmp_p"].append(row_major(cmp_rows))
            outs["sel_p"].append(row_major(sel_rows))
            outs["win_p"].append(row_major(win_rows))
            kv = lambda a, b, t, j: a[:, NSA_Q_DIM + j * NSA_KV_DIM:NSA_Q_DIM + (j + 1) * NSA_KV_DIM].reshape(
                (b, t) + kv_shape)
            xs, proj_s, win_out = nsa_sample_layer(xs, norm_mix[i], w, cmp_t, sel_t, win_t, l, pt_flat, batch=bs,
                                                   t_new=ts, past_len=past_len)
            outs["cmp_s"].append(kv(proj_s, bs, ts, 0))
            outs["sel_s"].append(kv(proj_s, bs, ts, 1))
            outs["win_s"].append(win_out.transpose(0, 4, 1, 2, 3))
        else:
            w = _ssd_weights(ssd_w_in[l], ssd_conv_w[l], ssd_conv_b[l], ssd_dt_bias[l], ssd_a_log[l], ssd_d[l],
                             ssd_norm[l], ssd_w_out[l])
            xp, proj, h_last = ssd_layer(xp, norm_mix[i], w, jnp.zeros((bp, SSM_CONV - 1, conv_dim), F32),
                                         jnp.zeros((bp, SSM_D_INNER, SSM_STATE), F32), batch=bp, t=tp)
            outs["conv_p"].append(proj.reshape(bp, tp, -1)[:, tp - (SSM_CONV - 1):, SSM_D_INNER:])
            outs["ssm_p"].append(h_last.reshape(bp, SSM_HEADS, SSM_HEAD_DIM, SSM_STATE))
            xs, proj_s, h_last = ssd_layer(xs, norm_mix[i], w, state_conv[l],
                                           state_ssm[l].reshape(bs, SSM_D_INNER, SSM_STATE), batch=bs, t=ts)
            outs["conv_s"].append(proj_s.reshape(bs, ts, -1)[:, ts - (SSM_CONV - 1):, SSM_D_INNER:])
            outs["ssm_s"].append(h_last.reshape(bs, SSM_HEADS, SSM_HEAD_DIM, SSM_STATE))
        wg, wu, wd = ffn_w_gate[i].astype(BF16), ffn_w_up[i].astype(BF16), ffn_w_down[i].astype(BF16)
        xp = ffn(xp, norm_ffn[i], wg, wu, wd, tm=1024, th=1408)
        xs = ffn(xs, norm_ffn[i], wg, wu, wd, tm=bs * ts, th=256)
    order = ("cmp_p", "sel_p", "win_p", "conv_p", "ssm_p", "cmp_s", "sel_s", "win_s", "conv_s", "ssm_s")
    return (xp.reshape(bp, tp, D_MODEL), xs.reshape(bs, ts, D_MODEL)) + tuple(jnp.stack(outs[k]) for k in order)
```

```python
import functools
import math

import jax
import jax.numpy as jnp
import ml_dtypes
import numpy as np
from jax import lax
from jax.experimental import pallas as pl
from jax.experimental.pallas import tpu as pltpu

F32 = jnp.float32
BF16 = jnp.bfloat16
HIGHEST = lax.Precision.HIGHEST

D_MODEL = 1024
PAGE_SIZE = 128
NSA_HEADS = 16
HEAD_DIM = 64
KV_GROUPS = 4
GROUP_SIZE = NSA_HEADS // KV_GROUPS
CMP_BLOCK = 32
CMP_STRIDE = 16
CMP_HIDDEN = 256
SEL_BLOCK = 64
SEL_CHUNKS = SEL_BLOCK // CMP_STRIDE
SEL_TOPN = 16
WINDOW = 512
NSA_Q_DIM = NSA_HEADS * HEAD_DIM
NSA_KV_DIM = 2 * KV_GROUPS * HEAD_DIM
KV_HALF = KV_GROUPS * HEAD_DIM
SSM_D_INNER = 2048
SSM_HEAD_DIM = 64
SSM_HEADS = 32
SSM_GROUPS = 4
SSM_HPG = SSM_HEADS // SSM_GROUPS
SSM_STATE = 128
SSM_CONV = 4
SSM_BC = 2 * SSM_GROUPS * SSM_STATE
SSM_GW = SSM_D_INNER // SSM_GROUPS
SSM_CHUNK = 128
NORM_EPS = 1e-6
NEG_INF = -1e30
FORCE_SCORE = 1e4
LANES = 128
VMEM_LIMIT = 56 * 1024 * 1024

_SLOPES = [float(np.float32((2.0 ** (-8.0 / NSA_HEADS)) ** (i + 1))) for i in range(NSA_HEADS)]


def _cparams(sem):
    return pltpu.CompilerParams(dimension_semantics=sem, vmem_limit_bytes=VMEM_LIMIT)


def _nt(a, b, precision=None):
    return lax.dot_general(a, b, (((1,), (1,)), ((), ())), preferred_element_type=F32, precision=precision)


def _dot(a, b, precision=None):
    return jnp.dot(a, b, preferred_element_type=F32, precision=precision)


def _rms(x, gain):
    return x * lax.rsqrt(jnp.mean(x * x, axis=-1, keepdims=True) + NORM_EPS) * gain


def _norm_matmul_kernel(x_ref, g_ref, w_ref, hg_ref, hm_ref, e_ref, et_ref, o_ref, xn_ref, *, head_norm):
    @pl.when(pl.program_id(1) == 0)
    def _():
        xn_ref[...] = _rms(x_ref[...], g_ref[...]).astype(BF16)

    y = _dot(xn_ref[...], w_ref[...])
    if head_norm:
        ss = _dot(y * y, e_ref[...], HIGHEST)
        inv = lax.rsqrt(ss * (1.0 / HEAD_DIM) + NORM_EPS)
        yn = y * _dot(inv, et_ref[...], HIGHEST) * hg_ref[...]
        y = jnp.where(hm_ref[...] > 0.5, yn, y)
    o_ref[...] = y


def norm_matmul(x, gain, w, head_gain=None, head_mask=None, *, tm, tn):
    m, k = x.shape
    n = w.shape[1]
    head_norm = head_gain is not None
    if not head_norm:
        head_gain = jnp.zeros((1, n), F32)
        head_mask = jnp.zeros((1, n), F32)
    lane = np.arange(tn)
    e = np.zeros((tn, LANES), np.float32)
    e[lane, lane // HEAD_DIM] = 1.0
    return pl.pallas_call(
        functools.partial(_norm_matmul_kernel, head_norm=head_norm),
        grid=(m // tm, n // tn),
        in_specs=[
            pl.BlockSpec((tm, k), lambda i, j: (i, 0)),
            pl.BlockSpec((1, k), lambda i, j: (0, 0)),
            pl.BlockSpec((k, tn), lambda i, j: (0, j)),
            pl.BlockSpec((1, tn), lambda i, j: (0, j)),
            pl.BlockSpec((1, tn), lambda i, j: (0, j)),
            pl.BlockSpec((tn, LANES), lambda i, j: (0, 0)),
            pl.BlockSpec((LANES, tn), lambda i, j: (0, 0)),
        ],
        out_specs=pl.BlockSpec((tm, tn), lambda i, j: (i, j)),
        out_shape=jax.ShapeDtypeStruct((m, n), F32),
        scratch_shapes=[pltpu.VMEM((tm, k), BF16)],
        compiler_params=_cparams(("parallel", "arbitrary")),
        name="norm_matmul",
    )(x, gain.reshape(1, k), w, head_gain, head_mask, jnp.asarray(e), jnp.asarray(e.T))


def _ffn_kernel(x_ref, g_ref, wg_ref, wu_ref, wd_ref, o_ref, xn_ref):
    @pl.when(pl.program_id(1) == 0)
    def _():
        x = x_ref[...]
        xn_ref[...] = _rms(x, g_ref[...]).astype(BF16)
        o_ref[...] = x

    h = xn_ref[...]
    gate = _dot(h, wg_ref[...])
    up = _dot(h, wu_ref[...])
    act = (gate * jax.nn.sigmoid(gate) * up).astype(BF16)
    o_ref[...] += _dot(act, wd_ref[...])


def ffn(x, gain, wg, wu, wd, *, tm, th):
    m, k = x.shape
    hid = wg.shape[1]
    return pl.pallas_call(
        _ffn_kernel,
        grid=(m // tm, hid // th),
        in_specs=[
            pl.BlockSpec((tm, k), lambda i, j: (i, 0)),
            pl.BlockSpec((1, k), lambda i, j: (0, 0)),
            pl.BlockSpec((k, th), lambda i, j: (0, j)),
            pl.BlockSpec((k, th), lambda i, j: (0, j)),
            pl.BlockSpec((th, k), lambda i, j: (j, 0)),
        ],
        out_specs=pl.BlockSpec((tm, k), lambda i, j: (i, 0)),
        out_shape=jax.ShapeDtypeStruct((m, k), F32),
        scratch_shapes=[pltpu.VMEM((tm, k), BF16)],
        compiler_params=_cparams(("parallel", "arbitrary")),
        name="ffn",
    )(x, gain.reshape(1, k), wg, wu, wd)


def _matmul_res_kernel(a_ref, w_ref, r_ref, o_ref):
    o_ref[...] = r_ref[...] + _dot(a_ref[...], w_ref[...])


def matmul_residual(a, w, res, *, tm):
    m, k = a.shape
    n = w.shape[1]
    return pl.pallas_call(
        _matmul_res_kernel,
        grid=(m // tm,),
        in_specs=[
            pl.BlockSpec((tm, k), lambda i: (i, 0)),
            pl.BlockSpec((k, n), lambda i: (0, 0)),
            pl.BlockSpec((tm, n), lambda i: (i, 0)),
        ],
        out_specs=pl.BlockSpec((tm, n), lambda i: (i, 0)),
        out_shape=jax.ShapeDtypeStruct((m, n), F32),
        compiler_params=_cparams(("parallel",)),
        name="matmul_residual",
    )(a, w, res)


CMP_PACK = 4
CMP_PITCH = 24


def _compress_accumulate(xs_ref, n_rows, pe_ref, w1_ref, kv):
    n_pair = NSA_KV_DIM // LANES
    low = lax.broadcasted_iota(jnp.int32, (n_rows, LANES), 1) < HEAD_DIM
    acc = [None, None]
    for jp in range(CMP_STRIDE // CMP_PACK):
        groups = []
        for c in range(kv * n_pair // 2, (kv + 1) * n_pair // 2):
            rows = [xs_ref.at[c][pl.ds(jp * CMP_PACK + i, n_rows, stride=CMP_PITCH), :] for i in range(CMP_PACK)]
            swapped = [pltpu.roll(r, HEAD_DIM, axis=1) for r in rows]
            groups.append(jnp.concatenate([jnp.where(low, rows[i], swapped[i + 1])
                                           for i in range(0, CMP_PACK, 2)], axis=1))
            groups.append(jnp.concatenate([jnp.where(low, swapped[i], rows[i + 1])
                                           for i in range(0, CMP_PACK, 2)], axis=1))
        xs = jnp.concatenate(groups, axis=0)
        for m in range(2):
            lhs = (xs + pe_ref[kv, m, jp:jp + 1, :]).astype(BF16)
            part = _dot(lhs, w1_ref[kv, m, jp])
            acc[m] = part if acc[m] is None else acc[m] + part
    return acc


def _compress_finish(acc, w2_ref, kg_ref, kv):
    rows = acc[0].shape[0]
    pre = acc[0] + pltpu.roll(acc[1], rows - 1, axis=0)
    hid = (pre * jax.nn.sigmoid(pre)).astype(BF16)
    out = _dot(hid, w2_ref[kv])
    if kv == 0:
        out = _rms(out, kg_ref[...])
    return out


def _compress_rows_kernel(x_ref, pe_ref, w1_ref, w2_ref, kg_ref, o_ref, xt_ref, xs_ref, *, n_chunk):
    xt_ref[...] = jnp.transpose(x_ref[...])
    n_pair = NSA_KV_DIM // LANES
    def stage_chunk(k, carry):
        src = pl.ds(pl.multiple_of(k * CMP_STRIDE, CMP_STRIDE), CMP_STRIDE)
        dst = pl.ds(pl.multiple_of(k * CMP_PITCH, 8), CMP_STRIDE)
        for c in range(n_pair):
            xs_ref[c, dst, :] = x_ref[src, c * LANES:(c + 1) * LANES]
        return carry

    lax.fori_loop(0, n_chunk, stage_chunk, 0)
    for kv in range(2):
        o_ref[kv] = _compress_finish(_compress_accumulate(xs_ref, n_chunk, pe_ref, w1_ref, kv), w2_ref, kg_ref, kv)


def compress_rows(proj, col_block, pe, w1, w2, kgain, *, batch, t):
    n_chunk = t // CMP_STRIDE
    return pl.pallas_call(
        functools.partial(_compress_rows_kernel, n_chunk=n_chunk),
        grid=(batch,),
        in_specs=[
            pl.BlockSpec((t, NSA_KV_DIM), lambda b: (b, col_block)),
            pl.BlockSpec(pe.shape, lambda b: (0, 0, 0, 0)),
            pl.BlockSpec(w1.shape, lambda b: (0, 0, 0, 0, 0)),
            pl.BlockSpec(w2.shape, lambda b: (0, 0, 0)),
            pl.BlockSpec((1, HEAD_DIM), lambda b: (0, 0)),
        ],
        out_specs=[pl.BlockSpec((None, 2, KV_GROUPS * n_chunk, HEAD_DIM), lambda b: (b, 0, 0, 0)),
                   pl.BlockSpec((None, NSA_KV_DIM, t), lambda b: (b, 0, 0))],
        out_shape=[jax.ShapeDtypeStruct((batch, 2, KV_GROUPS * n_chunk, HEAD_DIM), F32),
                   jax.ShapeDtypeStruct((batch, NSA_KV_DIM, t), F32)],
        scratch_shapes=[pltpu.VMEM((NSA_KV_DIM // LANES, n_chunk * CMP_PITCH, LANES), F32)],
        compiler_params=_cparams(("parallel",)),
        name="compress_rows",
    )(proj, pe, w1, w2, kgain.reshape(1, HEAD_DIM))


def _slope_col(g, rows, tq):
    rblk = lax.broadcasted_iota(jnp.int32, (rows, 1), 0) // tq
    col = jnp.full((rows, 1), _SLOPES[GROUP_SIZE * g + GROUP_SIZE - 1], F32)
    for r in range(GROUP_SIZE - 1):
        col = jnp.where(rblk == r, _SLOPES[GROUP_SIZE * g + r], col)
    return col


def _normalize(acc, l):
    return acc * jnp.where(l > 0.0, 1.0 / jnp.where(l > 0.0, l, 1.0), 0.0)


MASK_BIG = 2.0 ** 100
LOG2E = 1.4426950408889634
AUG = 64
AUG_POS = 32


def _bf16_terms(x, n=3):
    x = np.asarray(x, np.float64)
    terms = []
    for _ in range(n):
        term = x.astype(ml_dtypes.bfloat16).astype(np.float64)
        terms.append(term.astype(np.float32))
        x = x - term
    return terms


def _key_aug(pos, flags=None):
    pos = np.asarray(pos, np.int64)
    aug = np.zeros((pos.shape[0], AUG), np.float32)
    if flags is not None:
        aug[:, :flags.shape[1]] = flags
    coarse = (pos // SEL_BLOCK) * SEL_BLOCK
    aug[:, AUG_POS:AUG_POS + 3] = coarse[:, None]
    aug[:, AUG_POS + 3:AUG_POS + 6] = (pos - coarse)[:, None]
    return aug


def _query_aug_rows(tq):
    rows = np.zeros((KV_GROUPS, AUG - AUG_POS, GROUP_SIZE * tq), np.float32)
    for h in range(NSA_HEADS):
        terms = _bf16_terms(np.float64(_SLOPES[h]) * LOG2E)
        g, r = divmod(h, GROUP_SIZE)
        for i, term in enumerate(terms):
            rows[g, i, r * tq:(r + 1) * tq] = term
            rows[g, 3 + i, r * tq:(r + 1) * tq] = term
    return rows


def _online_update(s, v_t, m, l, acc):
    m_new = jnp.maximum(m, jnp.max(s, axis=0, keepdims=True))
    alpha = jnp.exp2(m - m_new)
    p = jnp.exp2(s - m_new)
    return m_new, alpha * l + jnp.sum(p, axis=0, keepdims=True), alpha * acc + _dot(v_t, p.astype(BF16))


def _nsa_prompt_kernel(q_ref, ks_ref, kw_ref, cmp_ref, gate_ref, gq_ref, gks_ref, gkw_ref, gksc_ref, gkwc_ref,
                       csel_ref, cwin_ref, ccmp_ref, qrows_ref, mmap_ref, eye_ref,
                       o_ref, kst_out_ref, kwt_out_ref,
                       ksa_ref, kwa_ref, vst_ref, vwt_ref, kca_ref, vct_ref, ot_ref, ss_ref, ps_ref, sacc_ref,
                       *, t, tq):
    qi = pl.program_id(1)
    n_sel = t // SEL_BLOCK
    n_cmp = t // CMP_STRIDE
    top_n = min(SEL_TOPN, n_sel)
    cols = GROUP_SIZE * tq
    nt = t // tq
    n_win = WINDOW // tq + 1
    n_pad = WINDOW // tq

    @pl.when(qi == 0)
    def _():
        def prepare(src_ref, grow_ref, gcol_ref, const_ref, ka_ref, vt_ref, out_ref, pad_tiles):
            pad = pad_tiles * tq
            x = src_ref[...]
            xt = jnp.transpose(x)
            width = out_ref.shape[1]
            out_ref[KV_HALF:, :] = xt[KV_HALF:, t - width:]
            for g in range(KV_GROUPS):
                lo, hi = g * HEAD_DIM, (g + 1) * HEAD_DIM
                kn = _rms(x[:, lo:hi], grow_ref[...])
                ka_ref[g, pad:pad + t, :] = jnp.concatenate([kn, const_ref[pad:pad + t, :]], axis=1).astype(BF16)
                if pad:
                    ka_ref[g, 0:pad, :] = jnp.concatenate(
                        [jnp.zeros((pad, HEAD_DIM), F32), const_ref[0:pad, :]], axis=1).astype(BF16)
                kt = xt[lo:hi, :]
                ktn = kt * lax.rsqrt(jnp.mean(kt * kt, axis=0, keepdims=True) + NORM_EPS) * gcol_ref[...]
                out_ref[lo:hi, :] = ktn[:, t - width:]
                vt = xt[KV_HALF + lo:KV_HALF + hi, :].astype(BF16)
                base = g * (nt + pad_tiles)
                for i in range(pad_tiles):
                    vt_ref[base + i] = jnp.zeros((HEAD_DIM, tq), BF16)
                for i in range(nt):
                    vt_ref[base + pad_tiles + i] = vt[:, i * tq:(i + 1) * tq]

        prepare(ks_ref, gks_ref, gksc_ref, csel_ref, ksa_ref, vst_ref, kst_out_ref, 0)
        prepare(kw_ref, gkw_ref, gkwc_ref, cwin_ref, kwa_ref, vwt_ref, kwt_out_ref, n_pad)
        for g in range(KV_GROUPS):
            kc = cmp_ref[0, g * n_cmp:(g + 1) * n_cmp, :]
            kca_ref[g] = jnp.concatenate([kc, ccmp_ref[...]], axis=1).astype(BF16)
            vct_ref[g] = _nt(eye_ref[...], cmp_ref[1, g * n_cmp:(g + 1) * n_cmp, :].astype(BF16)).astype(BF16)

    q0 = pl.multiple_of(qi * tq, tq)
    lane4 = lax.broadcasted_iota(jnp.int32, (1, cols), 1) % tq
    key_off = lax.broadcasted_iota(jnp.int32, (tq, cols), 0)
    causal = key_off <= lane4
    win_old = key_off > lane4
    cmp_ok = (lax.broadcasted_iota(jnp.int32, (n_cmp, cols), 0) * CMP_STRIDE + (CMP_BLOCK - 1)) <= q0 + lane4
    qt = jnp.transpose(q_ref[...])
    gt = jax.nn.sigmoid(jnp.transpose(gate_ref[...]))
    jblk = lax.broadcasted_iota(jnp.int32, (n_sel, tq), 0)
    t_lane = q0 + lax.broadcasted_iota(jnp.int32, (n_sel, tq), 1)
    cur = t_lane // SEL_BLOCK
    forced = (jblk == 0) | (jblk == cur) | (jblk == cur - 1)
    visible = jblk * SEL_BLOCK <= t_lane

    o_c, q_sel, q_win = [], [], []
    for g in range(KV_GROUPS):
        heads = []
        for r in range(GROUP_SIZE):
            h = GROUP_SIZE * g + r
            qh = qt[h * HEAD_DIM:(h + 1) * HEAD_DIM, :]
            qh = qh * lax.rsqrt(jnp.mean(qh * qh, axis=0, keepdims=True) + NORM_EPS)
            heads.append(qh * (gq_ref[...] * (HEAD_DIM ** -0.5 * LOG2E)))
        q_top = jnp.concatenate(heads, axis=1)
        q_low = qrows_ref[g]

        def with_mask_rows(rows_, q_top=q_top, q_low=q_low):
            return jnp.concatenate([q_top, rows_, q_low], axis=0).astype(BF16)

        s = _dot(kca_ref[g], with_mask_rows(jnp.zeros((AUG_POS, cols), F32)))
        s = jnp.where(cmp_ok, s, -MASK_BIG)
        p = jnp.where(cmp_ok, jnp.exp2(s - jnp.max(s, axis=0, keepdims=True)), 0.0)
        pc = _normalize(p, jnp.sum(p, axis=0, keepdims=True))
        o_c.append(_dot(vct_ref[g], pc.astype(BF16)))
        imp = pc[:, 0:tq]
        for r in range(1, GROUP_SIZE):
            imp = imp + pc[:, r * tq:(r + 1) * tq]

        p_slc = _dot(mmap_ref[...], imp, HIGHEST)
        score = jnp.where(forced, FORCE_SCORE, jnp.where(visible, p_slc, -1.0))
        cnt = jnp.zeros((n_sel, tq), jnp.int32)
        for k in range(n_sel):
            sk = score[k:k + 1, :]
            beats = (sk > score) | ((sk == score) & (jblk > k))
            cnt = cnt + beats.astype(jnp.int32)
        drop = jnp.where(cnt < top_n, 0.0, -MASK_BIG)
        if n_sel < AUG_POS:
            drop = jnp.concatenate([drop, jnp.zeros((AUG_POS - n_sel, tq), F32)], axis=0)
        q_sel.append(with_mask_rows(jnp.concatenate([drop] * GROUP_SIZE, axis=1)))
        q_win.append(with_mask_rows(jnp.full((AUG_POS, cols), -MASK_BIG, F32)))

    def sel_scores(tile, slot):
        k0 = pl.multiple_of(tile * tq, tq)
        for g in range(KV_GROUPS):
            ss_ref[slot, g] = _dot(ksa_ref[g, pl.ds(k0, tq), :], q_sel[g])

    def sel_values(tile):
        return [_dot(vst_ref[g * nt + tile], ps_ref[g]) for g in range(KV_GROUPS)]

    def sel_softmax(slot, stats, pv, mask=None):
        new_stats = []
        for g in range(KV_GROUPS):
            m, l = stats[g]
            s = ss_ref[slot, g]
            if mask is not None:
                s = jnp.where(mask, s, -MASK_BIG)
            m_new = jnp.maximum(m, jnp.max(s, axis=0, keepdims=True))
            alpha = jnp.exp2(m - m_new)
            p = jnp.exp2(s - m_new)
            sacc_ref[g] = alpha * (sacc_ref[g] + pv[g])
            ps_ref[g] = p.astype(BF16)
            new_stats.append((m_new, alpha * l + jnp.sum(p, axis=0, keepdims=True)))
        return tuple(new_stats)

    def sel_step(slot, next_tile, prev_tile, stats, mask=None):
        pv = sel_values(jnp.maximum(prev_tile, 0))
        if next_tile is not None:
            sel_scores(next_tile, 1 - slot)
        return sel_softmax(slot, stats, pv, mask)

    def sel_finish(stats):
        pv = sel_values(qi)
        for g in range(KV_GROUPS):
            sacc_ref[g] = _normalize(sacc_ref[g] + pv[g], stats[g][1])

    for g in range(KV_GROUPS):
        ps_ref[g] = jnp.zeros((tq, cols), BF16)
        sacc_ref[g] = jnp.zeros((HEAD_DIM, cols), F32)
    sel_scores(0, 0)

    def sel_pair(j, stats):
        stats = sel_step(0, 2 * j + 1, 2 * j - 1, stats)
        return sel_step(1, 2 * j + 2, 2 * j, stats)

    stats0 = ((jnp.full((1, cols), -MASK_BIG, F32), jnp.zeros((1, cols), F32)),) * KV_GROUPS
    stats = lax.fori_loop(0, qi // 2, sel_pair, stats0)

    @pl.when(qi % 2 == 1)
    def _():
        sel_finish(sel_step(1, None, qi - 1, sel_step(0, qi, qi - 2, stats), mask=causal))

    @pl.when(qi % 2 == 0)
    def _():
        sel_finish(sel_step(0, None, qi - 1, stats, mask=causal))

    o_s = [sacc_ref[g] for g in range(KV_GROUPS)]

    win_scores = [[_dot(kwa_ref[g, pl.ds(q0 + i * tq, tq), :], q_win[g]) for i in range(n_win)]
                  for g in range(KV_GROUPS)]
    o_w = []
    for g in range(KV_GROUPS):
        tiles = win_scores[g]
        tiles[0] = jnp.where(win_old, tiles[0], -MASK_BIG)
        tiles[-1] = jnp.where(causal, tiles[-1], -MASK_BIG)
        m = functools.reduce(jnp.maximum, [jnp.max(s, axis=0, keepdims=True) for s in tiles])
        probs = [jnp.exp2(s - m) for s in tiles]
        l = functools.reduce(jnp.add, [jnp.sum(p, axis=0, keepdims=True) for p in probs])
        acc = functools.reduce(jnp.add, [_dot(vwt_ref[g * (nt + n_pad) + qi + i], p.astype(BF16))
                                         for i, p in enumerate(probs)])
        o_w.append(_normalize(acc, l))

    for h in range(NSA_HEADS):
        g, r = divmod(h, GROUP_SIZE)
        sl = slice(r * tq, (r + 1) * tq)
        ot_ref[h * HEAD_DIM:(h + 1) * HEAD_DIM, :] = (
            gt[h:h + 1, :] * o_c[g][:, sl] + gt[NSA_HEADS + h:NSA_HEADS + h + 1, :] * o_s[g][:, sl]
            + gt[2 * NSA_HEADS + h:2 * NSA_HEADS + h + 1, :] * o_w[g][:, sl])
    o_ref[...] = jnp.transpose(ot_ref[...]).astype(o_ref.dtype)


def nsa_prompt_attention(proj, gates, cmp, qk_gain, *, batch, t, tq):
    n_sel = t // SEL_BLOCK
    n_cmp = t // CMP_STRIDE
    nq = t // tq
    nt = t // tq
    n_pad = WINDOW // tq
    keep = min(WINDOW, t)
    assert n_sel <= AUG_POS and t % tq == 0 and WINDOW % tq == 0 and tq % SEL_BLOCK == 0
    n = np.arange(n_cmp)
    mmap = np.zeros((n_sel, n_cmp), np.float32)
    real = n < n_cmp - 1
    np.add.at(mmap, (n[real] // SEL_CHUNKS, n[real]), 1.0)
    nxt = real & ((n + 1) // SEL_CHUNKS < n_sel)
    np.add.at(mmap, ((n[nxt] + 1) // SEL_CHUNKS, n[nxt]), 1.0)
    pos = np.arange(t)
    csel = _key_aug(pos, (pos[:, None] // SEL_BLOCK == np.arange(n_sel)[None, :]).astype(np.float32))
    wpos = np.arange(-WINDOW, t)
    cwin = _key_aug(wpos, (wpos < 0).astype(np.float32)[:, None])
    ccmp = _key_aug(np.arange(n_cmp) * CMP_STRIDE + (CMP_BLOCK - 1))
    const = lambda shape: pl.BlockSpec(shape, lambda b, i: (0,) * len(shape))
    col = lambda v: v.reshape(HEAD_DIM, 1)
    row = lambda v: v.reshape(1, HEAD_DIM)
    return pl.pallas_call(
        functools.partial(_nsa_prompt_kernel, t=t, tq=tq),
        grid=(batch, nq),
        in_specs=[
            pl.BlockSpec((tq, NSA_Q_DIM), lambda b, i: (b * nq + i, 0)),
            pl.BlockSpec((t, NSA_KV_DIM), lambda b, i: (b, 3)),
            pl.BlockSpec((t, NSA_KV_DIM), lambda b, i: (b, 4)),
            pl.BlockSpec((None, 2, KV_GROUPS * n_cmp, HEAD_DIM), lambda b, i: (b, 0, 0, 0)),
            pl.BlockSpec((tq, LANES), lambda b, i: (b * nq + i, 0)),
            const((HEAD_DIM, 1)), const((1, HEAD_DIM)), const((1, HEAD_DIM)), const((HEAD_DIM, 1)),
            const((HEAD_DIM, 1)),
            const((t, AUG)), const((t + WINDOW, AUG)), const((n_cmp, AUG)),
            const((KV_GROUPS, AUG - AUG_POS, GROUP_SIZE * tq)), const((n_sel, n_cmp)), const((HEAD_DIM, HEAD_DIM)),
        ],
        out_specs=[
            pl.BlockSpec((tq, NSA_Q_DIM), lambda b, i: (b * nq + i, 0)),
            pl.BlockSpec((None, NSA_KV_DIM, t), lambda b, i: (b, 0, 0)),
            pl.BlockSpec((None, NSA_KV_DIM, keep), lambda b, i: (b, 0, 0)),
        ],
        out_shape=[jax.ShapeDtypeStruct((batch * t, NSA_Q_DIM), BF16),
                   jax.ShapeDtypeStruct((batch, NSA_KV_DIM, t), F32),
                   jax.ShapeDtypeStruct((batch, NSA_KV_DIM, keep), F32)],
        scratch_shapes=[
            pltpu.VMEM((KV_GROUPS, t, HEAD_DIM + AUG), BF16),
            pltpu.VMEM((KV_GROUPS, t + WINDOW, HEAD_DIM + AUG), BF16),
            pltpu.VMEM((KV_GROUPS * nt, HEAD_DIM, tq), BF16),
            pltpu.VMEM((KV_GROUPS * (nt + n_pad), HEAD_DIM, tq), BF16),
            pltpu.VMEM((KV_GROUPS, n_cmp, HEAD_DIM + AUG), BF16),
            pltpu.VMEM((KV_GROUPS, HEAD_DIM, n_cmp), BF16),
            pltpu.VMEM((NSA_Q_DIM, tq), F32),
            pltpu.VMEM((2, KV_GROUPS, tq, GROUP_SIZE * tq), F32),
            pltpu.VMEM((KV_GROUPS, tq, GROUP_SIZE * tq), BF16),
            pltpu.VMEM((KV_GROUPS, HEAD_DIM, GROUP_SIZE * tq), F32),
        ],
        compiler_params=_cparams(("parallel", "arbitrary")),
        name="nsa_prompt_attention",
    )(proj, proj, proj, cmp, gates, col(qk_gain[0]), row(qk_gain[2]), row(qk_gain[3]), col(qk_gain[2]),
      col(qk_gain[3]), jnp.asarray(csel), jnp.asarray(cwin), jnp.asarray(ccmp), jnp.asarray(_query_aug_rows(tq)),
      jnp.asarray(mmap), jnp.eye(HEAD_DIM, dtype=BF16))


def _nsa_weights(w_in, qk_gain, cmp_pe, cmp_w1, cmp_w2, w_out):
    n_main = NSA_Q_DIM + 3 * NSA_KV_DIM
    zeros_kv = jnp.zeros((KV_HALF,), F32)
    ones_kv = jnp.ones((KV_HALF,), F32)
    head_gain = jnp.concatenate([
        jnp.tile(qk_gain[0], NSA_HEADS), zeros_kv, zeros_kv,
        jnp.tile(qk_gain[2], KV_GROUPS), zeros_kv, jnp.tile(qk_gain[3], KV_GROUPS), zeros_kv]).reshape(1, n_main)
    head_mask = jnp.concatenate([
        jnp.ones((NSA_Q_DIM,), F32), zeros_kv, zeros_kv, ones_kv, zeros_kv, ones_kv, zeros_kv]).reshape(1, n_main)
    w_gate = jnp.pad(w_in[:, n_main:], ((0, 0), (0, LANES - 3 * NSA_HEADS)))
    return dict(
        w_main=w_in[:, :n_main].astype(BF16), w_gate=w_gate.astype(BF16), head_gain=head_gain, head_mask=head_mask,
        pe=cmp_pe.reshape(2, 2, CMP_STRIDE // CMP_PACK, CMP_PACK * HEAD_DIM),
        w1=cmp_w1.reshape(2, 2, CMP_STRIDE // CMP_PACK, CMP_PACK * HEAD_DIM, CMP_HIDDEN).astype(BF16),
        w2=cmp_w2.astype(BF16), kgain=qk_gain[1], qk_gain=qk_gain, w_out=w_out.astype(BF16))


def _nsa_project(x, gain, w, *, tm, head_norm):
    if head_norm:
        proj = norm_matmul(x, gain, w["w_main"], w["head_gain"], w["head_mask"], tm=tm, tn=512)
    else:
        proj = norm_matmul(x, gain, w["w_main"], tm=tm, tn=1280)
    return proj, norm_matmul(x, gain, w["w_gate"], tm=tm, tn=LANES)


def nsa_prompt_layer(x, gain, w, *, batch, t):
    tm = min(1024, batch * t)
    proj, gates = _nsa_project(x, gain, w, tm=tm, head_norm=False)
    cmp, cmp_t = compress_rows(proj, 2, w["pe"], w["w1"], w["w2"], w["kgain"], batch=batch, t=t)
    o, sel_t, win_t = nsa_prompt_attention(proj, gates, cmp, w["qk_gain"], batch=batch, t=t, tq=128)
    return matmul_residual(o, w["w_out"], x, tm=tm), cmp_t, sel_t, win_t


PAGES_PER_STEP = 8
CHUNKS_PER_PAGE = PAGE_SIZE // CMP_STRIDE


def _compress_pages_kernel(pt_ref, *refs, n_chunk):
    del pt_ref
    pages = refs[:PAGES_PER_STEP]
    pe_ref, w1_ref, w2_ref, kg_ref, o_ref, xs_ref, pre_ref = refs[PAGES_PER_STEP:]
    s = pl.program_id(1)
    n_pair = NSA_KV_DIM // LANES
    step_chunks = PAGES_PER_STEP * CHUNKS_PER_PAGE
    for i, page in enumerate(pages):
        for c in range(n_pair):
            rows = jnp.transpose(page[c * LANES:(c + 1) * LANES, :])
            for k in range(CHUNKS_PER_PAGE):
                at = (i * CHUNKS_PER_PAGE + k) * CMP_PITCH
                xs_ref[c, at:at + CMP_STRIDE, :] = rows[k * CMP_STRIDE:(k + 1) * CMP_STRIDE]
    for kv in range(2):
        acc = _compress_accumulate(xs_ref, step_chunks, pe_ref, w1_ref, kv)
        for m in range(2):
            for g in range(KV_GROUPS):
                pre_ref[kv, m, pl.ds(pl.multiple_of(g * n_chunk + s * step_chunks, step_chunks), step_chunks), :] = (
                    acc[m][g * step_chunks:(g + 1) * step_chunks])

    @pl.when(s == pl.num_programs(1) - 1)
    def _():
        for kv in range(2):
            o_ref[kv] = _compress_finish([pre_ref[kv, 0], pre_ref[kv, 1]], w2_ref, kg_ref, kv)


def compress_pages(cache_t, layer, page_table_flat, pe, w1, w2, kgain, *, batch, n_pages):
    n_chunk = n_pages * CHUNKS_PER_PAGE
    n_steps = n_pages // PAGES_PER_STEP

    def page_spec(i):
        return pl.BlockSpec((None, None, NSA_KV_DIM, PAGE_SIZE),
                            lambda b, s, pt: (layer, pt[b * n_pages + s * PAGES_PER_STEP + i], 0, 0))

    grid_spec = pltpu.PrefetchScalarGridSpec(
        num_scalar_prefetch=1,
        grid=(batch, n_steps),
        in_specs=[page_spec(i) for i in range(PAGES_PER_STEP)] + [
            pl.BlockSpec(pe.shape, lambda b, s, pt: (0, 0, 0, 0)),
            pl.BlockSpec(w1.shape, lambda b, s, pt: (0, 0, 0, 0, 0)),
            pl.BlockSpec(w2.shape, lambda b, s, pt: (0, 0, 0)),
            pl.BlockSpec((1, HEAD_DIM), lambda b, s, pt: (0, 0)),
        ],
        out_specs=pl.BlockSpec((None, 2, KV_GROUPS * n_chunk, HEAD_DIM), lambda b, s, pt: (b, 0, 0, 0)),
        scratch_shapes=[pltpu.VMEM((NSA_KV_DIM // LANES, PAGES_PER_STEP * CHUNKS_PER_PAGE * CMP_PITCH, LANES), F32),
                        pltpu.VMEM((2, 2, KV_GROUPS * n_chunk, CMP_HIDDEN), F32)],
    )
    return pl.pallas_call(
        functools.partial(_compress_pages_kernel, n_chunk=n_chunk),
        grid_spec=grid_spec,
        out_shape=jax.ShapeDtypeStruct((batch, 2, KV_GROUPS * n_chunk, HEAD_DIM), F32),
        compiler_params=_cparams(("parallel", "arbitrary")),
        name="compress_pages",
    )(page_table_flat, *([cache_t] * PAGES_PER_STEP), pe, w1, w2, kgain.reshape(1, HEAD_DIM))


Q_PAD = 8
HEAD_PAD = 16


def _nsa_sample_select_kernel(q_ref, cmp_ref, mmap_ref, oc_ref, idx_ref, score_ref, *, past_len, n_cmp, n_sel):
    rows = GROUP_SIZE * Q_PAD
    n_selp = score_ref.shape[0]
    t_max = past_len + Q_PAD - 1
    row = lax.broadcasted_iota(jnp.int32, (rows, 1), 0)
    t_col = past_len + row % Q_PAD
    jblk = lax.broadcasted_iota(jnp.int32, (n_selp, Q_PAD), 0)
    t_lane = past_len + lax.broadcasted_iota(jnp.int32, (n_selp, Q_PAD), 1)
    cur = t_lane // SEL_BLOCK
    forced = (jblk == 0) | (jblk == cur) | (jblk == cur - 1)
    visible = jblk * SEL_BLOCK <= t_lane
    for g in range(KV_GROUPS):
        qg = (q_ref[g] * (HEAD_DIM ** -0.5)).astype(BF16)
        slope = _slope_col(g, rows, Q_PAD)
        kc = cmp_ref[0, g * n_cmp:(g + 1) * n_cmp, :].astype(BF16)
        vc = cmp_ref[1, g * n_cmp:(g + 1) * n_cmp, :].astype(BF16)
        kpos = lax.broadcasted_iota(jnp.int32, (1, n_cmp), 1) * CMP_STRIDE + (CMP_BLOCK - 1)
        s = _nt(qg, kc) + slope * (kpos - t_max).astype(F32)
        mask = kpos <= t_col
        s = jnp.where(mask, s, NEG_INF)
        p = jnp.where(mask, jnp.exp(s - jnp.max(s, axis=1, keepdims=True)), 0.0)
        pc = _normalize(p, jnp.sum(p, axis=1, keepdims=True))
        oc_ref[g] = _dot(pc.astype(BF16), vc)
        imp = pc[0:Q_PAD]
        for r in range(1, GROUP_SIZE):
            imp = imp + pc[r * Q_PAD:(r + 1) * Q_PAD]
        p_slc = _nt(mmap_ref[...], imp, HIGHEST)
        score = jnp.where(forced, FORCE_SCORE, jnp.where(visible, p_slc, -1.0))
        score_ref[:, g * Q_PAD:(g + 1) * Q_PAD] = jnp.where(jblk < n_sel, score, -2.0)

    score = score_ref[...]
    jall = lax.broadcasted_iota(jnp.int32, score.shape, 0)

    def rank_body(k, cnt):
        sk = score_ref[pl.ds(k, 1), :]
        beats = (sk > score) | ((sk == score) & (jall > k))
        return cnt + beats.astype(jnp.int32)

    cnt = lax.fori_loop(0, n_sel, rank_body, jnp.zeros(score.shape, jnp.int32), unroll=4)
    slots = [jnp.sum(jnp.where(cnt == sl, jall, 0), axis=0, keepdims=True) for sl in range(SEL_TOPN)]
    idx_ref[...] = jnp.concatenate(slots, axis=0)


def nsa_sample_select(q_stack, cmp, *, batch, past_len, t_new):
    n_cmp = past_len // CMP_STRIDE
    n_sel = -(-(past_len + t_new) // SEL_BLOCK)
    n_selp = -(-n_sel // 8) * 8
    n = np.arange(n_cmp)
    mmap = np.zeros((n_selp, n_cmp), np.float32)
    real = n < n_cmp - 1
    np.add.at(mmap, (n[real] // SEL_CHUNKS, n[real]), 1.0)
    np.add.at(mmap, ((n[real] + 1) // SEL_CHUNKS, n[real]), 1.0)
    rows = GROUP_SIZE * Q_PAD
    return pl.pallas_call(
        functools.partial(_nsa_sample_select_kernel, past_len=past_len, n_cmp=n_cmp, n_sel=n_sel),
        grid=(batch,),
        in_specs=[
            pl.BlockSpec((None, KV_GROUPS, rows, HEAD_DIM), lambda b: (b, 0, 0, 0)),
            pl.BlockSpec((None, 2, KV_GROUPS * n_cmp, HEAD_DIM), lambda b: (b, 0, 0, 0)),
            pl.BlockSpec((n_selp, n_cmp), lambda b: (0, 0)),
        ],
        out_specs=[
            pl.BlockSpec((None, KV_GROUPS, rows, HEAD_DIM), lambda b: (b, 0, 0, 0)),
            pl.BlockSpec((None, SEL_TOPN, KV_GROUPS * Q_PAD), lambda b: (b, 0, 0)),
        ],
        out_shape=[jax.ShapeDtypeStruct((batch, KV_GROUPS, rows, HEAD_DIM), F32),
                   jax.ShapeDtypeStruct((batch, SEL_TOPN, KV_GROUPS * Q_PAD), jnp.int32)],
        scratch_shapes=[pltpu.VMEM((n_selp, KV_GROUPS * Q_PAD), F32)],
        compiler_params=_cparams(("parallel",)),
        name="nsa_sample_select",
    )(q_stack, cmp, jnp.asarray(mmap))


def _nsa_sample_attend_kernel(pt_ref, idx_ref, cache_ref, q_ref, slope_ref, snew_ref, win_ref, wnew_ref, wnew_t_ref,
                              oc_ref, gate_ref, o_ref, wout_ref, kcat_ref, vcat_ref, page_ref, sem_ref,
                              *, layer, past_len, t_new, n_past_blocks, n_steps):
    b, g = pl.program_id(0), pl.program_id(1)
    n_pages = past_len // PAGE_SIZE
    n_copies = t_new * SEL_TOPN
    step = b * KV_GROUPS + g

    def page_copy(src_page, src_group, slot, n):
        return pltpu.make_async_copy(cache_ref.at[layer, src_page, :, src_group], page_ref.at[slot, n],
                                     sem_ref.at[slot, n])

    def start_gather(at_step, slot):
        at_b = at_step // KV_GROUPS
        at_g = at_step % KV_GROUPS
        for n in range(n_copies):
            blk = jnp.minimum(idx_ref[at_step * n_copies + n], n_past_blocks - 1)
            page_copy(pt_ref[at_b * n_pages + blk // 2], at_g, slot, n).start()

    @pl.when(step == 0)
    def _():
        start_gather(step, 0)

    @pl.when(step + 1 < n_steps)
    def _():
        start_gather(step + 1, (step + 1) % 2)

    slot = step % 2
    for n in range(n_copies):
        page_copy(0, 0, slot, n).wait()

    slope = slope_ref[:, 0:1]
    lane = lax.broadcasted_iota(jnp.int32, (1, LANES), 1)
    new_pos = past_len + lane
    buf = win_ref.shape[2]
    wpos = past_len - buf + lax.broadcasted_iota(jnp.int32, (1, buf), 1)
    k_new, v_new = snew_ref[0].astype(BF16), snew_ref[1].astype(BF16)
    kw_new, vw_new = wnew_ref[0].astype(BF16), wnew_ref[1].astype(BF16)
    kw_old, vw_old = win_ref[0].astype(BF16), win_ref[1].astype(BF16)

    def attend(tiles, v_last):
        m = jnp.full((HEAD_PAD, 1), NEG_INF, F32)
        for s, mask, _ in tiles:
            m = jnp.maximum(m, jnp.max(jnp.where(mask, s, NEG_INF), axis=1, keepdims=True))
        l = jnp.zeros((HEAD_PAD, 1), F32)
        acc = jnp.zeros((HEAD_PAD, HEAD_DIM), F32)
        for i, (s, mask, v) in enumerate(tiles):
            p = jnp.where(mask, jnp.exp(jnp.where(mask, s, NEG_INF) - m), 0.0)
            l = l + jnp.sum(p, axis=1, keepdims=True)
            if i == len(tiles) - 1:
                acc = acc + _dot(p.astype(BF16), v_last)
            else:
                acc = acc + _nt(p.astype(BF16), v)
        return _normalize(acc, l)

    for qi in range(t_new):
        t = past_len + qi
        q = (q_ref[qi] * (HEAD_DIM ** -0.5)).astype(BF16)
        new_ok = (lane < t_new) & (new_pos <= t)

        kpos, mask = [], []
        for sl in range(SEL_TOPN):
            n = qi * SEL_TOPN + sl
            blk = idx_ref[step * n_copies + n]
            kcat_ref[qi, :, sl * PAGE_SIZE:(sl + 1) * PAGE_SIZE] = page_ref[slot, n, 0].astype(BF16)
            vcat_ref[qi, :, sl * PAGE_SIZE:(sl + 1) * PAGE_SIZE] = page_ref[slot, n, 1].astype(BF16)
            kpos.append((blk // 2) * PAGE_SIZE + lane)
            mask.append((lane // SEL_BLOCK == blk % 2) & (blk < n_past_blocks) & (kpos[-1] <= t))
        kpos = jnp.concatenate(kpos, axis=1)
        mask = jnp.concatenate([m_.astype(jnp.int32) for m_ in mask], axis=1) > 0
        s_sel = _dot(q, kcat_ref[qi]) + slope * (kpos - t).astype(F32)
        s_new = _nt(q, k_new) + slope * (new_pos - t).astype(F32)
        o_s = attend([(s_sel, mask, vcat_ref[qi]), (s_new, new_ok, None)], v_new)

        dist = t - wpos
        s_w = _dot(q, kw_old) + slope * (wpos - t).astype(F32)
        s_wn = _nt(q, kw_new) + slope * (new_pos - t).astype(F32)
        o_w = attend([(s_w, (dist >= 0) & (dist < WINDOW), vw_old),
                      (s_wn, new_ok & (t - new_pos < WINDOW), None)], vw_new)

        gates = jax.nn.sigmoid(gate_ref[qi])
        o_ref[qi] = gates[:, 0:1] * oc_ref[qi] + gates[:, 1:2] * o_s + gates[:, 2:3] * o_w

    for kv in range(2):
        rolled = pltpu.roll(win_ref[kv], buf - t_new, axis=1)
        wout_ref[kv, :, 0:buf - LANES] = rolled[:, 0:buf - LANES]
        wout_ref[kv, :, buf - LANES:] = jnp.where(lane >= LANES - t_new, wnew_t_ref[kv], rolled[:, buf - LANES:])


def nsa_sample_attend(cache_t, win_t, layer, page_table_flat, idx_flat, q_heads, slopes, sel_new, win_new,
                      win_new_t, o_c, gates, *, batch, past_len, t_new):
    n_past_blocks = past_len // SEL_BLOCK
    buf = win_t.shape[-1]
    per_bg = lambda shape: pl.BlockSpec((None, None) + shape, lambda b, g, pt, idx: (b, g) + (0,) * len(shape))
    grid_spec = pltpu.PrefetchScalarGridSpec(
        num_scalar_prefetch=2,
        grid=(batch, KV_GROUPS),
        in_specs=[
            pl.BlockSpec(memory_space=pl.ANY),
            per_bg((t_new, HEAD_PAD, HEAD_DIM)),
            pl.BlockSpec((None, HEAD_PAD, LANES), lambda b, g, pt, idx: (g, 0, 0)),
            per_bg((2, LANES, HEAD_DIM)),
            pl.BlockSpec((None, None, 2, None, HEAD_DIM, buf), lambda b, g, pt, idx: (layer, b, 0, g, 0, 0)),
            per_bg((2, LANES, HEAD_DIM)),
            per_bg((2, HEAD_DIM, LANES)),
            per_bg((t_new, HEAD_PAD, HEAD_DIM)),
            per_bg((t_new, HEAD_PAD, LANES)),
        ],
        out_specs=[
            per_bg((t_new, HEAD_PAD, HEAD_DIM)),
            pl.BlockSpec((None, 2, None, HEAD_DIM, buf), lambda b, g, pt, idx: (b, 0, g, 0, 0)),
        ],
        scratch_shapes=[pltpu.VMEM((t_new, HEAD_DIM, SEL_TOPN * PAGE_SIZE), BF16)] * 2 + [
            pltpu.VMEM((2, t_new * SEL_TOPN, 2, HEAD_DIM, PAGE_SIZE), F32),
            pltpu.SemaphoreType.DMA((2, t_new * SEL_TOPN)),
        ],
    )
    return pl.pallas_call(
        functools.partial(_nsa_sample_attend_kernel, layer=layer, past_len=past_len, t_new=t_new,
                          n_past_blocks=n_past_blocks, n_steps=batch * KV_GROUPS),
        grid_spec=grid_spec,
        out_shape=[jax.ShapeDtypeStruct((batch, KV_GROUPS, t_new, HEAD_PAD, HEAD_DIM), F32),
                   jax.ShapeDtypeStruct((batch, 2, KV_GROUPS, HEAD_DIM, buf), F32)],
        compiler_params=_cparams(("arbitrary", "arbitrary")),
        name="nsa_sample_attend",
    )(page_table_flat, idx_flat, cache_t, q_heads, slopes, sel_new, win_t, win_new, win_new_t, o_c, gates)


def nsa_sample_layer(x, gain, w, cmp_t, sel_t, win_t, layer, page_table_flat, *, batch, t_new, past_len):
    m = batch * t_new
    proj, gates = _nsa_project(x, gain, w, tm=m, head_norm=True)
    cmp = compress_pages(cmp_t.reshape(cmp_t.shape[0], cmp_t.shape[1], NSA_KV_DIM, PAGE_SIZE), layer,
                         page_table_flat, w["pe"], w["w1"], w["w2"], w["kgain"], batch=batch,
                         n_pages=past_len // PAGE_SIZE)
    q = proj[:, :NSA_Q_DIM].reshape(batch, t_new, KV_GROUPS, GROUP_SIZE, HEAD_DIM)
    q_stack = jnp.pad(q.transpose(0, 2, 3, 1, 4), ((0, 0), (0, 0), (0, 0), (0, Q_PAD - t_new), (0, 0)))
    o_c, idx = nsa_sample_select(q_stack.reshape(batch, KV_GROUPS, GROUP_SIZE * Q_PAD, HEAD_DIM), cmp,
                                 batch=batch, past_len=past_len, t_new=t_new)
    pad_heads = ((0, 0), (0, 0), (0, 0), (0, HEAD_PAD - GROUP_SIZE), (0, 0))
    o_c = o_c.reshape(batch, KV_GROUPS, GROUP_SIZE, Q_PAD, HEAD_DIM)[:, :, :, :t_new].transpose(0, 1, 3, 2, 4)
    idx_flat = idx.reshape(batch, SEL_TOPN, KV_GROUPS, Q_PAD)[..., :t_new].transpose(0, 2, 3, 1).reshape(-1)
    q_heads = jnp.pad(q.transpose(0, 2, 1, 3, 4), pad_heads)

    def new_rows(cols):
        a = cols.reshape(batch, t_new, 2, KV_GROUPS, HEAD_DIM).transpose(0, 3, 2, 1, 4)
        return jnp.pad(a, ((0, 0), (0, 0), (0, 0), (0, LANES - t_new), (0, 0)))

    kvs = proj[:, NSA_Q_DIM + NSA_KV_DIM:NSA_Q_DIM + 2 * NSA_KV_DIM]
    kvw = proj[:, NSA_Q_DIM + 2 * NSA_KV_DIM:]
    win_new_t = kvw.reshape(batch, t_new, 2, KV_GROUPS, HEAD_DIM).transpose(0, 3, 2, 4, 1)
    win_new_t = jnp.pad(win_new_t, ((0, 0), (0, 0), (0, 0), (0, 0), (LANES - t_new, 0)))
    g_arr = gates[:, :3 * NSA_HEADS].reshape(batch, t_new, 3, KV_GROUPS, GROUP_SIZE).transpose(0, 3, 1, 4, 2)
    g_arr = jnp.pad(g_arr, ((0, 0), (0, 0), (0, 0), (0, HEAD_PAD - GROUP_SIZE), (0, LANES - 3)))
    slopes = np.zeros((KV_GROUPS, HEAD_PAD, LANES), np.float32)
    slopes[:, :GROUP_SIZE, :] = np.asarray(_SLOPES, np.float32).reshape(KV_GROUPS, GROUP_SIZE, 1)
    o, win_out = nsa_sample_attend(sel_t, win_t, layer, page_table_flat, idx_flat, q_heads, jnp.asarray(slopes),
                                   new_rows(kvs), new_rows(kvw), win_new_t, jnp.pad(o_c, pad_heads), g_arr,
                                   batch=batch, past_len=past_len, t_new=t_new)
    o = o[:, :, :, :GROUP_SIZE].transpose(0, 2, 1, 3, 4).reshape(m, NSA_Q_DIM)
    return matmul_residual(o.astype(BF16), w["w_out"], x, tm=m), proj, win_out


def _expand_heads(x, e):
    x1 = x.astype(BF16)
    rest = x - x1.astype(F32)
    x2 = rest.astype(BF16)
    x3 = (rest - x2.astype(F32)).astype(BF16)
    return (_dot(x1, e) + _dot(x2, e)) + _dot(x3, e)


def _ssd_kernel(z_ref, x_ref, bc_ref, dt_ref, cw_ref, cb_ref, dtb_ref, alog_ref, drow_ref, nw_ref, cinit_ref,
                hinit_ref, tril_ref, e_ref, y_ref, hout_ref, stage_ref, h_ref, yacc_ref, *, t_valid):
    L = SSM_CHUNK
    c = pl.program_id(1)

    @pl.when(c == 0)
    def _():
        stage_ref[...] = cinit_ref[...]
        h_ref[...] = hinit_ref[...]

    cur = jnp.concatenate([x_ref[...], bc_ref[...]], axis=1)
    prev = stage_ref[...]
    first = lax.broadcasted_iota(jnp.int32, (8, 1), 0)
    conv = cb_ref[...] + cur * cw_ref[SSM_CONV - 1:SSM_CONV, :]
    for s in range(1, SSM_CONV):
        rolled = pltpu.roll(cur, s, axis=0)
        head = jnp.where(first < s, pltpu.roll(prev, s, axis=0), rolled[0:8])
        conv = conv + jnp.concatenate([head, rolled[8:]], axis=0) * cw_ref[SSM_CONV - 1 - s:SSM_CONV - s, :]
    stage_ref[...] = cur[L - 8:L]
    conv = conv * jax.nn.sigmoid(conv)
    xh = conv[:, :SSM_D_INNER]

    row = lax.broadcasted_iota(jnp.int32, (L, LANES), 0)
    dt_in = dt_ref[...] + dtb_ref[...]
    dt = jnp.maximum(dt_in, 0.0) + jnp.log1p(jnp.exp(-jnp.abs(dt_in)))
    dt = jnp.where(row < t_valid, dt, 0.0)
    da = dt * -jnp.exp(alog_ref[...])
    acum = _dot(tril_ref[...], da, HIGHEST)
    acum_t = jnp.transpose(acum)
    dt_full = _expand_heads(dt, e_ref[...])
    acum_full = _expand_heads(acum, e_ref[...])
    grow_full = jnp.exp(acum_full)
    toend_full = jnp.exp(acum_full[L - 1:L, :] - acum_full)
    last_t = jnp.exp(acum_t[:, L - 1:L])
    xdt = xh * dt_full
    causal = lax.broadcasted_iota(jnp.int32, (L, L), 0) >= lax.broadcasted_iota(jnp.int32, (L, L), 1)

    for g in range(SSM_GROUPS):
        gl, gh = g * SSM_GW, (g + 1) * SSM_GW
        bm = conv[:, SSM_D_INNER + g * SSM_STATE:SSM_D_INNER + (g + 1) * SSM_STATE].astype(BF16)
        cm = conv[:, SSM_D_INNER + (SSM_GROUPS + g) * SSM_STATE:
                  SSM_D_INNER + (SSM_GROUPS + g + 1) * SSM_STATE].astype(BF16)
        cb = _nt(cm, bm)
        xdt_g = xdt[:, gl:gh]
        for r in range(SSM_HPG):
            h = g * SSM_HPG + r
            seg = acum[:, h:h + 1] - acum_t[h:h + 1, :]
            decay = jnp.exp(jnp.where(causal, seg, NEG_INF))
            yacc_ref[:, gl + r * SSM_HEAD_DIM:gl + (r + 1) * SSM_HEAD_DIM] = _dot(
                (cb * decay).astype(BF16), xdt_g[:, r * SSM_HEAD_DIM:(r + 1) * SSM_HEAD_DIM].astype(BF16))
        h_g = h_ref[gl:gh, :]
        y_state = _nt(cm, h_g.astype(BF16)) * grow_full[:, gl:gh]
        yacc_ref[:, gl:gh] = yacc_ref[:, gl:gh] + y_state
        xw = (xdt_g * toend_full[:, gl:gh]).astype(BF16)
        h_add = lax.dot_general(xw, bm, (((0,), (0,)), ((), ())), preferred_element_type=F32)
        for r in range(SSM_HPG):
            h = g * SSM_HPG + r
            rl, rh = r * SSM_HEAD_DIM, (r + 1) * SSM_HEAD_DIM
            keep = jnp.broadcast_to(last_t[h:h + 1, :], (SSM_HEAD_DIM, SSM_STATE))
            h_ref[gl + rl:gl + rh, :] = h_g[rl:rh] * keep + h_add[rl:rh]

    zf = z_ref[...]
    gated = (yacc_ref[...] + xh * drow_ref[...]) * (zf * jax.nn.sigmoid(zf))
    for g in range(SSM_GROUPS):
        gl, gh = g * SSM_GW, (g + 1) * SSM_GW
        y_ref[:, gl:gh] = _rms(gated[:, gl:gh], nw_ref[:, gl:gh]).astype(y_ref.dtype)

    @pl.when(c == pl.num_programs(1) - 1)
    def _():
        hout_ref[...] = h_ref[...]


def ssd_mixer(proj, dt_raw, conv_init, h_init, w, *, batch, n_chunks, t_valid):
    L = SSM_CHUNK
    rows = batch * n_chunks * L
    conv_dim = SSM_D_INNER + SSM_BC
    tril = np.tril(np.ones((L, L), np.float32))
    e = np.zeros((LANES, SSM_D_INNER), np.float32)
    e[np.arange(SSM_D_INNER) // SSM_HEAD_DIM, np.arange(SSM_D_INNER)] = 1.0
    const = lambda shape: pl.BlockSpec(shape, lambda b, c: (0,) * len(shape))
    return pl.pallas_call(
        functools.partial(_ssd_kernel, t_valid=t_valid),
        grid=(batch, n_chunks),
        in_specs=[
            pl.BlockSpec((L, SSM_D_INNER), lambda b, c: (b * n_chunks + c, 0)),
            pl.BlockSpec((L, SSM_D_INNER), lambda b, c: (b * n_chunks + c, 1)),
            pl.BlockSpec((L, SSM_BC), lambda b, c: (b * n_chunks + c, 2 * SSM_D_INNER // SSM_BC)),
            pl.BlockSpec((L, LANES), lambda b, c: (b * n_chunks + c, 0)),
            const((SSM_CONV, conv_dim)), const((1, conv_dim)), const((1, LANES)), const((1, LANES)),
            const((1, SSM_D_INNER)), const((1, SSM_D_INNER)),
            pl.BlockSpec((None, 8, conv_dim), lambda b, c: (b, 0, 0)),
            pl.BlockSpec((None, SSM_D_INNER, SSM_STATE), lambda b, c: (b, 0, 0)),
            const((L, L)), const((LANES, SSM_D_INNER)),
        ],
        out_specs=[
            pl.BlockSpec((L, SSM_D_INNER), lambda b, c: (b * n_chunks + c, 0)),
            pl.BlockSpec((None, SSM_D_INNER, SSM_STATE), lambda b, c: (b, 0, 0)),
        ],
        out_shape=[jax.ShapeDtypeStruct((rows, SSM_D_INNER), BF16),
                   jax.ShapeDtypeStruct((batch, SSM_D_INNER, SSM_STATE), F32)],
        scratch_shapes=[pltpu.VMEM((8, conv_dim), F32), pltpu.VMEM((SSM_D_INNER, SSM_STATE), F32),
                        pltpu.VMEM((L, SSM_D_INNER), F32)],
        compiler_params=_cparams(("parallel", "arbitrary")),
        name="ssd_mixer",
    )(proj, proj, proj, dt_raw, w["conv_w"], w["conv_b"], w["dt_bias"], w["a_log"], w["d_row"], w["norm_w"],
      conv_init, h_init, jnp.asarray(tril), jnp.asarray(e, BF16))


def _ssd_weights(w_in, conv_w, conv_b, dt_bias, a_log, d_skip, norm_w, w_out):
    n_main = 2 * SSM_D_INNER + SSM_BC
    pad = lambda v: jnp.pad(v, (0, LANES - SSM_HEADS)).reshape(1, LANES)
    return dict(
        w_main=w_in[:, :n_main].astype(BF16),
        w_dt=jnp.pad(w_in[:, n_main:], ((0, 0), (0, LANES - SSM_HEADS))).astype(BF16),
        conv_w=conv_w, conv_b=conv_b.reshape(1, -1), dt_bias=pad(dt_bias), a_log=pad(a_log),
        d_row=jnp.repeat(d_skip, SSM_HEAD_DIM).reshape(1, SSM_D_INNER), norm_w=norm_w.reshape(1, SSM_D_INNER),
        w_out=w_out.astype(BF16))


def ssd_layer(x, gain, w, conv_state, h_init, *, batch, t):
    tm = min(1024, x.shape[0])
    proj = norm_matmul(x, gain, w["w_main"], tm=tm, tn=1024)
    dt_raw = norm_matmul(x, gain, w["w_dt"], tm=tm, tn=LANES)
    conv_init = jnp.pad(conv_state, ((0, 0), (8 - (SSM_CONV - 1), 0), (0, 0)))
    if t % SSM_CHUNK == 0:
        y, h_last = ssd_mixer(proj, dt_raw, conv_init, h_init, w, batch=batch, n_chunks=t // SSM_CHUNK,
                              t_valid=SSM_CHUNK)
    else:
        pad_rows = lambda a: jnp.pad(a.reshape(batch, t, -1), ((0, 0), (0, SSM_CHUNK - t), (0, 0))).reshape(
            batch * SSM_CHUNK, -1)
        y, h_last = ssd_mixer(pad_rows(proj), pad_rows(dt_raw), conv_init, h_init, w, batch=batch, n_chunks=1,
                              t_valid=t)
        y = y.reshape(batch, SSM_CHUNK, -1)[:, :t].reshape(batch * t, -1)
    return matmul_residual(y, w["w_out"], x, tm=tm), proj, h_last


def kernel(x_prompt, x_sample, cache_kv_cmp, cache_kv_sel, cache_kv_win, state_conv, state_ssm, page_table,
           norm_mix, norm_ffn, nsa_w_in, nsa_qk_gain, nsa_cmp_pe, nsa_cmp_w1, nsa_cmp_w2, nsa_w_out,
           ssd_w_in, ssd_conv_w, ssd_conv_b, ssd_dt_bias, ssd_a_log, ssd_d, ssd_norm, ssd_w_out,
           ffn_w_gate, ffn_w_up, ffn_w_down):
    bp, tp = x_prompt.shape[0], x_prompt.shape[1]
    bs, ts = x_sample.shape[0], x_sample.shape[1]
    depth = norm_mix.shape[0]
    past_len = page_table.shape[1] * PAGE_SIZE
    conv_dim = SSM_D_INNER + SSM_BC
    assert tp % SSM_CHUNK == 0 and ts < SSM_CHUNK and ts <= Q_PAD and ts >= SSM_CONV - 1
    xp = x_prompt.reshape(bp * tp, D_MODEL)
    xs = x_sample.reshape(bs * ts, D_MODEL)
    pt_flat = page_table.reshape(-1)
    feature_major = lambda a: a.transpose(0, 1, 3, 4, 5, 2)
    cmp_t, sel_t, win_t = feature_major(cache_kv_cmp), feature_major(cache_kv_sel), feature_major(cache_kv_win)
    kv_shape = (2, KV_GROUPS, HEAD_DIM)
    outs = {k: [] for k in ("cmp_p", "sel_p", "win_p", "conv_p", "ssm_p", "cmp_s", "sel_s", "win_s", "conv_s",
                            "ssm_s")}
    for i in range(depth):
        l = i // 2
        if i % 2 == 0:
            w = _nsa_weights(nsa_w_in[l], nsa_qk_gain[l], nsa_cmp_pe[l], nsa_cmp_w1[l], nsa_cmp_w2[l], nsa_w_out[l])
            xp, cmp_rows, sel_rows, win_rows = nsa_prompt_layer(xp, norm_mix[i], w, batch=bp, t=tp)
            row_major = lambda a: a.reshape((bp,) + kv_shape + (a.shape[-1],)).transpose(0, 4, 1, 2, 3)
            outs["cmp_p"].append(row_major(cmp_rows))
            outs["sel_p"].append(row_major(sel_rows))
            outs["win_p"].append(row_major(win_rows))
            kv = lambda a, b, t, j: a[:, NSA_Q_DIM + j * NSA_KV_DIM:NSA_Q_DIM + (j + 1) * NSA_KV_DIM].reshape(
                (b, t) + kv_shape)
            xs, proj_s, win_out = nsa_sample_layer(xs, norm_mix[i], w, cmp_t, sel_t, win_t, l, pt_flat, batch=bs,
                                                   t_new=ts, past_len=past_len)
            outs["cmp_s"].append(kv(proj_s, bs, ts, 0))
            outs["sel_s"].append(kv(proj_s, bs, ts, 1))
            outs["win_s"].append(win_out.transpose(0, 4, 1, 2, 3))
        else:
            w = _ssd_weights(ssd_w_in[l], ssd_conv_w[l], ssd_conv_b[l], ssd_dt_bias[l], ssd_a_log[l], ssd_d[l],
                             ssd_norm[l], ssd_w_out[l])
            xp, proj, h_last = ssd_layer(xp, norm_mix[i], w, jnp.zeros((bp, SSM_CONV - 1, conv_dim), F32),
                                         jnp.zeros((bp, SSM_D_INNER, SSM_STATE), F32), batch=bp, t=tp)
            outs["conv_p"].append(proj.reshape(bp, tp, -1)[:, tp - (SSM_CONV - 1):, SSM_D_INNER:])
            outs["ssm_p"].append(h_last.reshape(bp, SSM_HEADS, SSM_HEAD_DIM, SSM_STATE))
            xs, proj_s, h_last = ssd_layer(xs, norm_mix[i], w, state_conv[l],
                                           state_ssm[l].reshape(bs, SSM_D_INNER, SSM_STATE), batch=bs, t=ts)
            outs["conv_s"].append(proj_s.reshape(bs, ts, -1)[:, ts - (SSM_CONV - 1):, SSM_D_INNER:])
            outs["ssm_s"].append(h_last.reshape(bs, SSM_HEADS, SSM_HEAD_DIM, SSM_STATE))
        wg, wu, wd = ffn_w_gate[i].astype(BF16), ffn_w_up[i].astype(BF16), ffn_w_down[i].astype(BF16)
        xp = ffn(xp, norm_ffn[i], wg, wu, wd, tm=1024, th=1408)
        xs = ffn(xs, norm_ffn[i], wg, wu, wd, tm=bs * ts, th=256)
    order = ("cmp_p", "sel_p", "win_p", "conv_p", "ssm_p", "cmp_s", "sel_s", "win_s", "conv_s", "ssm_s")
    return (xp.reshape(bp, tp, D_MODEL), xs.reshape(bs, ts, D_MODEL)) + tuple(jnp.stack(outs[k]) for k in order)
```

```python
import functools
import math

import jax
import jax.numpy as jnp
import ml_dtypes
import numpy as np
from jax import lax
from jax.experimental import pallas as pl
from jax.experimental.pallas import tpu as pltpu

F32 = jnp.float32
BF16 = jnp.bfloat16
HIGHEST = lax.Precision.HIGHEST

D_MODEL = 1024
PAGE_SIZE = 128
NSA_HEADS = 16
HEAD_DIM = 64
KV_GROUPS = 4
GROUP_SIZE = NSA_HEADS // KV_GROUPS
CMP_BLOCK = 32
CMP_STRIDE = 16
CMP_HIDDEN = 256
SEL_BLOCK = 64
SEL_CHUNKS = SEL_BLOCK // CMP_STRIDE
SEL_TOPN = 16
WINDOW = 512
NSA_Q_DIM = NSA_HEADS * HEAD_DIM
NSA_KV_DIM = 2 * KV_GROUPS * HEAD_DIM
KV_HALF = KV_GROUPS * HEAD_DIM
SSM_D_INNER = 2048
SSM_HEAD_DIM = 64
SSM_HEADS = 32
SSM_GROUPS = 4
SSM_HPG = SSM_HEADS // SSM_GROUPS
SSM_STATE = 128
SSM_CONV = 4
SSM_BC = 2 * SSM_GROUPS * SSM_STATE
SSM_GW = SSM_D_INNER // SSM_GROUPS
SSM_CHUNK = 128
NORM_EPS = 1e-6
NEG_INF = -1e30
FORCE_SCORE = 1e4
LANES = 128
VMEM_LIMIT = 56 * 1024 * 1024

_SLOPES = [float(np.float32((2.0 ** (-8.0 / NSA_HEADS)) ** (i + 1))) for i in range(NSA_HEADS)]


def _cparams(sem):
    return pltpu.CompilerParams(dimension_semantics=sem, vmem_limit_bytes=VMEM_LIMIT)


def _nt(a, b, precision=None):
    return lax.dot_general(a, b, (((1,), (1,)), ((), ())), preferred_element_type=F32, precision=precision)


def _dot(a, b, precision=None):
    return jnp.dot(a, b, preferred_element_type=F32, precision=precision)


def _rms(x, gain):
    return x * lax.rsqrt(jnp.mean(x * x, axis=-1, keepdims=True) + NORM_EPS) * gain


def _norm_matmul_kernel(x_ref, g_ref, w_ref, hg_ref, hm_ref, e_ref, et_ref, o_ref, xn_ref, *, head_norm):
    @pl.when(pl.program_id(1) == 0)
    def _():
        xn_ref[...] = _rms(x_ref[...], g_ref[...]).astype(BF16)

    y = _dot(xn_ref[...], w_ref[...])
    if head_norm:
        ss = _dot(y * y, e_ref[...], HIGHEST)
        inv = lax.rsqrt(ss * (1.0 / HEAD_DIM) + NORM_EPS)
        yn = y * _dot(inv, et_ref[...], HIGHEST) * hg_ref[...]
        y = jnp.where(hm_ref[...] > 0.5, yn, y)
    o_ref[...] = y


def norm_matmul(x, gain, w, head_gain=None, head_mask=None, *, tm, tn):
    m, k = x.shape
    n = w.shape[1]
    head_norm = head_gain is not None
    if not head_norm:
        head_gain = jnp.zeros((1, n), F32)
        head_mask = jnp.zeros((1, n), F32)
    lane = np.arange(tn)
    e = np.zeros((tn, LANES), np.float32)
    e[lane, lane // HEAD_DIM] = 1.0
    return pl.pallas_call(
        functools.partial(_norm_matmul_kernel, head_norm=head_norm),
        grid=(m // tm, n // tn),
        in_specs=[
            pl.BlockSpec((tm, k), lambda i, j: (i, 0)),
            pl.BlockSpec((1, k), lambda i, j: (0, 0)),
            pl.BlockSpec((k, tn), lambda i, j: (0, j)),
            pl.BlockSpec((1, tn), lambda i, j: (0, j)),
            pl.BlockSpec((1, tn), lambda i, j: (0, j)),
            pl.BlockSpec((tn, LANES), lambda i, j: (0, 0)),
            pl.BlockSpec((LANES, tn), lambda i, j: (0, 0)),
        ],
        out_specs=pl.BlockSpec((tm, tn), lambda i, j: (i, j)),
        out_shape=jax.ShapeDtypeStruct((m, n), F32),
        scratch_shapes=[pltpu.VMEM((tm, k), BF16)],
        compiler_params=_cparams(("parallel", "arbitrary")),
        name="norm_matmul",
    )(x, gain.reshape(1, k), w, head_gain, head_mask, jnp.asarray(e), jnp.asarray(e.T))


def _ffn_kernel(x_ref, g_ref, wg_ref, wu_ref, wd_ref, o_ref, xn_ref):
    @pl.when(pl.program_id(1) == 0)
    def _():
        x = x_ref[...]
        xn_ref[...] = _rms(x, g_ref[...]).astype(BF16)
        o_ref[...] = x

    h = xn_ref[...]
    gate = _dot(h, wg_ref[...])
    up = _dot(h, wu_ref[...])
    act = (gate * jax.nn.sigmoid(gate) * up).astype(BF16)
    o_ref[...] += _dot(act, wd_ref[...])


def ffn(x, gain, wg, wu, wd, *, tm, th):
    m, k = x.shape
    hid = wg.shape[1]
    return pl.pallas_call(
        _ffn_kernel,
        grid=(m // tm, hid // th),
        in_specs=[
            pl.BlockSpec((tm, k), lambda i, j: (i, 0)),
            pl.BlockSpec((1, k), lambda i, j: (0, 0)),
            pl.BlockSpec((k, th), lambda i, j: (0, j)),
            pl.BlockSpec((k, th), lambda i, j: (0, j)),
            pl.BlockSpec((th, k), lambda i, j: (j, 0)),
        ],
        out_specs=pl.BlockSpec((tm, k), lambda i, j: (i, 0)),
        out_shape=jax.ShapeDtypeStruct((m, k), F32),
        scratch_shapes=[pltpu.VMEM((tm, k), BF16)],
        compiler_params=_cparams(("parallel", "arbitrary")),
        name="ffn",
    )(x, gain.reshape(1, k), wg, wu, wd)


def _matmul_res_kernel(a_ref, w_ref, r_ref, o_ref):
    o_ref[...] = r_ref[...] + _dot(a_ref[...], w_ref[...])


def matmul_residual(a, w, res, *, tm):
    m, k = a.shape
    n = w.shape[1]
    return pl.pallas_call(
        _matmul_res_kernel,
        grid=(m // tm,),
        in_specs=[
            pl.BlockSpec((tm, k), lambda i: (i, 0)),
            pl.BlockSpec((k, n), lambda i: (0, 0)),
            pl.BlockSpec((tm, n), lambda i: (i, 0)),
        ],
        out_specs=pl.BlockSpec((tm, n), lambda i: (i, 0)),
        out_shape=jax.ShapeDtypeStruct((m, n), F32),
        compiler_params=_cparams(("parallel",)),
        name="matmul_residual",
    )(a, w, res)


CMP_PACK = 4
CMP_PITCH = 24


def _compress_accumulate(xs_ref, n_rows, pe_ref, w1_ref, kv, pitch=CMP_PITCH):
    n_pair = NSA_KV_DIM // LANES
    low = lax.broadcasted_iota(jnp.int32, (n_rows, LANES), 1) < HEAD_DIM
    acc = [None, None]
    for jp in range(CMP_STRIDE // CMP_PACK):
        groups = []
        for c in range(kv * n_pair // 2, (kv + 1) * n_pair // 2):
            rows = [xs_ref.at[c][pl.ds(jp * CMP_PACK + i, n_rows, stride=pitch), :] for i in range(CMP_PACK)]
            swapped = [pltpu.roll(r, HEAD_DIM, axis=1) for r in rows]
            groups.append(jnp.concatenate([jnp.where(low, rows[i], swapped[i + 1])
                                           for i in range(0, CMP_PACK, 2)], axis=1))
            groups.append(jnp.concatenate([jnp.where(low, swapped[i], rows[i + 1])
                                           for i in range(0, CMP_PACK, 2)], axis=1))
        xs = jnp.concatenate(groups, axis=0)
        for m in range(2):
            lhs = (xs + pe_ref[kv, m, jp:jp + 1, :]).astype(BF16)
            part = _dot(lhs, w1_ref[kv, m, jp])
            acc[m] = part if acc[m] is None else acc[m] + part
    return acc


def _compress_finish(acc, w2_ref, kg_ref, kv):
    rows = acc[0].shape[0]
    pre = acc[0] + pltpu.roll(acc[1], rows - 1, axis=0)
    hid = (pre * jax.nn.sigmoid(pre)).astype(BF16)
    out = _dot(hid, w2_ref[kv])
    if kv == 0:
        out = _rms(out, kg_ref[...])
    return out


def _compress_rows_kernel(x_ref, pe_ref, w1_ref, w2_ref, kg_ref, o_ref, xt_ref, xs_ref, *, n_chunk):
    xt_ref[...] = jnp.transpose(x_ref[...])
    n_pair = NSA_KV_DIM // LANES
    def stage_chunk(k, carry):
        src = pl.ds(pl.multiple_of(k * CMP_STRIDE, CMP_STRIDE), CMP_STRIDE)
        dst = pl.ds(pl.multiple_of(k * CMP_PITCH, 8), CMP_STRIDE)
        for c in range(n_pair):
            xs_ref[c, dst, :] = x_ref[src, c * LANES:(c + 1) * LANES]
        return carry

    lax.fori_loop(0, n_chunk, stage_chunk, 0)
    for kv in range(2):
        o_ref[kv] = _compress_finish(_compress_accumulate(xs_ref, n_chunk, pe_ref, w1_ref, kv), w2_ref, kg_ref, kv)


def compress_rows(proj, col_block, pe, w1, w2, kgain, *, batch, t):
    n_chunk = t // CMP_STRIDE
    return pl.pallas_call(
        functools.partial(_compress_rows_kernel, n_chunk=n_chunk),
        grid=(batch,),
        in_specs=[
            pl.BlockSpec((t, NSA_KV_DIM), lambda b: (b, col_block)),
            pl.BlockSpec(pe.shape, lambda b: (0, 0, 0, 0)),
            pl.BlockSpec(w1.shape, lambda b: (0, 0, 0, 0, 0)),
            pl.BlockSpec(w2.shape, lambda b: (0, 0, 0)),
            pl.BlockSpec((1, HEAD_DIM), lambda b: (0, 0)),
        ],
        out_specs=[pl.BlockSpec((None, 2, KV_GROUPS * n_chunk, HEAD_DIM), lambda b: (b, 0, 0, 0)),
                   pl.BlockSpec((None, NSA_KV_DIM, t), lambda b: (b, 0, 0))],
        out_shape=[jax.ShapeDtypeStruct((batch, 2, KV_GROUPS * n_chunk, HEAD_DIM), F32),
                   jax.ShapeDtypeStruct((batch, NSA_KV_DIM, t), F32)],
        scratch_shapes=[pltpu.VMEM((NSA_KV_DIM // LANES, n_chunk * CMP_PITCH, LANES), F32)],
        compiler_params=_cparams(("parallel",)),
        name="compress_rows",
    )(proj, pe, w1, w2, kgain.reshape(1, HEAD_DIM))


def _slope_col(g, rows, tq):
    rblk = lax.broadcasted_iota(jnp.int32, (rows, 1), 0) // tq
    col = jnp.full((rows, 1), _SLOPES[GROUP_SIZE * g + GROUP_SIZE - 1], F32)
    for r in range(GROUP_SIZE - 1):
        col = jnp.where(rblk == r, _SLOPES[GROUP_SIZE * g + r], col)
    return col


def _normalize(acc, l):
    return acc * jnp.where(l > 0.0, 1.0 / jnp.where(l > 0.0, l, 1.0), 0.0)


MASK_BIG = 2.0 ** 100
LOG2E = 1.4426950408889634
AUG = 64
AUG_POS = 32


def _bf16_terms(x, n=3):
    x = np.asarray(x, np.float64)
    terms = []
    for _ in range(n):
        term = x.astype(ml_dtypes.bfloat16).astype(np.float64)
        terms.append(term.astype(np.float32))
        x = x - term
    return terms


def _key_aug(pos, flags=None):
    pos = np.asarray(pos, np.int64)
    aug = np.zeros((pos.shape[0], AUG), np.float32)
    if flags is not None:
        aug[:, :flags.shape[1]] = flags
    coarse = (pos // SEL_BLOCK) * SEL_BLOCK
    aug[:, AUG_POS:AUG_POS + 3] = coarse[:, None]
    aug[:, AUG_POS + 3:AUG_POS + 6] = (pos - coarse)[:, None]
    return aug


def _query_aug_rows(tq):
    rows = np.zeros((KV_GROUPS, AUG - AUG_POS, GROUP_SIZE * tq), np.float32)
    for h in range(NSA_HEADS):
        terms = _bf16_terms(np.float64(_SLOPES[h]) * LOG2E)
        g, r = divmod(h, GROUP_SIZE)
        for i, term in enumerate(terms):
            rows[g, i, r * tq:(r + 1) * tq] = term
            rows[g, 3 + i, r * tq:(r + 1) * tq] = term
    return rows


def _online_update(s, v_t, m, l, acc):
    m_new = jnp.maximum(m, jnp.max(s, axis=0, keepdims=True))
    alpha = jnp.exp2(m - m_new)
    p = jnp.exp2(s - m_new)
    return m_new, alpha * l + jnp.sum(p, axis=0, keepdims=True), alpha * acc + _dot(v_t, p.astype(BF16))


def _nsa_prompt_kernel(q_ref, ks_ref, kw_ref, cmp_ref, gate_ref, gq_ref, gks_ref, gkw_ref, gksc_ref, gkwc_ref,
                       csel_ref, cwin_ref, ccmp_ref, qrows_ref, mmap_ref, eye_ref,
                       o_ref, kst_out_ref, kwt_out_ref,
                       ksa_ref, kwa_ref, vst_ref, vwt_ref, kca_ref, vct_ref, ot_ref, ss_ref, ps_ref, sacc_ref,
                       *, t, tq):
    qi = pl.program_id(1)
    n_sel = t // SEL_BLOCK
    n_cmp = t // CMP_STRIDE
    top_n = min(SEL_TOPN, n_sel)
    cols = GROUP_SIZE * tq
    nt = t // tq
    n_win = WINDOW // tq + 1
    n_pad = WINDOW // tq

    @pl.when(qi == 0)
    def _():
        def prepare(src_ref, grow_ref, gcol_ref, const_ref, ka_ref, vt_ref, out_ref, pad_tiles):
            pad = pad_tiles * tq
            x = src_ref[...]
            xt = jnp.transpose(x)
            width = out_ref.shape[1]
            out_ref[KV_HALF:, :] = xt[KV_HALF:, t - width:]
            for g in range(KV_GROUPS):
                lo, hi = g * HEAD_DIM, (g + 1) * HEAD_DIM
                kn = _rms(x[:, lo:hi], grow_ref[...])
                ka_ref[g, pad:pad + t, :] = jnp.concatenate([kn, const_ref[pad:pad + t, :]], axis=1).astype(BF16)
                if pad:
                    ka_ref[g, 0:pad, :] = jnp.concatenate(
                        [jnp.zeros((pad, HEAD_DIM), F32), const_ref[0:pad, :]], axis=1).astype(BF16)
                kt = xt[lo:hi, :]
                ktn = kt * lax.rsqrt(jnp.mean(kt * kt, axis=0, keepdims=True) + NORM_EPS) * gcol_ref[...]
                out_ref[lo:hi, :] = ktn[:, t - width:]
                vt = xt[KV_HALF + lo:KV_HALF + hi, :].astype(BF16)
                base = g * (nt + pad_tiles)
                for i in range(pad_tiles):
                    vt_ref[base + i] = jnp.zeros((HEAD_DIM, tq), BF16)
                for i in range(nt):
                    vt_ref[base + pad_tiles + i] = vt[:, i * tq:(i + 1) * tq]

        prepare(ks_ref, gks_ref, gksc_ref, csel_ref, ksa_ref, vst_ref, kst_out_ref, 0)
        prepare(kw_ref, gkw_ref, gkwc_ref, cwin_ref, kwa_ref, vwt_ref, kwt_out_ref, n_pad)
        for g in range(KV_GROUPS):
            kc = cmp_ref[0, g * n_cmp:(g + 1) * n_cmp, :]
            kca_ref[g] = jnp.concatenate([kc, ccmp_ref[...]], axis=1).astype(BF16)
            vct_ref[g] = _nt(eye_ref[...], cmp_ref[1, g * n_cmp:(g + 1) * n_cmp, :].astype(BF16)).astype(BF16)

    q0 = pl.multiple_of(qi * tq, tq)
    lane4 = lax.broadcasted_iota(jnp.int32, (1, cols), 1) % tq
    key_off = lax.broadcasted_iota(jnp.int32, (tq, cols), 0)
    causal = key_off <= lane4
    win_old = key_off > lane4
    cmp_ok = (lax.broadcasted_iota(jnp.int32, (n_cmp, cols), 0) * CMP_STRIDE + (CMP_BLOCK - 1)) <= q0 + lane4
    qt = jnp.transpose(q_ref[...])
    gt = jax.nn.sigmoid(jnp.transpose(gate_ref[...]))
    jblk = lax.broadcasted_iota(jnp.int32, (n_sel, tq), 0)
    t_lane = q0 + lax.broadcasted_iota(jnp.int32, (n_sel, tq), 1)
    cur = t_lane // SEL_BLOCK
    forced = (jblk == 0) | (jblk == cur) | (jblk == cur - 1)
    visible = jblk * SEL_BLOCK <= t_lane

    o_c, q_sel, q_win = [], [], []
    for g in range(KV_GROUPS):
        heads = []
        for r in range(GROUP_SIZE):
            h = GROUP_SIZE * g + r
            qh = qt[h * HEAD_DIM:(h + 1) * HEAD_DIM, :]
            qh = qh * lax.rsqrt(jnp.mean(qh * qh, axis=0, keepdims=True) + NORM_EPS)
            heads.append(qh * (gq_ref[...] * (HEAD_DIM ** -0.5 * LOG2E)))
        q_top = jnp.concatenate(heads, axis=1)
        q_low = qrows_ref[g]

        def with_mask_rows(rows_, q_top=q_top, q_low=q_low):
            return jnp.concatenate([q_top, rows_, q_low], axis=0).astype(BF16)

        s = _dot(kca_ref[g], with_mask_rows(jnp.zeros((AUG_POS, cols), F32)))
        s = jnp.where(cmp_ok, s, -MASK_BIG)
        p = jnp.where(cmp_ok, jnp.exp2(s - jnp.max(s, axis=0, keepdims=True)), 0.0)
        pc = _normalize(p, jnp.sum(p, axis=0, keepdims=True))
        o_c.append(_dot(vct_ref[g], pc.astype(BF16)))
        imp = pc[:, 0:tq]
        for r in range(1, GROUP_SIZE):
            imp = imp + pc[:, r * tq:(r + 1) * tq]

        p_slc = _dot(mmap_ref[...], imp, HIGHEST)
        score = jnp.where(forced, FORCE_SCORE, jnp.where(visible, p_slc, -1.0))
        cnt = jnp.zeros((n_sel, tq), jnp.int32)
        for k in range(n_sel):
            sk = score[k:k + 1, :]
            beats = (sk > score) | ((sk == score) & (jblk > k))
            cnt = cnt + beats.astype(jnp.int32)
        drop = jnp.where(cnt < top_n, 0.0, -MASK_BIG)
        if n_sel < AUG_POS:
            drop = jnp.concatenate([drop, jnp.zeros((AUG_POS - n_sel, tq), F32)], axis=0)
        q_sel.append(with_mask_rows(jnp.concatenate([drop] * GROUP_SIZE, axis=1)))
        q_win.append(with_mask_rows(jnp.full((AUG_POS, cols), -MASK_BIG, F32)))

    def sel_scores(tile, slot):
        k0 = pl.multiple_of(tile * tq, tq)
        for g in range(KV_GROUPS):
            ss_ref[slot, g] = _dot(ksa_ref[g, pl.ds(k0, tq), :], q_sel[g])

    def sel_values(tile):
        return [_dot(vst_ref[g * nt + tile], ps_ref[g]) for g in range(KV_GROUPS)]

    def sel_softmax(slot, stats, pv, mask=None):
        new_stats = []
        for g in range(KV_GROUPS):
            m, l = stats[g]
            s = ss_ref[slot, g]
            if mask is not None:
                s = jnp.where(mask, s, -MASK_BIG)
            m_new = jnp.maximum(m, jnp.max(s, axis=0, keepdims=True))
            alpha = jnp.exp2(m - m_new)
            p = jnp.exp2(s - m_new)
            sacc_ref[g] = alpha * (sacc_ref[g] + pv[g])
            ps_ref[g] = p.astype(BF16)
            new_stats.append((m_new, alpha * l + jnp.sum(p, axis=0, keepdims=True)))
        return tuple(new_stats)

    def sel_step(slot, next_tile, prev_tile, stats, mask=None):
        pv = sel_values(jnp.maximum(prev_tile, 0))
        if next_tile is not None:
            sel_scores(next_tile, 1 - slot)
        return sel_softmax(slot, stats, pv, mask)

    def sel_finish(stats):
        pv = sel_values(qi)
        for g in range(KV_GROUPS):
            sacc_ref[g] = _normalize(sacc_ref[g] + pv[g], stats[g][1])

    for g in range(KV_GROUPS):
        ps_ref[g] = jnp.zeros((tq, cols), BF16)
        sacc_ref[g] = jnp.zeros((HEAD_DIM, cols), F32)
    sel_scores(0, 0)

    def sel_pair(j, stats):
        stats = sel_step(0, 2 * j + 1, 2 * j - 1, stats)
        return sel_step(1, 2 * j + 2, 2 * j, stats)

    stats0 = ((jnp.full((1, cols), -MASK_BIG, F32), jnp.zeros((1, cols), F32)),) * KV_GROUPS
    stats = lax.fori_loop(0, qi // 2, sel_pair, stats0)

    @pl.when(qi % 2 == 1)
    def _():
        sel_finish(sel_step(1, None, qi - 1, sel_step(0, qi, qi - 2, stats), mask=causal))

    @pl.when(qi % 2 == 0)
    def _():
        sel_finish(sel_step(0, None, qi - 1, stats, mask=causal))

    o_s = [sacc_ref[g] for g in range(KV_GROUPS)]

    win_scores = [[_dot(kwa_ref[g, pl.ds(q0 + i * tq, tq), :], q_win[g]) for i in range(n_win)]
                  for g in range(KV_GROUPS)]
    o_w = []
    for g in range(KV_GROUPS):
        tiles = win_scores[g]
        tiles[0] = jnp.where(win_old, tiles[0], -MASK_BIG)
        tiles[-1] = jnp.where(causal, tiles[-1], -MASK_BIG)
        m = functools.reduce(jnp.maximum, [jnp.max(s, axis=0, keepdims=True) for s in tiles])
        probs = [jnp.exp2(s - m) for s in tiles]
        l = functools.reduce(jnp.add, [jnp.sum(p, axis=0, keepdims=True) for p in probs])
        acc = functools.reduce(jnp.add, [_dot(vwt_ref[g * (nt + n_pad) + qi + i], p.astype(BF16))
                                         for i, p in enumerate(probs)])
        o_w.append(_normalize(acc, l))

    for h in range(NSA_HEADS):
        g, r = divmod(h, GROUP_SIZE)
        sl = slice(r * tq, (r + 1) * tq)
        ot_ref[h * HEAD_DIM:(h + 1) * HEAD_DIM, :] = (
            gt[h:h + 1, :] * o_c[g][:, sl] + gt[NSA_HEADS + h:NSA_HEADS + h + 1, :] * o_s[g][:, sl]
            + gt[2 * NSA_HEADS + h:2 * NSA_HEADS + h + 1, :] * o_w[g][:, sl])
    o_ref[...] = jnp.transpose(ot_ref[...]).astype(o_ref.dtype)


def nsa_prompt_attention(proj, gates, cmp, qk_gain, *, batch, t, tq):
    n_sel = t // SEL_BLOCK
    n_cmp = t // CMP_STRIDE
    nq = t // tq
    nt = t // tq
    n_pad = WINDOW // tq
    keep = min(WINDOW, t)
    assert n_sel <= AUG_POS and t % tq == 0 and WINDOW % tq == 0 and tq % SEL_BLOCK == 0
    n = np.arange(n_cmp)
    mmap = np.zeros((n_sel, n_cmp), np.float32)
    real = n < n_cmp - 1
    np.add.at(mmap, (n[real] // SEL_CHUNKS, n[real]), 1.0)
    nxt = real & ((n + 1) // SEL_CHUNKS < n_sel)
    np.add.at(mmap, ((n[nxt] + 1) // SEL_CHUNKS, n[nxt]), 1.0)
    pos = np.arange(t)
    csel = _key_aug(pos, (pos[:, None] // SEL_BLOCK == np.arange(n_sel)[None, :]).astype(np.float32))
    wpos = np.arange(-WINDOW, t)
    cwin = _key_aug(wpos, (wpos < 0).astype(np.float32)[:, None])
    ccmp = _key_aug(np.arange(n_cmp) * CMP_STRIDE + (CMP_BLOCK - 1))
    const = lambda shape: pl.BlockSpec(shape, lambda b, i: (0,) * len(shape))
    col = lambda v: v.reshape(HEAD_DIM, 1)
    row = lambda v: v.reshape(1, HEAD_DIM)
    return pl.pallas_call(
        functools.partial(_nsa_prompt_kernel, t=t, tq=tq),
        grid=(batch, nq),
        in_specs=[
            pl.BlockSpec((tq, NSA_Q_DIM), lambda b, i: (b * nq + i, 0)),
            pl.BlockSpec((t, NSA_KV_DIM), lambda b, i: (b, 3)),
            pl.BlockSpec((t, NSA_KV_DIM), lambda b, i: (b, 4)),
            pl.BlockSpec((None, 2, KV_GROUPS * n_cmp, HEAD_DIM), lambda b, i: (b, 0, 0, 0)),
            pl.BlockSpec((tq, LANES), lambda b, i: (b * nq + i, 0)),
            const((HEAD_DIM, 1)), const((1, HEAD_DIM)), const((1, HEAD_DIM)), const((HEAD_DIM, 1)),
            const((HEAD_DIM, 1)),
            const((t, AUG)), const((t + WINDOW, AUG)), const((n_cmp, AUG)),
            const((KV_GROUPS, AUG - AUG_POS, GROUP_SIZE * tq)), const((n_sel, n_cmp)), const((HEAD_DIM, HEAD_DIM)),
        ],
        out_specs=[
            pl.BlockSpec((tq, NSA_Q_DIM), lambda b, i: (b * nq + i, 0)),
            pl.BlockSpec((None, NSA_KV_DIM, t), lambda b, i: (b, 0, 0)),
            pl.BlockSpec((None, NSA_KV_DIM, keep), lambda b, i: (b, 0, 0)),
        ],
        out_shape=[jax.ShapeDtypeStruct((batch * t, NSA_Q_DIM), BF16),
                   jax.ShapeDtypeStruct((batch, NSA_KV_DIM, t), F32),
                   jax.ShapeDtypeStruct((batch, NSA_KV_DIM, keep), F32)],
        scratch_shapes=[
            pltpu.VMEM((KV_GROUPS, t, HEAD_DIM + AUG), BF16),
            pltpu.VMEM((KV_GROUPS, t + WINDOW, HEAD_DIM + AUG), BF16),
            pltpu.VMEM((KV_GROUPS * nt, HEAD_DIM, tq), BF16),
            pltpu.VMEM((KV_GROUPS * (nt + n_pad), HEAD_DIM, tq), BF16),
            pltpu.VMEM((KV_GROUPS, n_cmp, HEAD_DIM + AUG), BF16),
            pltpu.VMEM((KV_GROUPS, HEAD_DIM, n_cmp), BF16),
            pltpu.VMEM((NSA_Q_DIM, tq), F32),
            pltpu.VMEM((2, KV_GROUPS, tq, GROUP_SIZE * tq), F32),
            pltpu.VMEM((KV_GROUPS, tq, GROUP_SIZE * tq), BF16),
            pltpu.VMEM((KV_GROUPS, HEAD_DIM, GROUP_SIZE * tq), F32),
        ],
        compiler_params=_cparams(("parallel", "arbitrary")),
        name="nsa_prompt_attention",
    )(proj, proj, proj, cmp, gates, col(qk_gain[0]), row(qk_gain[2]), row(qk_gain[3]), col(qk_gain[2]),
      col(qk_gain[3]), jnp.asarray(csel), jnp.asarray(cwin), jnp.asarray(ccmp), jnp.asarray(_query_aug_rows(tq)),
      jnp.asarray(mmap), jnp.eye(HEAD_DIM, dtype=BF16))


def _nsa_weights(w_in, qk_gain, cmp_pe, cmp_w1, cmp_w2, w_out):
    n_main = NSA_Q_DIM + 3 * NSA_KV_DIM
    zeros_kv = jnp.zeros((KV_HALF,), F32)
    ones_kv = jnp.ones((KV_HALF,), F32)
    head_gain = jnp.concatenate([
        jnp.tile(qk_gain[0], NSA_HEADS), zeros_kv, zeros_kv,
        jnp.tile(qk_gain[2], KV_GROUPS), zeros_kv, jnp.tile(qk_gain[3], KV_GROUPS), zeros_kv]).reshape(1, n_main)
    head_mask = jnp.concatenate([
        jnp.ones((NSA_Q_DIM,), F32), zeros_kv, zeros_kv, ones_kv, zeros_kv, ones_kv, zeros_kv]).reshape(1, n_main)
    w_gate = jnp.pad(w_in[:, n_main:], ((0, 0), (0, LANES - 3 * NSA_HEADS)))
    return dict(
        w_main=w_in[:, :n_main].astype(BF16), w_gate=w_gate.astype(BF16), head_gain=head_gain, head_mask=head_mask,
        pe=cmp_pe.reshape(2, 2, CMP_STRIDE // CMP_PACK, CMP_PACK * HEAD_DIM),
        w1=cmp_w1.reshape(2, 2, CMP_STRIDE // CMP_PACK, CMP_PACK * HEAD_DIM, CMP_HIDDEN).astype(BF16),
        w2=cmp_w2.astype(BF16), kgain=qk_gain[1], qk_gain=qk_gain, w_out=w_out.astype(BF16))


def _nsa_project(x, gain, w, *, tm, head_norm):
    if head_norm:
        proj = norm_matmul(x, gain, w["w_main"], w["head_gain"], w["head_mask"], tm=tm, tn=512)
    else:
        proj = norm_matmul(x, gain, w["w_main"], tm=tm, tn=1280)
    return proj, norm_matmul(x, gain, w["w_gate"], tm=tm, tn=LANES)


def nsa_prompt_layer(x, gain, w, *, batch, t):
    tm = min(1024, batch * t)
    proj, gates = _nsa_project(x, gain, w, tm=min(2048, batch * t), head_norm=False)
    cmp, cmp_t = compress_rows(proj, 2, w["pe"], w["w1"], w["w2"], w["kgain"], batch=batch, t=t)
    o, sel_t, win_t = nsa_prompt_attention(proj, gates, cmp, w["qk_gain"], batch=batch, t=t, tq=128)
    return matmul_residual(o, w["w_out"], x, tm=tm), cmp_t, sel_t, win_t


PAGES_PER_STEP = 8
CHUNKS_PER_PAGE = PAGE_SIZE // CMP_STRIDE
PAGE_PITCH = CMP_PITCH


def _compress_pages_kernel(pt_ref, *refs, n_chunk):
    del pt_ref
    pages = refs[:PAGES_PER_STEP]
    pe_ref, w1_ref, w2_ref, kg_ref, o_ref, xs_ref, pre_ref = refs[PAGES_PER_STEP:]
    s = pl.program_id(1)
    n_pair = NSA_KV_DIM // LANES
    step_chunks = PAGES_PER_STEP * CHUNKS_PER_PAGE
    for i, page in enumerate(pages):
        for c in range(n_pair):
            rows = jnp.transpose(page[c * LANES:(c + 1) * LANES, :])
            for k in range(CHUNKS_PER_PAGE):
                at = (i * CHUNKS_PER_PAGE + k) * PAGE_PITCH
                xs_ref[c, at:at + CMP_STRIDE, :] = rows[k * CMP_STRIDE:(k + 1) * CMP_STRIDE]
    for kv in range(2):
        acc = _compress_accumulate(xs_ref, step_chunks, pe_ref, w1_ref, kv, PAGE_PITCH)
        for m in range(2):
            for g in range(KV_GROUPS):
                pre_ref[kv, m, pl.ds(pl.multiple_of(g * n_chunk + s * step_chunks, step_chunks), step_chunks), :] = (
                    acc[m][g * step_chunks:(g + 1) * step_chunks])

    @pl.when(s == pl.num_programs(1) - 1)
    def _():
        for kv in range(2):
            o_ref[kv] = _compress_finish([pre_ref[kv, 0], pre_ref[kv, 1]], w2_ref, kg_ref, kv)


def compress_pages(cache_t, layer, page_table_flat, pe, w1, w2, kgain, *, batch, n_pages):
    n_chunk = n_pages * CHUNKS_PER_PAGE
    n_steps = n_pages // PAGES_PER_STEP

    def page_spec(i):
        return pl.BlockSpec((None, None, NSA_KV_DIM, PAGE_SIZE),
                            lambda b, s, pt: (layer, pt[b * n_pages + s * PAGES_PER_STEP + i], 0, 0))

    grid_spec = pltpu.PrefetchScalarGridSpec(
        num_scalar_prefetch=1,
        grid=(batch, n_steps),
        in_specs=[page_spec(i) for i in range(PAGES_PER_STEP)] + [
            pl.BlockSpec(pe.shape, lambda b, s, pt: (0, 0, 0, 0)),
            pl.BlockSpec(w1.shape, lambda b, s, pt: (0, 0, 0, 0, 0)),
            pl.BlockSpec(w2.shape, lambda b, s, pt: (0, 0, 0)),
            pl.BlockSpec((1, HEAD_DIM), lambda b, s, pt: (0, 0)),
        ],
        out_specs=pl.BlockSpec((None, 2, KV_GROUPS * n_chunk, HEAD_DIM), lambda b, s, pt: (b, 0, 0, 0)),
        scratch_shapes=[pltpu.VMEM((NSA_KV_DIM // LANES, PAGES_PER_STEP * CHUNKS_PER_PAGE * PAGE_PITCH, LANES), F32),
                        pltpu.VMEM((2, 2, KV_GROUPS * n_chunk, CMP_HIDDEN), F32)],
    )
    return pl.pallas_call(
        functools.partial(_compress_pages_kernel, n_chunk=n_chunk),
        grid_spec=grid_spec,
        out_shape=jax.ShapeDtypeStruct((batch, 2, KV_GROUPS * n_chunk, HEAD_DIM), F32),
        compiler_params=_cparams(("parallel", "arbitrary")),
        name="compress_pages",
    )(page_table_flat, *([cache_t] * PAGES_PER_STEP), pe, w1, w2, kgain.reshape(1, HEAD_DIM))


Q_PAD = 8
HEAD_PAD = 16


def _nsa_sample_select_kernel(q_ref, cmp_ref, mmap_ref, oc_ref, idx_ref, score_ref, *, past_len, n_cmp, n_sel):
    rows = GROUP_SIZE * Q_PAD
    n_selp = score_ref.shape[0]
    t_max = past_len + Q_PAD - 1
    row = lax.broadcasted_iota(jnp.int32, (rows, 1), 0)
    t_col = past_len + row % Q_PAD
    jblk = lax.broadcasted_iota(jnp.int32, (n_selp, Q_PAD), 0)
    t_lane = past_len + lax.broadcasted_iota(jnp.int32, (n_selp, Q_PAD), 1)
    cur = t_lane // SEL_BLOCK
    forced = (jblk == 0) | (jblk == cur) | (jblk == cur - 1)
    visible = jblk * SEL_BLOCK <= t_lane
    for g in range(KV_GROUPS):
        qg = (q_ref[g] * (HEAD_DIM ** -0.5)).astype(BF16)
        slope = _slope_col(g, rows, Q_PAD)
        kc = cmp_ref[0, g * n_cmp:(g + 1) * n_cmp, :].astype(BF16)
        vc = cmp_ref[1, g * n_cmp:(g + 1) * n_cmp, :].astype(BF16)
        kpos = lax.broadcasted_iota(jnp.int32, (1, n_cmp), 1) * CMP_STRIDE + (CMP_BLOCK - 1)
        s = _nt(qg, kc) + slope * (kpos - t_max).astype(F32)
        mask = kpos <= t_col
        s = jnp.where(mask, s, NEG_INF)
        p = jnp.where(mask, jnp.exp(s - jnp.max(s, axis=1, keepdims=True)), 0.0)
        pc = _normalize(p, jnp.sum(p, axis=1, keepdims=True))
        oc_ref[g] = _dot(pc.astype(BF16), vc)
        imp = pc[0:Q_PAD]
        for r in range(1, GROUP_SIZE):
            imp = imp + pc[r * Q_PAD:(r + 1) * Q_PAD]
        imp_hi = imp.astype(BF16)
        imp_rest = imp - imp_hi.astype(F32)
        imp_mid = imp_rest.astype(BF16)
        imp_lo = (imp_rest - imp_mid.astype(F32)).astype(BF16)
        p_slc = (_nt(mmap_ref[...], imp_hi) + _nt(mmap_ref[...], imp_mid)) + _nt(mmap_ref[...], imp_lo)
        score = jnp.where(forced, FORCE_SCORE, jnp.where(visible, p_slc, -1.0))
        score_ref[:, g * Q_PAD:(g + 1) * Q_PAD] = jnp.where(jblk < n_sel, score, -2.0)

    score = score_ref[...]
    jall = lax.broadcasted_iota(jnp.int32, score.shape, 0)

    def rank_body(k, cnt):
        sk = score_ref[pl.ds(k, 1), :]
        beats = (sk > score) | ((sk == score) & (jall > k))
        return cnt + beats.astype(jnp.int32)

    cnt = lax.fori_loop(0, n_sel, rank_body, jnp.zeros(score.shape, jnp.int32), unroll=4)
    slots = [jnp.sum(jnp.where(cnt == sl, jall, 0), axis=0, keepdims=True) for sl in range(SEL_TOPN)]
    idx_ref[...] = jnp.concatenate(slots, axis=0)


def nsa_sample_select(q_stack, cmp, *, batch, past_len, t_new):
    n_cmp = past_len // CMP_STRIDE
    n_sel = -(-(past_len + t_new) // SEL_BLOCK)
    n_selp = -(-n_sel // 8) * 8
    n = np.arange(n_cmp)
    mmap = np.zeros((n_selp, n_cmp), np.float32)
    real = n < n_cmp - 1
    np.add.at(mmap, (n[real] // SEL_CHUNKS, n[real]), 1.0)
    np.add.at(mmap, ((n[real] + 1) // SEL_CHUNKS, n[real]), 1.0)
    rows = GROUP_SIZE * Q_PAD
    return pl.pallas_call(
        functools.partial(_nsa_sample_select_kernel, past_len=past_len, n_cmp=n_cmp, n_sel=n_sel),
        grid=(batch,),
        in_specs=[
            pl.BlockSpec((None, KV_GROUPS, rows, HEAD_DIM), lambda b: (b, 0, 0, 0)),
            pl.BlockSpec((None, 2, KV_GROUPS * n_cmp, HEAD_DIM), lambda b: (b, 0, 0, 0)),
            pl.BlockSpec((n_selp, n_cmp), lambda b: (0, 0)),
        ],
        out_specs=[
            pl.BlockSpec((None, KV_GROUPS, rows, HEAD_DIM), lambda b: (b, 0, 0, 0)),
            pl.BlockSpec((None, SEL_TOPN, KV_GROUPS * Q_PAD), lambda b: (b, 0, 0)),
        ],
        out_shape=[jax.ShapeDtypeStruct((batch, KV_GROUPS, rows, HEAD_DIM), F32),
                   jax.ShapeDtypeStruct((batch, SEL_TOPN, KV_GROUPS * Q_PAD), jnp.int32)],
        scratch_shapes=[pltpu.VMEM((n_selp, KV_GROUPS * Q_PAD), F32)],
        compiler_params=_cparams(("parallel",)),
        name="nsa_sample_select",
    )(q_stack, cmp, jnp.asarray(mmap, BF16))


def _nsa_sample_attend_kernel(pt_ref, idx_ref, cache_ref, q_ref, slope_ref, snew_ref, win_ref, wnew_ref, wnew_t_ref,
                              oc_ref, gate_ref, o_ref, wout_ref, kcat_ref, vcat_ref, page_ref, sem_ref,
                              *, layer, past_len, t_new, n_past_blocks, n_steps):
    b, g = pl.program_id(0), pl.program_id(1)
    n_pages = past_len // PAGE_SIZE
    n_copies = t_new * SEL_TOPN
    step = b * KV_GROUPS + g

    def page_copy(src_page, src_group, slot, n):
        return pltpu.make_async_copy(cache_ref.at[layer, src_page, :, src_group], page_ref.at[slot, n],
                                     sem_ref.at[slot, n])

    def start_gather(at_step, slot):
        at_b = at_step // KV_GROUPS
        at_g = at_step % KV_GROUPS
        for n in range(n_copies):
            blk = jnp.minimum(idx_ref[at_step * n_copies + n], n_past_blocks - 1)
            page_copy(pt_ref[at_b * n_pages + blk // 2], at_g, slot, n).start()

    @pl.when(step == 0)
    def _():
        start_gather(step, 0)

    @pl.when(step + 1 < n_steps)
    def _():
        start_gather(step + 1, (step + 1) % 2)

    slot = step % 2
    for n in range(n_copies):
        page_copy(0, 0, slot, n).wait()

    slope = slope_ref[:, 0:1]
    lane = lax.broadcasted_iota(jnp.int32, (1, LANES), 1)
    new_pos = past_len + lane
    buf = win_ref.shape[2]
    wpos = past_len - buf + lax.broadcasted_iota(jnp.int32, (1, buf), 1)
    k_new, v_new = snew_ref[0].astype(BF16), snew_ref[1].astype(BF16)
    kw_new, vw_new = wnew_ref[0].astype(BF16), wnew_ref[1].astype(BF16)
    kw_old, vw_old = win_ref[0].astype(BF16), win_ref[1].astype(BF16)

    def attend(tiles, v_last):
        m = jnp.full((HEAD_PAD, 1), NEG_INF, F32)
        for s, mask, _ in tiles:
            m = jnp.maximum(m, jnp.max(jnp.where(mask, s, NEG_INF), axis=1, keepdims=True))
        l = jnp.zeros((HEAD_PAD, 1), F32)
        acc = jnp.zeros((HEAD_PAD, HEAD_DIM), F32)
        for i, (s, mask, v) in enumerate(tiles):
            p = jnp.where(mask, jnp.exp(jnp.where(mask, s, NEG_INF) - m), 0.0)
            l = l + jnp.sum(p, axis=1, keepdims=True)
            if i == len(tiles) - 1:
                acc = acc + _dot(p.astype(BF16), v_last)
            else:
                acc = acc + _nt(p.astype(BF16), v)
        return _normalize(acc, l)

    for qi in range(t_new):
        t = past_len + qi
        q = (q_ref[qi] * (HEAD_DIM ** -0.5)).astype(BF16)
        new_ok = (lane < t_new) & (new_pos <= t)

        kpos, mask = [], []
        for sl in range(SEL_TOPN):
            n = qi * SEL_TOPN + sl
            blk = idx_ref[step * n_copies + n]
            kcat_ref[qi, :, sl * PAGE_SIZE:(sl + 1) * PAGE_SIZE] = page_ref[slot, n, 0].astype(BF16)
            vcat_ref[qi, :, sl * PAGE_SIZE:(sl + 1) * PAGE_SIZE] = page_ref[slot, n, 1].astype(BF16)
            kpos.append((blk // 2) * PAGE_SIZE + lane)
            mask.append((lane // SEL_BLOCK == blk % 2) & (blk < n_past_blocks) & (kpos[-1] <= t))
        kpos = jnp.concatenate(kpos, axis=1)
        mask = jnp.concatenate([m_.astype(jnp.int32) for m_ in mask], axis=1) > 0
        s_sel = _dot(q, kcat_ref[qi]) + slope * (kpos - t).astype(F32)
        s_new = _nt(q, k_new) + slope * (new_pos - t).astype(F32)
        o_s = attend([(s_sel, mask, vcat_ref[qi]), (s_new, new_ok, None)], v_new)

        dist = t - wpos
        s_w = _dot(q, kw_old) + slope * (wpos - t).astype(F32)
        s_wn = _nt(q, kw_new) + slope * (new_pos - t).astype(F32)
        o_w = attend([(s_w, (dist >= 0) & (dist < WINDOW), vw_old),
                      (s_wn, new_ok & (t - new_pos < WINDOW), None)], vw_new)

        gates = jax.nn.sigmoid(gate_ref[qi])
        o_ref[qi] = gates[:, 0:1] * oc_ref[qi] + gates[:, 1:2] * o_s + gates[:, 2:3] * o_w

    for kv in range(2):
        rolled = pltpu.roll(win_ref[kv], buf - t_new, axis=1)
        wout_ref[kv, :, 0:buf - LANES] = rolled[:, 0:buf - LANES]
        wout_ref[kv, :, buf - LANES:] = jnp.where(lane >= LANES - t_new, wnew_t_ref[kv], rolled[:, buf - LANES:])


def nsa_sample_attend(cache_t, win_t, layer, page_table_flat, idx_flat, q_heads, slopes, sel_new, win_new,
                      win_new_t, o_c, gates, *, batch, past_len, t_new):
    n_past_blocks = past_len // SEL_BLOCK
    buf = win_t.shape[-1]
    per_bg = lambda shape: pl.BlockSpec((None, None) + shape, lambda b, g, pt, idx: (b, g) + (0,) * len(shape))
    grid_spec = pltpu.PrefetchScalarGridSpec(
        num_scalar_prefetch=2,
        grid=(batch, KV_GROUPS),
        in_specs=[
            pl.BlockSpec(memory_space=pl.ANY),
            per_bg((t_new, HEAD_PAD, HEAD_DIM)),
            pl.BlockSpec((None, HEAD_PAD, LANES), lambda b, g, pt, idx: (g, 0, 0)),
            per_bg((2, LANES, HEAD_DIM)),
            pl.BlockSpec((None, None, 2, None, HEAD_DIM, buf), lambda b, g, pt, idx: (layer, b, 0, g, 0, 0)),
            per_bg((2, LANES, HEAD_DIM)),
            per_bg((2, HEAD_DIM, LANES)),
            per_bg((t_new, HEAD_PAD, HEAD_DIM)),
            per_bg((t_new, HEAD_PAD, LANES)),
        ],
        out_specs=[
            per_bg((t_new, HEAD_PAD, HEAD_DIM)),
            pl.BlockSpec((None, 2, None, HEAD_DIM, buf), lambda b, g, pt, idx: (b, 0, g, 0, 0)),
        ],
        scratch_shapes=[pltpu.VMEM((t_new, HEAD_DIM, SEL_TOPN * PAGE_SIZE), BF16)] * 2 + [
            pltpu.VMEM((2, t_new * SEL_TOPN, 2, HEAD_DIM, PAGE_SIZE), F32),
            pltpu.SemaphoreType.DMA((2, t_new * SEL_TOPN)),
        ],
    )
    return pl.pallas_call(
        functools.partial(_nsa_sample_attend_kernel, layer=layer, past_len=past_len, t_new=t_new,
                          n_past_blocks=n_past_blocks, n_steps=batch * KV_GROUPS),
        grid_spec=grid_spec,
        out_shape=[jax.ShapeDtypeStruct((batch, KV_GROUPS, t_new, HEAD_PAD, HEAD_DIM), F32),
                   jax.ShapeDtypeStruct((batch, 2, KV_GROUPS, HEAD_DIM, buf), F32)],
        compiler_params=_cparams(("arbitrary", "arbitrary")),
        name="nsa_sample_attend",
    )(page_table_flat, idx_flat, cache_t, q_heads, slopes, sel_new, win_t, win_new, win_new_t, o_c, gates)


def nsa_sample_layer(x, gain, w, cmp_t, sel_t, win_t, layer, page_table_flat, *, batch, t_new, past_len):
    m = batch * t_new
    proj, gates = _nsa_project(x, gain, w, tm=m, head_norm=True)
    cmp = compress_pages(cmp_t.reshape(cmp_t.shape[0], cmp_t.shape[1], NSA_KV_DIM, PAGE_SIZE), layer,
                         page_table_flat, w["pe"], w["w1"], w["w2"], w["kgain"], batch=batch,
                         n_pages=past_len // PAGE_SIZE)
    q = proj[:, :NSA_Q_DIM].reshape(batch, t_new, KV_GROUPS, GROUP_SIZE, HEAD_DIM)
    q_stack = jnp.pad(q.transpose(0, 2, 3, 1, 4), ((0, 0), (0, 0), (0, 0), (0, Q_PAD - t_new), (0, 0)))
    o_c, idx = nsa_sample_select(q_stack.reshape(batch, KV_GROUPS, GROUP_SIZE * Q_PAD, HEAD_DIM), cmp,
                                 batch=batch, past_len=past_len, t_new=t_new)
    pad_heads = ((0, 0), (0, 0), (0, 0), (0, HEAD_PAD - GROUP_SIZE), (0, 0))
    o_c = o_c.reshape(batch, KV_GROUPS, GROUP_SIZE, Q_PAD, HEAD_DIM)[:, :, :, :t_new].transpose(0, 1, 3, 2, 4)
    idx_flat = idx.reshape(batch, SEL_TOPN, KV_GROUPS, Q_PAD)[..., :t_new].transpose(0, 2, 3, 1).reshape(-1)
    q_heads = jnp.pad(q.transpose(0, 2, 1, 3, 4), pad_heads)

    def new_rows(cols):
        a = cols.reshape(batch, t_new, 2, KV_GROUPS, HEAD_DIM).transpose(0, 3, 2, 1, 4)
        return jnp.pad(a, ((0, 0), (0, 0), (0, 0), (0, LANES - t_new), (0, 0)))

    kvs = proj[:, NSA_Q_DIM + NSA_KV_DIM:NSA_Q_DIM + 2 * NSA_KV_DIM]
    kvw = proj[:, NSA_Q_DIM + 2 * NSA_KV_DIM:]
    win_new_t = kvw.reshape(batch, t_new, 2, KV_GROUPS, HEAD_DIM).transpose(0, 3, 2, 4, 1)
    win_new_t = jnp.pad(win_new_t, ((0, 0), (0, 0), (0, 0), (0, 0), (LANES - t_new, 0)))
    g_arr = gates[:, :3 * NSA_HEADS].reshape(batch, t_new, 3, KV_GROUPS, GROUP_SIZE).transpose(0, 3, 1, 4, 2)
    g_arr = jnp.pad(g_arr, ((0, 0), (0, 0), (0, 0), (0, HEAD_PAD - GROUP_SIZE), (0, LANES - 3)))
    slopes = np.zeros((KV_GROUPS, HEAD_PAD, LANES), np.float32)
    slopes[:, :GROUP_SIZE, :] = np.asarray(_SLOPES, np.float32).reshape(KV_GROUPS, GROUP_SIZE, 1)
    o, win_out = nsa_sample_attend(sel_t, win_t, layer, page_table_flat, idx_flat, q_heads, jnp.asarray(slopes),
                                   new_rows(kvs), new_rows(kvw), win_new_t, jnp.pad(o_c, pad_heads), g_arr,
                                   batch=batch, past_len=past_len, t_new=t_new)
    o = o[:, :, :, :GROUP_SIZE].transpose(0, 2, 1, 3, 4).reshape(m, NSA_Q_DIM)
    return matmul_residual(o.astype(BF16), w["w_out"], x, tm=m), proj, win_out


def _expand_heads(x, e):
    x1 = x.astype(BF16)
    rest = x - x1.astype(F32)
    x2 = rest.astype(BF16)
    x3 = (rest - x2.astype(F32)).astype(BF16)
    return (_dot(x1, e) + _dot(x2, e)) + _dot(x3, e)


def _ssd_kernel(z_ref, x_ref, bc_ref, dt_ref, cw_ref, cb_ref, dtb_ref, alog_ref, drow_ref, nw_ref, cinit_ref,
                hinit_ref, tril_ref, e_ref, y_ref, hout_ref, stage_ref, h_ref, yacc_ref, *, t_valid):
    L = SSM_CHUNK
    c = pl.program_id(1)

    @pl.when(c == 0)
    def _():
        stage_ref[...] = cinit_ref[...]
        h_ref[...] = hinit_ref[...]

    cur = jnp.concatenate([x_ref[...], bc_ref[...]], axis=1)
    prev = stage_ref[...]
    first = lax.broadcasted_iota(jnp.int32, (8, 1), 0)
    conv = cb_ref[...] + cur * cw_ref[SSM_CONV - 1:SSM_CONV, :]
    for s in range(1, SSM_CONV):
        rolled = pltpu.roll(cur, s, axis=0)
        head = jnp.where(first < s, pltpu.roll(prev, s, axis=0), rolled[0:8])
        conv = conv + jnp.concatenate([head, rolled[8:]], axis=0) * cw_ref[SSM_CONV - 1 - s:SSM_CONV - s, :]
    stage_ref[...] = cur[L - 8:L]
    conv = conv * jax.nn.sigmoid(conv)
    xh = conv[:, :SSM_D_INNER]

    row = lax.broadcasted_iota(jnp.int32, (L, LANES), 0)
    dt_in = dt_ref[...] + dtb_ref[...]
    dt = jnp.maximum(dt_in, 0.0) + jnp.log1p(jnp.exp(-jnp.abs(dt_in)))
    dt = jnp.where(row < t_valid, dt, 0.0)
    da = dt * -jnp.exp(alog_ref[...])
    acum = _dot(tril_ref[...], da, HIGHEST)
    acum_t = jnp.transpose(acum)
    dt_full = _expand_heads(dt, e_ref[...])
    acum_full = _expand_heads(acum, e_ref[...])
    grow_full = jnp.exp(acum_full)
    toend_full = jnp.exp(acum_full[L - 1:L, :] - acum_full)
    last_t = jnp.exp(acum_t[:, L - 1:L])
    xdt = xh * dt_full
    causal = lax.broadcasted_iota(jnp.int32, (L, L), 0) >= lax.broadcasted_iota(jnp.int32, (L, L), 1)

    for g in range(SSM_GROUPS):
        gl, gh = g * SSM_GW, (g + 1) * SSM_GW
        bm = conv[:, SSM_D_INNER + g * SSM_STATE:SSM_D_INNER + (g + 1) * SSM_STATE].astype(BF16)
        cm = conv[:, SSM_D_INNER + (SSM_GROUPS + g) * SSM_STATE:
                  SSM_D_INNER + (SSM_GROUPS + g + 1) * SSM_STATE].astype(BF16)
        cb = _nt(cm, bm)
        xdt_g = xdt[:, gl:gh]
        for r in range(SSM_HPG):
            h = g * SSM_HPG + r
            seg = acum[:, h:h + 1] - acum_t[h:h + 1, :]
            decay = jnp.exp(jnp.where(causal, seg, NEG_INF))
            yacc_ref[:, gl + r * SSM_HEAD_DIM:gl + (r + 1) * SSM_HEAD_DIM] = _dot(
                (cb * decay).astype(BF16), xdt_g[:, r * SSM_HEAD_DIM:(r + 1) * SSM_HEAD_DIM].astype(BF16))
        h_g = h_ref[gl:gh, :]
        y_state = _nt(cm, h_g.astype(BF16)) * grow_full[:, gl:gh]
        yacc_ref[:, gl:gh] = yacc_ref[:, gl:gh] + y_state
        xw = (xdt_g * toend_full[:, gl:gh]).astype(BF16)
        h_add = lax.dot_general(xw, bm, (((0,), (0,)), ((), ())), preferred_element_type=F32)
        for r in range(SSM_HPG):
            h = g * SSM_HPG + r
            rl, rh = r * SSM_HEAD_DIM, (r + 1) * SSM_HEAD_DIM
            keep = jnp.broadcast_to(last_t[h:h + 1, :], (SSM_HEAD_DIM, SSM_STATE))
            h_ref[gl + rl:gl + rh, :] = h_g[rl:rh] * keep + h_add[rl:rh]

    zf = z_ref[...]
    gated = (yacc_ref[...] + xh * drow_ref[...]) * (zf * jax.nn.sigmoid(zf))
    for g in range(SSM_GROUPS):
        gl, gh = g * SSM_GW, (g + 1) * SSM_GW
        y_ref[:, gl:gh] = _rms(gated[:, gl:gh], nw_ref[:, gl:gh]).astype(y_ref.dtype)

    @pl.when(c == pl.num_programs(1) - 1)
    def _():
        hout_ref[...] = h_ref[...]


def ssd_mixer(proj, dt_raw, conv_init, h_init, w, *, batch, n_chunks, t_valid):
    L = SSM_CHUNK
    rows = batch * n_chunks * L
    conv_dim = SSM_D_INNER + SSM_BC
    tril = np.tril(np.ones((L, L), np.float32))
    e = np.zeros((LANES, SSM_D_INNER), np.float32)
    e[np.arange(SSM_D_INNER) // SSM_HEAD_DIM, np.arange(SSM_D_INNER)] = 1.0
    const = lambda shape: pl.BlockSpec(shape, lambda b, c: (0,) * len(shape))
    return pl.pallas_call(
        functools.partial(_ssd_kernel, t_valid=t_valid),
        grid=(batch, n_chunks),
        in_specs=[
            pl.BlockSpec((L, SSM_D_INNER), lambda b, c: (b * n_chunks + c, 0)),
            pl.BlockSpec((L, SSM_D_INNER), lambda b, c: (b * n_chunks + c, 1)),
            pl.BlockSpec((L, SSM_BC), lambda b, c: (b * n_chunks + c, 2 * SSM_D_INNER // SSM_BC)),
            pl.BlockSpec((L, LANES), lambda b, c: (b * n_chunks + c, 0)),
            const((SSM_CONV, conv_dim)), const((1, conv_dim)), const((1, LANES)), const((1, LANES)),
            const((1, SSM_D_INNER)), const((1, SSM_D_INNER)),
            pl.BlockSpec((None, 8, conv_dim), lambda b, c: (b, 0, 0)),
            pl.BlockSpec((None, SSM_D_INNER, SSM_STATE), lambda b, c: (b, 0, 0)),
            const((L, L)), const((LANES, SSM_D_INNER)),
        ],
        out_specs=[
            pl.BlockSpec((L, SSM_D_INNER), lambda b, c: (b * n_chunks + c, 0)),
            pl.BlockSpec((None, SSM_D_INNER, SSM_STATE), lambda b, c: (b, 0, 0)),
        ],
        out_shape=[jax.ShapeDtypeStruct((rows, SSM_D_INNER), BF16),
                   jax.ShapeDtypeStruct((batch, SSM_D_INNER, SSM_STATE), F32)],
        scratch_shapes=[pltpu.VMEM((8, conv_dim), F32), pltpu.VMEM((SSM_D_INNER, SSM_STATE), F32),
                        pltpu.VMEM((L, SSM_D_INNER), F32)],
        compiler_params=_cparams(("parallel", "arbitrary")),
        name="ssd_mixer",
    )(proj, proj, proj, dt_raw, w["conv_w"], w["conv_b"], w["dt_bias"], w["a_log"], w["d_row"], w["norm_w"],
      conv_init, h_init, jnp.asarray(tril), jnp.asarray(e, BF16))


def _ssd_weights(w_in, conv_w, conv_b, dt_bias, a_log, d_skip, norm_w, w_out):
    n_main = 2 * SSM_D_INNER + SSM_BC
    pad = lambda v: jnp.pad(v, (0, LANES - SSM_HEADS)).reshape(1, LANES)
    return dict(
        w_main=w_in[:, :n_main].astype(BF16),
        w_dt=jnp.pad(w_in[:, n_main:], ((0, 0), (0, LANES - SSM_HEADS))).astype(BF16),
        conv_w=conv_w, conv_b=conv_b.reshape(1, -1), dt_bias=pad(dt_bias), a_log=pad(a_log),
        d_row=jnp.repeat(d_skip, SSM_HEAD_DIM).reshape(1, SSM_D_INNER), norm_w=norm_w.reshape(1, SSM_D_INNER),
        w_out=w_out.astype(BF16))


def ssd_layer(x, gain, w, conv_state, h_init, *, batch, t):
    tm = min(1024, x.shape[0])
    proj = norm_matmul(x, gain, w["w_main"], tm=min(2048, x.shape[0]), tn=1024)
    dt_raw = norm_matmul(x, gain, w["w_dt"], tm=tm, tn=LANES)
    conv_init = jnp.pad(conv_state, ((0, 0), (8 - (SSM_CONV - 1), 0), (0, 0)))
    if t % SSM_CHUNK == 0:
        y, h_last = ssd_mixer(proj, dt_raw, conv_init, h_init, w, batch=batch, n_chunks=t // SSM_CHUNK,
                              t_valid=SSM_CHUNK)
    else:
        pad_rows = lambda a: jnp.pad(a.reshape(batch, t, -1), ((0, 0), (0, SSM_CHUNK - t), (0, 0))).reshape(
            batch * SSM_CHUNK, -1)
        y, h_last = ssd_mixer(pad_rows(proj), pad_rows(dt_raw), conv_init, h_init, w, batch=batch, n_chunks=1,
                              t_valid=t)
        y = y.reshape(batch, SSM_CHUNK, -1)[:, :t].reshape(batch * t, -1)
    return matmul_residual(y, w["w_out"], x, tm=tm), proj, h_last


def kernel(x_prompt, x_sample, cache_kv_cmp, cache_kv_sel, cache_kv_win, state_conv, state_ssm, page_table,
           norm_mix, norm_ffn, nsa_w_in, nsa_qk_gain, nsa_cmp_pe, nsa_cmp_w1, nsa_cmp_w2, nsa_w_out,
           ssd_w_in, ssd_conv_w, ssd_conv_b, ssd_dt_bias, ssd_a_log, ssd_d, ssd_norm, ssd_w_out,
           ffn_w_gate, ffn_w_up, ffn_w_down):
    bp, tp = x_prompt.shape[0], x_prompt.shape[1]
    bs, ts = x_sample.shape[0], x_sample.shape[1]
    depth = norm_mix.shape[0]
    past_len = page_table.shape[1] * PAGE_SIZE
    conv_dim = SSM_D_INNER + SSM_BC
    assert tp % SSM_CHUNK == 0 and ts < SSM_CHUNK and ts <= Q_PAD and ts >= SSM_CONV - 1
    xp = x_prompt.reshape(bp * tp, D_MODEL)
    xs = x_sample.reshape(bs * ts, D_MODEL)
    pt_flat = page_table.reshape(-1)
    feature_major = lambda a: a.transpose(0, 1, 3, 4, 5, 2)
    cmp_t, sel_t, win_t = feature_major(cache_kv_cmp), feature_major(cache_kv_sel), feature_major(cache_kv_win)
    kv_shape = (2, KV_GROUPS, HEAD_DIM)
    outs = {k: [] for k in ("cmp_p", "sel_p", "win_p", "conv_p", "ssm_p", "cmp_s", "sel_s", "win_s", "conv_s",
                            "ssm_s")}
    for i in range(depth):
        l = i // 2
        if i % 2 == 0:
            w = _nsa_weights(nsa_w_in[l], nsa_qk_gain[l], nsa_cmp_pe[l], nsa_cmp_w1[l], nsa_cmp_w2[l], nsa_w_out[l])
            xp, cmp_rows, sel_rows, win_rows = nsa_prompt_layer(xp, norm_mix[i], w, batch=bp, t=tp)
            row_major = lambda a: a.reshape((bp,) + kv_shape + (a.shape[-1],)).transpose(0, 4, 1, 2, 3)
            outs["cmp_p"].append(row_major(cmp_rows))
            outs["sel_p"].append(row_major(sel_rows))
            outs["win_p"].append(row_major(win_rows))
            kv = lambda a, b, t, j: a[:, NSA_Q_DIM + j * NSA_KV_DIM:NSA_Q_DIM + (j + 1) * NSA_KV_DIM].reshape(
                (b, t) + kv_shape)
            xs, proj_s, win_out = nsa_sample_layer(xs, norm_mix[i], w, cmp_t, sel_t, win_t, l, pt_flat, batch=bs,
                                                   t_new=ts, past_len=past_len)
            outs["cmp_s"].append(kv(proj_s, bs, ts, 0))
            outs["sel_s"].append(kv(proj_s, bs, ts, 1))
            outs["win_s"].append(win_out.transpose(0, 4, 1, 2, 3))
        else:
            w = _ssd_weights(ssd_w_in[l], ssd_conv_w[l], ssd_conv_b[l], ssd_dt_bias[l], ssd_a_log[l], ssd_d[l],
                             ssd_norm[l], ssd_w_out[l])
            xp, proj, h_last = ssd_layer(xp, norm_mix[i], w, jnp.zeros((bp, SSM_CONV - 1, conv_dim), F32),
                                         jnp.zeros((bp, SSM_D_INNER, SSM_STATE), F32), batch=bp, t=tp)
            outs["conv_p"].append(proj.reshape(bp, tp, -1)[:, tp - (SSM_CONV - 1):, SSM_D_INNER:])
            outs["ssm_p"].append(h_last.reshape(bp, SSM_HEADS, SSM_HEAD_DIM, SSM_STATE))
            xs, proj_s, h_last = ssd_layer(xs, norm_mix[i], w, state_conv[l],
                                           state_ssm[l].reshape(bs, SSM_D_INNER, SSM_STATE), batch=bs, t=ts)
            outs["conv_s"].append(proj_s.reshape(bs, ts, -1)[:, ts - (SSM_CONV - 1):, SSM_D_INNER:])
            outs["ssm_s"].append(h_last.reshape(bs, SSM_HEADS, SSM_HEAD_DIM, SSM_STATE))
        wg, wu, wd = ffn_w_gate[i].astype(BF16), ffn_w_up[i].astype(BF16), ffn_w_down[i].astype(BF16)
        xp = ffn(xp, norm_ffn[i], wg, wu, wd, tm=1024, th=1408)
        xs = ffn(xs, norm_ffn[i], wg, wu, wd, tm=bs * ts, th=256)
    order = ("cmp_p", "sel_p", "win_p", "conv_p", "ssm_p", "cmp_s", "sel_s", "win_s", "conv_s", "ssm_s")
    return (xp.reshape(bp, tp, D_MODEL), xs.reshape(bs, ts, D_MODEL)) + tuple(jnp.stack(outs[k]) for k in order)
```

```python
import functools
import math

import jax
import jax.numpy as jnp
import ml_dtypes
import numpy as np
from jax import lax
from jax.experimental import pallas as pl
from jax.experimental.pallas import tpu as pltpu

F32 = jnp.float32
BF16 = jnp.bfloat16
HIGHEST = lax.Precision.HIGHEST

D_MODEL = 1024
PAGE_SIZE = 128
NSA_HEADS = 16
HEAD_DIM = 64
KV_GROUPS = 4
GROUP_SIZE = NSA_HEADS // KV_GROUPS
CMP_BLOCK = 32
CMP_STRIDE = 16
CMP_HIDDEN = 256
SEL_BLOCK = 64
SEL_CHUNKS = SEL_BLOCK // CMP_STRIDE
SEL_TOPN = 16
WINDOW = 512
NSA_Q_DIM = NSA_HEADS * HEAD_DIM
NSA_KV_DIM = 2 * KV_GROUPS * HEAD_DIM
KV_HALF = KV_GROUPS * HEAD_DIM
SSM_D_INNER = 2048
SSM_HEAD_DIM = 64
SSM_HEADS = 32
SSM_GROUPS = 4
SSM_HPG = SSM_HEADS // SSM_GROUPS
SSM_STATE = 128
SSM_CONV = 4
SSM_BC = 2 * SSM_GROUPS * SSM_STATE
SSM_GW = SSM_D_INNER // SSM_GROUPS
SSM_CHUNK = 128
NORM_EPS = 1e-6
NEG_INF = -1e30
FORCE_SCORE = 1e4
LANES = 128
VMEM_LIMIT = 56 * 1024 * 1024

_SLOPES = [float(np.float32((2.0 ** (-8.0 / NSA_HEADS)) ** (i + 1))) for i in range(NSA_HEADS)]


def _cparams(sem):
    return pltpu.CompilerParams(dimension_semantics=sem, vmem_limit_bytes=VMEM_LIMIT)


def _nt(a, b, precision=None):
    return lax.dot_general(a, b, (((1,), (1,)), ((), ())), preferred_element_type=F32, precision=precision)


def _dot(a, b, precision=None):
    return jnp.dot(a, b, preferred_element_type=F32, precision=precision)


def _rms(x, gain):
    return x * lax.rsqrt(jnp.mean(x * x, axis=-1, keepdims=True) + NORM_EPS) * gain


def _norm_matmul_kernel(x_ref, g_ref, w_ref, hg_ref, hm_ref, e_ref, et_ref, o_ref, xn_ref, *, head_norm):
    @pl.when(pl.program_id(1) == 0)
    def _():
        xn_ref[...] = _rms(x_ref[...], g_ref[...]).astype(BF16)

    y = _dot(xn_ref[...], w_ref[...])
    if head_norm:
        ss = _dot(y * y, e_ref[...], HIGHEST)
        inv = lax.rsqrt(ss * (1.0 / HEAD_DIM) + NORM_EPS)
        yn = y * _dot(inv, et_ref[...], HIGHEST) * hg_ref[...]
        y = jnp.where(hm_ref[...] > 0.5, yn, y)
    o_ref[...] = y


def norm_matmul(x, gain, w, head_gain=None, head_mask=None, *, tm, tn):
    m, k = x.shape
    n = w.shape[1]
    head_norm = head_gain is not None
    if not head_norm:
        head_gain = jnp.zeros((1, n), F32)
        head_mask = jnp.zeros((1, n), F32)
    lane = np.arange(tn)
    e = np.zeros((tn, LANES), np.float32)
    e[lane, lane // HEAD_DIM] = 1.0
    return pl.pallas_call(
        functools.partial(_norm_matmul_kernel, head_norm=head_norm),
        grid=(m // tm, n // tn),
        in_specs=[
            pl.BlockSpec((tm, k), lambda i, j: (i, 0)),
            pl.BlockSpec((1, k), lambda i, j: (0, 0)),
            pl.BlockSpec((k, tn), lambda i, j: (0, j)),
            pl.BlockSpec((1, tn), lambda i, j: (0, j)),
            pl.BlockSpec((1, tn), lambda i, j: (0, j)),
            pl.BlockSpec((tn, LANES), lambda i, j: (0, 0)),
            pl.BlockSpec((LANES, tn), lambda i, j: (0, 0)),
        ],
        out_specs=pl.BlockSpec((tm, tn), lambda i, j: (i, j)),
        out_shape=jax.ShapeDtypeStruct((m, n), F32),
        scratch_shapes=[pltpu.VMEM((tm, k), BF16)],
        compiler_params=_cparams(("parallel", "arbitrary")),
        name="norm_matmul",
    )(x, gain.reshape(1, k), w, head_gain, head_mask, jnp.asarray(e), jnp.asarray(e.T))


def _ffn_kernel(x_ref, g_ref, wg_ref, wu_ref, wd_ref, o_ref, xn_ref):
    @pl.when(pl.program_id(1) == 0)
    def _():
        x = x_ref[...]
        xn_ref[...] = _rms(x, g_ref[...]).astype(BF16)
        o_ref[...] = x

    h = xn_ref[...]
    gate = _dot(h, wg_ref[...])
    up = _dot(h, wu_ref[...])
    act = (gate * jax.nn.sigmoid(gate) * up).astype(BF16)
    o_ref[...] += _dot(act, wd_ref[...])


def ffn(x, gain, wg, wu, wd, *, tm, th):
    m, k = x.shape
    hid = wg.shape[1]
    return pl.pallas_call(
        _ffn_kernel,
        grid=(m // tm, hid // th),
        in_specs=[
            pl.BlockSpec((tm, k), lambda i, j: (i, 0)),
            pl.BlockSpec((1, k), lambda i, j: (0, 0)),
            pl.BlockSpec((k, th), lambda i, j: (0, j)),
            pl.BlockSpec((k, th), lambda i, j: (0, j)),
            pl.BlockSpec((th, k), lambda i, j: (j, 0)),
        ],
        out_specs=pl.BlockSpec((tm, k), lambda i, j: (i, 0)),
        out_shape=jax.ShapeDtypeStruct((m, k), F32),
        scratch_shapes=[pltpu.VMEM((tm, k), BF16)],
        compiler_params=_cparams(("parallel", "arbitrary")),
        name="ffn",
    )(x, gain.reshape(1, k), wg, wu, wd)


def _matmul_res_kernel(a_ref, w_ref, r_ref, o_ref):
    o_ref[...] = r_ref[...] + _dot(a_ref[...], w_ref[...])


def matmul_residual(a, w, res, *, tm):
    m, k = a.shape
    n = w.shape[1]
    return pl.pallas_call(
        _matmul_res_kernel,
        grid=(m // tm,),
        in_specs=[
            pl.BlockSpec((tm, k), lambda i: (i, 0)),
            pl.BlockSpec((k, n), lambda i: (0, 0)),
            pl.BlockSpec((tm, n), lambda i: (i, 0)),
        ],
        out_specs=pl.BlockSpec((tm, n), lambda i: (i, 0)),
        out_shape=jax.ShapeDtypeStruct((m, n), F32),
        compiler_params=_cparams(("parallel",)),
        name="matmul_residual",
    )(a, w, res)


CMP_PACK = 4
CMP_PITCH = 24


def _compress_accumulate(xs_ref, n_rows, pe_ref, w1_ref, kv, pitch=CMP_PITCH):
    n_pair = NSA_KV_DIM // LANES
    low = lax.broadcasted_iota(jnp.int32, (n_rows, LANES), 1) < HEAD_DIM
    acc = [None, None]
    for jp in range(CMP_STRIDE // CMP_PACK):
        groups = []
        for c in range(kv * n_pair // 2, (kv + 1) * n_pair // 2):
            rows = [xs_ref.at[c][pl.ds(jp * CMP_PACK + i, n_rows, stride=pitch), :] for i in range(CMP_PACK)]
            swapped = [pltpu.roll(r, HEAD_DIM, axis=1) for r in rows]
            groups.append(jnp.concatenate([jnp.where(low, rows[i], swapped[i + 1])
                                           for i in range(0, CMP_PACK, 2)], axis=1))
            groups.append(jnp.concatenate([jnp.where(low, swapped[i], rows[i + 1])
                                           for i in range(0, CMP_PACK, 2)], axis=1))
        xs = jnp.concatenate(groups, axis=0)
        for m in range(2):
            lhs = (xs + pe_ref[kv, m, jp:jp + 1, :]).astype(BF16)
            part = _dot(lhs, w1_ref[kv, m, jp])
            acc[m] = part if acc[m] is None else acc[m] + part
    return acc


def _compress_finish(acc, w2_ref, kg_ref, kv):
    rows = acc[0].shape[0]
    pre = acc[0] + pltpu.roll(acc[1], rows - 1, axis=0)
    hid = (pre * jax.nn.sigmoid(pre)).astype(BF16)
    out = _dot(hid, w2_ref[kv])
    if kv == 0:
        out = _rms(out, kg_ref[...])
    return out


def _compress_rows_kernel(x_ref, pe_ref, w1_ref, w2_ref, kg_ref, o_ref, xt_ref, xs_ref, *, n_chunk):
    xt_ref[...] = jnp.transpose(x_ref[...])
    n_pair = NSA_KV_DIM // LANES
    def stage_chunk(k, carry):
        src = pl.ds(pl.multiple_of(k * CMP_STRIDE, CMP_STRIDE), CMP_STRIDE)
        dst = pl.ds(pl.multiple_of(k * CMP_PITCH, 8), CMP_STRIDE)
        for c in range(n_pair):
            xs_ref[c, dst, :] = x_ref[src, c * LANES:(c + 1) * LANES]
        return carry

    lax.fori_loop(0, n_chunk, stage_chunk, 0)
    for kv in range(2):
        o_ref[kv] = _compress_finish(_compress_accumulate(xs_ref, n_chunk, pe_ref, w1_ref, kv), w2_ref, kg_ref, kv)


def compress_rows(proj, col_block, pe, w1, w2, kgain, *, batch, t):
    n_chunk = t // CMP_STRIDE
    return pl.pallas_call(
        functools.partial(_compress_rows_kernel, n_chunk=n_chunk),
        grid=(batch,),
        in_specs=[
            pl.BlockSpec((t, NSA_KV_DIM), lambda b: (b, col_block)),
            pl.BlockSpec(pe.shape, lambda b: (0, 0, 0, 0)),
            pl.BlockSpec(w1.shape, lambda b: (0, 0, 0, 0, 0)),
            pl.BlockSpec(w2.shape, lambda b: (0, 0, 0)),
            pl.BlockSpec((1, HEAD_DIM), lambda b: (0, 0)),
        ],
        out_specs=[pl.BlockSpec((None, 2, KV_GROUPS * n_chunk, HEAD_DIM), lambda b: (b, 0, 0, 0)),
                   pl.BlockSpec((None, NSA_KV_DIM, t), lambda b: (b, 0, 0))],
        out_shape=[jax.ShapeDtypeStruct((batch, 2, KV_GROUPS * n_chunk, HEAD_DIM), F32),
                   jax.ShapeDtypeStruct((batch, NSA_KV_DIM, t), F32)],
        scratch_shapes=[pltpu.VMEM((NSA_KV_DIM // LANES, n_chunk * CMP_PITCH, LANES), F32)],
        compiler_params=_cparams(("parallel",)),
        name="compress_rows",
    )(proj, pe, w1, w2, kgain.reshape(1, HEAD_DIM))


def _slope_col(g, rows, tq):
    rblk = lax.broadcasted_iota(jnp.int32, (rows, 1), 0) // tq
    col = jnp.full((rows, 1), _SLOPES[GROUP_SIZE * g + GROUP_SIZE - 1], F32)
    for r in range(GROUP_SIZE - 1):
        col = jnp.where(rblk == r, _SLOPES[GROUP_SIZE * g + r], col)
    return col


def _normalize(acc, l):
    return acc * jnp.where(l > 0.0, 1.0 / jnp.where(l > 0.0, l, 1.0), 0.0)


MASK_BIG = 2.0 ** 100
LOG2E = 1.4426950408889634
AUG = 64
AUG_POS = 32


def _bf16_terms(x, n=3):
    x = np.asarray(x, np.float64)
    terms = []
    for _ in range(n):
        term = x.astype(ml_dtypes.bfloat16).astype(np.float64)
        terms.append(term.astype(np.float32))
        x = x - term
    return terms


def _key_aug(pos, flags=None):
    pos = np.asarray(pos, np.int64)
    aug = np.zeros((pos.shape[0], AUG), np.float32)
    if flags is not None:
        aug[:, :flags.shape[1]] = flags
    coarse = (pos // SEL_BLOCK) * SEL_BLOCK
    aug[:, AUG_POS:AUG_POS + 3] = coarse[:, None]
    aug[:, AUG_POS + 3:AUG_POS + 6] = (pos - coarse)[:, None]
    return aug


def _query_aug_rows(tq):
    rows = np.zeros((KV_GROUPS, AUG - AUG_POS, GROUP_SIZE * tq), np.float32)
    for h in range(NSA_HEADS):
        terms = _bf16_terms(np.float64(_SLOPES[h]) * LOG2E)
        g, r = divmod(h, GROUP_SIZE)
        for i, term in enumerate(terms):
            rows[g, i, r * tq:(r + 1) * tq] = term
            rows[g, 3 + i, r * tq:(r + 1) * tq] = term
    return rows


def _online_update(s, v_t, m, l, acc):
    m_new = jnp.maximum(m, jnp.max(s, axis=0, keepdims=True))
    alpha = jnp.exp2(m - m_new)
    p = jnp.exp2(s - m_new)
    return m_new, alpha * l + jnp.sum(p, axis=0, keepdims=True), alpha * acc + _dot(v_t, p.astype(BF16))


def _nsa_prompt_kernel(q_ref, ks_ref, kw_ref, cmp_ref, gate_ref, gq_ref, gks_ref, gkw_ref, gksc_ref, gkwc_ref,
                       csel_ref, cwin_ref, ccmp_ref, qrows_ref, mmap_ref, eye_ref,
                       o_ref, kst_out_ref, kwt_out_ref,
                       ksa_ref, kwa_ref, vst_ref, vwt_ref, kca_ref, vct_ref, ot_ref, ss_ref, ps_ref, sacc_ref,
                       *, t, tq):
    qi = pl.program_id(1)
    n_sel = t // SEL_BLOCK
    n_cmp = t // CMP_STRIDE
    top_n = min(SEL_TOPN, n_sel)
    cols = GROUP_SIZE * tq
    nt = t // tq
    n_win = WINDOW // tq + 1
    n_pad = WINDOW // tq

    @pl.when(qi == 0)
    def _():
        def prepare(src_ref, grow_ref, gcol_ref, const_ref, ka_ref, vt_ref, out_ref, pad_tiles):
            pad = pad_tiles * tq
            x = src_ref[...]
            xt = jnp.transpose(x)
            width = out_ref.shape[1]
            out_ref[KV_HALF:, :] = xt[KV_HALF:, t - width:]
            for g in range(KV_GROUPS):
                lo, hi = g * HEAD_DIM, (g + 1) * HEAD_DIM
                kn = _rms(x[:, lo:hi], grow_ref[...])
                ka_ref[g, pad:pad + t, :] = jnp.concatenate([kn, const_ref[pad:pad + t, :]], axis=1).astype(BF16)
                if pad:
                    ka_ref[g, 0:pad, :] = jnp.concatenate(
                        [jnp.zeros((pad, HEAD_DIM), F32), const_ref[0:pad, :]], axis=1).astype(BF16)
                kt = xt[lo:hi, :]
                ktn = kt * lax.rsqrt(jnp.mean(kt * kt, axis=0, keepdims=True) + NORM_EPS) * gcol_ref[...]
                out_ref[lo:hi, :] = ktn[:, t - width:]
                vt = xt[KV_HALF + lo:KV_HALF + hi, :].astype(BF16)
                base = g * (nt + pad_tiles)
                for i in range(pad_tiles):
                    vt_ref[base + i] = jnp.zeros((HEAD_DIM, tq), BF16)
                for i in range(nt):
                    vt_ref[base + pad_tiles + i] = vt[:, i * tq:(i + 1) * tq]

        prepare(ks_ref, gks_ref, gksc_ref, csel_ref, ksa_ref, vst_ref, kst_out_ref, 0)
        prepare(kw_ref, gkw_ref, gkwc_ref, cwin_ref, kwa_ref, vwt_ref, kwt_out_ref, n_pad)
        for g in range(KV_GROUPS):
            kc = cmp_ref[0, g * n_cmp:(g + 1) * n_cmp, :]
            kca_ref[g] = jnp.concatenate([kc, ccmp_ref[...]], axis=1).astype(BF16)
            vct_ref[g] = _nt(eye_ref[...], cmp_ref[1, g * n_cmp:(g + 1) * n_cmp, :].astype(BF16)).astype(BF16)

    q0 = pl.multiple_of(qi * tq, tq)
    lane4 = lax.broadcasted_iota(jnp.int32, (1, cols), 1) % tq
    key_off = lax.broadcasted_iota(jnp.int32, (tq, cols), 0)
    causal = key_off <= lane4
    win_old = key_off > lane4
    cmp_ok = (lax.broadcasted_iota(jnp.int32, (n_cmp, cols), 0) * CMP_STRIDE + (CMP_BLOCK - 1)) <= q0 + lane4
    qt = jnp.transpose(q_ref[...])
    gt = jax.nn.sigmoid(jnp.transpose(gate_ref[...]))
    jblk = lax.broadcasted_iota(jnp.int32, (n_sel, tq), 0)
    t_lane = q0 + lax.broadcasted_iota(jnp.int32, (n_sel, tq), 1)
    cur = t_lane // SEL_BLOCK
    forced = (jblk == 0) | (jblk == cur) | (jblk == cur - 1)
    visible = jblk * SEL_BLOCK <= t_lane

    o_c, q_sel, q_win = [], [], []
    for g in range(KV_GROUPS):
        heads = []
        for r in range(GROUP_SIZE):
            h = GROUP_SIZE * g + r
            qh = qt[h * HEAD_DIM:(h + 1) * HEAD_DIM, :]
            qh = qh * lax.rsqrt(jnp.mean(qh * qh, axis=0, keepdims=True) + NORM_EPS)
            heads.append(qh * (gq_ref[...] * (HEAD_DIM ** -0.5 * LOG2E)))
        q_top = jnp.concatenate(heads, axis=1)
        q_low = qrows_ref[g]

        def with_mask_rows(rows_, q_top=q_top, q_low=q_low):
            return jnp.concatenate([q_top, rows_, q_low], axis=0).astype(BF16)

        s = _dot(kca_ref[g], with_mask_rows(jnp.zeros((AUG_POS, cols), F32)))
        s = jnp.where(cmp_ok, s, -MASK_BIG)
        p = jnp.where(cmp_ok, jnp.exp2(s - jnp.max(s, axis=0, keepdims=True)), 0.0)
        pc = _normalize(p, jnp.sum(p, axis=0, keepdims=True))
        o_c.append(_dot(vct_ref[g], pc.astype(BF16)))
        imp = pc[:, 0:tq]
        for r in range(1, GROUP_SIZE):
            imp = imp + pc[:, r * tq:(r + 1) * tq]

        p_slc = _dot(mmap_ref[...], imp, HIGHEST)
        score = jnp.where(forced, FORCE_SCORE, jnp.where(visible, p_slc, -1.0))
        cnt = jnp.zeros((n_sel, tq), jnp.int32)
        for k in range(n_sel):
            sk = score[k:k + 1, :]
            beats = (sk > score) | ((sk == score) & (jblk > k))
            cnt = cnt + beats.astype(jnp.int32)
        drop = jnp.where(cnt < top_n, 0.0, -MASK_BIG)
        if n_sel < AUG_POS:
            drop = jnp.concatenate([drop, jnp.zeros((AUG_POS - n_sel, tq), F32)], axis=0)
        q_sel.append(with_mask_rows(jnp.concatenate([drop] * GROUP_SIZE, axis=1)))
        q_win.append(with_mask_rows(jnp.full((AUG_POS, cols), -MASK_BIG, F32)))

    def sel_scores(tile, slot):
        k0 = pl.multiple_of(tile * tq, tq)
        for g in range(KV_GROUPS):
            ss_ref[slot, g] = _dot(ksa_ref[g, pl.ds(k0, tq), :], q_sel[g])

    def sel_values(tile):
        return [_dot(vst_ref[g * nt + tile], ps_ref[g]) for g in range(KV_GROUPS)]

    def sel_softmax(slot, stats, pv, mask=None):
        new_stats = []
        for g in range(KV_GROUPS):
            m, l = stats[g]
            s = ss_ref[slot, g]
            if mask is not None:
                s = jnp.where(mask, s, -MASK_BIG)
            m_new = jnp.maximum(m, jnp.max(s, axis=0, keepdims=True))
            alpha = jnp.exp2(m - m_new)
            p = jnp.exp2(s - m_new)
            sacc_ref[g] = alpha * (sacc_ref[g] + pv[g])
            ps_ref[g] = p.astype(BF16)
            new_stats.append((m_new, alpha * l + jnp.sum(p, axis=0, keepdims=True)))
        return tuple(new_stats)

    def sel_step(slot, next_tile, prev_tile, stats, mask=None):
        pv = sel_values(jnp.maximum(prev_tile, 0))
        if next_tile is not None:
            sel_scores(next_tile, 1 - slot)
        return sel_softmax(slot, stats, pv, mask)

    def sel_finish(stats):
        pv = sel_values(qi)
        for g in range(KV_GROUPS):
            sacc_ref[g] = _normalize(sacc_ref[g] + pv[g], stats[g][1])

    for g in range(KV_GROUPS):
        ps_ref[g] = jnp.zeros((tq, cols), BF16)
        sacc_ref[g] = jnp.zeros((HEAD_DIM, cols), F32)
    sel_scores(0, 0)

    def sel_pair(j, stats):
        stats = sel_step(0, 2 * j + 1, 2 * j - 1, stats)
        return sel_step(1, 2 * j + 2, 2 * j, stats)

    stats0 = ((jnp.full((1, cols), -MASK_BIG, F32), jnp.zeros((1, cols), F32)),) * KV_GROUPS
    stats = lax.fori_loop(0, qi // 2, sel_pair, stats0)

    @pl.when(qi % 2 == 1)
    def _():
        sel_finish(sel_step(1, None, qi - 1, sel_step(0, qi, qi - 2, stats), mask=causal))

    @pl.when(qi % 2 == 0)
    def _():
        sel_finish(sel_step(0, None, qi - 1, stats, mask=causal))

    o_s = [sacc_ref[g] for g in range(KV_GROUPS)]

    win_scores = [[_dot(kwa_ref[g, pl.ds(q0 + i * tq, tq), :], q_win[g]) for i in range(n_win)]
                  for g in range(KV_GROUPS)]
    o_w = []
    for g in range(KV_GROUPS):
        tiles = win_scores[g]
        tiles[0] = jnp.where(win_old, tiles[0], -MASK_BIG)
        tiles[-1] = jnp.where(causal, tiles[-1], -MASK_BIG)
        m = functools.reduce(jnp.maximum, [jnp.max(s, axis=0, keepdims=True) for s in tiles])
        probs = [jnp.exp2(s - m) for s in tiles]
        l = functools.reduce(jnp.add, [jnp.sum(p, axis=0, keepdims=True) for p in probs])
        acc = functools.reduce(jnp.add, [_dot(vwt_ref[g * (nt + n_pad) + qi + i], p.astype(BF16))
                                         for i, p in enumerate(probs)])
        o_w.append(_normalize(acc, l))

    for h in range(NSA_HEADS):
        g, r = divmod(h, GROUP_SIZE)
        sl = slice(r * tq, (r + 1) * tq)
        ot_ref[h * HEAD_DIM:(h + 1) * HEAD_DIM, :] = (
            gt[h:h + 1, :] * o_c[g][:, sl] + gt[NSA_HEADS + h:NSA_HEADS + h + 1, :] * o_s[g][:, sl]
            + gt[2 * NSA_HEADS + h:2 * NSA_HEADS + h + 1, :] * o_w[g][:, sl])
    o_ref[...] = jnp.transpose(ot_ref[...]).astype(o_ref.dtype)


def nsa_prompt_attention(proj, gates, cmp, qk_gain, *, batch, t, tq):
    n_sel = t // SEL_BLOCK
    n_cmp = t // CMP_STRIDE
    nq = t // tq
    nt = t // tq
    n_pad = WINDOW // tq
    keep = min(WINDOW, t)
    assert n_sel <= AUG_POS and t % tq == 0 and WINDOW % tq == 0 and tq % SEL_BLOCK == 0
    n = np.arange(n_cmp)
    mmap = np.zeros((n_sel, n_cmp), np.float32)
    real = n < n_cmp - 1
    np.add.at(mmap, (n[real] // SEL_CHUNKS, n[real]), 1.0)
    nxt = real & ((n + 1) // SEL_CHUNKS < n_sel)
    np.add.at(mmap, ((n[nxt] + 1) // SEL_CHUNKS, n[nxt]), 1.0)
    pos = np.arange(t)
    csel = _key_aug(pos, (pos[:, None] // SEL_BLOCK == np.arange(n_sel)[None, :]).astype(np.float32))
    wpos = np.arange(-WINDOW, t)
    cwin = _key_aug(wpos, (wpos < 0).astype(np.float32)[:, None])
    ccmp = _key_aug(np.arange(n_cmp) * CMP_STRIDE + (CMP_BLOCK - 1))
    const = lambda shape: pl.BlockSpec(shape, lambda b, i: (0,) * len(shape))
    col = lambda v: v.reshape(HEAD_DIM, 1)
    row = lambda v: v.reshape(1, HEAD_DIM)
    return pl.pallas_call(
        functools.partial(_nsa_prompt_kernel, t=t, tq=tq),
        grid=(batch, nq),
        in_specs=[
            pl.BlockSpec((tq, NSA_Q_DIM), lambda b, i: (b * nq + i, 0)),
            pl.BlockSpec((t, NSA_KV_DIM), lambda b, i: (b, 3), pipeline_mode=pl.Buffered(1)),
            pl.BlockSpec((t, NSA_KV_DIM), lambda b, i: (b, 4), pipeline_mode=pl.Buffered(1)),
            pl.BlockSpec((None, 2, KV_GROUPS * n_cmp, HEAD_DIM), lambda b, i: (b, 0, 0, 0)),
            pl.BlockSpec((tq, LANES), lambda b, i: (b * nq + i, 0)),
            const((HEAD_DIM, 1)), const((1, HEAD_DIM)), const((1, HEAD_DIM)), const((HEAD_DIM, 1)),
            const((HEAD_DIM, 1)),
            const((t, AUG)), const((t + WINDOW, AUG)), const((n_cmp, AUG)),
            const((KV_GROUPS, AUG - AUG_POS, GROUP_SIZE * tq)), const((n_sel, n_cmp)), const((HEAD_DIM, HEAD_DIM)),
        ],
        out_specs=[
            pl.BlockSpec((tq, NSA_Q_DIM), lambda b, i: (b * nq + i, 0)),
            pl.BlockSpec((None, NSA_KV_DIM, t), lambda b, i: (b, 0, 0)),
            pl.BlockSpec((None, NSA_KV_DIM, keep), lambda b, i: (b, 0, 0)),
        ],
        out_shape=[jax.ShapeDtypeStruct((batch * t, NSA_Q_DIM), BF16),
                   jax.ShapeDtypeStruct((batch, NSA_KV_DIM, t), F32),
                   jax.ShapeDtypeStruct((batch, NSA_KV_DIM, keep), F32)],
        scratch_shapes=[
            pltpu.VMEM((KV_GROUPS, t, HEAD_DIM + AUG), BF16),
            pltpu.VMEM((KV_GROUPS, t + WINDOW, HEAD_DIM + AUG), BF16),
            pltpu.VMEM((KV_GROUPS * nt, HEAD_DIM, tq), BF16),
            pltpu.VMEM((KV_GROUPS * (nt + n_pad), HEAD_DIM, tq), BF16),
            pltpu.VMEM((KV_GROUPS, n_cmp, HEAD_DIM + AUG), BF16),
            pltpu.VMEM((KV_GROUPS, HEAD_DIM, n_cmp), BF16),
            pltpu.VMEM((NSA_Q_DIM, tq), F32),
            pltpu.VMEM((2, KV_GROUPS, tq, GROUP_SIZE * tq), F32),
            pltpu.VMEM((KV_GROUPS, tq, GROUP_SIZE * tq), BF16),
            pltpu.VMEM((KV_GROUPS, HEAD_DIM, GROUP_SIZE * tq), F32),
        ],
        compiler_params=pltpu.CompilerParams(dimension_semantics=("parallel", "arbitrary"),
                                             vmem_limit_bytes=62 * 1024 * 1024),
        name="nsa_prompt_attention",
    )(proj, proj, proj, cmp, gates, col(qk_gain[0]), row(qk_gain[2]), row(qk_gain[3]), col(qk_gain[2]),
      col(qk_gain[3]), jnp.asarray(csel), jnp.asarray(cwin), jnp.asarray(ccmp), jnp.asarray(_query_aug_rows(tq)),
      jnp.asarray(mmap), jnp.eye(HEAD_DIM, dtype=BF16))


def _nsa_weights(w_in, qk_gain, cmp_pe, cmp_w1, cmp_w2, w_out):
    n_main = NSA_Q_DIM + 3 * NSA_KV_DIM
    zeros_kv = jnp.zeros((KV_HALF,), F32)
    ones_kv = jnp.ones((KV_HALF,), F32)
    head_gain = jnp.concatenate([
        jnp.tile(qk_gain[0], NSA_HEADS), zeros_kv, zeros_kv,
        jnp.tile(qk_gain[2], KV_GROUPS), zeros_kv, jnp.tile(qk_gain[3], KV_GROUPS), zeros_kv]).reshape(1, n_main)
    head_mask = jnp.concatenate([
        jnp.ones((NSA_Q_DIM,), F32), zeros_kv, zeros_kv, ones_kv, zeros_kv, ones_kv, zeros_kv]).reshape(1, n_main)
    w_gate = jnp.pad(w_in[:, n_main:], ((0, 0), (0, LANES - 3 * NSA_HEADS)))
    return dict(
        w_main=w_in[:, :n_main].astype(BF16), w_gate=w_gate.astype(BF16), head_gain=head_gain, head_mask=head_mask,
        pe=cmp_pe.reshape(2, 2, CMP_STRIDE // CMP_PACK, CMP_PACK * HEAD_DIM),
        w1=cmp_w1.reshape(2, 2, CMP_STRIDE // CMP_PACK, CMP_PACK * HEAD_DIM, CMP_HIDDEN).astype(BF16),
        w2=cmp_w2.astype(BF16), kgain=qk_gain[1], qk_gain=qk_gain, w_out=w_out.astype(BF16))


def _nsa_project(x, gain, w, *, tm, head_norm):
    if head_norm:
        proj = norm_matmul(x, gain, w["w_main"], w["head_gain"], w["head_mask"], tm=tm, tn=512)
    else:
        proj = norm_matmul(x, gain, w["w_main"], tm=tm, tn=1280)
    return proj, norm_matmul(x, gain, w["w_gate"], tm=tm, tn=LANES)


def nsa_prompt_layer(x, gain, w, *, batch, t):
    tm = min(1024, batch * t)
    proj, gates = _nsa_project(x, gain, w, tm=min(2048, batch * t), head_norm=False)
    cmp, cmp_t = compress_rows(proj, 2, w["pe"], w["w1"], w["w2"], w["kgain"], batch=batch, t=t)
    o, sel_t, win_t = nsa_prompt_attention(proj, gates, cmp, w["qk_gain"], batch=batch, t=t, tq=256)
    return matmul_residual(o, w["w_out"], x, tm=tm), cmp_t, sel_t, win_t


PAGES_PER_STEP = 16
CHUNKS_PER_PAGE = PAGE_SIZE // CMP_STRIDE
PAGE_PITCH = CMP_PITCH


def _compress_pages_kernel(pt_ref, *refs, n_chunk):
    del pt_ref
    pages = refs[:PAGES_PER_STEP]
    pe_ref, w1_ref, w2_ref, kg_ref, o_ref, xs_ref, pre_ref = refs[PAGES_PER_STEP:]
    s = pl.program_id(1)
    n_pair = NSA_KV_DIM // LANES
    step_chunks = PAGES_PER_STEP * CHUNKS_PER_PAGE
    for i, page in enumerate(pages):
        for c in range(n_pair):
            rows = jnp.transpose(page[c * LANES:(c + 1) * LANES, :])
            for k in range(CHUNKS_PER_PAGE):
                at = (i * CHUNKS_PER_PAGE + k) * PAGE_PITCH
                xs_ref[c, at:at + CMP_STRIDE, :] = rows[k * CMP_STRIDE:(k + 1) * CMP_STRIDE]
    for kv in range(2):
        acc = _compress_accumulate(xs_ref, step_chunks, pe_ref, w1_ref, kv, PAGE_PITCH)
        for m in range(2):
            for g in range(KV_GROUPS):
                pre_ref[kv, m, pl.ds(pl.multiple_of(g * n_chunk + s * step_chunks, step_chunks), step_chunks), :] = (
                    acc[m][g * step_chunks:(g + 1) * step_chunks])

    @pl.when(s == pl.num_programs(1) - 1)
    def _():
        for kv in range(2):
            o_ref[kv] = _compress_finish([pre_ref[kv, 0], pre_ref[kv, 1]], w2_ref, kg_ref, kv)


def compress_pages(cache_t, layer, page_table_flat, pe, w1, w2, kgain, *, batch, n_pages):
    n_chunk = n_pages * CHUNKS_PER_PAGE
    n_steps = n_pages // PAGES_PER_STEP

    def page_spec(i):
        return pl.BlockSpec((None, None, NSA_KV_DIM, PAGE_SIZE),
                            lambda b, s, pt: (layer, pt[b * n_pages + s * PAGES_PER_STEP + i], 0, 0))

    grid_spec = pltpu.PrefetchScalarGridSpec(
        num_scalar_prefetch=1,
        grid=(batch, n_steps),
        in_specs=[page_spec(i) for i in range(PAGES_PER_STEP)] + [
            pl.BlockSpec(pe.shape, lambda b, s, pt: (0, 0, 0, 0)),
            pl.BlockSpec(w1.shape, lambda b, s, pt: (0, 0, 0, 0, 0)),
            pl.BlockSpec(w2.shape, lambda b, s, pt: (0, 0, 0)),
            pl.BlockSpec((1, HEAD_DIM), lambda b, s, pt: (0, 0)),
        ],
        out_specs=pl.BlockSpec((None, 2, KV_GROUPS * n_chunk, HEAD_DIM), lambda b, s, pt: (b, 0, 0, 0)),
        scratch_shapes=[pltpu.VMEM((NSA_KV_DIM // LANES, PAGES_PER_STEP * CHUNKS_PER_PAGE * PAGE_PITCH, LANES), F32),
                        pltpu.VMEM((2, 2, KV_GROUPS * n_chunk, CMP_HIDDEN), F32)],
    )
    return pl.pallas_call(
        functools.partial(_compress_pages_kernel, n_chunk=n_chunk),
        grid_spec=grid_spec,
        out_shape=jax.ShapeDtypeStruct((batch, 2, KV_GROUPS * n_chunk, HEAD_DIM), F32),
        compiler_params=_cparams(("parallel", "arbitrary")),
        name="compress_pages",
    )(page_table_flat, *([cache_t] * PAGES_PER_STEP), pe, w1, w2, kgain.reshape(1, HEAD_DIM))


Q_PAD = 8
HEAD_PAD = 16


def _nsa_sample_select_kernel(q_ref, cmp_ref, mmap_ref, oc_ref, idx_ref, score_ref, *, past_len, n_cmp, n_sel):
    rows = GROUP_SIZE * Q_PAD
    n_selp = score_ref.shape[0]
    t_max = past_len + Q_PAD - 1
    row = lax.broadcasted_iota(jnp.int32, (rows, 1), 0)
    t_col = past_len + row % Q_PAD
    jblk = lax.broadcasted_iota(jnp.int32, (n_selp, Q_PAD), 0)
    t_lane = past_len + lax.broadcasted_iota(jnp.int32, (n_selp, Q_PAD), 1)
    cur = t_lane // SEL_BLOCK
    forced = (jblk == 0) | (jblk == cur) | (jblk == cur - 1)
    visible = jblk * SEL_BLOCK <= t_lane
    for g in range(KV_GROUPS):
        qg = (q_ref[g] * (HEAD_DIM ** -0.5)).astype(BF16)
        slope = _slope_col(g, rows, Q_PAD)
        kc = cmp_ref[0, g * n_cmp:(g + 1) * n_cmp, :].astype(BF16)
        vc = cmp_ref[1, g * n_cmp:(g + 1) * n_cmp, :].astype(BF16)
        kpos = lax.broadcasted_iota(jnp.int32, (1, n_cmp), 1) * CMP_STRIDE + (CMP_BLOCK - 1)
        s = _nt(qg, kc) + slope * (kpos - t_max).astype(F32)
        mask = kpos <= t_col
        s = jnp.where(mask, s, NEG_INF)
        p = jnp.where(mask, jnp.exp(s - jnp.max(s, axis=1, keepdims=True)), 0.0)
        pc = _normalize(p, jnp.sum(p, axis=1, keepdims=True))
        oc_ref[g] = _dot(pc.astype(BF16), vc)
        imp = pc[0:Q_PAD]
        for r in range(1, GROUP_SIZE):
            imp = imp + pc[r * Q_PAD:(r + 1) * Q_PAD]
        imp_hi = imp.astype(BF16)
        imp_rest = imp - imp_hi.astype(F32)
        imp_mid = imp_rest.astype(BF16)
        imp_lo = (imp_rest - imp_mid.astype(F32)).astype(BF16)
        p_slc = (_nt(mmap_ref[...], imp_hi) + _nt(mmap_ref[...], imp_mid)) + _nt(mmap_ref[...], imp_lo)
        score = jnp.where(forced, FORCE_SCORE, jnp.where(visible, p_slc, -1.0))
        score_ref[:, g * Q_PAD:(g + 1) * Q_PAD] = jnp.where(jblk < n_sel, score, -2.0)

    score = score_ref[...]
    jall = lax.broadcasted_iota(jnp.int32, score.shape, 0)

    def rank_body(k, cnt):
        sk = score_ref[pl.ds(k, 1), :]
        beats = (sk > score) | ((sk == score) & (jall > k))
        return cnt + beats.astype(jnp.int32)

    cnt = lax.fori_loop(0, n_sel, rank_body, jnp.zeros(score.shape, jnp.int32), unroll=4)
    slots = [jnp.sum(jnp.where(cnt == sl, jall, 0), axis=0, keepdims=True) for sl in range(SEL_TOPN)]
    idx_ref[...] = jnp.concatenate(slots, axis=0)


def nsa_sample_select(q_stack, cmp, *, batch, past_len, t_new):
    n_cmp = past_len // CMP_STRIDE
    n_sel = -(-(past_len + t_new) // SEL_BLOCK)
    n_selp = -(-n_sel // 8) * 8
    n = np.arange(n_cmp)
    mmap = np.zeros((n_selp, n_cmp), np.float32)
    real = n < n_cmp - 1
    np.add.at(mmap, (n[real] // SEL_CHUNKS, n[real]), 1.0)
    np.add.at(mmap, ((n[real] + 1) // SEL_CHUNKS, n[real]), 1.0)
    rows = GROUP_SIZE * Q_PAD
    return pl.pallas_call(
        functools.partial(_nsa_sample_select_kernel, past_len=past_len, n_cmp=n_cmp, n_sel=n_sel),
        grid=(batch,),
        in_specs=[
            pl.BlockSpec((None, KV_GROUPS, rows, HEAD_DIM), lambda b: (b, 0, 0, 0)),
            pl.BlockSpec((None, 2, KV_GROUPS * n_cmp, HEAD_DIM), lambda b: (b, 0, 0, 0)),
            pl.BlockSpec((n_selp, n_cmp), lambda b: (0, 0)),
        ],
        out_specs=[
            pl.BlockSpec((None, KV_GROUPS, rows, HEAD_DIM), lambda b: (b, 0, 0, 0)),
            pl.BlockSpec((None, SEL_TOPN, KV_GROUPS * Q_PAD), lambda b: (b, 0, 0)),
        ],
        out_shape=[jax.ShapeDtypeStruct((batch, KV_GROUPS, rows, HEAD_DIM), F32),
                   jax.ShapeDtypeStruct((batch, SEL_TOPN, KV_GROUPS * Q_PAD), jnp.int32)],
        scratch_shapes=[pltpu.VMEM((n_selp, KV_GROUPS * Q_PAD), F32)],
        compiler_params=_cparams(("parallel",)),
        name="nsa_sample_select",
    )(q_stack, cmp, jnp.asarray(mmap, BF16))


def _nsa_sample_attend_kernel(pt_ref, idx_ref, cache_ref, q_ref, slope_ref, snew_ref, win_ref, wnew_ref, wnew_t_ref,
                              oc_ref, gate_ref, o_ref, wout_ref, kcat_ref, vcat_ref, page_ref, sem_ref,
                              *, layer, past_len, t_new, n_past_blocks, n_steps):
    b, g = pl.program_id(0), pl.program_id(1)
    n_pages = past_len // PAGE_SIZE
    n_copies = t_new * SEL_TOPN
    step = b * KV_GROUPS + g

    def page_copy(src_page, src_group, slot, n):
        return pltpu.make_async_copy(cache_ref.at[layer, src_page, :, src_group], page_ref.at[slot, n],
                                     sem_ref.at[slot, n])

    def start_gather(at_step, slot):
        at_b = at_step // KV_GROUPS
        at_g = at_step % KV_GROUPS
        for n in range(n_copies):
            blk = jnp.minimum(idx_ref[at_step * n_copies + n], n_past_blocks - 1)
            page_copy(pt_ref[at_b * n_pages + blk // 2], at_g, slot, n).start()

    @pl.when(step == 0)
    def _():
        start_gather(step, 0)

    @pl.when(step + 1 < n_steps)
    def _():
        start_gather(step + 1, (step + 1) % 2)

    slot = step % 2
    for n in range(n_copies):
        page_copy(0, 0, slot, n).wait()

    slope = slope_ref[:, 0:1]
    lane = lax.broadcasted_iota(jnp.int32, (1, LANES), 1)
    new_pos = past_len + lane
    buf = win_ref.shape[2]
    wpos = past_len - buf + lax.broadcasted_iota(jnp.int32, (1, buf), 1)
    k_new, v_new = snew_ref[0].astype(BF16), snew_ref[1].astype(BF16)
    kw_new, vw_new = wnew_ref[0].astype(BF16), wnew_ref[1].astype(BF16)
    kw_old, vw_old = win_ref[0].astype(BF16), win_ref[1].astype(BF16)

    def attend(tiles, v_last):
        m = jnp.full((HEAD_PAD, 1), NEG_INF, F32)
        for s, mask, _ in tiles:
            m = jnp.maximum(m, jnp.max(jnp.where(mask, s, NEG_INF), axis=1, keepdims=True))
        l = jnp.zeros((HEAD_PAD, 1), F32)
        acc = jnp.zeros((HEAD_PAD, HEAD_DIM), F32)
        for i, (s, mask, v) in enumerate(tiles):
            p = jnp.where(mask, jnp.exp(jnp.where(mask, s, NEG_INF) - m), 0.0)
            l = l + jnp.sum(p, axis=1, keepdims=True)
            if i == len(tiles) - 1:
                acc = acc + _dot(p.astype(BF16), v_last)
            else:
                acc = acc + _nt(p.astype(BF16), v)
        return _normalize(acc, l)

    for qi in range(t_new):
        t = past_len + qi
        q = (q_ref[qi] * (HEAD_DIM ** -0.5)).astype(BF16)
        new_ok = (lane < t_new) & (new_pos <= t)

        kpos, mask = [], []
        for sl in range(SEL_TOPN):
            n = qi * SEL_TOPN + sl
            blk = idx_ref[step * n_copies + n]
            kcat_ref[qi, :, sl * PAGE_SIZE:(sl + 1) * PAGE_SIZE] = page_ref[slot, n, 0].astype(BF16)
            vcat_ref[qi, :, sl * PAGE_SIZE:(sl + 1) * PAGE_SIZE] = page_ref[slot, n, 1].astype(BF16)
            kpos.append((blk // 2) * PAGE_SIZE + lane)
            mask.append((lane // SEL_BLOCK == blk % 2) & (blk < n_past_blocks) & (kpos[-1] <= t))
        kpos = jnp.concatenate(kpos, axis=1)
        mask = jnp.concatenate([m_.astype(jnp.int32) for m_ in mask], axis=1) > 0
        s_sel = _dot(q, kcat_ref[qi]) + slope * (kpos - t).astype(F32)
        s_new = _nt(q, k_new) + slope * (new_pos - t).astype(F32)
        o_s = attend([(s_sel, mask, vcat_ref[qi]), (s_new, new_ok, None)], v_new)

        dist = t - wpos
        s_w = _dot(q, kw_old) + slope * (wpos - t).astype(F32)
        s_wn = _nt(q, kw_new) + slope * (new_pos - t).astype(F32)
        o_w = attend([(s_w, (dist >= 0) & (dist < WINDOW), vw_old),
                      (s_wn, new_ok & (t - new_pos < WINDOW), None)], vw_new)

        gates = jax.nn.sigmoid(gate_ref[qi])
        o_ref[qi] = gates[:, 0:1] * oc_ref[qi] + gates[:, 1:2] * o_s + gates[:, 2:3] * o_w

    for kv in range(2):
        rolled = pltpu.roll(win_ref[kv], buf - t_new, axis=1)
        wout_ref[kv, :, 0:buf - LANES] = rolled[:, 0:buf - LANES]
        wout_ref[kv, :, buf - LANES:] = jnp.where(lane >= LANES - t_new, wnew_t_ref[kv], rolled[:, buf - LANES:])


def nsa_sample_attend(cache_t, win_t, layer, page_table_flat, idx_flat, q_heads, slopes, sel_new, win_new,
                      win_new_t, o_c, gates, *, batch, past_len, t_new):
    n_past_blocks = past_len // SEL_BLOCK
    buf = win_t.shape[-1]
    per_bg = lambda shape: pl.BlockSpec((None, None) + shape, lambda b, g, pt, idx: (b, g) + (0,) * len(shape))
    grid_spec = pltpu.PrefetchScalarGridSpec(
        num_scalar_prefetch=2,
        grid=(batch, KV_GROUPS),
        in_specs=[
            pl.BlockSpec(memory_space=pl.ANY),
            per_bg((t_new, HEAD_PAD, HEAD_DIM)),
            pl.BlockSpec((None, HEAD_PAD, LANES), lambda b, g, pt, idx: (g, 0, 0)),
            per_bg((2, LANES, HEAD_DIM)),
            pl.BlockSpec((None, None, 2, None, HEAD_DIM, buf), lambda b, g, pt, idx: (layer, b, 0, g, 0, 0)),
            per_bg((2, LANES, HEAD_DIM)),
            per_bg((2, HEAD_DIM, LANES)),
            per_bg((t_new, HEAD_PAD, HEAD_DIM)),
            per_bg((t_new, HEAD_PAD, LANES)),
        ],
        out_specs=[
            per_bg((t_new, HEAD_PAD, HEAD_DIM)),
            pl.BlockSpec((None, 2, None, HEAD_DIM, buf), lambda b, g, pt, idx: (b, 0, g, 0, 0)),
        ],
        scratch_shapes=[pltpu.VMEM((t_new, HEAD_DIM, SEL_TOPN * PAGE_SIZE), BF16)] * 2 + [
            pltpu.VMEM((2, t_new * SEL_TOPN, 2, HEAD_DIM, PAGE_SIZE), F32),
            pltpu.SemaphoreType.DMA((2, t_new * SEL_TOPN)),
        ],
    )
    return pl.pallas_call(
        functools.partial(_nsa_sample_attend_kernel, layer=layer, past_len=past_len, t_new=t_new,
                          n_past_blocks=n_past_blocks, n_steps=batch * KV_GROUPS),
        grid_spec=grid_spec,
        out_shape=[jax.ShapeDtypeStruct((batch, KV_GROUPS, t_new, HEAD_PAD, HEAD_DIM), F32),
                   jax.ShapeDtypeStruct((batch, 2, KV_GROUPS, HEAD_DIM, buf), F32)],
        compiler_params=_cparams(("arbitrary", "arbitrary")),
        name="nsa_sample_attend",
    )(page_table_flat, idx_flat, cache_t, q_heads, slopes, sel_new, win_t, win_new, win_new_t, o_c, gates)


def nsa_sample_layer(x, gain, w, cmp_t, sel_t, win_t, layer, page_table_flat, *, batch, t_new, past_len):
    m = batch * t_new
    proj, gates = _nsa_project(x, gain, w, tm=m, head_norm=True)
    cmp = compress_pages(cmp_t.reshape(cmp_t.shape[0], cmp_t.shape[1], NSA_KV_DIM, PAGE_SIZE), layer,
                         page_table_flat, w["pe"], w["w1"], w["w2"], w["kgain"], batch=batch,
                         n_pages=past_len // PAGE_SIZE)
    q = proj[:, :NSA_Q_DIM].reshape(batch, t_new, KV_GROUPS, GROUP_SIZE, HEAD_DIM)
    q_stack = jnp.pad(q.transpose(0, 2, 3, 1, 4), ((0, 0), (0, 0), (0, 0), (0, Q_PAD - t_new), (0, 0)))
    o_c, idx = nsa_sample_select(q_stack.reshape(batch, KV_GROUPS, GROUP_SIZE * Q_PAD, HEAD_DIM), cmp,
                                 batch=batch, past_len=past_len, t_new=t_new)
    pad_heads = ((0, 0), (0, 0), (0, 0), (0, HEAD_PAD - GROUP_SIZE), (0, 0))
    o_c = o_c.reshape(batch, KV_GROUPS, GROUP_SIZE, Q_PAD, HEAD_DIM)[:, :, :, :t_new].transpose(0, 1, 3, 2, 4)
    idx_flat = idx.reshape(batch, SEL_TOPN, KV_GROUPS, Q_PAD)[..., :t_new].transpose(0, 2, 3, 1).reshape(-1)
    q_heads = jnp.pad(q.transpose(0, 2, 1, 3, 4), pad_heads)

    def new_rows(cols):
        a = cols.reshape(batch, t_new, 2, KV_GROUPS, HEAD_DIM).transpose(0, 3, 2, 1, 4)
        return jnp.pad(a, ((0, 0), (0, 0), (0, 0), (0, LANES - t_new), (0, 0)))

    kvs = proj[:, NSA_Q_DIM + NSA_KV_DIM:NSA_Q_DIM + 2 * NSA_KV_DIM]
    kvw = proj[:, NSA_Q_DIM + 2 * NSA_KV_DIM:]
    win_new_t = kvw.reshape(batch, t_new, 2, KV_GROUPS, HEAD_DIM).transpose(0, 3, 2, 4, 1)
    win_new_t = jnp.pad(win_new_t, ((0, 0), (0, 0), (0, 0), (0, 0), (LANES - t_new, 0)))
    g_arr = gates[:, :3 * NSA_HEADS].reshape(batch, t_new, 3, KV_GROUPS, GROUP_SIZE).transpose(0, 3, 1, 4, 2)
    g_arr = jnp.pad(g_arr, ((0, 0), (0, 0), (0, 0), (0, HEAD_PAD - GROUP_SIZE), (0, LANES - 3)))
    slopes = np.zeros((KV_GROUPS, HEAD_PAD, LANES), np.float32)
    slopes[:, :GROUP_SIZE, :] = np.asarray(_SLOPES, np.float32).reshape(KV_GROUPS, GROUP_SIZE, 1)
    o, win_out = nsa_sample_attend(sel_t, win_t, layer, page_table_flat, idx_flat, q_heads, jnp.asarray(slopes),
                                   new_rows(kvs), new_rows(kvw), win_new_t, jnp.pad(o_c, pad_heads), g_arr,
                                   batch=batch, past_len=past_len, t_new=t_new)
    o = o[:, :, :, :GROUP_SIZE].transpose(0, 2, 1, 3, 4).reshape(m, NSA_Q_DIM)
    return matmul_residual(o.astype(BF16), w["w_out"], x, tm=m), proj, win_out


def _expand_heads(x, e):
    x1 = x.astype(BF16)
    rest = x - x1.astype(F32)
    x2 = rest.astype(BF16)
    x3 = (rest - x2.astype(F32)).astype(BF16)
    return (_dot(x1, e) + _dot(x2, e)) + _dot(x3, e)


def _ssd_kernel(z_ref, x_ref, bc_ref, dt_ref, cw_ref, cb_ref, dtb_ref, alog_ref, drow_ref, nw_ref, cinit_ref,
                hinit_ref, tril_ref, e_ref, y_ref, hout_ref, stage_ref, h_ref, yacc_ref, *, t_valid):
    L = SSM_CHUNK
    c = pl.program_id(1)

    @pl.when(c == 0)
    def _():
        stage_ref[...] = cinit_ref[...]
        h_ref[...] = hinit_ref[...]

    cur = jnp.concatenate([x_ref[...], bc_ref[...]], axis=1)
    prev = stage_ref[...]
    first = lax.broadcasted_iota(jnp.int32, (8, 1), 0)
    conv = cb_ref[...] + cur * cw_ref[SSM_CONV - 1:SSM_CONV, :]
    for s in range(1, SSM_CONV):
        rolled = pltpu.roll(cur, s, axis=0)
        head = jnp.where(first < s, pltpu.roll(prev, s, axis=0), rolled[0:8])
        conv = conv + jnp.concatenate([head, rolled[8:]], axis=0) * cw_ref[SSM_CONV - 1 - s:SSM_CONV - s, :]
    stage_ref[...] = cur[L - 8:L]
    conv = conv * jax.nn.sigmoid(conv)
    xh = conv[:, :SSM_D_INNER]

    row = lax.broadcasted_iota(jnp.int32, (L, LANES), 0)
    dt_in = dt_ref[...] + dtb_ref[...]
    dt = jnp.maximum(dt_in, 0.0) + jnp.log1p(jnp.exp(-jnp.abs(dt_in)))
    dt = jnp.where(row < t_valid, dt, 0.0)
    da = dt * -jnp.exp(alog_ref[...])
    acum = _dot(tril_ref[...], da, HIGHEST)
    acum_t = jnp.transpose(acum)
    dt_full = _expand_heads(dt, e_ref[...])
    acum_full = _expand_heads(acum, e_ref[...])
    grow_full = jnp.exp(acum_full)
    toend_full = jnp.exp(acum_full[L - 1:L, :] - acum_full)
    last_t = jnp.exp(acum_t[:, L - 1:L])
    xdt = xh * dt_full
    causal = lax.broadcasted_iota(jnp.int32, (L, L), 0) >= lax.broadcasted_iota(jnp.int32, (L, L), 1)

    for g in range(SSM_GROUPS):
        gl, gh = g * SSM_GW, (g + 1) * SSM_GW
        bm = conv[:, SSM_D_INNER + g * SSM_STATE:SSM_D_INNER + (g + 1) * SSM_STATE].astype(BF16)
        cm = conv[:, SSM_D_INNER + (SSM_GROUPS + g) * SSM_STATE:
                  SSM_D_INNER + (SSM_GROUPS + g + 1) * SSM_STATE].astype(BF16)
        cb = _nt(cm, bm)
        xdt_g = xdt[:, gl:gh]
        for r in range(SSM_HPG):
            h = g * SSM_HPG + r
            seg = acum[:, h:h + 1] - acum_t[h:h + 1, :]
            decay = jnp.exp(jnp.where(causal, seg, NEG_INF))
            yacc_ref[:, gl + r * SSM_HEAD_DIM:gl + (r + 1) * SSM_HEAD_DIM] = _dot(
                (cb * decay).astype(BF16), xdt_g[:, r * SSM_HEAD_DIM:(r + 1) * SSM_HEAD_DIM].astype(BF16))
        h_g = h_ref[gl:gh, :]
        y_state = _nt(cm, h_g.astype(BF16)) * grow_full[:, gl:gh]
        yacc_ref[:, gl:gh] = yacc_ref[:, gl:gh] + y_state
        xw = (xdt_g * toend_full[:, gl:gh]).astype(BF16)
        h_add = lax.dot_general(xw, bm, (((0,), (0,)), ((), ())), preferred_element_type=F32)
        for r in range(SSM_HPG):
            h = g * SSM_HPG + r
            rl, rh = r * SSM_HEAD_DIM, (r + 1) * SSM_HEAD_DIM
            keep = jnp.broadcast_to(last_t[h:h + 1, :], (SSM_HEAD_DIM, SSM_STATE))
            h_ref[gl + rl:gl + rh, :] = h_g[rl:rh] * keep + h_add[rl:rh]

    zf = z_ref[...]
    gated = (yacc_ref[...] + xh * drow_ref[...]) * (zf * jax.nn.sigmoid(zf))
    for g in range(SSM_GROUPS):
        gl, gh = g * SSM_GW, (g + 1) * SSM_GW
        y_ref[:, gl:gh] = _rms(gated[:, gl:gh], nw_ref[:, gl:gh]).astype(y_ref.dtype)

    @pl.when(c == pl.num_programs(1) - 1)
    def _():
        hout_ref[...] = h_ref[...]


def ssd_mixer(proj, dt_raw, conv_init, h_init, w, *, batch, n_chunks, t_valid):
    L = SSM_CHUNK
    rows = batch * n_chunks * L
    conv_dim = SSM_D_INNER + SSM_BC
    tril = np.tril(np.ones((L, L), np.float32))
    e = np.zeros((LANES, SSM_D_INNER), np.float32)
    e[np.arange(SSM_D_INNER) // SSM_HEAD_DIM, np.arange(SSM_D_INNER)] = 1.0
    const = lambda shape: pl.BlockSpec(shape, lambda b, c: (0,) * len(shape))
    return pl.pallas_call(
        functools.partial(_ssd_kernel, t_valid=t_valid),
        grid=(batch, n_chunks),
        in_specs=[
            pl.BlockSpec((L, SSM_D_INNER), lambda b, c: (b * n_chunks + c, 0)),
            pl.BlockSpec((L, SSM_D_INNER), lambda b, c: (b * n_chunks + c, 1)),
            pl.BlockSpec((L, SSM_BC), lambda b, c: (b * n_chunks + c, 2 * SSM_D_INNER // SSM_BC)),
            pl.BlockSpec((L, LANES), lambda b, c: (b * n_chunks + c, 0)),
            const((SSM_CONV, conv_dim)), const((1, conv_dim)), const((1, LANES)), const((1, LANES)),
            const((1, SSM_D_INNER)), const((1, SSM_D_INNER)),
            pl.BlockSpec((None, 8, conv_dim), lambda b, c: (b, 0, 0)),
            pl.BlockSpec((None, SSM_D_INNER, SSM_STATE), lambda b, c: (b, 0, 0)),
            const((L, L)), const((LANES, SSM_D_INNER)),
        ],
        out_specs=[
            pl.BlockSpec((L, SSM_D_INNER), lambda b, c: (b * n_chunks + c, 0)),
            pl.BlockSpec((None, SSM_D_INNER, SSM_STATE), lambda b, c: (b, 0, 0)),
        ],
        out_shape=[jax.ShapeDtypeStruct((rows, SSM_D_INNER), BF16),
                   jax.ShapeDtypeStruct((batch, SSM_D_INNER, SSM_STATE), F32)],
        scratch_shapes=[pltpu.VMEM((8, conv_dim), F32), pltpu.VMEM((SSM_D_INNER, SSM_STATE), F32),
                        pltpu.VMEM((L, SSM_D_INNER), F32)],
        compiler_params=_cparams(("parallel", "arbitrary")),
        name="ssd_mixer",
    )(proj, proj, proj, dt_raw, w["conv_w"], w["conv_b"], w["dt_bias"], w["a_log"], w["d_row"], w["norm_w"],
      conv_init, h_init, jnp.asarray(tril), jnp.asarray(e, BF16))


def _ssd_weights(w_in, conv_w, conv_b, dt_bias, a_log, d_skip, norm_w, w_out):
    n_main = 2 * SSM_D_INNER + SSM_BC
    pad = lambda v: jnp.pad(v, (0, LANES - SSM_HEADS)).reshape(1, LANES)
    return dict(
        w_main=w_in[:, :n_main].astype(BF16),
        w_dt=jnp.pad(w_in[:, n_main:], ((0, 0), (0, LANES - SSM_HEADS))).astype(BF16),
        conv_w=conv_w, conv_b=conv_b.reshape(1, -1), dt_bias=pad(dt_bias), a_log=pad(a_log),
        d_row=jnp.repeat(d_skip, SSM_HEAD_DIM).reshape(1, SSM_D_INNER), norm_w=norm_w.reshape(1, SSM_D_INNER),
        w_out=w_out.astype(BF16))


def ssd_layer(x, gain, w, conv_state, h_init, *, batch, t):
    tm = min(1024, x.shape[0])
    proj = norm_matmul(x, gain, w["w_main"], tm=min(2048, x.shape[0]), tn=1024)
    dt_raw = norm_matmul(x, gain, w["w_dt"], tm=tm, tn=LANES)
    conv_init = jnp.pad(conv_state, ((0, 0), (8 - (SSM_CONV - 1), 0), (0, 0)))
    if t % SSM_CHUNK == 0:
        y, h_last = ssd_mixer(proj, dt_raw, conv_init, h_init, w, batch=batch, n_chunks=t // SSM_CHUNK,
                              t_valid=SSM_CHUNK)
    else:
        pad_rows = lambda a: jnp.pad(a.reshape(batch, t, -1), ((0, 0), (0, SSM_CHUNK - t), (0, 0))).reshape(
            batch * SSM_CHUNK, -1)
        y, h_last = ssd_mixer(pad_rows(proj), pad_rows(dt_raw), conv_init, h_init, w, batch=batch, n_chunks=1,
                              t_valid=t)
        y = y.reshape(batch, SSM_CHUNK, -1)[:, :t].reshape(batch * t, -1)
    return matmul_residual(y, w["w_out"], x, tm=tm), proj, h_last


def kernel(x_prompt, x_sample, cache_kv_cmp, cache_kv_sel, cache_kv_win, state_conv, state_ssm, page_table,
           norm_mix, norm_ffn, nsa_w_in, nsa_qk_gain, nsa_cmp_pe, nsa_cmp_w1, nsa_cmp_w2, nsa_w_out,
           ssd_w_in, ssd_conv_w, ssd_conv_b, ssd_dt_bias, ssd_a_log, ssd_d, ssd_norm, ssd_w_out,
           ffn_w_gate, ffn_w_up, ffn_w_down):
    bp, tp = x_prompt.shape[0], x_prompt.shape[1]
    bs, ts = x_sample.shape[0], x_sample.shape[1]
    depth = norm_mix.shape[0]
    past_len = page_table.shape[1] * PAGE_SIZE
    conv_dim = SSM_D_INNER + SSM_BC
    assert tp % SSM_CHUNK == 0 and ts < SSM_CHUNK and ts <= Q_PAD and ts >= SSM_CONV - 1
    xp = x_prompt.reshape(bp * tp, D_MODEL)
    xs = x_sample.reshape(bs * ts, D_MODEL)
    pt_flat = page_table.reshape(-1)
    feature_major = lambda a: a.transpose(0, 1, 3, 4, 5, 2)
    cmp_t, sel_t, win_t = feature_major(cache_kv_cmp), feature_major(cache_kv_sel), feature_major(cache_kv_win)
    kv_shape = (2, KV_GROUPS, HEAD_DIM)
    outs = {k: [] for k in ("cmp_p", "sel_p", "win_p", "conv_p", "ssm_p", "cmp_s", "sel_s", "win_s", "conv_s",
                            "ssm_s")}
    for i in range(depth):
        l = i // 2
        if i % 2 == 0:
            w = _nsa_weights(nsa_w_in[l], nsa_qk_gain[l], nsa_cmp_pe[l], nsa_cmp_w1[l], nsa_cmp_w2[l], nsa_w_out[l])
            xp, cmp_rows, sel_rows, win_rows = nsa_prompt_layer(xp, norm_mix[i], w, batch=bp, t=tp)
            row_major = lambda a: a.reshape((bp,) + kv_shape + (a.shape[-1],)).transpose(0, 4, 1, 2, 3)
            outs["cmp_p"].append(row_major(cmp_rows))
            outs["sel_p"].append(row_major(sel_rows))
            outs["win_p"].append(row_major(win_rows))
            kv = lambda a, b, t, j: a[:, NSA_Q_DIM + j * NSA_KV_DIM:NSA_Q_DIM + (j + 1) * NSA_KV_DIM].reshape(
                (b, t) + kv_shape)
            xs, proj_s, win_out = nsa_sample_layer(xs, norm_mix[i], w, cmp_t, sel_t, win_t, l, pt_flat, batch=bs,
                                                   t_new=ts, past_len=past_len)
            outs["cmp_s"].append(kv(proj_s, bs, ts, 0))
            outs["sel_s"].append(kv(proj_s, bs, ts, 1))
            outs["win_s"].append(win_out.transpose(0, 4, 1, 2, 3))
        else:
            w = _ssd_weights(ssd_w_in[l], ssd_conv_w[l], ssd_conv_b[l], ssd_dt_bias[l], ssd_a_log[l], ssd_d[l],
                             ssd_norm[l], ssd_w_out[l])
            xp, proj, h_last = ssd_layer(xp, norm_mix[i], w, jnp.zeros((bp, SSM_CONV - 1, conv_dim), F32),
                                         jnp.zeros((bp, SSM_D_INNER, SSM_STATE), F32), batch=bp, t=tp)
            outs["conv_p"].append(proj.reshape(bp, tp, -1)[:, tp - (SSM_CONV - 1):, SSM_D_INNER:])
            outs["ssm_p"].append(h_last.reshape(bp, SSM_HEADS, SSM_HEAD_DIM, SSM_STATE))
            xs, proj_s, h_last = ssd_layer(xs, norm_mix[i], w, state_conv[l],
                                           state_ssm[l].reshape(bs, SSM_D_INNER, SSM_STATE), batch=bs, t=ts)
            outs["conv_s"].append(proj_s.reshape(bs, ts, -1)[:, ts - (SSM_CONV - 1):, SSM_D_INNER:])
            outs["ssm_s"].append(h_last.reshape(bs, SSM_HEADS, SSM_HEAD_DIM, SSM_STATE))
        wg, wu, wd = ffn_w_gate[i].astype(BF16), ffn_w_up[i].astype(BF16), ffn_w_down[i].astype(BF16)
        xp = ffn(xp, norm_ffn[i], wg, wu, wd, tm=1024, th=1408)
        xs = ffn(xs, norm_ffn[i], wg, wu, wd, tm=bs * ts, th=256)
    order = ("cmp_p", "sel_p", "win_p", "conv_p", "ssm_p", "cmp_s", "sel_s", "win_s", "conv_s", "ssm_s")
    return (xp.reshape(bp, tp, D_MODEL), xs.reshape(bs, ts, D_MODEL)) + tuple(jnp.stack(outs[k]) for k in order)
```

```python
import functools
import math

import jax
import jax.numpy as jnp
import ml_dtypes
import numpy as np
from jax import lax
from jax.experimental import pallas as pl
from jax.experimental.pallas import tpu as pltpu

F32 = jnp.float32
BF16 = jnp.bfloat16
HIGHEST = lax.Precision.HIGHEST

D_MODEL = 1024
PAGE_SIZE = 128
NSA_HEADS = 16
HEAD_DIM = 64
KV_GROUPS = 4
GROUP_SIZE = NSA_HEADS // KV_GROUPS
CMP_BLOCK = 32
CMP_STRIDE = 16
CMP_HIDDEN = 256
SEL_BLOCK = 64
SEL_CHUNKS = SEL_BLOCK // CMP_STRIDE
SEL_TOPN = 16
WINDOW = 512
NSA_Q_DIM = NSA_HEADS * HEAD_DIM
NSA_KV_DIM = 2 * KV_GROUPS * HEAD_DIM
KV_HALF = KV_GROUPS * HEAD_DIM
SSM_D_INNER = 2048
SSM_HEAD_DIM = 64
SSM_HEADS = 32
SSM_GROUPS = 4
SSM_HPG = SSM_HEADS // SSM_GROUPS
SSM_STATE = 128
SSM_CONV = 4
SSM_BC = 2 * SSM_GROUPS * SSM_STATE
SSM_GW = SSM_D_INNER // SSM_GROUPS
SSM_CHUNK = 128
NORM_EPS = 1e-6
NEG_INF = -1e30
FORCE_SCORE = 1e4
LANES = 128
VMEM_LIMIT = 56 * 1024 * 1024

_SLOPES = [float(np.float32((2.0 ** (-8.0 / NSA_HEADS)) ** (i + 1))) for i in range(NSA_HEADS)]


def _cparams(sem):
    return pltpu.CompilerParams(dimension_semantics=sem, vmem_limit_bytes=VMEM_LIMIT)


def _nt(a, b, precision=None):
    return lax.dot_general(a, b, (((1,), (1,)), ((), ())), preferred_element_type=F32, precision=precision)


def _dot(a, b, precision=None):
    return jnp.dot(a, b, preferred_element_type=F32, precision=precision)


def _rms(x, gain):
    return x * lax.rsqrt(jnp.mean(x * x, axis=-1, keepdims=True) + NORM_EPS) * gain


def _norm_matmul_kernel(x_ref, g_ref, w_ref, hg_ref, hm_ref, e_ref, et_ref, o_ref, xn_ref, *, head_norm):
    @pl.when(pl.program_id(1) == 0)
    def _():
        xn_ref[...] = _rms(x_ref[...], g_ref[...]).astype(BF16)

    y = _dot(xn_ref[...], w_ref[...])
    if head_norm:
        ss = _dot(y * y, e_ref[...], HIGHEST)
        inv = lax.rsqrt(ss * (1.0 / HEAD_DIM) + NORM_EPS)
        yn = y * _dot(inv, et_ref[...], HIGHEST) * hg_ref[...]
        y = jnp.where(hm_ref[...] > 0.5, yn, y)
    o_ref[...] = y


def norm_matmul(x, gain, w, head_gain=None, head_mask=None, *, tm, tn):
    m, k = x.shape
    n = w.shape[1]
    head_norm = head_gain is not None
    if not head_norm:
        head_gain = jnp.zeros((1, n), F32)
        head_mask = jnp.zeros((1, n), F32)
    lane = np.arange(tn)
    e = np.zeros((tn, LANES), np.float32)
    e[lane, lane // HEAD_DIM] = 1.0
    return pl.pallas_call(
        functools.partial(_norm_matmul_kernel, head_norm=head_norm),
        grid=(m // tm, n // tn),
        in_specs=[
            pl.BlockSpec((tm, k), lambda i, j: (i, 0)),
            pl.BlockSpec((1, k), lambda i, j: (0, 0)),
            pl.BlockSpec((k, tn), lambda i, j: (0, j)),
            pl.BlockSpec((1, tn), lambda i, j: (0, j)),
            pl.BlockSpec((1, tn), lambda i, j: (0, j)),
            pl.BlockSpec((tn, LANES), lambda i, j: (0, 0)),
            pl.BlockSpec((LANES, tn), lambda i, j: (0, 0)),
        ],
        out_specs=pl.BlockSpec((tm, tn), lambda i, j: (i, j)),
        out_shape=jax.ShapeDtypeStruct((m, n), F32),
        scratch_shapes=[pltpu.VMEM((tm, k), BF16)],
        compiler_params=_cparams(("parallel", "arbitrary")),
        name="norm_matmul",
    )(x, gain.reshape(1, k), w, head_gain, head_mask, jnp.asarray(e), jnp.asarray(e.T))


def _ffn_kernel(x_ref, g_ref, wg_ref, wu_ref, wd_ref, o_ref, xn_ref):
    @pl.when(pl.program_id(1) == 0)
    def _():
        x = x_ref[...]
        xn_ref[...] = _rms(x, g_ref[...]).astype(BF16)
        o_ref[...] = x

    h = xn_ref[...]
    gate = _dot(h, wg_ref[...])
    up = _dot(h, wu_ref[...])
    act = (gate * jax.nn.sigmoid(gate) * up).astype(BF16)
    o_ref[...] += _dot(act, wd_ref[...])


def ffn(x, gain, wg, wu, wd, *, tm, th):
    m, k = x.shape
    hid = wg.shape[1]
    return pl.pallas_call(
        _ffn_kernel,
        grid=(m // tm, hid // th),
        in_specs=[
            pl.BlockSpec((tm, k), lambda i, j: (i, 0)),
            pl.BlockSpec((1, k), lambda i, j: (0, 0)),
            pl.BlockSpec((k, th), lambda i, j: (0, j)),
            pl.BlockSpec((k, th), lambda i, j: (0, j)),
            pl.BlockSpec((th, k), lambda i, j: (j, 0)),
        ],
        out_specs=pl.BlockSpec((tm, k), lambda i, j: (i, 0)),
        out_shape=jax.ShapeDtypeStruct((m, k), F32),
        scratch_shapes=[pltpu.VMEM((tm, k), BF16)],
        compiler_params=_cparams(("parallel", "arbitrary")),
        name="ffn",
    )(x, gain.reshape(1, k), wg, wu, wd)


def _matmul_res_kernel(a_ref, w_ref, r_ref, o_ref):
    o_ref[...] = r_ref[...] + _dot(a_ref[...], w_ref[...])


def matmul_residual(a, w, res, *, tm):
    m, k = a.shape
    n = w.shape[1]
    return pl.pallas_call(
        _matmul_res_kernel,
        grid=(m // tm,),
        in_specs=[
            pl.BlockSpec((tm, k), lambda i: (i, 0)),
            pl.BlockSpec((k, n), lambda i: (0, 0)),
            pl.BlockSpec((tm, n), lambda i: (i, 0)),
        ],
        out_specs=pl.BlockSpec((tm, n), lambda i: (i, 0)),
        out_shape=jax.ShapeDtypeStruct((m, n), F32),
        compiler_params=_cparams(("parallel",)),
        name="matmul_residual",
    )(a, w, res)


CMP_PACK = 4
CMP_PITCH = 24


def _compress_accumulate(xs_ref, n_rows, pe_ref, w1_ref, kv, pitch=CMP_PITCH):
    n_pair = NSA_KV_DIM // LANES
    low = lax.broadcasted_iota(jnp.int32, (n_rows, LANES), 1) < HEAD_DIM
    acc = [None, None]
    for jp in range(CMP_STRIDE // CMP_PACK):
        groups = []
        for c in range(kv * n_pair // 2, (kv + 1) * n_pair // 2):
            rows = [xs_ref.at[c][pl.ds(jp * CMP_PACK + i, n_rows, stride=pitch), :] for i in range(CMP_PACK)]
            swapped = [pltpu.roll(r, HEAD_DIM, axis=1) for r in rows]
            groups.append(jnp.concatenate([jnp.where(low, rows[i], swapped[i + 1])
                                           for i in range(0, CMP_PACK, 2)], axis=1))
            groups.append(jnp.concatenate([jnp.where(low, swapped[i], rows[i + 1])
                                           for i in range(0, CMP_PACK, 2)], axis=1))
        xs = jnp.concatenate(groups, axis=0)
        for m in range(2):
            lhs = (xs + pe_ref[kv, m, jp:jp + 1, :]).astype(BF16)
            part = _dot(lhs, w1_ref[kv, m, jp])
            acc[m] = part if acc[m] is None else acc[m] + part
    return acc


def _compress_finish(acc, w2_ref, kg_ref, kv):
    rows = acc[0].shape[0]
    pre = acc[0] + pltpu.roll(acc[1], rows - 1, axis=0)
    hid = (pre * jax.nn.sigmoid(pre)).astype(BF16)
    out = _dot(hid, w2_ref[kv])
    if kv == 0:
        out = _rms(out, kg_ref[...])
    return out


def _compress_rows_kernel(x_ref, pe_ref, w1_ref, w2_ref, kg_ref, o_ref, xt_ref, xs_ref, *, n_chunk):
    xt_ref[...] = jnp.transpose(x_ref[...])
    n_pair = NSA_KV_DIM // LANES
    def stage_chunk(k, carry):
        src = pl.ds(pl.multiple_of(k * CMP_STRIDE, CMP_STRIDE), CMP_STRIDE)
        dst = pl.ds(pl.multiple_of(k * CMP_PITCH, 8), CMP_STRIDE)
        for c in range(n_pair):
            xs_ref[c, dst, :] = x_ref[src, c * LANES:(c + 1) * LANES]
        return carry

    lax.fori_loop(0, n_chunk, stage_chunk, 0)
    for kv in range(2):
        o_ref[kv] = _compress_finish(_compress_accumulate(xs_ref, n_chunk, pe_ref, w1_ref, kv), w2_ref, kg_ref, kv)


def compress_rows(proj, col_block, pe, w1, w2, kgain, *, batch, t):
    n_chunk = t // CMP_STRIDE
    return pl.pallas_call(
        functools.partial(_compress_rows_kernel, n_chunk=n_chunk),
        grid=(batch,),
        in_specs=[
            pl.BlockSpec((t, NSA_KV_DIM), lambda b: (b, col_block)),
            pl.BlockSpec(pe.shape, lambda b: (0, 0, 0, 0)),
            pl.BlockSpec(w1.shape, lambda b: (0, 0, 0, 0, 0)),
            pl.BlockSpec(w2.shape, lambda b: (0, 0, 0)),
            pl.BlockSpec((1, HEAD_DIM), lambda b: (0, 0)),
        ],
        out_specs=[pl.BlockSpec((None, 2, KV_GROUPS * n_chunk, HEAD_DIM), lambda b: (b, 0, 0, 0)),
                   pl.BlockSpec((None, NSA_KV_DIM, t), lambda b: (b, 0, 0))],
        out_shape=[jax.ShapeDtypeStruct((batch, 2, KV_GROUPS * n_chunk, HEAD_DIM), F32),
                   jax.ShapeDtypeStruct((batch, NSA_KV_DIM, t), F32)],
        scratch_shapes=[pltpu.VMEM((NSA_KV_DIM // LANES, n_chunk * CMP_PITCH, LANES), F32)],
        compiler_params=_cparams(("parallel",)),
        name="compress_rows",
    )(proj, pe, w1, w2, kgain.reshape(1, HEAD_DIM))


def _slope_col(g, rows, tq):
    rblk = lax.broadcasted_iota(jnp.int32, (rows, 1), 0) // tq
    col = jnp.full((rows, 1), _SLOPES[GROUP_SIZE * g + GROUP_SIZE - 1], F32)
    for r in range(GROUP_SIZE - 1):
        col = jnp.where(rblk == r, _SLOPES[GROUP_SIZE * g + r], col)
    return col


def _normalize(acc, l):
    return acc * jnp.where(l > 0.0, 1.0 / jnp.where(l > 0.0, l, 1.0), 0.0)


MASK_BIG = 2.0 ** 100
LOG2E = 1.4426950408889634
AUG = 64
AUG_POS = 32


def _bf16_terms(x, n=3):
    x = np.asarray(x, np.float64)
    terms = []
    for _ in range(n):
        term = x.astype(ml_dtypes.bfloat16).astype(np.float64)
        terms.append(term.astype(np.float32))
        x = x - term
    return terms


def _key_aug(pos, flags=None):
    pos = np.asarray(pos, np.int64)
    aug = np.zeros((pos.shape[0], AUG), np.float32)
    if flags is not None:
        aug[:, :flags.shape[1]] = flags
    coarse = (pos // SEL_BLOCK) * SEL_BLOCK
    aug[:, AUG_POS:AUG_POS + 3] = coarse[:, None]
    aug[:, AUG_POS + 3:AUG_POS + 6] = (pos - coarse)[:, None]
    return aug


def _query_aug_rows(tq):
    rows = np.zeros((KV_GROUPS, AUG - AUG_POS, GROUP_SIZE * tq), np.float32)
    for h in range(NSA_HEADS):
        terms = _bf16_terms(np.float64(_SLOPES[h]) * LOG2E)
        g, r = divmod(h, GROUP_SIZE)
        for i, term in enumerate(terms):
            rows[g, i, r * tq:(r + 1) * tq] = term
            rows[g, 3 + i, r * tq:(r + 1) * tq] = term
    return rows


def _online_update(s, v_t, m, l, acc):
    m_new = jnp.maximum(m, jnp.max(s, axis=0, keepdims=True))
    alpha = jnp.exp2(m - m_new)
    p = jnp.exp2(s - m_new)
    return m_new, alpha * l + jnp.sum(p, axis=0, keepdims=True), alpha * acc + _dot(v_t, p.astype(BF16))


def _nsa_prompt_kernel(q_ref, ks_ref, kw_ref, cmp_ref, gate_ref, gq_ref, gks_ref, gkw_ref, gksc_ref, gkwc_ref,
                       csel_ref, cwin_ref, ccmp_ref, qrows_ref, mmap_ref, eye_ref,
                       o_ref, kst_out_ref, kwt_out_ref,
                       ksa_ref, kwa_ref, vst_ref, vwt_ref, kca_ref, vct_ref, ot_ref, ss_ref, ps_ref, sacc_ref,
                       *, t, tq):
    qi = pl.program_id(1)
    n_sel = t // SEL_BLOCK
    n_cmp = t // CMP_STRIDE
    top_n = min(SEL_TOPN, n_sel)
    cols = GROUP_SIZE * tq
    nt = t // tq
    n_win = WINDOW // tq + 1
    n_pad = WINDOW // tq

    @pl.when(qi == 0)
    def _():
        def prepare(src_ref, grow_ref, gcol_ref, const_ref, ka_ref, vt_ref, out_ref, pad_tiles):
            pad = pad_tiles * tq
            x = src_ref[...]
            xt = jnp.transpose(x)
            width = out_ref.shape[1]
            out_ref[KV_HALF:, :] = xt[KV_HALF:, t - width:]
            for g in range(KV_GROUPS):
                lo, hi = g * HEAD_DIM, (g + 1) * HEAD_DIM
                kn = _rms(x[:, lo:hi], grow_ref[...])
                ka_ref[g, pad:pad + t, :] = jnp.concatenate([kn, const_ref[pad:pad + t, :]], axis=1).astype(BF16)
                if pad:
                    ka_ref[g, 0:pad, :] = jnp.concatenate(
                        [jnp.zeros((pad, HEAD_DIM), F32), const_ref[0:pad, :]], axis=1).astype(BF16)
                kt = xt[lo:hi, :]
                ktn = kt * lax.rsqrt(jnp.mean(kt * kt, axis=0, keepdims=True) + NORM_EPS) * gcol_ref[...]
                out_ref[lo:hi, :] = ktn[:, t - width:]
                vt = xt[KV_HALF + lo:KV_HALF + hi, :].astype(BF16)
                base = g * (nt + pad_tiles)
                for i in range(pad_tiles):
                    vt_ref[base + i] = jnp.zeros((HEAD_DIM, tq), BF16)
                for i in range(nt):
                    vt_ref[base + pad_tiles + i] = vt[:, i * tq:(i + 1) * tq]

        prepare(ks_ref, gks_ref, gksc_ref, csel_ref, ksa_ref, vst_ref, kst_out_ref, 0)
        prepare(kw_ref, gkw_ref, gkwc_ref, cwin_ref, kwa_ref, vwt_ref, kwt_out_ref, n_pad)
        for g in range(KV_GROUPS):
            kc = cmp_ref[0, g * n_cmp:(g + 1) * n_cmp, :]
            kca_ref[g] = jnp.concatenate([kc, ccmp_ref[...]], axis=1).astype(BF16)
            vct_ref[g] = _nt(eye_ref[...], cmp_ref[1, g * n_cmp:(g + 1) * n_cmp, :].astype(BF16)).astype(BF16)

    q0 = pl.multiple_of(qi * tq, tq)
    lane4 = lax.broadcasted_iota(jnp.int32, (1, cols), 1) % tq
    key_off = lax.broadcasted_iota(jnp.int32, (tq, cols), 0)
    causal = key_off <= lane4
    win_old = key_off > lane4
    cmp_ok = (lax.broadcasted_iota(jnp.int32, (n_cmp, cols), 0) * CMP_STRIDE + (CMP_BLOCK - 1)) <= q0 + lane4
    qt = jnp.transpose(q_ref[...])
    gt = jax.nn.sigmoid(jnp.transpose(gate_ref[...]))
    jblk = lax.broadcasted_iota(jnp.int32, (n_sel, tq), 0)
    t_lane = q0 + lax.broadcasted_iota(jnp.int32, (n_sel, tq), 1)
    cur = t_lane // SEL_BLOCK
    forced = (jblk == 0) | (jblk == cur) | (jblk == cur - 1)
    visible = jblk * SEL_BLOCK <= t_lane

    o_c, q_sel, q_win = [], [], []
    for g in range(KV_GROUPS):
        heads = []
        for r in range(GROUP_SIZE):
            h = GROUP_SIZE * g + r
            qh = qt[h * HEAD_DIM:(h + 1) * HEAD_DIM, :]
            qh = qh * lax.rsqrt(jnp.mean(qh * qh, axis=0, keepdims=True) + NORM_EPS)
            heads.append(qh * (gq_ref[...] * (HEAD_DIM ** -0.5 * LOG2E)))
        q_top = jnp.concatenate(heads, axis=1)
        q_low = qrows_ref[g]

        def with_mask_rows(rows_, q_top=q_top, q_low=q_low):
            return jnp.concatenate([q_top, rows_, q_low], axis=0).astype(BF16)

        s = _dot(kca_ref[g], with_mask_rows(jnp.zeros((AUG_POS, cols), F32)))
        s = jnp.where(cmp_ok, s, -MASK_BIG)
        p = jnp.where(cmp_ok, jnp.exp2(s - jnp.max(s, axis=0, keepdims=True)), 0.0)
        pc = _normalize(p, jnp.sum(p, axis=0, keepdims=True))
        o_c.append(_dot(vct_ref[g], pc.astype(BF16)))
        imp = pc[:, 0:tq]
        for r in range(1, GROUP_SIZE):
            imp = imp + pc[:, r * tq:(r + 1) * tq]

        p_slc = _dot(mmap_ref[...], imp, HIGHEST)
        score = jnp.where(forced, FORCE_SCORE, jnp.where(visible, p_slc, -1.0))
        cnt = jnp.zeros((n_sel, tq), jnp.int32)
        for k in range(n_sel):
            sk = score[k:k + 1, :]
            beats = (sk > score) | ((sk == score) & (jblk > k))
            cnt = cnt + beats.astype(jnp.int32)
        drop = jnp.where(cnt < top_n, 0.0, -MASK_BIG)
        if n_sel < AUG_POS:
            drop = jnp.concatenate([drop, jnp.zeros((AUG_POS - n_sel, tq), F32)], axis=0)
        q_sel.append(with_mask_rows(jnp.concatenate([drop] * GROUP_SIZE, axis=1)))
        q_win.append(with_mask_rows(jnp.full((AUG_POS, cols), -MASK_BIG, F32)))

    def sel_scores(tile, slot):
        k0 = pl.multiple_of(tile * tq, tq)
        for g in range(KV_GROUPS):
            ss_ref[slot, g] = _dot(ksa_ref[g, pl.ds(k0, tq), :], q_sel[g])

    def sel_values(tile):
        return [_dot(vst_ref[g * nt + tile], ps_ref[g]) for g in range(KV_GROUPS)]

    def sel_softmax(slot, stats, pv, mask=None):
        new_stats = []
        for g in range(KV_GROUPS):
            m, l = stats[g]
            s = ss_ref[slot, g]
            if mask is not None:
                s = jnp.where(mask, s, -MASK_BIG)
            m_new = jnp.maximum(m, jnp.max(s, axis=0, keepdims=True))
            alpha = jnp.exp2(m - m_new)
            p = jnp.exp2(s - m_new)
            sacc_ref[g] = alpha * (sacc_ref[g] + pv[g])
            ps_ref[g] = p.astype(BF16)
            new_stats.append((m_new, alpha * l + jnp.sum(p, axis=0, keepdims=True)))
        return tuple(new_stats)

    def sel_step(slot, next_tile, prev_tile, stats, mask=None):
        pv = sel_values(jnp.maximum(prev_tile, 0))
        if next_tile is not None:
            sel_scores(next_tile, 1 - slot)
        return sel_softmax(slot, stats, pv, mask)

    def sel_finish(stats):
        pv = sel_values(qi)
        for g in range(KV_GROUPS):
            sacc_ref[g] = _normalize(sacc_ref[g] + pv[g], stats[g][1])

    for g in range(KV_GROUPS):
        ps_ref[g] = jnp.zeros((tq, cols), BF16)
        sacc_ref[g] = jnp.zeros((HEAD_DIM, cols), F32)
    sel_scores(0, 0)

    def sel_pair(j, stats):
        stats = sel_step(0, 2 * j + 1, 2 * j - 1, stats)
        return sel_step(1, 2 * j + 2, 2 * j, stats)

    stats0 = ((jnp.full((1, cols), -MASK_BIG, F32), jnp.zeros((1, cols), F32)),) * KV_GROUPS
    stats = lax.fori_loop(0, qi // 2, sel_pair, stats0)

    @pl.when(qi % 2 == 1)
    def _():
        sel_finish(sel_step(1, None, qi - 1, sel_step(0, qi, qi - 2, stats), mask=causal))

    @pl.when(qi % 2 == 0)
    def _():
        sel_finish(sel_step(0, None, qi - 1, stats, mask=causal))

    o_s = [sacc_ref[g] for g in range(KV_GROUPS)]

    win_scores = [[_dot(kwa_ref[g, pl.ds(q0 + i * tq, tq), :], q_win[g]) for i in range(n_win)]
                  for g in range(KV_GROUPS)]
    o_w = []
    for g in range(KV_GROUPS):
        tiles = win_scores[g]
        tiles[0] = jnp.where(win_old, tiles[0], -MASK_BIG)
        tiles[-1] = jnp.where(causal, tiles[-1], -MASK_BIG)
        m = functools.reduce(jnp.maximum, [jnp.max(s, axis=0, keepdims=True) for s in tiles])
        probs = [jnp.exp2(s - m) for s in tiles]
        l = functools.reduce(jnp.add, [jnp.sum(p, axis=0, keepdims=True) for p in probs])
        acc = functools.reduce(jnp.add, [_dot(vwt_ref[g * (nt + n_pad) + qi + i], p.astype(BF16))
                                         for i, p in enumerate(probs)])
        o_w.append(_normalize(acc, l))

    for h in range(NSA_HEADS):
        g, r = divmod(h, GROUP_SIZE)
        sl = slice(r * tq, (r + 1) * tq)
        ot_ref[h * HEAD_DIM:(h + 1) * HEAD_DIM, :] = (
            gt[h:h + 1, :] * o_c[g][:, sl] + gt[NSA_HEADS + h:NSA_HEADS + h + 1, :] * o_s[g][:, sl]
            + gt[2 * NSA_HEADS + h:2 * NSA_HEADS + h + 1, :] * o_w[g][:, sl])
    o_ref[...] = jnp.transpose(ot_ref[...]).astype(o_ref.dtype)


def nsa_prompt_attention(proj, gates, cmp, qk_gain, *, batch, t, tq):
    n_sel = t // SEL_BLOCK
    n_cmp = t // CMP_STRIDE
    nq = t // tq
    nt = t // tq
    n_pad = WINDOW // tq
    keep = min(WINDOW, t)
    assert n_sel <= AUG_POS and t % tq == 0 and WINDOW % tq == 0 and tq % SEL_BLOCK == 0
    n = np.arange(n_cmp)
    mmap = np.zeros((n_sel, n_cmp), np.float32)
    real = n < n_cmp - 1
    np.add.at(mmap, (n[real] // SEL_CHUNKS, n[real]), 1.0)
    nxt = real & ((n + 1) // SEL_CHUNKS < n_sel)
    np.add.at(mmap, ((n[nxt] + 1) // SEL_CHUNKS, n[nxt]), 1.0)
    pos = np.arange(t)
    csel = _key_aug(pos, (pos[:, None] // SEL_BLOCK == np.arange(n_sel)[None, :]).astype(np.float32))
    wpos = np.arange(-WINDOW, t)
    cwin = _key_aug(wpos, (wpos < 0).astype(np.float32)[:, None])
    ccmp = _key_aug(np.arange(n_cmp) * CMP_STRIDE + (CMP_BLOCK - 1))
    const = lambda shape: pl.BlockSpec(shape, lambda b, i: (0,) * len(shape))
    col = lambda v: v.reshape(HEAD_DIM, 1)
    row = lambda v: v.reshape(1, HEAD_DIM)
    return pl.pallas_call(
        functools.partial(_nsa_prompt_kernel, t=t, tq=tq),
        grid=(batch, nq),
        in_specs=[
            pl.BlockSpec((tq, NSA_Q_DIM), lambda b, i: (b * nq + i, 0)),
            pl.BlockSpec((t, NSA_KV_DIM), lambda b, i: (b, 3), pipeline_mode=pl.Buffered(1)),
            pl.BlockSpec((t, NSA_KV_DIM), lambda b, i: (b, 4), pipeline_mode=pl.Buffered(1)),
            pl.BlockSpec((None, 2, KV_GROUPS * n_cmp, HEAD_DIM), lambda b, i: (b, 0, 0, 0)),
            pl.BlockSpec((tq, LANES), lambda b, i: (b * nq + i, 0)),
            const((HEAD_DIM, 1)), const((1, HEAD_DIM)), const((1, HEAD_DIM)), const((HEAD_DIM, 1)),
            const((HEAD_DIM, 1)),
            const((t, AUG)), const((t + WINDOW, AUG)), const((n_cmp, AUG)),
            const((KV_GROUPS, AUG - AUG_POS, GROUP_SIZE * tq)), const((n_sel, n_cmp)), const((HEAD_DIM, HEAD_DIM)),
        ],
        out_specs=[
            pl.BlockSpec((tq, NSA_Q_DIM), lambda b, i: (b * nq + i, 0)),
            pl.BlockSpec((None, NSA_KV_DIM, t), lambda b, i: (b, 0, 0)),
            pl.BlockSpec((None, NSA_KV_DIM, keep), lambda b, i: (b, 0, 0)),
        ],
        out_shape=[jax.ShapeDtypeStruct((batch * t, NSA_Q_DIM), BF16),
                   jax.ShapeDtypeStruct((batch, NSA_KV_DIM, t), F32),
                   jax.ShapeDtypeStruct((batch, NSA_KV_DIM, keep), F32)],
        scratch_shapes=[
            pltpu.VMEM((KV_GROUPS, t, HEAD_DIM + AUG), BF16),
            pltpu.VMEM((KV_GROUPS, t + WINDOW, HEAD_DIM + AUG), BF16),
            pltpu.VMEM((KV_GROUPS * nt, HEAD_DIM, tq), BF16),
            pltpu.VMEM((KV_GROUPS * (nt + n_pad), HEAD_DIM, tq), BF16),
            pltpu.VMEM((KV_GROUPS, n_cmp, HEAD_DIM + AUG), BF16),
            pltpu.VMEM((KV_GROUPS, HEAD_DIM, n_cmp), BF16),
            pltpu.VMEM((NSA_Q_DIM, tq), F32),
            pltpu.VMEM((2, KV_GROUPS, tq, GROUP_SIZE * tq), F32),
            pltpu.VMEM((KV_GROUPS, tq, GROUP_SIZE * tq), BF16),
            pltpu.VMEM((KV_GROUPS, HEAD_DIM, GROUP_SIZE * tq), F32),
        ],
        compiler_params=pltpu.CompilerParams(dimension_semantics=("parallel", "arbitrary"),
                                             vmem_limit_bytes=62 * 1024 * 1024),
        name="nsa_prompt_attention",
    )(proj, proj, proj, cmp, gates, col(qk_gain[0]), row(qk_gain[2]), row(qk_gain[3]), col(qk_gain[2]),
      col(qk_gain[3]), jnp.asarray(csel), jnp.asarray(cwin), jnp.asarray(ccmp), jnp.asarray(_query_aug_rows(tq)),
      jnp.asarray(mmap), jnp.eye(HEAD_DIM, dtype=BF16))


def _nsa_weights(w_in, qk_gain, cmp_pe, cmp_w1, cmp_w2, w_out):
    n_main = NSA_Q_DIM + 3 * NSA_KV_DIM
    zeros_kv = jnp.zeros((KV_HALF,), F32)
    ones_kv = jnp.ones((KV_HALF,), F32)
    head_gain = jnp.concatenate([
        jnp.tile(qk_gain[0], NSA_HEADS), zeros_kv, zeros_kv,
        jnp.tile(qk_gain[2], KV_GROUPS), zeros_kv, jnp.tile(qk_gain[3], KV_GROUPS), zeros_kv]).reshape(1, n_main)
    head_mask = jnp.concatenate([
        jnp.ones((NSA_Q_DIM,), F32), zeros_kv, zeros_kv, ones_kv, zeros_kv, ones_kv, zeros_kv]).reshape(1, n_main)
    w_gate = jnp.pad(w_in[:, n_main:], ((0, 0), (0, LANES - 3 * NSA_HEADS)))
    return dict(
        w_main=w_in[:, :n_main].astype(BF16), w_gate=w_gate.astype(BF16), head_gain=head_gain, head_mask=head_mask,
        pe=cmp_pe.reshape(2, 2, CMP_STRIDE // CMP_PACK, CMP_PACK * HEAD_DIM),
        w1=cmp_w1.reshape(2, 2, CMP_STRIDE // CMP_PACK, CMP_PACK * HEAD_DIM, CMP_HIDDEN).astype(BF16),
        w2=cmp_w2.astype(BF16), kgain=qk_gain[1], qk_gain=qk_gain, w_out=w_out.astype(BF16))


def _nsa_project(x, gain, w, *, tm, head_norm):
    if head_norm:
        proj = norm_matmul(x, gain, w["w_main"], w["head_gain"], w["head_mask"], tm=tm, tn=512)
    else:
        proj = norm_matmul(x, gain, w["w_main"], tm=tm, tn=1280)
    return proj, norm_matmul(x, gain, w["w_gate"], tm=tm, tn=LANES)


def nsa_prompt_layer(x, gain, w, *, batch, t):
    tm = min(1024, batch * t)
    proj, gates = _nsa_project(x, gain, w, tm=min(2048, batch * t), head_norm=False)
    cmp, cmp_t = compress_rows(proj, 2, w["pe"], w["w1"], w["w2"], w["kgain"], batch=batch, t=t)
    o, sel_t, win_t = nsa_prompt_attention(proj, gates, cmp, w["qk_gain"], batch=batch, t=t, tq=256)
    return matmul_residual(o, w["w_out"], x, tm=tm), cmp_t, sel_t, win_t


PAGES_PER_STEP = 32
CHUNKS_PER_PAGE = PAGE_SIZE // CMP_STRIDE
PAGE_PITCH = CMP_PITCH


def _compress_pages_kernel(pt_ref, *refs, n_chunk):
    del pt_ref
    pages = refs[:PAGES_PER_STEP]
    pe_ref, w1_ref, w2_ref, kg_ref, o_ref, xs_ref, pre_ref = refs[PAGES_PER_STEP:]
    s = pl.program_id(1)
    n_pair = NSA_KV_DIM // LANES
    step_chunks = PAGES_PER_STEP * CHUNKS_PER_PAGE
    for i, page in enumerate(pages):
        for c in range(n_pair):
            rows = jnp.transpose(page[c * LANES:(c + 1) * LANES, :])
            for k in range(CHUNKS_PER_PAGE):
                at = (i * CHUNKS_PER_PAGE + k) * PAGE_PITCH
                xs_ref[c, at:at + CMP_STRIDE, :] = rows[k * CMP_STRIDE:(k + 1) * CMP_STRIDE]
    for kv in range(2):
        acc = _compress_accumulate(xs_ref, step_chunks, pe_ref, w1_ref, kv, PAGE_PITCH)
        for m in range(2):
            for g in range(KV_GROUPS):
                pre_ref[kv, m, pl.ds(pl.multiple_of(g * n_chunk + s * step_chunks, step_chunks), step_chunks), :] = (
                    acc[m][g * step_chunks:(g + 1) * step_chunks])

    @pl.when(s == pl.num_programs(1) - 1)
    def _():
        for kv in range(2):
            o_ref[kv] = _compress_finish([pre_ref[kv, 0], pre_ref[kv, 1]], w2_ref, kg_ref, kv)


def compress_pages(cache_t, layer, page_table_flat, pe, w1, w2, kgain, *, batch, n_pages):
    n_chunk = n_pages * CHUNKS_PER_PAGE
    n_steps = n_pages // PAGES_PER_STEP

    def page_spec(i):
        return pl.BlockSpec((None, None, NSA_KV_DIM, PAGE_SIZE),
                            lambda b, s, pt: (layer, pt[b * n_pages + s * PAGES_PER_STEP + i], 0, 0))

    grid_spec = pltpu.PrefetchScalarGridSpec(
        num_scalar_prefetch=1,
        grid=(batch, n_steps),
        in_specs=[page_spec(i) for i in range(PAGES_PER_STEP)] + [
            pl.BlockSpec(pe.shape, lambda b, s, pt: (0, 0, 0, 0)),
            pl.BlockSpec(w1.shape, lambda b, s, pt: (0, 0, 0, 0, 0)),
            pl.BlockSpec(w2.shape, lambda b, s, pt: (0, 0, 0)),
            pl.BlockSpec((1, HEAD_DIM), lambda b, s, pt: (0, 0)),
        ],
        out_specs=pl.BlockSpec((None, 2, KV_GROUPS * n_chunk, HEAD_DIM), lambda b, s, pt: (b, 0, 0, 0)),
        scratch_shapes=[pltpu.VMEM((NSA_KV_DIM // LANES, PAGES_PER_STEP * CHUNKS_PER_PAGE * PAGE_PITCH, LANES), F32),
                        pltpu.VMEM((2, 2, KV_GROUPS * n_chunk, CMP_HIDDEN), F32)],
    )
    return pl.pallas_call(
        functools.partial(_compress_pages_kernel, n_chunk=n_chunk),
        grid_spec=grid_spec,
        out_shape=jax.ShapeDtypeStruct((batch, 2, KV_GROUPS * n_chunk, HEAD_DIM), F32),
        compiler_params=_cparams(("parallel", "arbitrary")),
        name="compress_pages",
    )(page_table_flat, *([cache_t] * PAGES_PER_STEP), pe, w1, w2, kgain.reshape(1, HEAD_DIM))


Q_PAD = 8
HEAD_PAD = 16


def _nsa_sample_select_kernel(q_ref, cmp_ref, mmap_ref, oc_ref, idx_ref, score_ref, *, past_len, n_cmp, n_sel):
    rows = GROUP_SIZE * Q_PAD
    n_selp = score_ref.shape[0]
    t_max = past_len + Q_PAD - 1
    row = lax.broadcasted_iota(jnp.int32, (rows, 1), 0)
    t_col = past_len + row % Q_PAD
    jblk = lax.broadcasted_iota(jnp.int32, (n_selp, Q_PAD), 0)
    t_lane = past_len + lax.broadcasted_iota(jnp.int32, (n_selp, Q_PAD), 1)
    cur = t_lane // SEL_BLOCK
    forced = (jblk == 0) | (jblk == cur) | (jblk == cur - 1)
    visible = jblk * SEL_BLOCK <= t_lane
    for g in range(KV_GROUPS):
        qg = (q_ref[g] * (HEAD_DIM ** -0.5)).astype(BF16)
        slope = _slope_col(g, rows, Q_PAD)
        kc = cmp_ref[0, g * n_cmp:(g + 1) * n_cmp, :].astype(BF16)
        vc = cmp_ref[1, g * n_cmp:(g + 1) * n_cmp, :].astype(BF16)
        kpos = lax.broadcasted_iota(jnp.int32, (1, n_cmp), 1) * CMP_STRIDE + (CMP_BLOCK - 1)
        s = _nt(qg, kc) + slope * (kpos - t_max).astype(F32)
        mask = kpos <= t_col
        s = jnp.where(mask, s, NEG_INF)
        p = jnp.where(mask, jnp.exp(s - jnp.max(s, axis=1, keepdims=True)), 0.0)
        pc = _normalize(p, jnp.sum(p, axis=1, keepdims=True))
        oc_ref[g] = _dot(pc.astype(BF16), vc)
        imp = pc[0:Q_PAD]
        for r in range(1, GROUP_SIZE):
            imp = imp + pc[r * Q_PAD:(r + 1) * Q_PAD]
        imp_hi = imp.astype(BF16)
        imp_rest = imp - imp_hi.astype(F32)
        imp_mid = imp_rest.astype(BF16)
        imp_lo = (imp_rest - imp_mid.astype(F32)).astype(BF16)
        p_slc = (_nt(mmap_ref[...], imp_hi) + _nt(mmap_ref[...], imp_mid)) + _nt(mmap_ref[...], imp_lo)
        score = jnp.where(forced, FORCE_SCORE, jnp.where(visible, p_slc, -1.0))
        score_ref[:, g * Q_PAD:(g + 1) * Q_PAD] = jnp.where(jblk < n_sel, score, -2.0)

    score = score_ref[...]
    jall = lax.broadcasted_iota(jnp.int32, score.shape, 0)

    def rank_body(k, cnt):
        sk = score_ref[pl.ds(k, 1), :]
        beats = (sk > score) | ((sk == score) & (jall > k))
        return cnt + beats.astype(jnp.int32)

    cnt = lax.fori_loop(0, n_sel, rank_body, jnp.zeros(score.shape, jnp.int32), unroll=4)
    slots = [jnp.sum(jnp.where(cnt == sl, jall, 0), axis=0, keepdims=True) for sl in range(SEL_TOPN)]
    idx_ref[...] = jnp.concatenate(slots, axis=0)


def nsa_sample_select(q_stack, cmp, *, batch, past_len, t_new):
    n_cmp = past_len // CMP_STRIDE
    n_sel = -(-(past_len + t_new) // SEL_BLOCK)
    n_selp = -(-n_sel // 8) * 8
    n = np.arange(n_cmp)
    mmap = np.zeros((n_selp, n_cmp), np.float32)
    real = n < n_cmp - 1
    np.add.at(mmap, (n[real] // SEL_CHUNKS, n[real]), 1.0)
    np.add.at(mmap, ((n[real] + 1) // SEL_CHUNKS, n[real]), 1.0)
    rows = GROUP_SIZE * Q_PAD
    return pl.pallas_call(
        functools.partial(_nsa_sample_select_kernel, past_len=past_len, n_cmp=n_cmp, n_sel=n_sel),
        grid=(batch,),
        in_specs=[
            pl.BlockSpec((None, KV_GROUPS, rows, HEAD_DIM), lambda b: (b, 0, 0, 0)),
            pl.BlockSpec((None, 2, KV_GROUPS * n_cmp, HEAD_DIM), lambda b: (b, 0, 0, 0)),
            pl.BlockSpec((n_selp, n_cmp), lambda b: (0, 0)),
        ],
        out_specs=[
            pl.BlockSpec((None, KV_GROUPS, rows, HEAD_DIM), lambda b: (b, 0, 0, 0)),
            pl.BlockSpec((None, SEL_TOPN, KV_GROUPS * Q_PAD), lambda b: (b, 0, 0)),
        ],
        out_shape=[jax.ShapeDtypeStruct((batch, KV_GROUPS, rows, HEAD_DIM), F32),
                   jax.ShapeDtypeStruct((batch, SEL_TOPN, KV_GROUPS * Q_PAD), jnp.int32)],
        scratch_shapes=[pltpu.VMEM((n_selp, KV_GROUPS * Q_PAD), F32)],
        compiler_params=_cparams(("parallel",)),
        name="nsa_sample_select",
    )(q_stack, cmp, jnp.asarray(mmap, BF16))


def _nsa_sample_attend_kernel(pt_ref, idx_ref, cache_ref, q_ref, slope_ref, snew_ref, win_ref, wnew_ref, wnew_t_ref,
                              oc_ref, gate_ref, o_ref, wout_ref, kcat_ref, vcat_ref, page_ref, sem_ref,
                              *, layer, past_len, t_new, n_past_blocks, n_steps):
    b, g = pl.program_id(0), pl.program_id(1)
    n_pages = past_len // PAGE_SIZE
    n_copies = t_new * SEL_TOPN
    step = b * KV_GROUPS + g

    def page_copy(src_page, src_group, slot, n):
        return pltpu.make_async_copy(cache_ref.at[layer, src_page, :, src_group], page_ref.at[slot, n],
                                     sem_ref.at[slot, n])

    def start_gather(at_step, slot):
        at_b = at_step // KV_GROUPS
        at_g = at_step % KV_GROUPS
        for n in range(n_copies):
            blk = jnp.minimum(idx_ref[at_step * n_copies + n], n_past_blocks - 1)
            page_copy(pt_ref[at_b * n_pages + blk // 2], at_g, slot, n).start()

    @pl.when(step == 0)
    def _():
        start_gather(step, 0)

    @pl.when(step + 1 < n_steps)
    def _():
        start_gather(step + 1, (step + 1) % 2)

    slot = step % 2
    for n in range(n_copies):
        page_copy(0, 0, slot, n).wait()

    slope = slope_ref[:, 0:1]
    lane = lax.broadcasted_iota(jnp.int32, (1, LANES), 1)
    new_pos = past_len + lane
    buf = win_ref.shape[2]
    wpos = past_len - buf + lax.broadcasted_iota(jnp.int32, (1, buf), 1)
    k_new, v_new = snew_ref[0].astype(BF16), snew_ref[1].astype(BF16)
    kw_new, vw_new = wnew_ref[0].astype(BF16), wnew_ref[1].astype(BF16)
    kw_old, vw_old = win_ref[0].astype(BF16), win_ref[1].astype(BF16)

    def attend(tiles, v_last):
        m = jnp.full((HEAD_PAD, 1), NEG_INF, F32)
        for s, mask, _ in tiles:
            m = jnp.maximum(m, jnp.max(jnp.where(mask, s, NEG_INF), axis=1, keepdims=True))
        l = jnp.zeros((HEAD_PAD, 1), F32)
        acc = jnp.zeros((HEAD_PAD, HEAD_DIM), F32)
        for i, (s, mask, v) in enumerate(tiles):
            p = jnp.where(mask, jnp.exp(jnp.where(mask, s, NEG_INF) - m), 0.0)
            l = l + jnp.sum(p, axis=1, keepdims=True)
            if i == len(tiles) - 1:
                acc = acc + _dot(p.astype(BF16), v_last)
            else:
                acc = acc + _nt(p.astype(BF16), v)
        return _normalize(acc, l)

    for qi in range(t_new):
        t = past_len + qi
        q = (q_ref[qi] * (HEAD_DIM ** -0.5)).astype(BF16)
        new_ok = (lane < t_new) & (new_pos <= t)

        kpos, mask = [], []
        for sl in range(SEL_TOPN):
            n = qi * SEL_TOPN + sl
            blk = idx_ref[step * n_copies + n]
            kcat_ref[qi, :, sl * PAGE_SIZE:(sl + 1) * PAGE_SIZE] = page_ref[slot, n, 0].astype(BF16)
            vcat_ref[qi, :, sl * PAGE_SIZE:(sl + 1) * PAGE_SIZE] = page_ref[slot, n, 1].astype(BF16)
            kpos.append((blk // 2) * PAGE_SIZE + lane)
            mask.append((lane // SEL_BLOCK == blk % 2) & (blk < n_past_blocks) & (kpos[-1] <= t))
        kpos = jnp.concatenate(kpos, axis=1)
        mask = jnp.concatenate([m_.astype(jnp.int32) for m_ in mask], axis=1) > 0
        s_sel = _dot(q, kcat_ref[qi]) + slope * (kpos - t).astype(F32)
        s_new = _nt(q, k_new) + slope * (new_pos - t).astype(F32)
        o_s = attend([(s_sel, mask, vcat_ref[qi]), (s_new, new_ok, None)], v_new)

        dist = t - wpos
        s_w = _dot(q, kw_old) + slope * (wpos - t).astype(F32)
        s_wn = _nt(q, kw_new) + slope * (new_pos - t).astype(F32)
        o_w = attend([(s_w, (dist >= 0) & (dist < WINDOW), vw_old),
                      (s_wn, new_ok & (t - new_pos < WINDOW), None)], vw_new)

        gates = jax.nn.sigmoid(gate_ref[qi])
        o_ref[qi] = gates[:, 0:1] * oc_ref[qi] + gates[:, 1:2] * o_s + gates[:, 2:3] * o_w

    for kv in range(2):
        rolled = pltpu.roll(win_ref[kv], buf - t_new, axis=1)
        wout_ref[kv, :, 0:buf - LANES] = rolled[:, 0:buf - LANES]
        wout_ref[kv, :, buf - LANES:] = jnp.where(lane >= LANES - t_new, wnew_t_ref[kv], rolled[:, buf - LANES:])


def nsa_sample_attend(cache_t, win_t, layer, page_table_flat, idx_flat, q_heads, slopes, sel_new, win_new,
                      win_new_t, o_c, gates, *, batch, past_len, t_new):
    n_past_blocks = past_len // SEL_BLOCK
    buf = win_t.shape[-1]
    per_bg = lambda shape: pl.BlockSpec((None, None) + shape, lambda b, g, pt, idx: (b, g) + (0,) * len(shape))
    grid_spec = pltpu.PrefetchScalarGridSpec(
        num_scalar_prefetch=2,
        grid=(batch, KV_GROUPS),
        in_specs=[
            pl.BlockSpec(memory_space=pl.ANY),
            per_bg((t_new, HEAD_PAD, HEAD_DIM)),
            pl.BlockSpec((None, HEAD_PAD, LANES), lambda b, g, pt, idx: (g, 0, 0)),
            per_bg((2, LANES, HEAD_DIM)),
            pl.BlockSpec((None, None, 2, None, HEAD_DIM, buf), lambda b, g, pt, idx: (layer, b, 0, g, 0, 0)),
            per_bg((2, LANES, HEAD_DIM)),
            per_bg((2, HEAD_DIM, LANES)),
            per_bg((t_new, HEAD_PAD, HEAD_DIM)),
            per_bg((t_new, HEAD_PAD, LANES)),
        ],
        out_specs=[
            per_bg((t_new, HEAD_PAD, HEAD_DIM)),
            pl.BlockSpec((None, 2, None, HEAD_DIM, buf), lambda b, g, pt, idx: (b, 0, g, 0, 0)),
        ],
        scratch_shapes=[pltpu.VMEM((t_new, HEAD_DIM, SEL_TOPN * PAGE_SIZE), BF16)] * 2 + [
            pltpu.VMEM((2, t_new * SEL_TOPN, 2, HEAD_DIM, PAGE_SIZE), F32),
            pltpu.SemaphoreType.DMA((2, t_new * SEL_TOPN)),
        ],
    )
    return pl.pallas_call(
        functools.partial(_nsa_sample_attend_kernel, layer=layer, past_len=past_len, t_new=t_new,
                          n_past_blocks=n_past_blocks, n_steps=batch * KV_GROUPS),
        grid_spec=grid_spec,
        out_shape=[jax.ShapeDtypeStruct((batch, KV_GROUPS, t_new, HEAD_PAD, HEAD_DIM), F32),
                   jax.ShapeDtypeStruct((batch, 2, KV_GROUPS, HEAD_DIM, buf), F32)],
        compiler_params=_cparams(("arbitrary", "arbitrary")),
        name="nsa_sample_attend",
    )(page_table_flat, idx_flat, cache_t, q_heads, slopes, sel_new, win_t, win_new, win_new_t, o_c, gates)


def nsa_sample_layer(x, gain, w, cmp_t, sel_t, win_t, layer, page_table_flat, *, batch, t_new, past_len):
    m = batch * t_new
    proj, gates = _nsa_project(x, gain, w, tm=m, head_norm=True)
    cmp = compress_pages(cmp_t.reshape(cmp_t.shape[0], cmp_t.shape[1], NSA_KV_DIM, PAGE_SIZE), layer,
                         page_table_flat, w["pe"], w["w1"], w["w2"], w["kgain"], batch=batch,
                         n_pages=past_len // PAGE_SIZE)
    q = proj[:, :NSA_Q_DIM].reshape(batch, t_new, KV_GROUPS, GROUP_SIZE, HEAD_DIM)
    q_stack = jnp.pad(q.transpose(0, 2, 3, 1, 4), ((0, 0), (0, 0), (0, 0), (0, Q_PAD - t_new), (0, 0)))
    o_c, idx = nsa_sample_select(q_stack.reshape(batch, KV_GROUPS, GROUP_SIZE * Q_PAD, HEAD_DIM), cmp,
                                 batch=batch, past_len=past_len, t_new=t_new)
    pad_heads = ((0, 0), (0, 0), (0, 0), (0, HEAD_PAD - GROUP_SIZE), (0, 0))
    o_c = o_c.reshape(batch, KV_GROUPS, GROUP_SIZE, Q_PAD, HEAD_DIM)[:, :, :, :t_new].transpose(0, 1, 3, 2, 4)
    idx_flat = idx.reshape(batch, SEL_TOPN, KV_GROUPS, Q_PAD)[..., :t_new].transpose(0, 2, 3, 1).reshape(-1)
    q_heads = jnp.pad(q.transpose(0, 2, 1, 3, 4), pad_heads)

    def new_rows(cols):
        a = cols.reshape(batch, t_new, 2, KV_GROUPS, HEAD_DIM).transpose(0, 3, 2, 1, 4)
        return jnp.pad(a, ((0, 0), (0, 0), (0, 0), (0, LANES - t_new), (0, 0)))

    kvs = proj[:, NSA_Q_DIM + NSA_KV_DIM:NSA_Q_DIM + 2 * NSA_KV_DIM]
    kvw = proj[:, NSA_Q_DIM + 2 * NSA_KV_DIM:]
    win_new_t = kvw.reshape(batch, t_new, 2, KV_GROUPS, HEAD_DIM).transpose(0, 3, 2, 4, 1)
    win_new_t = jnp.pad(win_new_t, ((0, 0), (0, 0), (0, 0), (0, 0), (LANES - t_new, 0)))
    g_arr = gates[:, :3 * NSA_HEADS].reshape(batch, t_new, 3, KV_GROUPS, GROUP_SIZE).transpose(0, 3, 1, 4, 2)
    g_arr = jnp.pad(g_arr, ((0, 0), (0, 0), (0, 0), (0, HEAD_PAD - GROUP_SIZE), (0, LANES - 3)))
    slopes = np.zeros((KV_GROUPS, HEAD_PAD, LANES), np.float32)
    slopes[:, :GROUP_SIZE, :] = np.asarray(_SLOPES, np.float32).reshape(KV_GROUPS, GROUP_SIZE, 1)
    o, win_out = nsa_sample_attend(sel_t, win_t, layer, page_table_flat, idx_flat, q_heads, jnp.asarray(slopes),
                                   new_rows(kvs), new_rows(kvw), win_new_t, jnp.pad(o_c, pad_heads), g_arr,
                                   batch=batch, past_len=past_len, t_new=t_new)
    o = o[:, :, :, :GROUP_SIZE].transpose(0, 2, 1, 3, 4).reshape(m, NSA_Q_DIM)
    return matmul_residual(o.astype(BF16), w["w_out"], x, tm=m), proj, win_out


def _expand_heads(x, e):
    x1 = x.astype(BF16)
    rest = x - x1.astype(F32)
    x2 = rest.astype(BF16)
    x3 = (rest - x2.astype(F32)).astype(BF16)
    return (_dot(x1, e) + _dot(x2, e)) + _dot(x3, e)


def _ssd_kernel(z_ref, x_ref, bc_ref, dt_ref, cw_ref, cb_ref, dtb_ref, alog_ref, drow_ref, nw_ref, cinit_ref,
                hinit_ref, tril_ref, e_ref, y_ref, hout_ref, stage_ref, h_ref, yacc_ref, *, t_valid):
    L = SSM_CHUNK
    c = pl.program_id(1)

    @pl.when(c == 0)
    def _():
        stage_ref[...] = cinit_ref[...]
        h_ref[...] = hinit_ref[...]

    cur = jnp.concatenate([x_ref[...], bc_ref[...]], axis=1)
    prev = stage_ref[...]
    first = lax.broadcasted_iota(jnp.int32, (8, 1), 0)
    conv = cb_ref[...] + cur * cw_ref[SSM_CONV - 1:SSM_CONV, :]
    for s in range(1, SSM_CONV):
        rolled = pltpu.roll(cur, s, axis=0)
        head = jnp.where(first < s, pltpu.roll(prev, s, axis=0), rolled[0:8])
        conv = conv + jnp.concatenate([head, rolled[8:]], axis=0) * cw_ref[SSM_CONV - 1 - s:SSM_CONV - s, :]
    stage_ref[...] = cur[L - 8:L]
    conv = conv * jax.nn.sigmoid(conv)
    xh = conv[:, :SSM_D_INNER]

    row = lax.broadcasted_iota(jnp.int32, (L, LANES), 0)
    dt_in = dt_ref[...] + dtb_ref[...]
    dt = jnp.maximum(dt_in, 0.0) + jnp.log1p(jnp.exp(-jnp.abs(dt_in)))
    dt = jnp.where(row < t_valid, dt, 0.0)
    da = dt * -jnp.exp(alog_ref[...])
    acum = _dot(tril_ref[...], da, HIGHEST)
    acum_t = jnp.transpose(acum)
    dt_full = _expand_heads(dt, e_ref[...])
    acum_full = _expand_heads(acum, e_ref[...])
    grow_full = jnp.exp(acum_full)
    toend_full = jnp.exp(acum_full[L - 1:L, :] - acum_full)
    last_t = jnp.exp(acum_t[:, L - 1:L])
    xdt = xh * dt_full
    causal = lax.broadcasted_iota(jnp.int32, (L, L), 0) >= lax.broadcasted_iota(jnp.int32, (L, L), 1)

    for g in range(SSM_GROUPS):
        gl, gh = g * SSM_GW, (g + 1) * SSM_GW
        bm = conv[:, SSM_D_INNER + g * SSM_STATE:SSM_D_INNER + (g + 1) * SSM_STATE].astype(BF16)
        cm = conv[:, SSM_D_INNER + (SSM_GROUPS + g) * SSM_STATE:
                  SSM_D_INNER + (SSM_GROUPS + g + 1) * SSM_STATE].astype(BF16)
        cb = _nt(cm, bm)
        xdt_g = xdt[:, gl:gh]
        for r in range(SSM_HPG):
            h = g * SSM_HPG + r
            seg = acum[:, h:h + 1] - acum_t[h:h + 1, :]
            decay = jnp.exp(jnp.where(causal, seg, NEG_INF))
            yacc_ref[:, gl + r * SSM_HEAD_DIM:gl + (r + 1) * SSM_HEAD_DIM] = _dot(
                (cb * decay).astype(BF16), xdt_g[:, r * SSM_HEAD_DIM:(r + 1) * SSM_HEAD_DIM].astype(BF16))
        h_g = h_ref[gl:gh, :]
        y_state = _nt(cm, h_g.astype(BF16)) * grow_full[:, gl:gh]
        yacc_ref[:, gl:gh] = yacc_ref[:, gl:gh] + y_state
        xw = (xdt_g * toend_full[:, gl:gh]).astype(BF16)
        h_add = lax.dot_general(xw, bm, (((0,), (0,)), ((), ())), preferred_element_type=F32)
        for r in range(SSM_HPG):
            h = g * SSM_HPG + r
            rl, rh = r * SSM_HEAD_DIM, (r + 1) * SSM_HEAD_DIM
            keep = jnp.broadcast_to(last_t[h:h + 1, :], (SSM_HEAD_DIM, SSM_STATE))
            h_ref[gl + rl:gl + rh, :] = h_g[rl:rh] * keep + h_add[rl:rh]

    zf = z_ref[...]
    gated = (yacc_ref[...] + xh * drow_ref[...]) * (zf * jax.nn.sigmoid(zf))
    for g in range(SSM_GROUPS):
        gl, gh = g * SSM_GW, (g + 1) * SSM_GW
        y_ref[:, gl:gh] = _rms(gated[:, gl:gh], nw_ref[:, gl:gh]).astype(y_ref.dtype)

    @pl.when(c == pl.num_programs(1) - 1)
    def _():
        hout_ref[...] = h_ref[...]


def ssd_mixer(proj, dt_raw, conv_init, h_init, w, *, batch, n_chunks, t_valid):
    L = SSM_CHUNK
    rows = batch * n_chunks * L
    conv_dim = SSM_D_INNER + SSM_BC
    tril = np.tril(np.ones((L, L), np.float32))
    e = np.zeros((LANES, SSM_D_INNER), np.float32)
    e[np.arange(SSM_D_INNER) // SSM_HEAD_DIM, np.arange(SSM_D_INNER)] = 1.0
    const = lambda shape: pl.BlockSpec(shape, lambda b, c: (0,) * len(shape))
    return pl.pallas_call(
        functools.partial(_ssd_kernel, t_valid=t_valid),
        grid=(batch, n_chunks),
        in_specs=[
            pl.BlockSpec((L, SSM_D_INNER), lambda b, c: (b * n_chunks + c, 0)),
            pl.BlockSpec((L, SSM_D_INNER), lambda b, c: (b * n_chunks + c, 1)),
            pl.BlockSpec((L, SSM_BC), lambda b, c: (b * n_chunks + c, 2 * SSM_D_INNER // SSM_BC)),
            pl.BlockSpec((L, LANES), lambda b, c: (b * n_chunks + c, 0)),
            const((SSM_CONV, conv_dim)), const((1, conv_dim)), const((1, LANES)), const((1, LANES)),
            const((1, SSM_D_INNER)), const((1, SSM_D_INNER)),
            pl.BlockSpec((None, 8, conv_dim), lambda b, c: (b, 0, 0)),
            pl.BlockSpec((None, SSM_D_INNER, SSM_STATE), lambda b, c: (b, 0, 0)),
            const((L, L)), const((LANES, SSM_D_INNER)),
        ],
        out_specs=[
            pl.BlockSpec((L, SSM_D_INNER), lambda b, c: (b * n_chunks + c, 0)),
            pl.BlockSpec((None, SSM_D_INNER, SSM_STATE), lambda b, c: (b, 0, 0)),
        ],
        out_shape=[jax.ShapeDtypeStruct((rows, SSM_D_INNER), BF16),
                   jax.ShapeDtypeStruct((batch, SSM_D_INNER, SSM_STATE), F32)],
        scratch_shapes=[pltpu.VMEM((8, conv_dim), F32), pltpu.VMEM((SSM_D_INNER, SSM_STATE), F32),
                        pltpu.VMEM((L, SSM_D_INNER), F32)],
        compiler_params=_cparams(("parallel", "arbitrary")),
        name="ssd_mixer",
    )(proj, proj, proj, dt_raw, w["conv_w"], w["conv_b"], w["dt_bias"], w["a_log"], w["d_row"], w["norm_w"],
      conv_init, h_init, jnp.asarray(tril), jnp.asarray(e, BF16))


def _ssd_weights(w_in, conv_w, conv_b, dt_bias, a_log, d_skip, norm_w, w_out):
    n_main = 2 * SSM_D_INNER + SSM_BC
    pad = lambda v: jnp.pad(v, (0, LANES - SSM_HEADS)).reshape(1, LANES)
    return dict(
        w_main=w_in[:, :n_main].astype(BF16),
        w_dt=jnp.pad(w_in[:, n_main:], ((0, 0), (0, LANES - SSM_HEADS))).astype(BF16),
        conv_w=conv_w, conv_b=conv_b.reshape(1, -1), dt_bias=pad(dt_bias), a_log=pad(a_log),
        d_row=jnp.repeat(d_skip, SSM_HEAD_DIM).reshape(1, SSM_D_INNER), norm_w=norm_w.reshape(1, SSM_D_INNER),
        w_out=w_out.astype(BF16))


def ssd_layer(x, gain, w, conv_state, h_init, *, batch, t):
    tm = min(1024, x.shape[0])
    proj = norm_matmul(x, gain, w["w_main"], tm=min(2048, x.shape[0]), tn=1280)
    dt_raw = norm_matmul(x, gain, w["w_dt"], tm=tm, tn=LANES)
    conv_init = jnp.pad(conv_state, ((0, 0), (8 - (SSM_CONV - 1), 0), (0, 0)))
    if t % SSM_CHUNK == 0:
        y, h_last = ssd_mixer(proj, dt_raw, conv_init, h_init, w, batch=batch, n_chunks=t // SSM_CHUNK,
                              t_valid=SSM_CHUNK)
    else:
        pad_rows = lambda a: jnp.pad(a.reshape(batch, t, -1), ((0, 0), (0, SSM_CHUNK - t), (0, 0))).reshape(
            batch * SSM_CHUNK, -1)
        y, h_last = ssd_mixer(pad_rows(proj), pad_rows(dt_raw), conv_init, h_init, w, batch=batch, n_chunks=1,
                              t_valid=t)
        y = y.reshape(batch, SSM_CHUNK, -1)[:, :t].reshape(batch * t, -1)
    return matmul_residual(y, w["w_out"], x, tm=tm), proj, h_last


def kernel(x_prompt, x_sample, cache_kv_cmp, cache_kv_sel, cache_kv_win, state_conv, state_ssm, page_table,
           norm_mix, norm_ffn, nsa_w_in, nsa_qk_gain, nsa_cmp_pe, nsa_cmp_w1, nsa_cmp_w2, nsa_w_out,
           ssd_w_in, ssd_conv_w, ssd_conv_b, ssd_dt_bias, ssd_a_log, ssd_d, ssd_norm, ssd_w_out,
           ffn_w_gate, ffn_w_up, ffn_w_down):
    bp, tp = x_prompt.shape[0], x_prompt.shape[1]
    bs, ts = x_sample.shape[0], x_sample.shape[1]
    depth = norm_mix.shape[0]
    past_len = page_table.shape[1] * PAGE_SIZE
    conv_dim = SSM_D_INNER + SSM_BC
    assert tp % SSM_CHUNK == 0 and ts < SSM_CHUNK and ts <= Q_PAD and ts >= SSM_CONV - 1
    xp = x_prompt.reshape(bp * tp, D_MODEL)
    xs = x_sample.reshape(bs * ts, D_MODEL)
    pt_flat = page_table.reshape(-1)
    feature_major = lambda a: a.transpose(0, 1, 3, 4, 5, 2)
    cmp_t, sel_t, win_t = feature_major(cache_kv_cmp), feature_major(cache_kv_sel), feature_major(cache_kv_win)
    kv_shape = (2, KV_GROUPS, HEAD_DIM)
    outs = {k: [] for k in ("cmp_p", "sel_p", "win_p", "conv_p", "ssm_p", "cmp_s", "sel_s", "win_s", "conv_s",
                            "ssm_s")}
    for i in range(depth):
        l = i // 2
        if i % 2 == 0:
            w = _nsa_weights(nsa_w_in[l], nsa_qk_gain[l], nsa_cmp_pe[l], nsa_cmp_w1[l], nsa_cmp_w2[l], nsa_w_out[l])
            xp, cmp_rows, sel_rows, win_rows = nsa_prompt_layer(xp, norm_mix[i], w, batch=bp, t=tp)
            row_major = lambda a: a.reshape((bp,) + kv_shape + (a.shape[-1],)).transpose(0, 4, 1, 2, 3)
            outs["cmp_p"].append(row_major(cmp_rows))
            outs["sel_p"].append(row_major(sel_rows))
            outs["win_p"].append(row_major(win_rows))
            kv = lambda a, b, t, j: a[:, NSA_Q_DIM + j * NSA_KV_DIM:NSA_Q_DIM + (j + 1) * NSA_KV_DIM].reshape(
                (b, t) + kv_shape)
            xs, proj_s, win_out = nsa_sample_layer(xs, norm_mix[i], w, cmp_t, sel_t, win_t, l, pt_flat, batch=bs,
                                                   t_new=ts, past_len=past_len)
            outs["cmp_s"].append(kv(proj_s, bs, ts, 0))
            outs["sel_s"].append(kv(proj_s, bs, ts, 1))
            outs["win_s"].append(win_out.transpose(0, 4, 1, 2, 3))
        else:
            w = _ssd_weights(ssd_w_in[l], ssd_conv_w[l], ssd_conv_b[l], ssd_dt_bias[l], ssd_a_log[l], ssd_d[l],
                             ssd_norm[l], ssd_w_out[l])
            xp, proj, h_last = ssd_layer(xp, norm_mix[i], w, jnp.zeros((bp, SSM_CONV - 1, conv_dim), F32),
                                         jnp.zeros((bp, SSM_D_INNER, SSM_STATE), F32), batch=bp, t=tp)
            outs["conv_p"].append(proj.reshape(bp, tp, -1)[:, tp - (SSM_CONV - 1):, SSM_D_INNER:])
            outs["ssm_p"].append(h_last.reshape(bp, SSM_HEADS, SSM_HEAD_DIM, SSM_STATE))
            xs, proj_s, h_last = ssd_layer(xs, norm_mix[i], w, state_conv[l],
                                           state_ssm[l].reshape(bs, SSM_D_INNER, SSM_STATE), batch=bs, t=ts)
            outs["conv_s"].append(proj_s.reshape(bs, ts, -1)[:, ts - (SSM_CONV - 1):, SSM_D_INNER:])
            outs["ssm_s"].append(h_last.reshape(bs, SSM_HEADS, SSM_HEAD_DIM, SSM_STATE))
        wg, wu, wd = ffn_w_gate[i].astype(BF16), ffn_w_up[i].astype(BF16), ffn_w_down[i].astype(BF16)
        xp = ffn(xp, norm_ffn[i], wg, wu, wd, tm=1024, th=1408)
        xs = ffn(xs, norm_ffn[i], wg, wu, wd, tm=bs * ts, th=256)
    order = ("cmp_p", "sel_p", "win_p", "conv_p", "ssm_p", "cmp_s", "sel_s", "win_s", "conv_s", "ssm_s")
    return (xp.reshape(bp, tp, D_MODEL), xs.reshape(bs, ts, D_MODEL)) + tuple(jnp.stack(outs[k]) for k in order)
```
